```python
import jax, jax.numpy as jnp
from jax import lax
import numpy as np

D_MODEL = 2048
BATCH = 1
SEQ = 8192
DEPTH = 1

PLE_DIM = 256
EPS = 1e-6
GLA_HEADS = 4
GLA_DK = 256
GLA_DV = 512
GLA_KEY = GLA_HEADS * GLA_DK
GLA_VAL = GLA_HEADS * GLA_DV
GLA_GATE_RANK = 16
GLA_GATE_NORM = 16.0
GLA_CHUNK = 64
CONV_CH = D_MODEL
CONV_K = 3
N_BRANCHES = 2
N_EXPERT_GROUPS = 4
EXPERTS_PER_GROUP = 8
N_EXPERTS = N_EXPERT_GROUPS * EXPERTS_PER_GROUP
TOP_K_IN_GROUP = 2
D_FF_EXPERT = 512

IN_SPLITS = [GLA_KEY, GLA_KEY, GLA_VAL, GLA_VAL, GLA_GATE_RANK, CONV_CH, CONV_CH, CONV_CH, N_BRANCHES * D_MODEL]
IN_COLS = int(sum(IN_SPLITS))
IN_IDX = np.cumsum(IN_SPLITS)[:-1].tolist()

kernel_name = "hybrid_gla_shortconv_hmoe_ple"


def rmsnorm(x, w):
    xf = x.astype(jnp.float32)
    y = xf * lax.rsqrt(jnp.mean(xf * xf, axis=-1, keepdims=True) + EPS) * w.astype(jnp.float32)
    return y.astype(x.dtype)


def gla_chunked(q, k, v, log_a):
    B, T, H, _ = q.shape
    N, C = T // GLA_CHUNK, GLA_CHUNK

    def to_chunks(t):
        return t.reshape(B, N, C, H, t.shape[-1]).transpose(1, 0, 3, 2, 4)

    q, k, v, log_a = to_chunks(q), to_chunks(k), to_chunks(v), to_chunks(log_a)
    b = jnp.cumsum(log_a, axis=3)
    b_last = b[:, :, :, -1:, :]
    q_dec = q * jnp.exp(b)
    k_inv = k * jnp.exp(-b)
    k_dec = k * jnp.exp(b_last - b)
    causal = jnp.tril(jnp.ones((C, C), dtype=bool))
    att = jnp.einsum('nbhik,nbhjk->nbhij', q_dec, k_inv)
    att = jnp.where(causal, att, 0.0)
    o_intra = jnp.einsum('nbhij,nbhjv->nbhiv', att, v)

    def step(S, inp):
        qd, kd, vc, dl = inp
        o = jnp.einsum('bhik,bhkv->bhiv', qd, S)
        S = S * dl[:, :, 0, :, None] + jnp.einsum('bhjk,bhjv->bhkv', kd, vc)
        return S, o

    S0 = jnp.zeros((B, H, q.shape[-1], v.shape[-1]), jnp.float32)
    _, o_inter = lax.scan(step, S0, (q_dec, k_dec, v, jnp.exp(b_last)))
    o = o_intra + o_inter
    return o.transpose(1, 0, 3, 2, 4).reshape(B, T, H, v.shape[-1])


def causal_dwconv(u, w, bias):
    y = lax.conv_general_dilated(u, w[:, None, :].astype(u.dtype), window_strides=(1,),
                                 padding=[(CONV_K - 1, 0)],
                                 dimension_numbers=('NWC', 'WIO', 'NWC'),
                                 feature_group_count=u.shape[-1])
    return y + bias


def setup_inputs(seed: int = 0) -> dict:
    key = jax.random.key(seed)
    ks = jax.random.split(key, 32)
    f32 = jnp.float32

    def nrm(k, shape, fan_in):
        return jax.random.normal(k, shape, f32) * (fan_in ** -0.5)

    def gain(k, shape):
        return 1.0 + 0.02 * jax.random.normal(k, shape, f32)

    def small(k, shape, s=0.02):
        return s * jax.random.normal(k, shape, f32)

    L = DEPTH
    return {
        "x": jax.random.normal(ks[0], (BATCH, SEQ, D_MODEL), f32),
        "p": jax.random.normal(ks[1], (DEPTH, BATCH, SEQ, PLE_DIM), f32),
        "norm_mix_w": gain(ks[2], (L, D_MODEL)),
        "w_in": nrm(ks[3], (L, D_MODEL, IN_COLS), D_MODEL),
        "b_merge": small(ks[4], (L, N_BRANCHES * D_MODEL)),
        "w_alpha_up": nrm(ks[5], (L, GLA_GATE_RANK, GLA_KEY), GLA_GATE_RANK),
        "b_alpha_up": small(ks[6], (L, GLA_KEY), 0.1),
        "gla_norm_w": gain(ks[7], (L, GLA_HEADS, GLA_DV)),
        "w_gla_out": nrm(ks[8], (L, GLA_VAL, D_MODEL), GLA_VAL),
        "conv_w": nrm(ks[9], (L, CONV_K, CONV_CH), CONV_K),
        "conv_b": small(ks[10], (L, CONV_CH)),
        "w_conv_out": nrm(ks[11], (L, CONV_CH, D_MODEL), CONV_CH),
        "w_mix_out": nrm(ks[12], (L, D_MODEL, D_MODEL), D_MODEL),
        "norm_ffn_w": gain(ks[13], (L, D_MODEL)),
        "w_router_group": nrm(ks[14], (L, D_MODEL, N_EXPERT_GROUPS), D_MODEL),
        "b_router_group": small(ks[15], (L, N_EXPERT_GROUPS), 0.01),
        "w_router_expert": nrm(ks[16], (L, D_MODEL, N_EXPERTS), D_MODEL),
        "b_router_expert": small(ks[17], (L, N_EXPERTS), 0.01),
        "w_e_gate": nrm(ks[18], (L, N_EXPERTS, D_MODEL, D_FF_EXPERT), D_MODEL),
        "w_e_up": nrm(ks[19], (L, N_EXPERTS, D_MODEL, D_FF_EXPERT), D_MODEL),
        "w_e_down": nrm(ks[20], (L, N_EXPERTS, D_FF_EXPERT, D_MODEL), D_FF_EXPERT),
        "ple_norm_w": gain(ks[21], (L, D_MODEL)),
        "w_ple_gate": nrm(ks[22], (L, D_MODEL, D_MODEL), D_MODEL),
        "w_ple_proj": nrm(ks[23], (L, PLE_DIM, D_MODEL), PLE_DIM),
        "final_norm_w": gain(ks[24], (D_MODEL,)),
    }


def reference(x, p, norm_mix_w, w_in, b_merge, w_alpha_up, b_alpha_up, gla_norm_w, w_gla_out,
              conv_w, conv_b, w_conv_out, w_mix_out, norm_ffn_w, w_router_group, b_router_group,
              w_router_expert, b_router_expert, w_e_gate, w_e_up, w_e_down, ple_norm_w,
              w_ple_gate, w_ple_proj, final_norm_w):
    B, T, D = x.shape
    dt = x.dtype
    f32 = jnp.float32
    for i in range(DEPTH):
        h = rmsnorm(x, norm_mix_w[i])
        proj = h @ w_in[i]
        q, k, v, g, a_low, cb, cc, cx, gm = jnp.split(proj, IN_IDX, axis=-1)

        qh = q.reshape(B, T, GLA_HEADS, GLA_DK).astype(f32) * (GLA_DK ** -0.5)
        kh = k.reshape(B, T, GLA_HEADS, GLA_DK).astype(f32)
        vh = v.reshape(B, T, GLA_HEADS, GLA_DV).astype(f32)
        z = (a_low @ w_alpha_up[i] + b_alpha_up[i]).astype(f32)
        log_a = (jax.nn.log_sigmoid(z) / GLA_GATE_NORM).reshape(B, T, GLA_HEADS, GLA_DK)
        o = gla_chunked(qh, kh, vh, log_a)
        o = o * lax.rsqrt(jnp.mean(o * o, axis=-1, keepdims=True) + EPS) * gla_norm_w[i].astype(f32)
        o = o.astype(dt).reshape(B, T, GLA_VAL) * jax.nn.silu(g)
        o_a = o @ w_gla_out[i]

        u = causal_dwconv(cc * cx, conv_w[i], conv_b[i])
        o_b = (cb * u) @ w_conv_out[i]

        gates = jax.nn.sigmoid(gm + b_merge[i]).reshape(B, T, N_BRANCHES, D)
        mixed = gates[:, :, 0, :] * o_a + gates[:, :, 1, :] * o_b
        x = x + mixed @ w_mix_out[i]

        h = rmsnorm(x, norm_ffn_w[i])
        hf = h.astype(f32)
        grp_prob = jax.nn.softmax(hf @ w_router_group[i].astype(f32) + b_router_group[i].astype(f32), axis=-1)
        g_p, g_idx = lax.top_k(grp_prob, 1)
        e_logits = (hf @ w_router_expert[i].astype(f32) + b_router_expert[i].astype(f32))
        e_logits = e_logits.reshape(B, T, N_EXPERT_GROUPS, EXPERTS_PER_GROUP)
        grp_onehot = jax.nn.one_hot(g_idx[..., 0], N_EXPERT_GROUPS, dtype=f32)
        sel_logits = jnp.einsum('btge,btg->bte', e_logits, grp_onehot)
        top_p, top_i = lax.top_k(jax.nn.softmax(sel_logits, axis=-1), TOP_K_IN_GROUP)
        w_pair = g_p * top_p / jnp.sum(top_p, axis=-1, keepdims=True)
        expert_id = g_idx * EXPERTS_PER_GROUP + top_i
        combine = jnp.einsum('btke,btk->bte', jax.nn.one_hot(expert_id, N_EXPERTS, dtype=f32), w_pair)
        hg = jnp.einsum('btd,edf->btef', h, w_e_gate[i])
        hu = jnp.einsum('btd,edf->btef', h, w_e_up[i])
        act = jax.nn.silu(hg) * hu * combine.astype(dt)[..., None]
        x = x + jnp.einsum('btef,efd->btd', act, w_e_down[i])

        pg = jax.nn.sigmoid(rmsnorm(x, ple_norm_w[i]) @ w_ple_gate[i])
        x = x + pg * (p[i] @ w_ple_proj[i])
    return rmsnorm(x, final_norm_w)
```

```python
import functools

import jax
import jax.numpy as jnp
from jax import lax
from jax.experimental import pallas as pl
from jax.experimental.pallas import tpu as pltpu

F32 = jnp.float32
BF16 = jnp.bfloat16
I32 = jnp.int32

D_MODEL = 2048
PLE_DIM = 256
EPS = 1e-6
GLA_HEADS = 4
GLA_DK = 256
GLA_DV = 512
GLA_KEY = GLA_HEADS * GLA_DK
GLA_VAL = GLA_HEADS * GLA_DV
GLA_GATE_RANK = 16
GLA_GATE_NORM = 16.0
GLA_CHUNK = 64
CONV_K = 3
N_GROUPS = 4
EXPERTS_PER_GROUP = 8
N_EXPERTS = N_GROUPS * EXPERTS_PER_GROUP
D_FF = 512

QKVG_COLS = 2 * GLA_KEY + 2 * GLA_VAL

LANES = 128
SUBLANES = 8
ROW_TILES = D_MODEL // LANES

EXPERT_TILE = 256
ROUTER_ROWS = 64
EXPERT_ROW0 = 8

MIB = 1024 * 1024


def _cp(sem, vmem_mib, **kw):
    return pltpu.CompilerParams(dimension_semantics=sem, vmem_limit_bytes=int(vmem_mib * MIB), **kw)


def _rms(x, w):
    return x * lax.rsqrt(jnp.mean(x * x, axis=-1, keepdims=True) + EPS) * w


def _dot(a, b):
    return jnp.dot(a, b, preferred_element_type=F32)


def _dot_nt(a, b):
    return lax.dot_general(a, b, (((1,), (1,)), ((), ())), preferred_element_type=F32)


def _dot_tn(a, b):
    return lax.dot_general(a, b, (((0,), (0,)), ((), ())), preferred_element_type=F32)


def _split_bf16(x):
    hi = x.astype(BF16)
    lo = (x - hi.astype(F32)).astype(BF16)
    return hi, lo


def _to_token_major(val):
    return val.reshape(val.shape[0], ROW_TILES, LANES)


def _from_token_major(val):
    return val.reshape(val.shape[0], D_MODEL)


def _cast_shifted(dst_ref, wa_ref, wb_ref):
    tn = wa_ref.shape[1]
    rows = 256

    def body(r, carry):
        sl = pl.ds(pl.multiple_of(r * rows, rows), rows)
        w = jnp.concatenate([wa_ref[sl, :], wb_ref[sl, :]], axis=1)
        dst_ref[sl, :] = w[:, GLA_GATE_RANK:GLA_GATE_RANK + tn].astype(BF16)
        return carry

    lax.fori_loop(0, D_MODEL // rows, body, 0)


def _norm_in_kernel(x_ref, w_ref, wal_ref, h_ref, al_ref, walbf_ref):
    @pl.when(pl.program_id(0) == 0)
    def _():
        walbf_ref[...] = wal_ref[...].astype(BF16)

    h = _rms(x_ref[...], w_ref[...]).astype(BF16)
    h_ref[...] = h
    al_ref[...] = _dot(h, walbf_ref[...])


def _norm_in(x, w, w_in, tm=512):
    t = x.shape[0]
    return pl.pallas_call(
        _norm_in_kernel,
        grid=(t // tm,),
        in_specs=[pl.BlockSpec((tm, D_MODEL), lambda i: (i, 0)),
                  pl.BlockSpec((1, D_MODEL), lambda i: (0, 0)),
                  pl.BlockSpec((D_MODEL, LANES), lambda i: (0, QKVG_COLS // LANES))],
        out_specs=[pl.BlockSpec((tm, D_MODEL), lambda i: (i, 0)),
                   pl.BlockSpec((tm, LANES), lambda i: (i, 0))],
        out_shape=[jax.ShapeDtypeStruct((t, D_MODEL), BF16),
                   jax.ShapeDtypeStruct((t, LANES), F32)],
        scratch_shapes=[pltpu.VMEM((D_MODEL, LANES), BF16)],
        compiler_params=_cp(("arbitrary",), 32),
        name="norm_in",
    )(x, w, w_in)


def _proj_qkvg_kernel(h_ref, w_ref, o_ref, wbf_ref):
    @pl.when(pl.program_id(1) == 0)
    def _():
        wbf_ref[...] = w_ref[...].astype(BF16)

    o_ref[...] = _dot(h_ref[...], wbf_ref[...]).astype(o_ref.dtype)


def _proj_qkvg(h, w_in, tm=1024, tn=512):
    t = h.shape[0]
    return pl.pallas_call(
        _proj_qkvg_kernel,
        grid=(QKVG_COLS // tn, t // tm),
        in_specs=[pl.BlockSpec((tm, D_MODEL), lambda n, m: (m, 0)),
                  pl.BlockSpec((D_MODEL, tn), lambda n, m: (0, n))],
        out_specs=pl.BlockSpec((tm, tn), lambda n, m: (m, n)),
        out_shape=jax.ShapeDtypeStruct((t, QKVG_COLS), BF16),
        scratch_shapes=[pltpu.VMEM((D_MODEL, tn), BF16)],
        compiler_params=_cp(("arbitrary", "arbitrary"), 40),
        name="proj_qkvg",
    )(h, w_in)


def _shifted_specs(col0, tn):
    return [pl.BlockSpec((D_MODEL, tn), lambda n, m: (0, col0 // tn + n)),
            pl.BlockSpec((D_MODEL, LANES), lambda n, m: (0, (col0 + (n + 1) * tn) // LANES))]


def _proj_conv_kernel(h_ref, wba_ref, wbb_ref, wca_ref, wcb_ref, wxa_ref, wxb_ref, cw_ref, cb_ref, o_ref,
                      wb_ref, wc_ref, wx_ref, prev_ref):
    m = pl.program_id(1)

    @pl.when(m == 0)
    def _():
        prev_ref[...] = jnp.zeros_like(prev_ref)
        _cast_shifted(wb_ref, wba_ref, wbb_ref)
        _cast_shifted(wc_ref, wca_ref, wcb_ref)
        _cast_shifted(wx_ref, wxa_ref, wxb_ref)

    h = h_ref[...]
    b = _dot(h, wb_ref[...])
    s = _dot(h, wc_ref[...]) * _dot(h, wx_ref[...])
    tm = s.shape[0]
    row = lax.broadcasted_iota(I32, s.shape, 0)
    prev = prev_ref[...]
    p1 = prev[SUBLANES - 1:SUBLANES, :]
    p2 = prev[SUBLANES - 2:SUBLANES - 1, :]
    s1 = jnp.where(row == 0, p1, pltpu.roll(s, 1, 0))
    s2 = jnp.where(row == 0, p2, jnp.where(row == 1, p1, pltpu.roll(s, 2, 0)))
    cw = cw_ref[...]
    u = cw[2:3, :] * s + cw[1:2, :] * s1 + cw[0:1, :] * s2 + cb_ref[...]
    o_ref[...] = (b * u).astype(o_ref.dtype)
    prev_ref[...] = s[tm - SUBLANES:, :]


def _proj_conv(h, w_in, conv_w, conv_b, tm=1024, tn=512):
    t = h.shape[0]
    specs = []
    for seg in range(3):
        specs += _shifted_specs(QKVG_COLS + seg * D_MODEL, tn)
    return pl.pallas_call(
        _proj_conv_kernel,
        grid=(D_MODEL // tn, t // tm),
        in_specs=[pl.BlockSpec((tm, D_MODEL), lambda n, m: (m, 0))] + specs +
                 [pl.BlockSpec((CONV_K, tn), lambda n, m: (0, n)),
                  pl.BlockSpec((1, tn), lambda n, m: (0, n))],
        out_specs=pl.BlockSpec((tm, tn), lambda n, m: (m, n)),
        out_shape=jax.ShapeDtypeStruct((t, D_MODEL), BF16),
        scratch_shapes=[pltpu.VMEM((D_MODEL, tn), BF16), pltpu.VMEM((D_MODEL, tn), BF16),
                        pltpu.VMEM((D_MODEL, tn), BF16), pltpu.VMEM((SUBLANES, tn), F32)],
        compiler_params=_cp(("arbitrary", "arbitrary"), 56),
        name="proj_conv",
    )(h, w_in, w_in, w_in, w_in, w_in, w_in, conv_w, conv_b)


def _proj_gates_kernel(h_ref, wa_ref, wb_ref, b_ref, o_ref, wbf_ref):
    @pl.when(pl.program_id(1) == 0)
    def _():
        _cast_shifted(wbf_ref, wa_ref, wb_ref)

    o_ref[...] = jax.nn.sigmoid(_dot(h_ref[...], wbf_ref[...]) + b_ref[...]).astype(o_ref.dtype)


def _proj_gates(h, w_in, b_merge, tm=1024, tn=512):
    t = h.shape[0]
    return pl.pallas_call(
        _proj_gates_kernel,
        grid=(2 * D_MODEL // tn, t // tm),
        in_specs=[pl.BlockSpec((tm, D_MODEL), lambda n, m: (m, 0))] +
                 _shifted_specs(QKVG_COLS + 3 * D_MODEL, tn) +
                 [pl.BlockSpec((1, tn), lambda n, m: (0, n))],
        out_specs=pl.BlockSpec((tm, tn), lambda n, m: (m, n)),
        out_shape=jax.ShapeDtypeStruct((t, 2 * D_MODEL), BF16),
        scratch_shapes=[pltpu.VMEM((D_MODEL, tn), BF16)],
        compiler_params=_cp(("arbitrary", "arbitrary"), 40),
        name="proj_gates",
    )(h, w_in, w_in, b_merge)


def _gla_kernel(q_ref, k_ref, v_ref, g_ref, al_ref, wup_ref, bup_ref, nw_ref, o_ref, st_ref, *, n_chunks):
    c_len = GLA_CHUNK

    @pl.when(pl.program_id(1) == 0)
    def _():
        st_ref[...] = jnp.zeros_like(st_ref)

    row = lax.broadcasted_iota(I32, (c_len, c_len), 0)
    col = lax.broadcasted_iota(I32, (c_len, c_len), 1)
    causal = col <= row
    tril = jnp.where(causal, 1.0, 0.0).astype(BF16)
    wup = wup_ref[0]
    bup = bup_ref[0]
    nw = nw_ref[0]

    for c in range(n_chunks):
        sl = pl.ds(c * c_len, c_len)
        z = _dot(al_ref[sl, :].astype(BF16), wup) + bup
        la = (jnp.minimum(z, 0.0) - jnp.log1p(jnp.exp(-jnp.abs(z)))) * (1.0 / GLA_GATE_NORM)
        la_hi, la_lo = _split_bf16(la)
        b = _dot(tril, la_hi) + _dot(tril, la_lo)
        b_last = b[c_len - 1:c_len, :]
        q = q_ref[sl, :].astype(F32)
        k = k_ref[sl, :].astype(F32)
        v = v_ref[sl, :]
        qd = (q * (GLA_DK ** -0.5) * jnp.exp(b)).astype(BF16)
        ki = (k * jnp.exp(-b)).astype(BF16)
        kd = (k * jnp.exp(b_last - b)).astype(BF16)
        att = jnp.where(causal, _dot_nt(qd, ki), 0.0).astype(BF16)
        st = st_ref[...]
        o = _dot(att, v) + _dot_nt(qd, st.astype(BF16))
        st_ref[...] = st * jnp.exp(b_last) + _dot_tn(v, kd)
        o = o * lax.rsqrt(jnp.mean(o * o, axis=-1, keepdims=True) + EPS) * nw
        g = g_ref[sl, :].astype(F32)
        o_ref[sl, :] = (o * (g * jax.nn.sigmoid(g))).astype(o_ref.dtype)


def _gla(qkvg, a_low, w_up, b_up, norm_w, tb=512):
    t = qkvg.shape[0]
    kq = GLA_KEY // GLA_DK
    kv = 2 * GLA_KEY // GLA_DV
    kg = kv + GLA_VAL // GLA_DV
    kern = functools.partial(_gla_kernel, n_chunks=tb // GLA_CHUNK)
    return pl.pallas_call(
        kern,
        grid=(GLA_HEADS, t // tb),
        in_specs=[pl.BlockSpec((tb, GLA_DK), lambda h, i: (i, h)),
                  pl.BlockSpec((tb, GLA_DK), lambda h, i: (i, kq + h)),
                  pl.BlockSpec((tb, GLA_DV), lambda h, i: (i, kv + h)),
                  pl.BlockSpec((tb, GLA_DV), lambda h, i: (i, kg + h)),
                  pl.BlockSpec((tb, LANES), lambda h, i: (i, 0)),
                  pl.BlockSpec((1, LANES, GLA_DK), lambda h, i: (h, 0, 0)),
                  pl.BlockSpec((1, 1, GLA_DK), lambda h, i: (h, 0, 0)),
                  pl.BlockSpec((1, 1, GLA_DV), lambda h, i: (h, 0, 0))],
        out_specs=pl.BlockSpec((tb, GLA_DV), lambda h, i: (i, h)),
        out_shape=jax.ShapeDtypeStruct((t, GLA_VAL), BF16),
        scratch_shapes=[pltpu.VMEM((GLA_DV, GLA_DK), F32)],
        compiler_params=_cp(("arbitrary", "arbitrary"), 32),
        name="gla",
    )(qkvg, qkvg, qkvg, qkvg, a_low, w_up, b_up, norm_w)


def _merge_kernel(oa_ref, ob_ref, wa_ref, wb_ref, g0_ref, g1_ref, o_ref, wabf_ref, wbbf_ref):
    @pl.when(pl.program_id(1) == 0)
    def _():
        wabf_ref[...] = wa_ref[...].astype(BF16)
        wbbf_ref[...] = wb_ref[...].astype(BF16)

    a = _dot(oa_ref[...], wabf_ref[...])
    b = _dot(ob_ref[...], wbbf_ref[...])
    o_ref[...] = (g0_ref[...].astype(F32) * a + g1_ref[...].astype(F32) * b).astype(o_ref.dtype)


def _merge(oa, ob, w_a, w_b, gates, tm=1024, tn=512):
    t = oa.shape[0]
    nb = D_MODEL // tn
    return pl.pallas_call(
        _merge_kernel,
        grid=(nb, t // tm),
        in_specs=[pl.BlockSpec((tm, D_MODEL), lambda n, m: (m, 0)),
                  pl.BlockSpec((tm, D_MODEL), lambda n, m: (m, 0)),
                  pl.BlockSpec((D_MODEL, tn), lambda n, m: (0, n)),
                  pl.BlockSpec((D_MODEL, tn), lambda n, m: (0, n)),
                  pl.BlockSpec((tm, tn), lambda n, m: (m, n)),
                  pl.BlockSpec((tm, tn), lambda n, m: (m, nb + n))],
        out_specs=pl.BlockSpec((tm, tn), lambda n, m: (m, n)),
        out_shape=jax.ShapeDtypeStruct((t, D_MODEL), BF16),
        scratch_shapes=[pltpu.VMEM((D_MODEL, tn), BF16), pltpu.VMEM((D_MODEL, tn), BF16)],
        compiler_params=_cp(("arbitrary", "arbitrary"), 48),
        name="merge",
    )(oa, ob, w_a, w_b, gates, gates)


def _mix_kernel(a_ref, w_ref, x_ref, o_ref, wbf_ref):
    @pl.when(pl.program_id(1) == 0)
    def _():
        wbf_ref[...] = w_ref[...].astype(BF16)

    o_ref[...] = x_ref[...] + _dot(a_ref[...], wbf_ref[...])


def _mix(mixed, w, x, tm=1024, tn=512):
    t = x.shape[0]
    return pl.pallas_call(
        _mix_kernel,
        grid=(D_MODEL // tn, t // tm),
        in_specs=[pl.BlockSpec((tm, D_MODEL), lambda n, m: (m, 0)),
                  pl.BlockSpec((D_MODEL, tn), lambda n, m: (0, n)),
                  pl.BlockSpec((tm, tn), lambda n, m: (m, n))],
        out_specs=pl.BlockSpec((tm, tn), lambda n, m: (m, n)),
        out_shape=jax.ShapeDtypeStruct((t, D_MODEL), F32),
        scratch_shapes=[pltpu.VMEM((D_MODEL, tn), BF16)],
        compiler_params=_cp(("arbitrary", "arbitrary"), 40),
        name="mix",
    )(mixed, w, x)


def _route_kernel(x_ref, nw_ref, wr_ref, br_ref, hp_ref, ids_ref, wts_ref):
    h = _rms(x_ref[...], nw_ref[...])
    hp_ref[...] = _to_token_major(h.astype(BF16))
    h_hi, h_lo = _split_bf16(h)
    w_hi, w_lo = _split_bf16(wr_ref[...])
    logits = _dot_nt(w_hi, h_hi) + _dot_nt(w_hi, h_lo) + _dot_nt(w_lo, h_hi) + br_ref[...]
    tm = logits.shape[1]

    best = logits[0:1, :]
    gidx = jnp.zeros((1, tm), I32)
    for i in range(1, N_GROUPS):
        li = logits[i:i + 1, :]
        take = li > best
        best = jnp.where(take, li, best)
        gidx = jnp.where(take, i, gidx)
    gsum = jnp.zeros((1, tm), F32)
    for i in range(N_GROUPS):
        gsum = gsum + jnp.exp(logits[i:i + 1, :] - best)
    g_p = 1.0 / gsum

    sel = logits[EXPERT_ROW0:EXPERT_ROW0 + EXPERTS_PER_GROUP, :]
    for g in range(1, N_GROUPS):
        r0 = EXPERT_ROW0 + g * EXPERTS_PER_GROUP
        sel = jnp.where(gidx == g, logits[r0:r0 + EXPERTS_PER_GROUP, :], sel)
    eio = lax.broadcasted_iota(I32, sel.shape, 0)
    m1 = jnp.max(sel, axis=0, keepdims=True)
    i1 = jnp.min(jnp.where(sel == m1, eio, EXPERTS_PER_GROUP), axis=0, keepdims=True)
    rest = jnp.where(eio == i1, -jnp.inf, sel)
    m2 = jnp.max(rest, axis=0, keepdims=True)
    i2 = jnp.min(jnp.where(rest == m2, eio, EXPERTS_PER_GROUP), axis=0, keepdims=True)
    p2 = jnp.exp(m2 - m1)
    w1 = g_p / (1.0 + p2)
    w2 = g_p * p2 / (1.0 + p2)
    e1 = gidx * EXPERTS_PER_GROUP + i1
    e2 = gidx * EXPERTS_PER_GROUP + i2
    rio = lax.broadcasted_iota(I32, (SUBLANES, tm), 0)
    ids_ref[...] = jnp.where(rio == 0, e1, jnp.where(rio == 1, e2, 0))
    wts_ref[...] = jnp.where(rio == 0, w1, jnp.where(rio == 1, w2, 0.0))


def _route(x1, norm_w, wr_t, br, tm=256):
    t = x1.shape[0]
    return pl.pallas_call(
        _route_kernel,
        grid=(t // tm,),
        in_specs=[pl.BlockSpec((tm, D_MODEL), lambda i: (i, 0)),
                  pl.BlockSpec((1, D_MODEL), lambda i: (0, 0)),
                  pl.BlockSpec((ROUTER_ROWS, D_MODEL), lambda i: (0, 0)),
                  pl.BlockSpec((ROUTER_ROWS, 1), lambda i: (0, 0))],
        out_specs=[pl.BlockSpec((tm, ROW_TILES, LANES), lambda i: (i, 0, 0)),
                   pl.BlockSpec((SUBLANES, tm), lambda i: (0, i)),
                   pl.BlockSpec((SUBLANES, tm), lambda i: (0, i))],
        out_shape=[jax.ShapeDtypeStruct((t, ROW_TILES, LANES), BF16),
                   jax.ShapeDtypeStruct((SUBLANES, t), I32),
                   jax.ShapeDtypeStruct((SUBLANES, t), F32)],
        compiler_params=_cp(("arbitrary",), 32),
        name="route",
    )(x1, norm_w, wr_t, br)


def _plan_kernel(ids_ref, dest_ref, cnt_ref, base_ref):
    phase = pl.program_id(0)
    step = pl.program_id(1)
    tm = ids_ref.shape[1]
    eio = lax.broadcasted_iota(I32, (N_EXPERTS, tm), 0)
    ids = ids_ref[...]
    oh = [jnp.where(eio == ids[k:k + 1, :], 1.0, 0.0) for k in range(2)]

    @pl.when((phase == 0) & (step == 0))
    def _():
        base_ref[...] = jnp.zeros_like(base_ref)

    @pl.when(phase == 0)
    def _():
        cnt = jnp.sum(oh[0] + oh[1], axis=1, keepdims=True)
        base_ref[...] = base_ref[...] + cnt
        dest_ref[0] = jnp.zeros(dest_ref.shape[1:], I32)
        cnt_ref[...] = base_ref[...]

    @pl.when((phase == 1) & (step == 0))
    def _():
        tiles = jnp.floor((base_ref[...] + (EXPERT_TILE - 1)) * (1.0 / EXPERT_TILE))
        r = lax.broadcasted_iota(I32, (N_EXPERTS, N_EXPERTS), 0)
        c = lax.broadcasted_iota(I32, (N_EXPERTS, N_EXPERTS), 1)
        lower = jnp.where(c < r, 1.0, 0.0).astype(BF16)
        base_ref[...] = _dot(lower, tiles.astype(BF16)) * float(EXPERT_TILE)

    @pl.when(phase == 1)
    def _():
        r = lax.broadcasted_iota(I32, (tm, tm), 0)
        c = lax.broadcasted_iota(I32, (tm, tm), 1)
        upper = jnp.where(r <= c, 1.0, 0.0).astype(BF16)
        base = base_ref[...][:, 0:1]
        rows = []
        for k in range(2):
            cum = _dot(oh[k].astype(BF16), upper)
            rows.append(jnp.sum(oh[k] * (cum - 1.0 + base), axis=0, keepdims=True))
            base = base + cum[:, tm - 1:tm]
        base_ref[...] = jnp.broadcast_to(base, base_ref.shape)
        rio = lax.broadcasted_iota(I32, (SUBLANES, tm), 0)
        d0 = rows[0].astype(I32)
        d1 = rows[1].astype(I32)
        dest_ref[0] = jnp.where(rio == 0, d0, jnp.where(rio == 1, d1, 0))


def _plan(ids, tm=512):
    t = ids.shape[1]
    return pl.pallas_call(
        _plan_kernel,
        grid=(2, t // tm),
        in_specs=[pl.BlockSpec((SUBLANES, tm), lambda p, i: (0, i))],
        out_specs=[pl.BlockSpec((1, SUBLANES, tm), lambda p, i: (p, 0, i)),
                   pl.BlockSpec((N_EXPERTS, LANES), lambda p, i: (0, 0))],
        out_shape=[jax.ShapeDtypeStruct((2, SUBLANES, t), I32),
                   jax.ShapeDtypeStruct((N_EXPERTS, LANES), F32)],
        scratch_shapes=[pltpu.VMEM((N_EXPERTS, LANES), F32)],
        compiler_params=_cp(("arbitrary", "arbitrary"), 32),
        name="plan",
    )(ids)


def _dispatch_kernel(dest_ref, hp_ref, zero_ref, xs_ref, sem, *, tm, t_total):
    del zero_ref
    base = pl.program_id(0) * tm

    def row_copy(k, tok):
        return pltpu.make_async_copy(hp_ref.at[tok], xs_ref.at[dest_ref[k * t_total + tok]], sem)

    def issue(i, c):
        row_copy(0, base + i).start()
        row_copy(1, base + i).start()
        return c

    def drain(i, c):
        row_copy(0, base + i).wait()
        row_copy(1, base + i).wait()
        return c

    lax.fori_loop(0, tm, issue, 0)
    lax.fori_loop(0, tm, drain, 0)


def _dispatch(dest_flat, hp3, n_rows, tm=512):
    t = hp3.shape[0]
    zeros = jnp.zeros((n_rows, ROW_TILES, LANES), BF16)
    kern = functools.partial(_dispatch_kernel, tm=tm, t_total=t)
    return pl.pallas_call(
        kern,
        grid_spec=pltpu.PrefetchScalarGridSpec(
            num_scalar_prefetch=1,
            grid=(t // tm,),
            in_specs=[pl.BlockSpec(memory_space=pl.ANY), pl.BlockSpec(memory_space=pl.ANY)],
            out_specs=pl.BlockSpec(memory_space=pl.ANY),
            scratch_shapes=[pltpu.SemaphoreType.DMA(())]),
        out_shape=jax.ShapeDtypeStruct((n_rows, ROW_TILES, LANES), BF16),
        input_output_aliases={2: 0},
        compiler_params=_cp(("arbitrary",), 16, has_side_effects=True),
        name="dispatch",
    )(dest_flat, hp3, zeros)


def _experts_kernel(te_ref, na_ref, xs_ref, wg_ref, wu_ref, wd_ref, ys_ref, wgbf_ref, wubf_ref, wdbf_ref):
    i = pl.program_id(0)
    active = i < na_ref[0]
    changed = (i == 0) | (te_ref[i] != te_ref[jnp.maximum(i - 1, 0)])

    @pl.when(active & changed)
    def _():
        wgbf_ref[...] = wg_ref[0].astype(BF16)
        wubf_ref[...] = wu_ref[0].astype(BF16)
        wdbf_ref[...] = wd_ref[0].astype(BF16)

    @pl.when(active)
    def _():
        x = _from_token_major(xs_ref[...])
        hg = _dot(x, wgbf_ref[...])
        hu = _dot(x, wubf_ref[...])
        act = (hg * jax.nn.sigmoid(hg) * hu).astype(BF16)
        ys_ref[...] = _to_token_major(_dot(act, wdbf_ref[...]).astype(BF16))


def _experts(tile_e, n_act, xs, w_gate, w_up, w_down, n_tiles):
    def row_map(i, te, na):
        return (jnp.minimum(i, na[0] - 1), 0, 0)

    def w_map(i, te, na):
        return (te[i], 0, 0)

    return pl.pallas_call(
        _experts_kernel,
        grid_spec=pltpu.PrefetchScalarGridSpec(
            num_scalar_prefetch=2,
            grid=(n_tiles,),
            in_specs=[pl.BlockSpec((EXPERT_TILE, ROW_TILES, LANES), row_map),
                      pl.BlockSpec((1, D_MODEL, D_FF), w_map),
                      pl.BlockSpec((1, D_MODEL, D_FF), w_map),
                      pl.BlockSpec((1, D_FF, D_MODEL), w_map)],
            out_specs=pl.BlockSpec((EXPERT_TILE, ROW_TILES, LANES), row_map),
            scratch_shapes=[pltpu.VMEM((D_MODEL, D_FF), BF16),
                            pltpu.VMEM((D_MODEL, D_FF), BF16),
                            pltpu.VMEM((D_FF, D_MODEL), BF16)]),
        out_shape=jax.ShapeDtypeStruct(xs.shape, BF16),
        input_output_aliases={2: 0},
        compiler_params=_cp(("arbitrary",), 48),
        name="experts",
    )(tile_e, n_act, xs, w_gate, w_up, w_down)


def _combine_kernel(dest_ref, ys_ref, x_ref, wt_ref, p_ref, nw_ref, wg_ref, wp_ref, fw_ref, o_ref,
                    buf0_ref, buf1_ref, sem, *, tm, t_total):
    base = pl.program_id(0) * tm
    bufs = (buf0_ref, buf1_ref)

    def row_copy(k, i):
        return pltpu.make_async_copy(ys_ref.at[dest_ref[k * t_total + base + i]], bufs[k].at[i], sem)

    def issue(i, c):
        row_copy(0, i).start()
        row_copy(1, i).start()
        return c

    def drain(i, c):
        row_copy(0, i).wait()
        row_copy(1, i).wait()
        return c

    lax.fori_loop(0, tm, issue, 0)
    lax.fori_loop(0, tm, drain, 0)

    wt = wt_ref[...]
    y0 = _from_token_major(buf0_ref[...]).astype(F32)
    y1 = _from_token_major(buf1_ref[...]).astype(F32)
    x2 = x_ref[...] + wt[:, 0:1] * y0 + wt[:, 1:2] * y1
    hn = _rms(x2, nw_ref[...]).astype(BF16)
    pg = jax.nn.sigmoid(_dot(hn, wg_ref[...]))
    x3 = x2 + pg * _dot(p_ref[...].astype(BF16), wp_ref[...])
    o_ref[...] = _rms(x3, fw_ref[...])


def _combine_ple(dest_flat, ys3, x1, wts_t, p, ple_norm_w, w_gate, w_proj, final_w, tm=256):
    t = x1.shape[0]
    kern = functools.partial(_combine_kernel, tm=tm, t_total=t)
    row = lambda i, d: (i, 0)
    fix = lambda i, d: (0, 0)
    return pl.pallas_call(
        kern,
        grid_spec=pltpu.PrefetchScalarGridSpec(
            num_scalar_prefetch=1,
            grid=(t // tm,),
            in_specs=[pl.BlockSpec(memory_space=pl.ANY),
                      pl.BlockSpec((tm, D_MODEL), row),
                      pl.BlockSpec((tm, SUBLANES), row),
                      pl.BlockSpec((tm, PLE_DIM), row),
                      pl.BlockSpec((1, D_MODEL), fix),
                      pl.BlockSpec((D_MODEL, D_MODEL), fix),
                      pl.BlockSpec((PLE_DIM, D_MODEL), fix),
                      pl.BlockSpec((1, D_MODEL), fix)],
            out_specs=pl.BlockSpec((tm, D_MODEL), row),
            scratch_shapes=[pltpu.VMEM((tm, ROW_TILES, LANES), BF16),
                            pltpu.VMEM((tm, ROW_TILES, LANES), BF16),
                            pltpu.SemaphoreType.DMA(())]),
        out_shape=jax.ShapeDtypeStruct((t, D_MODEL), F32),
        compiler_params=_cp(("arbitrary",), 48),
        name="combine_ple",
    )(dest_flat, ys3, x1, wts_t, p, ple_norm_w, w_gate, w_proj, final_w)


def _tile_table(counts, n_tiles):
    tiles = (counts.astype(I32) + (EXPERT_TILE - 1)) // EXPERT_TILE
    ends = jnp.cumsum(tiles)
    n_act = ends[-1]
    idx = jnp.minimum(jnp.arange(n_tiles, dtype=I32), n_act - 1)
    tile_e = jnp.sum((idx[:, None] >= ends[None, :]).astype(I32), axis=1)
    return tile_e.astype(I32), n_act.reshape(1).astype(I32)


def _block(x, p, norm_mix_w, w_in, b_merge, w_alpha_up, b_alpha_up, gla_norm_w, w_gla_out,
           conv_w, conv_b, w_conv_out, w_mix_out, norm_ffn_w, w_router_group, b_router_group,
           w_router_expert, b_router_expert, w_e_gate, w_e_up, w_e_down, ple_norm_w,
           w_ple_gate, w_ple_proj, final_norm_w):
    t = x.shape[0]
    n_tiles = (2 * t) // EXPERT_TILE + N_EXPERTS
    n_rows = n_tiles * EXPERT_TILE

    w_up = jnp.pad(w_alpha_up, ((0, LANES - GLA_GATE_RANK), (0, 0))).astype(BF16)
    w_up = w_up.reshape(LANES, GLA_HEADS, GLA_DK).transpose(1, 0, 2)
    b_up = b_alpha_up.reshape(GLA_HEADS, 1, GLA_DK)
    gnw = gla_norm_w.reshape(GLA_HEADS, 1, GLA_DV)
    wr_t = jnp.zeros((ROUTER_ROWS, D_MODEL), F32)
    wr_t = wr_t.at[0:N_GROUPS].set(w_router_group.T)
    wr_t = wr_t.at[EXPERT_ROW0:EXPERT_ROW0 + N_EXPERTS].set(w_router_expert.T)
    br = jnp.zeros((ROUTER_ROWS, 1), F32)
    br = br.at[0:N_GROUPS, 0].set(b_router_group)
    br = br.at[EXPERT_ROW0:EXPERT_ROW0 + N_EXPERTS, 0].set(b_router_expert)

    h, a_low = _norm_in(x, norm_mix_w.reshape(1, D_MODEL), w_in)
    qkvg = _proj_qkvg(h, w_in)
    ob = _proj_conv(h, w_in, conv_w, conv_b.reshape(1, D_MODEL))
    gates = _proj_gates(h, w_in, b_merge.reshape(1, 2 * D_MODEL))
    oa = _gla(qkvg, a_low, w_up, b_up, gnw)
    mixed = _merge(oa, ob, w_gla_out, w_conv_out, gates)
    x1 = _mix(mixed, w_mix_out, x)

    hp, ids, wts = _route(x1, norm_ffn_w.reshape(1, D_MODEL), wr_t, br)
    dest, counts = _plan(ids)
    dest_flat = dest[1, 0:2].reshape(2 * t)
    tile_e, n_act = _tile_table(counts[:, 0], n_tiles)
    xs = _dispatch(dest_flat, hp, n_rows)
    ys = _experts(tile_e, n_act, xs, w_e_gate, w_e_up, w_e_down, n_tiles)
    return _combine_ple(dest_flat, ys, x1, wts.T, p,
                        ple_norm_w.reshape(1, D_MODEL), w_ple_gate.astype(BF16),
                        w_ple_proj.astype(BF16), final_norm_w.reshape(1, D_MODEL))


def kernel(x, p, norm_mix_w, w_in, b_merge, w_alpha_up, b_alpha_up, gla_norm_w, w_gla_out, conv_w, conv_b, w_conv_out, w_mix_out, norm_ffn_w, w_router_group, b_router_group, w_router_expert, b_router_expert, w_e_gate, w_e_up, w_e_down, ple_norm_w, w_ple_gate, w_ple_proj, final_norm_w):
    depth, batch = p.shape[0], x.shape[0]
    assert depth == 1 and batch == 1, "kernel is specialised to one layer and one sequence"
    out = _block(x[0], p[0, 0], norm_mix_w[0], w_in[0], b_merge[0], w_alpha_up[0], b_alpha_up[0],
                 gla_norm_w[0], w_gla_out[0], conv_w[0], conv_b[0], w_conv_out[0], w_mix_out[0],
                 norm_ffn_w[0], w_router_group[0], b_router_group[0], w_router_expert[0],
                 b_router_expert[0], w_e_gate[0], w_e_up[0], w_e_down[0], ple_norm_w[0],
                 w_ple_gate[0], w_ple_proj[0], final_norm_w)
    return out[None]
```

```python
import functools

import jax
import jax.numpy as jnp
from jax import lax
from jax.experimental import pallas as pl
from jax.experimental.pallas import tpu as pltpu

F32 = jnp.float32
BF16 = jnp.bfloat16
I32 = jnp.int32

D_MODEL = 2048
PLE_DIM = 256
EPS = 1e-6
GLA_HEADS = 4
GLA_DK = 256
GLA_DV = 512
GLA_KEY = GLA_HEADS * GLA_DK
GLA_VAL = GLA_HEADS * GLA_DV
GLA_GATE_RANK = 16
GLA_GATE_NORM = 16.0
GLA_CHUNK = 64
CONV_K = 3
N_GROUPS = 4
EXPERTS_PER_GROUP = 8
N_EXPERTS = N_GROUPS * EXPERTS_PER_GROUP
D_FF = 512

QKVG_COLS = 2 * GLA_KEY + 2 * GLA_VAL
CONV_COL0 = QKVG_COLS + GLA_GATE_RANK

LANES = 128
SUBLANES = 8
ROW_TILES = D_MODEL // LANES

EXPERT_TILE = 256
DMA_UNROLL = 8
ROUTER_ROWS = 64
EXPERT_ROW0 = 8

MIB = 1024 * 1024


def _cp(sem, vmem_mib, **kw):
    return pltpu.CompilerParams(dimension_semantics=sem, vmem_limit_bytes=int(vmem_mib * MIB), **kw)


def _rms(x, w):
    return x * lax.rsqrt(jnp.mean(x * x, axis=-1, keepdims=True) + EPS) * w


def _dot(a, b):
    return jnp.dot(a, b, preferred_element_type=F32)


def _dot_nt(a, b):
    return lax.dot_general(a, b, (((1,), (1,)), ((), ())), preferred_element_type=F32)


def _dot_tn(a, b):
    return lax.dot_general(a, b, (((0,), (0,)), ((), ())), preferred_element_type=F32)


def _split_bf16(x):
    hi = x.astype(BF16)
    lo = (x - hi.astype(F32)).astype(BF16)
    return hi, lo


def _to_token_major(val):
    return val.reshape(val.shape[0], ROW_TILES, LANES)


def _from_token_major(val):
    return val.reshape(val.shape[0], D_MODEL)


def _norm_in_kernel(x_ref, w_ref, wal_ref, h_ref, al_ref, walbf_ref):
    @pl.when(pl.program_id(0) == 0)
    def _():
        walbf_ref[...] = wal_ref[...].astype(BF16)

    h = _rms(x_ref[...], w_ref[...]).astype(BF16)
    h_ref[...] = h
    al_ref[...] = _dot_nt(h, walbf_ref[...])


def _norm_in(x, w, w_in_t, tm=512):
    t = x.shape[0]
    return pl.pallas_call(
        _norm_in_kernel,
        grid=(t // tm,),
        in_specs=[pl.BlockSpec((tm, D_MODEL), lambda i: (i, 0)),
                  pl.BlockSpec((1, D_MODEL), lambda i: (0, 0)),
                  pl.BlockSpec((LANES, D_MODEL), lambda i: (QKVG_COLS // LANES, 0))],
        out_specs=[pl.BlockSpec((tm, D_MODEL), lambda i: (i, 0)),
                   pl.BlockSpec((tm, LANES), lambda i: (i, 0))],
        out_shape=[jax.ShapeDtypeStruct((t, D_MODEL), BF16),
                   jax.ShapeDtypeStruct((t, LANES), F32)],
        scratch_shapes=[pltpu.VMEM((LANES, D_MODEL), BF16)],
        compiler_params=_cp(("arbitrary",), 32),
        name="norm_in",
    )(x, w, w_in_t)


def _w_rows_spec(row0, tn):
    assert row0 % SUBLANES == 0 and tn % SUBLANES == 0
    return pl.BlockSpec((pl.Element(tn), pl.Element(D_MODEL)),
                        lambda n, m: (pl.multiple_of(row0 + n * tn, SUBLANES), 0))


def _proj_qkvg_kernel(h_ref, w_ref, o_ref, wbf_ref):
    @pl.when(pl.program_id(1) == 0)
    def _():
        wbf_ref[...] = w_ref[...].astype(BF16)

    o_ref[...] = _dot_nt(h_ref[...], wbf_ref[...]).astype(o_ref.dtype)


def _proj_qkvg(h, w_in_t, tm=1024, tn=1024):
    t = h.shape[0]
    return pl.pallas_call(
        _proj_qkvg_kernel,
        grid=(QKVG_COLS // tn, t // tm),
        in_specs=[pl.BlockSpec((tm, D_MODEL), lambda n, m: (m, 0)),
                  pl.BlockSpec((tn, D_MODEL), lambda n, m: (n, 0))],
        out_specs=pl.BlockSpec((tm, tn), lambda n, m: (m, n)),
        out_shape=jax.ShapeDtypeStruct((t, QKVG_COLS), BF16),
        scratch_shapes=[pltpu.VMEM((tn, D_MODEL), BF16)],
        compiler_params=_cp(("arbitrary", "arbitrary"), 48),
        name="proj_qkvg",
    )(h, w_in_t)


def _proj_conv_kernel(h_ref, wb_ref, wc_ref, wx_ref, cw_ref, cb_ref, o_ref,
                      wbbf_ref, wcbf_ref, wxbf_ref, prev_ref):
    m = pl.program_id(1)

    @pl.when(m == 0)
    def _():
        prev_ref[...] = jnp.zeros_like(prev_ref)
        wbbf_ref[...] = wb_ref[...].astype(BF16)
        wcbf_ref[...] = wc_ref[...].astype(BF16)
        wxbf_ref[...] = wx_ref[...].astype(BF16)

    h = h_ref[...]
    b = _dot_nt(h, wbbf_ref[...])
    s = _dot_nt(h, wcbf_ref[...]) * _dot_nt(h, wxbf_ref[...])
    tm = s.shape[0]
    row = lax.broadcasted_iota(I32, s.shape, 0)
    prev = prev_ref[...]
    p1 = prev[SUBLANES - 1:SUBLANES, :]
    p2 = prev[SUBLANES - 2:SUBLANES - 1, :]
    s1 = jnp.where(row == 0, p1, pltpu.roll(s, 1, 0))
    s2 = jnp.where(row == 0, p2, jnp.where(row == 1, p1, pltpu.roll(s, 2, 0)))
    cw = cw_ref[...]
    u = cw[2:3, :] * s + cw[1:2, :] * s1 + cw[0:1, :] * s2 + cb_ref[...]
    o_ref[...] = (b * u).astype(o_ref.dtype)
    prev_ref[...] = s[tm - SUBLANES:, :]


def _proj_conv(h, w_in_t, conv_w, conv_b, tm=1024, tn=512):
    t = h.shape[0]
    return pl.pallas_call(
        _proj_conv_kernel,
        grid=(D_MODEL // tn, t // tm),
        in_specs=[pl.BlockSpec((tm, D_MODEL), lambda n, m: (m, 0))] +
                 [_w_rows_spec(CONV_COL0 + seg * D_MODEL, tn) for seg in range(3)] +
                 [pl.BlockSpec((CONV_K, tn), lambda n, m: (0, n)),
                  pl.BlockSpec((1, tn), lambda n, m: (0, n))],
        out_specs=pl.BlockSpec((tm, tn), lambda n, m: (m, n)),
        out_shape=jax.ShapeDtypeStruct((t, D_MODEL), BF16),
        scratch_shapes=[pltpu.VMEM((tn, D_MODEL), BF16), pltpu.VMEM((tn, D_MODEL), BF16),
                        pltpu.VMEM((tn, D_MODEL), BF16), pltpu.VMEM((SUBLANES, tn), F32)],
        compiler_params=_cp(("arbitrary", "arbitrary"), 48),
        name="proj_conv",
    )(h, w_in_t, w_in_t, w_in_t, conv_w, conv_b)


def _proj_gates_kernel(h_ref, w_ref, b_ref, o_ref, wbf_ref):
    @pl.when(pl.program_id(1) == 0)
    def _():
        wbf_ref[...] = w_ref[...].astype(BF16)

    o_ref[...] = jax.nn.sigmoid(_dot_nt(h_ref[...], wbf_ref[...]) + b_ref[...]).astype(o_ref.dtype)


def _proj_gates(h, w_in_t, b_merge, tm=1024, tn=1024):
    t = h.shape[0]
    return pl.pallas_call(
        _proj_gates_kernel,
        grid=(2 * D_MODEL // tn, t // tm),
        in_specs=[pl.BlockSpec((tm, D_MODEL), lambda n, m: (m, 0)),
                  _w_rows_spec(CONV_COL0 + 3 * D_MODEL, tn),
                  pl.BlockSpec((1, tn), lambda n, m: (0, n))],
        out_specs=pl.BlockSpec((tm, tn), lambda n, m: (m, n)),
        out_shape=jax.ShapeDtypeStruct((t, 2 * D_MODEL), BF16),
        scratch_shapes=[pltpu.VMEM((tn, D_MODEL), BF16)],
        compiler_params=_cp(("arbitrary", "arbitrary"), 48),
        name="proj_gates",
    )(h, w_in_t, b_merge)


def _gla_kernel(q_ref, k_ref, v_ref, g_ref, al_ref, wup_ref, bup_ref, nw_ref, o_ref,
                st_ref, qd_ref, oi_ref, u_ref, dec_ref, *, n_chunks):
    c_len = GLA_CHUNK

    @pl.when(pl.program_id(1) == 0)
    def _():
        st_ref[...] = jnp.zeros_like(st_ref)

    row = lax.broadcasted_iota(I32, (c_len, c_len), 0)
    col = lax.broadcasted_iota(I32, (c_len, c_len), 1)
    causal = col <= row
    tril = jnp.where(causal, 1.0, 0.0).astype(BF16)

    z = _dot(al_ref[...].astype(BF16), wup_ref[0]) + bup_ref[0]
    la = (jnp.minimum(z, 0.0) - jnp.log1p(jnp.exp(-jnp.abs(z)))) * (1.0 / GLA_GATE_NORM)
    la_hi, la_lo = _split_bf16(la)
    for c in range(n_chunks):
        r0 = c * c_len
        sl = pl.ds(r0, c_len)
        b = _dot(tril, la_hi[r0:r0 + c_len]) + _dot(tril, la_lo[r0:r0 + c_len])
        b_last = b[c_len - 1:c_len, :]
        q = q_ref[sl, :].astype(F32)
        k = k_ref[sl, :].astype(F32)
        v = v_ref[sl, :]
        qd = (q * (GLA_DK ** -0.5) * jnp.exp(b)).astype(BF16)
        ki = (k * jnp.exp(-b)).astype(BF16)
        kd = (k * jnp.exp(b_last - b)).astype(BF16)
        att = jnp.where(causal, _dot_nt(qd, ki), 0.0).astype(BF16)
        qd_ref[sl, :] = qd
        oi_ref[sl, :] = _dot(att, v)
        u_ref[c] = _dot_tn(v, kd)
        dec_ref[pl.ds(c * SUBLANES, SUBLANES), :] = jnp.broadcast_to(jnp.exp(b_last), (SUBLANES, GLA_DK))

    nw = nw_ref[0]
    for c in range(n_chunks):
        sl = pl.ds(c * c_len, c_len)
        st = st_ref[...]
        o = oi_ref[sl, :] + _dot_nt(qd_ref[sl, :], st.astype(BF16))
        st_ref[...] = st * dec_ref[pl.ds(c * SUBLANES, 1), :] + u_ref[c]
        o = o * lax.rsqrt(jnp.mean(o * o, axis=-1, keepdims=True) + EPS) * nw
        g = g_ref[sl, :].astype(F32)
        o_ref[sl, :] = (o * (g * jax.nn.sigmoid(g))).astype(o_ref.dtype)


def _gla(qkvg, a_low, w_up, b_up, norm_w, tb=512):
    t = qkvg.shape[0]
    kq = GLA_KEY // GLA_DK
    kv = 2 * GLA_KEY // GLA_DV
    kg = kv + GLA_VAL // GLA_DV
    kern = functools.partial(_gla_kernel, n_chunks=tb // GLA_CHUNK)
    return pl.pallas_call(
        kern,
        grid=(GLA_HEADS, t // tb),
        in_specs=[pl.BlockSpec((tb, GLA_DK), lambda h, i: (i, h)),
                  pl.BlockSpec((tb, GLA_DK), lambda h, i: (i, kq + h)),
                  pl.BlockSpec((tb, GLA_DV), lambda h, i: (i, kv + h)),
                  pl.BlockSpec((tb, GLA_DV), lambda h, i: (i, kg + h)),
                  pl.BlockSpec((tb, LANES), lambda h, i: (i, 0)),
                  pl.BlockSpec((1, LANES, GLA_DK), lambda h, i: (h, 0, 0)),
                  pl.BlockSpec((1, 1, GLA_DK), lambda h, i: (h, 0, 0)),
                  pl.BlockSpec((1, 1, GLA_DV), lambda h, i: (h, 0, 0))],
        out_specs=pl.BlockSpec((tb, GLA_DV), lambda h, i: (i, h)),
        out_shape=jax.ShapeDtypeStruct((t, GLA_VAL), BF16),
        scratch_shapes=[pltpu.VMEM((GLA_DV, GLA_DK), F32),
                        pltpu.VMEM((tb, GLA_DK), BF16),
                        pltpu.VMEM((tb, GLA_DV), F32),
                        pltpu.VMEM((tb // GLA_CHUNK, GLA_DV, GLA_DK), F32),
                        pltpu.VMEM((tb // GLA_CHUNK * SUBLANES, GLA_DK), F32)],
        compiler_params=_cp(("arbitrary", "arbitrary"), 32),
        name="gla",
    )(qkvg, qkvg, qkvg, qkvg, a_low, w_up, b_up, norm_w)


def _merge_kernel(oa_ref, ob_ref, wa_ref, wb_ref, g0_ref, g1_ref, o_ref, wabf_ref, wbbf_ref):
    @pl.when(pl.program_id(1) == 0)
    def _():
        wabf_ref[...] = wa_ref[...].astype(BF16)
        wbbf_ref[...] = wb_ref[...].astype(BF16)

    a = _dot(oa_ref[...], wabf_ref[...])
    b = _dot(ob_ref[...], wbbf_ref[...])
    o_ref[...] = (g0_ref[...].astype(F32) * a + g1_ref[...].astype(F32) * b).astype(o_ref.dtype)


def _merge(oa, ob, w_a, w_b, gates, tm=1024, tn=512):
    t = oa.shape[0]
    nb = D_MODEL // tn
    return pl.pallas_call(
        _merge_kernel,
        grid=(nb, t // tm),
        in_specs=[pl.BlockSpec((tm, D_MODEL), lambda n, m: (m, 0)),
                  pl.BlockSpec((tm, D_MODEL), lambda n, m: (m, 0)),
                  pl.BlockSpec((D_MODEL, tn), lambda n, m: (0, n)),
                  pl.BlockSpec((D_MODEL, tn), lambda n, m: (0, n)),
                  pl.BlockSpec((tm, tn), lambda n, m: (m, n)),
                  pl.BlockSpec((tm, tn), lambda n, m: (m, nb + n))],
        out_specs=pl.BlockSpec((tm, tn), lambda n, m: (m, n)),
        out_shape=jax.ShapeDtypeStruct((t, D_MODEL), BF16),
        scratch_shapes=[pltpu.VMEM((D_MODEL, tn), BF16), pltpu.VMEM((D_MODEL, tn), BF16)],
        compiler_params=_cp(("arbitrary", "arbitrary"), 48),
        name="merge",
    )(oa, ob, w_a, w_b, gates, gates)


def _mix_kernel(a_ref, w_ref, x_ref, o_ref, wbf_ref):
    @pl.when(pl.program_id(1) == 0)
    def _():
        wbf_ref[...] = w_ref[...].astype(BF16)

    o_ref[...] = x_ref[...] + _dot(a_ref[...], wbf_ref[...])


def _mix(mixed, w, x, tm=1024, tn=1024):
    t = x.shape[0]
    return pl.pallas_call(
        _mix_kernel,
        grid=(D_MODEL // tn, t // tm),
        in_specs=[pl.BlockSpec((tm, D_MODEL), lambda n, m: (m, 0)),
                  pl.BlockSpec((D_MODEL, tn), lambda n, m: (0, n)),
                  pl.BlockSpec((tm, tn), lambda n, m: (m, n))],
        out_specs=pl.BlockSpec((tm, tn), lambda n, m: (m, n)),
        out_shape=jax.ShapeDtypeStruct((t, D_MODEL), F32),
        scratch_shapes=[pltpu.VMEM((D_MODEL, tn), BF16)],
        compiler_params=_cp(("arbitrary", "arbitrary"), 52),
        name="mix",
    )(mixed, w, x)


def _route_kernel(x_ref, nw_ref, wr_ref, br_ref, hp_ref, ids_ref, wts_ref):
    h = _rms(x_ref[...], nw_ref[...])
    hp_ref[...] = _to_token_major(h.astype(BF16))
    h_hi, h_lo = _split_bf16(h)
    w_hi, w_lo = _split_bf16(wr_ref[...])
    logits = _dot_nt(w_hi, h_hi) + _dot_nt(w_hi, h_lo) + _dot_nt(w_lo, h_hi) + br_ref[...]
    tm = logits.shape[1]

    best = logits[0:1, :]
    gidx = jnp.zeros((1, tm), I32)
    for i in range(1, N_GROUPS):
        li = logits[i:i + 1, :]
        take = li > best
        best = jnp.where(take, li, best)
        gidx = jnp.where(take, i, gidx)
    gsum = jnp.zeros((1, tm), F32)
    for i in range(N_GROUPS):
        gsum = gsum + jnp.exp(logits[i:i + 1, :] - best)
    g_p = 1.0 / gsum

    sel = logits[EXPERT_ROW0:EXPERT_ROW0 + EXPERTS_PER_GROUP, :]
    for g in range(1, N_GROUPS):
        r0 = EXPERT_ROW0 + g * EXPERTS_PER_GROUP
        sel = jnp.where(gidx == g, logits[r0:r0 + EXPERTS_PER_GROUP, :], sel)
    eio = lax.broadcasted_iota(I32, sel.shape, 0)
    m1 = jnp.max(sel, axis=0, keepdims=True)
    i1 = jnp.min(jnp.where(sel == m1, eio, EXPERTS_PER_GROUP), axis=0, keepdims=True)
    rest = jnp.where(eio == i1, -jnp.inf, sel)
    m2 = jnp.max(rest, axis=0, keepdims=True)
    i2 = jnp.min(jnp.where(rest == m2, eio, EXPERTS_PER_GROUP), axis=0, keepdims=True)
    p2 = jnp.exp(m2 - m1)
    w1 = g_p / (1.0 + p2)
    w2 = g_p * p2 / (1.0 + p2)
    e1 = gidx * EXPERTS_PER_GROUP + i1
    e2 = gidx * EXPERTS_PER_GROUP + i2
    rio = lax.broadcasted_iota(I32, (SUBLANES, tm), 0)
    ids_ref[...] = jnp.where(rio == 0, e1, jnp.where(rio == 1, e2, 0))
    wts_ref[...] = jnp.where(rio == 0, w1, jnp.where(rio == 1, w2, 0.0))


def _route(x1, norm_w, wr_t, br, tm=256):
    t = x1.shape[0]
    return pl.pallas_call(
        _route_kernel,
        grid=(t // tm,),
        in_specs=[pl.BlockSpec((tm, D_MODEL), lambda i: (i, 0)),
                  pl.BlockSpec((1, D_MODEL), lambda i: (0, 0)),
                  pl.BlockSpec((ROUTER_ROWS, D_MODEL), lambda i: (0, 0)),
                  pl.BlockSpec((ROUTER_ROWS, 1), lambda i: (0, 0))],
        out_specs=[pl.BlockSpec((tm, ROW_TILES, LANES), lambda i: (i, 0, 0)),
                   pl.BlockSpec((SUBLANES, tm), lambda i: (0, i)),
                   pl.BlockSpec((SUBLANES, tm), lambda i: (0, i))],
        out_shape=[jax.ShapeDtypeStruct((t, ROW_TILES, LANES), BF16),
                   jax.ShapeDtypeStruct((SUBLANES, t), I32),
                   jax.ShapeDtypeStruct((SUBLANES, t), F32)],
        compiler_params=_cp(("arbitrary",), 32),
        name="route",
    )(x1, norm_w, wr_t, br)


def _plan_kernel(ids_ref, dest_ref, cnt_ref, base_ref):
    phase = pl.program_id(0)
    step = pl.program_id(1)
    tm = ids_ref.shape[1]
    eio = lax.broadcasted_iota(I32, (N_EXPERTS, tm), 0)
    ids = ids_ref[...]
    oh = [jnp.where(eio == ids[k:k + 1, :], 1.0, 0.0) for k in range(2)]

    @pl.when((phase == 0) & (step == 0))
    def _():
        base_ref[...] = jnp.zeros_like(base_ref)

    @pl.when(phase == 0)
    def _():
        cnt = jnp.sum(oh[0] + oh[1], axis=1, keepdims=True)
        base_ref[...] = base_ref[...] + cnt
        dest_ref[0] = jnp.zeros(dest_ref.shape[1:], I32)
        cnt_ref[...] = base_ref[...]

    @pl.when((phase == 1) & (step == 0))
    def _():
        tiles = jnp.floor((base_ref[...] + (EXPERT_TILE - 1)) * (1.0 / EXPERT_TILE))
        r = lax.broadcasted_iota(I32, (N_EXPERTS, N_EXPERTS), 0)
        c = lax.broadcasted_iota(I32, (N_EXPERTS, N_EXPERTS), 1)
        lower = jnp.where(c < r, 1.0, 0.0).astype(BF16)
        base_ref[...] = _dot(lower, tiles.astype(BF16)) * float(EXPERT_TILE)

    @pl.when(phase == 1)
    def _():
        r = lax.broadcasted_iota(I32, (tm, tm), 0)
        c = lax.broadcasted_iota(I32, (tm, tm), 1)
        upper = jnp.where(r <= c, 1.0, 0.0).astype(BF16)
        base = base_ref[...][:, 0:1]
        rows = []
        for k in range(2):
            cum = _dot(oh[k].astype(BF16), upper)
            rows.append(jnp.sum(oh[k] * (cum - 1.0 + base), axis=0, keepdims=True))
            base = base + cum[:, tm - 1:tm]
        base_ref[...] = jnp.broadcast_to(base, base_ref.shape)
        rio = lax.broadcasted_iota(I32, (SUBLANES, tm), 0)
        d0 = rows[0].astype(I32)
        d1 = rows[1].astype(I32)
        dest_ref[0] = jnp.where(rio == 0, d0, jnp.where(rio == 1, d1, 0))


def _plan(ids, tm=512):
    t = ids.shape[1]
    return pl.pallas_call(
        _plan_kernel,
        grid=(2, t // tm),
        in_specs=[pl.BlockSpec((SUBLANES, tm), lambda p, i: (0, i))],
        out_specs=[pl.BlockSpec((1, SUBLANES, tm), lambda p, i: (p, 0, i)),
                   pl.BlockSpec((N_EXPERTS, LANES), lambda p, i: (0, 0))],
        out_shape=[jax.ShapeDtypeStruct((2, SUBLANES, t), I32),
                   jax.ShapeDtypeStruct((N_EXPERTS, LANES), F32)],
        scratch_shapes=[pltpu.VMEM((N_EXPERTS, LANES), F32)],
        compiler_params=_cp(("arbitrary", "arbitrary"), 32),
        name="plan",
    )(ids)


def _dispatch_kernel(dest_ref, hp_ref, zero_ref, xs_ref, sem, *, tm, t_total):
    del zero_ref
    base = pl.program_id(0) * tm

    def row_copy(k, tok):
        return pltpu.make_async_copy(hp_ref.at[tok], xs_ref.at[dest_ref[k * t_total + tok]], sem)

    def issue(j, c):
        for u in range(DMA_UNROLL):
            row_copy(0, base + j * DMA_UNROLL + u).start()
            row_copy(1, base + j * DMA_UNROLL + u).start()
        return c

    def drain(j, c):
        for u in range(DMA_UNROLL):
            row_copy(0, base + j * DMA_UNROLL + u).wait()
            row_copy(1, base + j * DMA_UNROLL + u).wait()
        return c

    lax.fori_loop(0, tm // DMA_UNROLL, issue, 0)
    lax.fori_loop(0, tm // DMA_UNROLL, drain, 0)


def _dispatch(dest_flat, hp3, n_rows, tm=512):
    t = hp3.shape[0]
    zeros = jnp.zeros((n_rows, ROW_TILES, LANES), BF16)
    kern = functools.partial(_dispatch_kernel, tm=tm, t_total=t)
    return pl.pallas_call(
        kern,
        grid_spec=pltpu.PrefetchScalarGridSpec(
            num_scalar_prefetch=1,
            grid=(t // tm,),
            in_specs=[pl.BlockSpec(memory_space=pl.ANY), pl.BlockSpec(memory_space=pl.ANY)],
            out_specs=pl.BlockSpec(memory_space=pl.ANY),
            scratch_shapes=[pltpu.SemaphoreType.DMA(())]),
        out_shape=jax.ShapeDtypeStruct((n_rows, ROW_TILES, LANES), BF16),
        input_output_aliases={2: 0},
        compiler_params=_cp(("arbitrary",), 16, has_side_effects=True),
        name="dispatch",
    )(dest_flat, hp3, zeros)


def _experts_kernel(te_ref, na_ref, xs_ref, wg_ref, wu_ref, wd_ref, ys_ref, wgbf_ref, wubf_ref, wdbf_ref):
    i = pl.program_id(0)
    active = i < na_ref[0]
    changed = (i == 0) | (te_ref[i] != te_ref[jnp.maximum(i - 1, 0)])

    @pl.when(active & changed)
    def _():
        wgbf_ref[...] = wg_ref[0].astype(BF16)
        wubf_ref[...] = wu_ref[0].astype(BF16)
        wdbf_ref[...] = wd_ref[0].astype(BF16)

    @pl.when(active)
    def _():
        x = _from_token_major(xs_ref[...])
        hg = _dot(x, wgbf_ref[...])
        hu = _dot(x, wubf_ref[...])
        act = (hg * jax.nn.sigmoid(hg) * hu).astype(BF16)
        ys_ref[...] = _to_token_major(_dot(act, wdbf_ref[...]).astype(BF16))


def _experts(tile_e, n_act, xs, w_gate, w_up, w_down, n_tiles):
    def row_map(i, te, na):
        return (jnp.minimum(i, na[0] - 1), 0, 0)

    def w_map(i, te, na):
        return (te[i], 0, 0)

    return pl.pallas_call(
        _experts_kernel,
        grid_spec=pltpu.PrefetchScalarGridSpec(
            num_scalar_prefetch=2,
            grid=(n_tiles,),
            in_specs=[pl.BlockSpec((EXPERT_TILE, ROW_TILES, LANES), row_map),
                      pl.BlockSpec((1, D_MODEL, D_FF), w_map),
                      pl.BlockSpec((1, D_MODEL, D_FF), w_map),
                      pl.BlockSpec((1, D_FF, D_MODEL), w_map)],
            out_specs=pl.BlockSpec((EXPERT_TILE, ROW_TILES, LANES), row_map),
            scratch_shapes=[pltpu.VMEM((D_MODEL, D_FF), BF16),
                            pltpu.VMEM((D_MODEL, D_FF), BF16),
                            pltpu.VMEM((D_FF, D_MODEL), BF16)]),
        out_shape=jax.ShapeDtypeStruct(xs.shape, BF16),
        input_output_aliases={2: 0},
        compiler_params=_cp(("arbitrary",), 48),
        name="experts",
    )(tile_e, n_act, xs, w_gate, w_up, w_down)


def _combine_kernel(dest_ref, ys_ref, x_ref, wt_ref, p_ref, nw_ref, wg_ref, wp_ref, fw_ref, o_ref,
                    buf0_ref, buf1_ref, sems, *, tm, t_total):
    step = pl.program_id(0)
    bufs = (buf0_ref, buf1_ref)

    def row_copy(tile, k, i):
        slot = tile % 2
        src = ys_ref.at[dest_ref[k * t_total + tile * tm + i]]
        return pltpu.make_async_copy(src, bufs[k].at[slot, i], sems.at[slot])

    def issue(tile):
        def body(j, c):
            for u in range(DMA_UNROLL):
                row_copy(tile, 0, j * DMA_UNROLL + u).start()
                row_copy(tile, 1, j * DMA_UNROLL + u).start()
            return c
        lax.fori_loop(0, tm // DMA_UNROLL, body, 0)

    def drain(tile):
        def body(j, c):
            for u in range(DMA_UNROLL):
                row_copy(tile, 0, j * DMA_UNROLL + u).wait()
                row_copy(tile, 1, j * DMA_UNROLL + u).wait()
            return c
        lax.fori_loop(0, tm // DMA_UNROLL, body, 0)

    @pl.when(step == 0)
    def _():
        issue(step)

    @pl.when(step + 1 < pl.num_programs(0))
    def _():
        issue(step + 1)

    drain(step)

    wt = wt_ref[...]
    slot = step % 2
    y0 = _from_token_major(buf0_ref[slot]).astype(F32)
    y1 = _from_token_major(buf1_ref[slot]).astype(F32)
    x2 = x_ref[...] + wt[:, 0:1] * y0 + wt[:, 1:2] * y1
    hn = _rms(x2, nw_ref[...]).astype(BF16)
    pg = jax.nn.sigmoid(_dot(hn, wg_ref[...]))
    x3 = x2 + pg * _dot(p_ref[...].astype(BF16), wp_ref[...])
    o_ref[...] = _rms(x3, fw_ref[...])


def _combine_ple(dest_flat, ys3, x1, wts_t, p, ple_norm_w, w_gate, w_proj, final_w, tm=256):
    t = x1.shape[0]
    kern = functools.partial(_combine_kernel, tm=tm, t_total=t)
    row = lambda i, d: (i, 0)
    fix = lambda i, d: (0, 0)
    return pl.pallas_call(
        kern,
        grid_spec=pltpu.PrefetchScalarGridSpec(
            num_scalar_prefetch=1,
            grid=(t // tm,),
            in_specs=[pl.BlockSpec(memory_space=pl.ANY),
                      pl.BlockSpec((tm, D_MODEL), row),
                      pl.BlockSpec((tm, SUBLANES), row),
                      pl.BlockSpec((tm, PLE_DIM), row),
                      pl.BlockSpec((1, D_MODEL), fix),
                      pl.BlockSpec((D_MODEL, D_MODEL), fix),
                      pl.BlockSpec((PLE_DIM, D_MODEL), fix),
                      pl.BlockSpec((1, D_MODEL), fix)],
            out_specs=pl.BlockSpec((tm, D_MODEL), row),
            scratch_shapes=[pltpu.VMEM((2, tm, ROW_TILES, LANES), BF16),
                            pltpu.VMEM((2, tm, ROW_TILES, LANES), BF16),
                            pltpu.SemaphoreType.DMA((2,))]),
        out_shape=jax.ShapeDtypeStruct((t, D_MODEL), F32),
        compiler_params=_cp(("arbitrary",), 48),
        name="combine_ple",
    )(dest_flat, ys3, x1, wts_t, p, ple_norm_w, w_gate, w_proj, final_w)


def _tile_table(counts, n_tiles):
    tiles = (counts.astype(I32) + (EXPERT_TILE - 1)) // EXPERT_TILE
    ends = jnp.cumsum(tiles)
    n_act = ends[-1]
    idx = jnp.minimum(jnp.arange(n_tiles, dtype=I32), n_act - 1)
    tile_e = jnp.sum((idx[:, None] >= ends[None, :]).astype(I32), axis=1)
    return tile_e.astype(I32), n_act.reshape(1).astype(I32)


def _block(x, p, norm_mix_w, w_in, b_merge, w_alpha_up, b_alpha_up, gla_norm_w, w_gla_out,
           conv_w, conv_b, w_conv_out, w_mix_out, norm_ffn_w, w_router_group, b_router_group,
           w_router_expert, b_router_expert, w_e_gate, w_e_up, w_e_down, ple_norm_w,
           w_ple_gate, w_ple_proj, final_norm_w):
    t = x.shape[0]
    n_tiles = (2 * t) // EXPERT_TILE + N_EXPERTS
    n_rows = n_tiles * EXPERT_TILE

    w_up = jnp.pad(w_alpha_up, ((0, LANES - GLA_GATE_RANK), (0, 0))).astype(BF16)
    w_up = w_up.reshape(LANES, GLA_HEADS, GLA_DK).transpose(1, 0, 2)
    b_up = b_alpha_up.reshape(GLA_HEADS, 1, GLA_DK)
    gnw = gla_norm_w.reshape(GLA_HEADS, 1, GLA_DV)
    wr_t = jnp.zeros((ROUTER_ROWS, D_MODEL), F32)
    wr_t = wr_t.at[0:N_GROUPS].set(w_router_group.T)
    wr_t = wr_t.at[EXPERT_ROW0:EXPERT_ROW0 + N_EXPERTS].set(w_router_expert.T)
    br = jnp.zeros((ROUTER_ROWS, 1), F32)
    br = br.at[0:N_GROUPS, 0].set(b_router_group)
    br = br.at[EXPERT_ROW0:EXPERT_ROW0 + N_EXPERTS, 0].set(b_router_expert)

    w_in_t = w_in.T
    h, a_low = _norm_in(x, norm_mix_w.reshape(1, D_MODEL), w_in_t)
    qkvg = _proj_qkvg(h, w_in_t)
    ob = _proj_conv(h, w_in_t, conv_w, conv_b.reshape(1, D_MODEL))
    gates = _proj_gates(h, w_in_t, b_merge.reshape(1, 2 * D_MODEL))
    oa = _gla(qkvg, a_low, w_up, b_up, gnw)
    mixed = _merge(oa, ob, w_gla_out, w_conv_out, gates)
    x1 = _mix(mixed, w_mix_out, x)

    hp, ids, wts = _route(x1, norm_ffn_w.reshape(1, D_MODEL), wr_t, br)
    dest, counts = _plan(ids)
    dest_flat = dest[1, 0:2].reshape(2 * t)
    tile_e, n_act = _tile_table(counts[:, 0], n_tiles)
    xs = _dispatch(dest_flat, hp, n_rows)
    ys = _experts(tile_e, n_act, xs, w_e_gate, w_e_up, w_e_down, n_tiles)
    return _combine_ple(dest_flat, ys, x1, wts.T, p,
                        ple_norm_w.reshape(1, D_MODEL), w_ple_gate.astype(BF16),
                        w_ple_proj.astype(BF16), final_norm_w.reshape(1, D_MODEL))


def kernel(x, p, norm_mix_w, w_in, b_merge, w_alpha_up, b_alpha_up, gla_norm_w, w_gla_out, conv_w, conv_b, w_conv_out, w_mix_out, norm_ffn_w, w_router_group, b_router_group, w_router_expert, b_router_expert, w_e_gate, w_e_up, w_e_down, ple_norm_w, w_ple_gate, w_ple_proj, final_norm_w):
    depth, batch = p.shape[0], x.shape[0]
    assert depth == 1 and batch == 1, "kernel is specialised to one layer and one sequence"
    out = _block(x[0], p[0, 0], norm_mix_w[0], w_in[0], b_merge[0], w_alpha_up[0], b_alpha_up[0],
                 gla_norm_w[0], w_gla_out[0], conv_w[0], conv_b[0], w_conv_out[0], w_mix_out[0],
                 norm_ffn_w[0], w_router_group[0], b_router_group[0], w_router_expert[0],
                 b_router_expert[0], w_e_gate[0], w_e_up[0], w_e_down[0], ple_norm_w[0],
                 w_ple_gate[0], w_ple_proj[0], final_norm_w)
    return out[None]
```

```python
import functools

import jax
import jax.numpy as jnp
from jax import lax
from jax.experimental import pallas as pl
from jax.experimental.pallas import tpu as pltpu

F32 = jnp.float32
BF16 = jnp.bfloat16
I32 = jnp.int32

D_MODEL = 2048
PLE_DIM = 256
EPS = 1e-6
LOG2_E = 1.4426950408889634
GLA_HEADS = 4
GLA_DK = 256
GLA_DV = 512
GLA_KEY = GLA_HEADS * GLA_DK
GLA_VAL = GLA_HEADS * GLA_DV
GLA_GATE_RANK = 16
GLA_GATE_NORM = 16.0
GLA_CHUNK = 64
CONV_K = 3
N_GROUPS = 4
EXPERTS_PER_GROUP = 8
N_EXPERTS = N_GROUPS * EXPERTS_PER_GROUP
D_FF = 512

QKVG_COLS = 2 * GLA_KEY + 2 * GLA_VAL
CONV_COL0 = QKVG_COLS + GLA_GATE_RANK

LANES = 128
SUBLANES = 8
ROW_TILES = D_MODEL // LANES

EXPERT_TILE = 256
DMA_UNROLL = 8
ROUTER_ROWS = 64
EXPERT_ROW0 = 8

MIB = 1024 * 1024


def _cp(sem, vmem_mib, **kw):
    return pltpu.CompilerParams(dimension_semantics=sem, vmem_limit_bytes=int(vmem_mib * MIB), **kw)


def _rms(x, w):
    return x * lax.rsqrt(jnp.mean(x * x, axis=-1, keepdims=True) + EPS) * w


def _dot(a, b):
    return jnp.dot(a, b, preferred_element_type=F32)


def _dot_nt(a, b):
    return lax.dot_general(a, b, (((1,), (1,)), ((), ())), preferred_element_type=F32)


def _dot_tn(a, b):
    return lax.dot_general(a, b, (((0,), (0,)), ((), ())), preferred_element_type=F32)


def _split_bf16(x):
    hi = x.astype(BF16)
    lo = (x - hi.astype(F32)).astype(BF16)
    return hi, lo


def _to_token_major(val):
    return val.reshape(val.shape[0], ROW_TILES, LANES)


def _from_token_major(val):
    return val.reshape(val.shape[0], D_MODEL)


def _norm_in_kernel(x_ref, w_ref, wal_ref, h_ref, al_ref, walbf_ref):
    @pl.when(pl.program_id(0) == 0)
    def _():
        walbf_ref[...] = wal_ref[...].astype(BF16)

    h = _rms(x_ref[...], w_ref[...]).astype(BF16)
    h_ref[...] = h
    al_ref[...] = _dot_nt(h, walbf_ref[...])


def _norm_in(x, w, w_in_t, tm=512):
    t = x.shape[0]
    return pl.pallas_call(
        _norm_in_kernel,
        grid=(t // tm,),
        in_specs=[pl.BlockSpec((tm, D_MODEL), lambda i: (i, 0)),
                  pl.BlockSpec((1, D_MODEL), lambda i: (0, 0)),
                  pl.BlockSpec((LANES, D_MODEL), lambda i: (QKVG_COLS // LANES, 0))],
        out_specs=[pl.BlockSpec((tm, D_MODEL), lambda i: (i, 0)),
                   pl.BlockSpec((tm, LANES), lambda i: (i, 0))],
        out_shape=[jax.ShapeDtypeStruct((t, D_MODEL), BF16),
                   jax.ShapeDtypeStruct((t, LANES), F32)],
        scratch_shapes=[pltpu.VMEM((LANES, D_MODEL), BF16)],
        compiler_params=_cp(("arbitrary",), 32),
        name="norm_in",
    )(x, w, w_in_t)


def _w_rows_spec(row0, tn):
    assert row0 % SUBLANES == 0 and tn % SUBLANES == 0
    return pl.BlockSpec((pl.Element(tn), pl.Element(D_MODEL)),
                        lambda n, m: (pl.multiple_of(row0 + n * tn, SUBLANES), 0))


def _proj_qkvg_kernel(h_ref, w_ref, o_ref, wbf_ref):
    @pl.when(pl.program_id(1) == 0)
    def _():
        wbf_ref[...] = w_ref[...].astype(BF16)

    o_ref[...] = _dot_nt(h_ref[...], wbf_ref[...]).astype(o_ref.dtype)


def _proj_qkvg(h, w_in_t, tm=1024, tn=1024):
    t = h.shape[0]
    return pl.pallas_call(
        _proj_qkvg_kernel,
        grid=(QKVG_COLS // tn, t // tm),
        in_specs=[pl.BlockSpec((tm, D_MODEL), lambda n, m: (m, 0)),
                  pl.BlockSpec((tn, D_MODEL), lambda n, m: (n, 0))],
        out_specs=pl.BlockSpec((tm, tn), lambda n, m: (m, n)),
        out_shape=jax.ShapeDtypeStruct((t, QKVG_COLS), BF16),
        scratch_shapes=[pltpu.VMEM((tn, D_MODEL), BF16)],
        compiler_params=_cp(("arbitrary", "arbitrary"), 48),
        name="proj_qkvg",
    )(h, w_in_t)


def _proj_conv_kernel(h_ref, wb_ref, wc_ref, wx_ref, cw_ref, cb_ref, o_ref,
                      wbbf_ref, wcbf_ref, wxbf_ref, prev_ref):
    m = pl.program_id(1)

    @pl.when(m == 0)
    def _():
        prev_ref[...] = jnp.zeros_like(prev_ref)
        wbbf_ref[...] = wb_ref[...].astype(BF16)
        wcbf_ref[...] = wc_ref[...].astype(BF16)
        wxbf_ref[...] = wx_ref[...].astype(BF16)

    h = h_ref[...]
    b = _dot_nt(h, wbbf_ref[...])
    s = _dot_nt(h, wcbf_ref[...]) * _dot_nt(h, wxbf_ref[...])
    tm = s.shape[0]
    row = lax.broadcasted_iota(I32, s.shape, 0)
    prev = prev_ref[...]
    p1 = prev[SUBLANES - 1:SUBLANES, :]
    p2 = prev[SUBLANES - 2:SUBLANES - 1, :]
    s1 = jnp.where(row == 0, p1, pltpu.roll(s, 1, 0))
    s2 = jnp.where(row == 0, p2, jnp.where(row == 1, p1, pltpu.roll(s, 2, 0)))
    cw = cw_ref[...]
    u = cw[2:3, :] * s + cw[1:2, :] * s1 + cw[0:1, :] * s2 + cb_ref[...]
    o_ref[...] = (b * u).astype(o_ref.dtype)
    prev_ref[...] = s[tm - SUBLANES:, :]


def _proj_conv(h, w_in_t, conv_w, conv_b, tm=1024, tn=512):
    t = h.shape[0]
    return pl.pallas_call(
        _proj_conv_kernel,
        grid=(D_MODEL // tn, t // tm),
        in_specs=[pl.BlockSpec((tm, D_MODEL), lambda n, m: (m, 0))] +
                 [_w_rows_spec(CONV_COL0 + seg * D_MODEL, tn) for seg in range(3)] +
                 [pl.BlockSpec((CONV_K, tn), lambda n, m: (0, n)),
                  pl.BlockSpec((1, tn), lambda n, m: (0, n))],
        out_specs=pl.BlockSpec((tm, tn), lambda n, m: (m, n)),
        out_shape=jax.ShapeDtypeStruct((t, D_MODEL), BF16),
        scratch_shapes=[pltpu.VMEM((tn, D_MODEL), BF16), pltpu.VMEM((tn, D_MODEL), BF16),
                        pltpu.VMEM((tn, D_MODEL), BF16), pltpu.VMEM((SUBLANES, tn), F32)],
        compiler_params=_cp(("arbitrary", "arbitrary"), 48),
        name="proj_conv",
    )(h, w_in_t, w_in_t, w_in_t, conv_w, conv_b)


def _proj_gates_kernel(h_ref, w_ref, b_ref, o_ref, wbf_ref):
    @pl.when(pl.program_id(1) == 0)
    def _():
        wbf_ref[...] = w_ref[...].astype(BF16)

    o_ref[...] = jax.nn.sigmoid(_dot_nt(h_ref[...], wbf_ref[...]) + b_ref[...]).astype(o_ref.dtype)


def _proj_gates(h, w_in_t, b_merge, tm=1024, tn=1024):
    t = h.shape[0]
    return pl.pallas_call(
        _proj_gates_kernel,
        grid=(2 * D_MODEL // tn, t // tm),
        in_specs=[pl.BlockSpec((tm, D_MODEL), lambda n, m: (m, 0)),
                  _w_rows_spec(CONV_COL0 + 3 * D_MODEL, tn),
                  pl.BlockSpec((1, tn), lambda n, m: (0, n))],
        out_specs=pl.BlockSpec((tm, tn), lambda n, m: (m, n)),
        out_shape=jax.ShapeDtypeStruct((t, 2 * D_MODEL), BF16),
        scratch_shapes=[pltpu.VMEM((tn, D_MODEL), BF16)],
        compiler_params=_cp(("arbitrary", "arbitrary"), 48),
        name="proj_gates",
    )(h, w_in_t, b_merge)


def _gla_kernel(q_ref, k_ref, v_ref, g_ref, al_ref, wup_ref, bup_ref, nw_ref, o_ref,
                st_ref, b_ref, bl_ref, qd_ref, ki_ref, kd_ref, oi_ref, u_ref, *, n_chunks):
    c_len = GLA_CHUNK

    @pl.when(pl.program_id(1) == 0)
    def _():
        st_ref[...] = jnp.zeros_like(st_ref)

    row = lax.broadcasted_iota(I32, (c_len, c_len), 0)
    col = lax.broadcasted_iota(I32, (c_len, c_len), 1)
    causal = col <= row
    tril = jnp.where(causal, 1.0, 0.0).astype(BF16)

    z = _dot(al_ref[...].astype(BF16), wup_ref[0]) + bup_ref[0]
    la = (jnp.minimum(z, 0.0) - jnp.log1p(jnp.exp(-jnp.abs(z)))) * (LOG2_E / GLA_GATE_NORM)
    la_hi, la_lo = _split_bf16(la)
    for c in range(n_chunks):
        r0 = c * c_len
        b = _dot(tril, la_hi[r0:r0 + c_len]) + _dot(tril, la_lo[r0:r0 + c_len])
        b_ref[pl.ds(r0, c_len), :] = b
        bl_ref[pl.ds(r0, c_len), :] = jnp.broadcast_to(b[c_len - 1:c_len, :], b.shape)

    b = b_ref[...]
    b_last = bl_ref[...]
    q = q_ref[...].astype(F32)
    k = k_ref[...].astype(F32)
    qd_ref[...] = (q * jnp.exp2(b)).astype(BF16)
    ki_ref[...] = (k * jnp.exp2(-b)).astype(BF16)
    kd_ref[...] = (k * jnp.exp2(b_last - b)).astype(BF16)

    for c in range(n_chunks):
        sl = pl.ds(c * c_len, c_len)
        v = v_ref[sl, :]
        att = jnp.where(causal, _dot_nt(qd_ref[sl, :], ki_ref[sl, :]), 0.0).astype(BF16)
        oi_ref[sl, :] = _dot(att, v)
        u_ref[c] = _dot_tn(v, kd_ref[sl, :])

    for c in range(n_chunks):
        sl = pl.ds(c * c_len, c_len)
        st = st_ref[...]
        oi_ref[sl, :] = oi_ref[sl, :] + _dot_nt(qd_ref[sl, :], st.astype(BF16))
        st_ref[...] = st * jnp.exp2(bl_ref[pl.ds(c * c_len, 1), :]) + u_ref[c]

    o = oi_ref[...]
    o = o * lax.rsqrt(jnp.mean(o * o, axis=-1, keepdims=True) + EPS * GLA_DK) * nw_ref[0]
    g = g_ref[...].astype(F32)
    o_ref[...] = (o * (g * jax.nn.sigmoid(g))).astype(o_ref.dtype)


def _gla(qkvg, a_low, w_up, b_up, norm_w, tb=512):
    t = qkvg.shape[0]
    kq = GLA_KEY // GLA_DK
    kv = 2 * GLA_KEY // GLA_DV
    kg = kv + GLA_VAL // GLA_DV
    kern = functools.partial(_gla_kernel, n_chunks=tb // GLA_CHUNK)
    return pl.pallas_call(
        kern,
        grid=(GLA_HEADS, t // tb),
        in_specs=[pl.BlockSpec((tb, GLA_DK), lambda h, i: (i, h)),
                  pl.BlockSpec((tb, GLA_DK), lambda h, i: (i, kq + h)),
                  pl.BlockSpec((tb, GLA_DV), lambda h, i: (i, kv + h)),
                  pl.BlockSpec((tb, GLA_DV), lambda h, i: (i, kg + h)),
                  pl.BlockSpec((tb, LANES), lambda h, i: (i, 0)),
                  pl.BlockSpec((1, LANES, GLA_DK), lambda h, i: (h, 0, 0)),
                  pl.BlockSpec((1, 1, GLA_DK), lambda h, i: (h, 0, 0)),
                  pl.BlockSpec((1, 1, GLA_DV), lambda h, i: (h, 0, 0))],
        out_specs=pl.BlockSpec((tb, GLA_DV), lambda h, i: (i, h)),
        out_shape=jax.ShapeDtypeStruct((t, GLA_VAL), BF16),
        scratch_shapes=[pltpu.VMEM((GLA_DV, GLA_DK), F32),
                        pltpu.VMEM((tb, GLA_DK), F32), pltpu.VMEM((tb, GLA_DK), F32),
                        pltpu.VMEM((tb, GLA_DK), BF16), pltpu.VMEM((tb, GLA_DK), BF16),
                        pltpu.VMEM((tb, GLA_DK), BF16),
                        pltpu.VMEM((tb, GLA_DV), F32),
                        pltpu.VMEM((tb // GLA_CHUNK, GLA_DV, GLA_DK), F32)],
        compiler_params=_cp(("arbitrary", "arbitrary"), 32),
        name="gla",
    )(qkvg, qkvg, qkvg, qkvg, a_low, w_up, b_up, norm_w)


def _merge_kernel(oa_ref, ob_ref, wa_ref, wb_ref, g0_ref, g1_ref, o_ref, wabf_ref, wbbf_ref):
    @pl.when(pl.program_id(1) == 0)
    def _():
        wabf_ref[...] = wa_ref[...].astype(BF16)
        wbbf_ref[...] = wb_ref[...].astype(BF16)

    a = _dot(oa_ref[...], wabf_ref[...])
    b = _dot(ob_ref[...], wbbf_ref[...])
    o_ref[...] = (g0_ref[...].astype(F32) * a + g1_ref[...].astype(F32) * b).astype(o_ref.dtype)


def _merge(oa, ob, w_a, w_b, gates, tm=1024, tn=512):
    t = oa.shape[0]
    nb = D_MODEL // tn
    return pl.pallas_call(
        _merge_kernel,
        grid=(nb, t // tm),
        in_specs=[pl.BlockSpec((tm, D_MODEL), lambda n, m: (m, 0)),
                  pl.BlockSpec((tm, D_MODEL), lambda n, m: (m, 0)),
                  pl.BlockSpec((D_MODEL, tn), lambda n, m: (0, n)),
                  pl.BlockSpec((D_MODEL, tn), lambda n, m: (0, n)),
                  pl.BlockSpec((tm, tn), lambda n, m: (m, n)),
                  pl.BlockSpec((tm, tn), lambda n, m: (m, nb + n))],
        out_specs=pl.BlockSpec((tm, tn), lambda n, m: (m, n)),
        out_shape=jax.ShapeDtypeStruct((t, D_MODEL), BF16),
        scratch_shapes=[pltpu.VMEM((D_MODEL, tn), BF16), pltpu.VMEM((D_MODEL, tn), BF16)],
        compiler_params=_cp(("arbitrary", "arbitrary"), 48),
        name="merge",
    )(oa, ob, w_a, w_b, gates, gates)


def _mix_kernel(a_ref, w_ref, x_ref, o_ref, wbf_ref):
    @pl.when(pl.program_id(1) == 0)
    def _():
        wbf_ref[...] = w_ref[...].astype(BF16)

    o_ref[...] = x_ref[...] + _dot(a_ref[...], wbf_ref[...])


def _mix(mixed, w, x, tm=1024, tn=1024):
    t = x.shape[0]
    return pl.pallas_call(
        _mix_kernel,
        grid=(D_MODEL // tn, t // tm),
        in_specs=[pl.BlockSpec((tm, D_MODEL), lambda n, m: (m, 0)),
                  pl.BlockSpec((D_MODEL, tn), lambda n, m: (0, n)),
                  pl.BlockSpec((tm, tn), lambda n, m: (m, n))],
        out_specs=pl.BlockSpec((tm, tn), lambda n, m: (m, n)),
        out_shape=jax.ShapeDtypeStruct((t, D_MODEL), F32),
        scratch_shapes=[pltpu.VMEM((D_MODEL, tn), BF16)],
        compiler_params=_cp(("arbitrary", "arbitrary"), 52),
        name="mix",
    )(mixed, w, x)


def _route_kernel(x_ref, nw_ref, wr_ref, br_ref, hp_ref, ids_ref, wts_ref):
    h = _rms(x_ref[...], nw_ref[...])
    hp_ref[...] = _to_token_major(h.astype(BF16))
    h_hi, h_lo = _split_bf16(h)
    w_hi, w_lo = _split_bf16(wr_ref[...])
    logits = _dot_nt(w_hi, h_hi) + _dot_nt(w_hi, h_lo) + _dot_nt(w_lo, h_hi) + br_ref[...]
    tm = logits.shape[1]

    best = logits[0:1, :]
    gidx = jnp.zeros((1, tm), I32)
    for i in range(1, N_GROUPS):
        li = logits[i:i + 1, :]
        take = li > best
        best = jnp.where(take, li, best)
        gidx = jnp.where(take, i, gidx)
    gsum = jnp.zeros((1, tm), F32)
    for i in range(N_GROUPS):
        gsum = gsum + jnp.exp(logits[i:i + 1, :] - best)
    g_p = 1.0 / gsum

    sel = logits[EXPERT_ROW0:EXPERT_ROW0 + EXPERTS_PER_GROUP, :]
    for g in range(1, N_GROUPS):
        r0 = EXPERT_ROW0 + g * EXPERTS_PER_GROUP
        sel = jnp.where(gidx == g, logits[r0:r0 + EXPERTS_PER_GROUP, :], sel)
    eio = lax.broadcasted_iota(I32, sel.shape, 0)
    m1 = jnp.max(sel, axis=0, keepdims=True)
    i1 = jnp.min(jnp.where(sel == m1, eio, EXPERTS_PER_GROUP), axis=0, keepdims=True)
    rest = jnp.where(eio == i1, -jnp.inf, sel)
    m2 = jnp.max(rest, axis=0, keepdims=True)
    i2 = jnp.min(jnp.where(rest == m2, eio, EXPERTS_PER_GROUP), axis=0, keepdims=True)
    p2 = jnp.exp(m2 - m1)
    w1 = g_p / (1.0 + p2)
    w2 = g_p * p2 / (1.0 + p2)
    e1 = gidx * EXPERTS_PER_GROUP + i1
    e2 = gidx * EXPERTS_PER_GROUP + i2
    rio = lax.broadcasted_iota(I32, (SUBLANES, tm), 0)
    ids_ref[...] = jnp.where(rio == 0, e1, jnp.where(rio == 1, e2, 0))
    wts_ref[...] = jnp.where(rio == 0, w1, jnp.where(rio == 1, w2, 0.0))


def _route(x1, norm_w, wr_t, br, tm=256):
    t = x1.shape[0]
    return pl.pallas_call(
        _route_kernel,
        grid=(t // tm,),
        in_specs=[pl.BlockSpec((tm, D_MODEL), lambda i: (i, 0)),
                  pl.BlockSpec((1, D_MODEL), lambda i: (0, 0)),
                  pl.BlockSpec((ROUTER_ROWS, D_MODEL), lambda i: (0, 0)),
                  pl.BlockSpec((ROUTER_ROWS, 1), lambda i: (0, 0))],
        out_specs=[pl.BlockSpec((tm, ROW_TILES, LANES), lambda i: (i, 0, 0)),
                   pl.BlockSpec((SUBLANES, tm), lambda i: (0, i)),
                   pl.BlockSpec((SUBLANES, tm), lambda i: (0, i))],
        out_shape=[jax.ShapeDtypeStruct((t, ROW_TILES, LANES), BF16),
                   jax.ShapeDtypeStruct((SUBLANES, t), I32),
                   jax.ShapeDtypeStruct((SUBLANES, t), F32)],
        compiler_params=_cp(("arbitrary",), 32),
        name="route",
    )(x1, norm_w, wr_t, br)


def _plan_kernel(ids_ref, dest_ref, cnt_ref, base_ref):
    phase = pl.program_id(0)
    step = pl.program_id(1)
    tm = ids_ref.shape[1]
    eio = lax.broadcasted_iota(I32, (N_EXPERTS, tm), 0)
    ids = ids_ref[...]
    oh = [jnp.where(eio == ids[k:k + 1, :], 1.0, 0.0) for k in range(2)]

    @pl.when((phase == 0) & (step == 0))
    def _():
        base_ref[...] = jnp.zeros_like(base_ref)

    @pl.when(phase == 0)
    def _():
        cnt = jnp.sum(oh[0] + oh[1], axis=1, keepdims=True)
        base_ref[...] = base_ref[...] + cnt
        dest_ref[0] = jnp.zeros(dest_ref.shape[1:], I32)
        cnt_ref[...] = base_ref[...]

    @pl.when((phase == 1) & (step == 0))
    def _():
        tiles = jnp.floor((base_ref[...] + (EXPERT_TILE - 1)) * (1.0 / EXPERT_TILE))
        r = lax.broadcasted_iota(I32, (N_EXPERTS, N_EXPERTS), 0)
        c = lax.broadcasted_iota(I32, (N_EXPERTS, N_EXPERTS), 1)
        lower = jnp.where(c < r, 1.0, 0.0).astype(BF16)
        base_ref[...] = _dot(lower, tiles.astype(BF16)) * float(EXPERT_TILE)

    @pl.when(phase == 1)
    def _():
        r = lax.broadcasted_iota(I32, (tm, tm), 0)
        c = lax.broadcasted_iota(I32, (tm, tm), 1)
        upper = jnp.where(r <= c, 1.0, 0.0).astype(BF16)
        base = base_ref[...][:, 0:1]
        rows = []
        for k in range(2):
            cum = _dot(oh[k].astype(BF16), upper)
            rows.append(jnp.sum(oh[k] * (cum - 1.0 + base), axis=0, keepdims=True))
            base = base + cum[:, tm - 1:tm]
        base_ref[...] = jnp.broadcast_to(base, base_ref.shape)
        rio = lax.broadcasted_iota(I32, (SUBLANES, tm), 0)
        d0 = rows[0].astype(I32)
        d1 = rows[1].astype(I32)
        dest_ref[0] = jnp.where(rio == 0, d0, jnp.where(rio == 1, d1, 0))


def _plan(ids, tm=512):
    t = ids.shape[1]
    return pl.pallas_call(
        _plan_kernel,
        grid=(2, t // tm),
        in_specs=[pl.BlockSpec((SUBLANES, tm), lambda p, i: (0, i))],
        out_specs=[pl.BlockSpec((1, SUBLANES, tm), lambda p, i: (p, 0, i)),
                   pl.BlockSpec((N_EXPERTS, LANES), lambda p, i: (0, 0))],
        out_shape=[jax.ShapeDtypeStruct((2, SUBLANES, t), I32),
                   jax.ShapeDtypeStruct((N_EXPERTS, LANES), F32)],
        scratch_shapes=[pltpu.VMEM((N_EXPERTS, LANES), F32)],
        compiler_params=_cp(("arbitrary", "arbitrary"), 32),
        name="plan",
    )(ids)


def _dispatch_kernel(dest_ref, hp_ref, zero_ref, xs_ref, sem, *, tm, t_total):
    del zero_ref
    base = pl.program_id(0) * tm

    def row_copy(k, tok):
        return pltpu.make_async_copy(hp_ref.at[tok], xs_ref.at[dest_ref[k * t_total + tok]], sem)

    def issue(j, c):
        for u in range(DMA_UNROLL):
            row_copy(0, base + j * DMA_UNROLL + u).start()
            row_copy(1, base + j * DMA_UNROLL + u).start()
        return c

    def drain(j, c):
        for u in range(DMA_UNROLL):
            row_copy(0, base + j * DMA_UNROLL + u).wait()
            row_copy(1, base + j * DMA_UNROLL + u).wait()
        return c

    lax.fori_loop(0, tm // DMA_UNROLL, issue, 0)
    lax.fori_loop(0, tm // DMA_UNROLL, drain, 0)


def _dispatch(dest_flat, hp3, n_rows, tm=512):
    t = hp3.shape[0]
    zeros = jnp.zeros((n_rows, ROW_TILES, LANES), BF16)
    kern = functools.partial(_dispatch_kernel, tm=tm, t_total=t)
    return pl.pallas_call(
        kern,
        grid_spec=pltpu.PrefetchScalarGridSpec(
            num_scalar_prefetch=1,
            grid=(t // tm,),
            in_specs=[pl.BlockSpec(memory_space=pl.ANY), pl.BlockSpec(memory_space=pl.ANY)],
            out_specs=pl.BlockSpec(memory_space=pl.ANY),
            scratch_shapes=[pltpu.SemaphoreType.DMA(())]),
        out_shape=jax.ShapeDtypeStruct((n_rows, ROW_TILES, LANES), BF16),
        input_output_aliases={2: 0},
        compiler_params=_cp(("arbitrary",), 16, has_side_effects=True),
        name="dispatch",
    )(dest_flat, hp3, zeros)


def _experts_kernel(te_ref, nx_ref, sl_ref, na_ref, xs_ref, wg_hbm, wu_hbm, wd_hbm, ys_ref,
                    wgf_ref, wuf_ref, wdf_ref, wgbf_ref, wubf_ref, wdbf_ref, sems):
    i = pl.program_id(0)
    active = i < na_ref[0]
    changed = (i == 0) | (te_ref[i] != te_ref[jnp.maximum(i - 1, 0)])

    def weight_copies(e, slot):
        return (pltpu.make_async_copy(wg_hbm.at[e], wgf_ref.at[slot], sems.at[slot, 0]),
                pltpu.make_async_copy(wu_hbm.at[e], wuf_ref.at[slot], sems.at[slot, 1]),
                pltpu.make_async_copy(wd_hbm.at[e], wdf_ref.at[slot], sems.at[slot, 2]))

    @pl.when(i == 0)
    def _():
        for cp in weight_copies(te_ref[0], 0):
            cp.start()

    @pl.when(active & changed)
    def _():
        slot = sl_ref[i]
        nxt = nx_ref[i]

        @pl.when(nxt >= 0)
        def _():
            for cp in weight_copies(nxt, 1 - slot):
                cp.start()

        for cp in weight_copies(te_ref[i], slot):
            cp.wait()
        wgbf_ref[...] = wgf_ref[slot].astype(BF16)
        wubf_ref[...] = wuf_ref[slot].astype(BF16)
        wdbf_ref[...] = wdf_ref[slot].astype(BF16)

    @pl.when(active)
    def _():
        x = _from_token_major(xs_ref[...])
        hg = _dot(x, wgbf_ref[...])
        hu = _dot(x, wubf_ref[...])
        act = (hg * jax.nn.sigmoid(hg) * hu).astype(BF16)
        ys_ref[...] = _to_token_major(_dot(act, wdbf_ref[...]).astype(BF16))


def _experts(tile_e, next_e, slot, n_act, xs, w_gate, w_up, w_down, n_tiles):
    def row_map(i, te, nx, sl, na):
        return (jnp.minimum(i, na[0] - 1), 0, 0)

    return pl.pallas_call(
        _experts_kernel,
        grid_spec=pltpu.PrefetchScalarGridSpec(
            num_scalar_prefetch=4,
            grid=(n_tiles,),
            in_specs=[pl.BlockSpec((EXPERT_TILE, ROW_TILES, LANES), row_map),
                      pl.BlockSpec(memory_space=pl.ANY),
                      pl.BlockSpec(memory_space=pl.ANY),
                      pl.BlockSpec(memory_space=pl.ANY)],
            out_specs=pl.BlockSpec((EXPERT_TILE, ROW_TILES, LANES), row_map),
            scratch_shapes=[pltpu.VMEM((2, D_MODEL, D_FF), F32),
                            pltpu.VMEM((2, D_MODEL, D_FF), F32),
                            pltpu.VMEM((2, D_FF, D_MODEL), F32),
                            pltpu.VMEM((D_MODEL, D_FF), BF16),
                            pltpu.VMEM((D_MODEL, D_FF), BF16),
                            pltpu.VMEM((D_FF, D_MODEL), BF16),
                            pltpu.SemaphoreType.DMA((2, 3))]),
        out_shape=jax.ShapeDtypeStruct(xs.shape, BF16),
        input_output_aliases={4: 0},
        compiler_params=_cp(("arbitrary",), 48),
        name="experts",
    )(tile_e, next_e, slot, n_act, xs, w_gate, w_up, w_down)


def _combine_kernel(dest_ref, ys_ref, x_ref, wt_ref, p_ref, nw_ref, wg_ref, wp_ref, fw_ref, o_ref,
                    buf0_ref, buf1_ref, sems, *, tm, t_total):
    step = pl.program_id(0)
    bufs = (buf0_ref, buf1_ref)

    def row_copy(tile, k, i):
        slot = tile % 2
        src = ys_ref.at[dest_ref[k * t_total + tile * tm + i]]
        return pltpu.make_async_copy(src, bufs[k].at[slot, i], sems.at[slot])

    def issue(tile):
        def body(j, c):
            for u in range(DMA_UNROLL):
                row_copy(tile, 0, j * DMA_UNROLL + u).start(priority=0)
                row_copy(tile, 1, j * DMA_UNROLL + u).start(priority=1)
            return c
        lax.fori_loop(0, tm // DMA_UNROLL, body, 0)

    def drain(tile):
        def body(j, c):
            for u in range(DMA_UNROLL):
                row_copy(tile, 0, j * DMA_UNROLL + u).wait()
                row_copy(tile, 1, j * DMA_UNROLL + u).wait()
            return c
        lax.fori_loop(0, tm // DMA_UNROLL, body, 0)

    @pl.when(step == 0)
    def _():
        issue(step)

    @pl.when(step + 1 < pl.num_programs(0))
    def _():
        issue(step + 1)

    drain(step)

    wt = wt_ref[...]
    slot = step % 2
    y0 = _from_token_major(buf0_ref[slot]).astype(F32)
    y1 = _from_token_major(buf1_ref[slot]).astype(F32)
    x2 = x_ref[...] + wt[:, 0:1] * y0 + wt[:, 1:2] * y1
    hn = _rms(x2, nw_ref[...]).astype(BF16)
    pg = jax.nn.sigmoid(_dot(hn, wg_ref[...]))
    x3 = x2 + pg * _dot(p_ref[...].astype(BF16), wp_ref[...])
    o_ref[...] = _rms(x3, fw_ref[...])


def _combine_ple(dest_flat, ys3, x1, wts_t, p, ple_norm_w, w_gate, w_proj, final_w, tm=256):
    t = x1.shape[0]
    kern = functools.partial(_combine_kernel, tm=tm, t_total=t)
    row = lambda i, d: (i, 0)
    fix = lambda i, d: (0, 0)
    return pl.pallas_call(
        kern,
        grid_spec=pltpu.PrefetchScalarGridSpec(
            num_scalar_prefetch=1,
            grid=(t // tm,),
            in_specs=[pl.BlockSpec(memory_space=pl.ANY),
                      pl.BlockSpec((tm, D_MODEL), row),
                      pl.BlockSpec((tm, SUBLANES), row),
                      pl.BlockSpec((tm, PLE_DIM), row),
                      pl.BlockSpec((1, D_MODEL), fix),
                      pl.BlockSpec((D_MODEL, D_MODEL), fix),
                      pl.BlockSpec((PLE_DIM, D_MODEL), fix),
                      pl.BlockSpec((1, D_MODEL), fix)],
            out_specs=pl.BlockSpec((tm, D_MODEL), row),
            scratch_shapes=[pltpu.VMEM((2, tm, ROW_TILES, LANES), BF16),
                            pltpu.VMEM((2, tm, ROW_TILES, LANES), BF16),
                            pltpu.SemaphoreType.DMA((2,))]),
        out_shape=jax.ShapeDtypeStruct((t, D_MODEL), F32),
        compiler_params=_cp(("arbitrary",), 48),
        name="combine_ple",
    )(dest_flat, ys3, x1, wts_t, p, ple_norm_w, w_gate, w_proj, final_w)


def _tile_table(counts, n_tiles):
    tiles = (counts.astype(I32) + (EXPERT_TILE - 1)) // EXPERT_TILE
    ends = jnp.cumsum(tiles)
    n_act = ends[-1]
    idx = jnp.minimum(jnp.arange(n_tiles, dtype=I32), n_act - 1)
    tile_e = jnp.sum((idx[:, None] >= ends[None, :]).astype(I32), axis=1).astype(I32)
    run_end = ends[tile_e]
    next_e = jnp.where(run_end < n_act, tile_e[jnp.minimum(run_end, n_tiles - 1)], -1).astype(I32)
    new_run = jnp.concatenate([jnp.ones((1,), I32), (tile_e[1:] != tile_e[:-1]).astype(I32)])
    slot = ((jnp.cumsum(new_run) - 1) % 2).astype(I32)
    return tile_e, next_e, slot, n_act.reshape(1).astype(I32)


def _block(x, p, norm_mix_w, w_in, b_merge, w_alpha_up, b_alpha_up, gla_norm_w, w_gla_out,
           conv_w, conv_b, w_conv_out, w_mix_out, norm_ffn_w, w_router_group, b_router_group,
           w_router_expert, b_router_expert, w_e_gate, w_e_up, w_e_down, ple_norm_w,
           w_ple_gate, w_ple_proj, final_norm_w):
    t = x.shape[0]
    n_tiles = (2 * t) // EXPERT_TILE + N_EXPERTS
    n_rows = n_tiles * EXPERT_TILE

    w_up = jnp.pad(w_alpha_up, ((0, LANES - GLA_GATE_RANK), (0, 0))).astype(BF16)
    w_up = w_up.reshape(LANES, GLA_HEADS, GLA_DK).transpose(1, 0, 2)
    b_up = b_alpha_up.reshape(GLA_HEADS, 1, GLA_DK)
    gnw = gla_norm_w.reshape(GLA_HEADS, 1, GLA_DV)
    wr_t = jnp.zeros((ROUTER_ROWS, D_MODEL), F32)
    wr_t = wr_t.at[0:N_GROUPS].set(w_router_group.T)
    wr_t = wr_t.at[EXPERT_ROW0:EXPERT_ROW0 + N_EXPERTS].set(w_router_expert.T)
    br = jnp.zeros((ROUTER_ROWS, 1), F32)
    br = br.at[0:N_GROUPS, 0].set(b_router_group)
    br = br.at[EXPERT_ROW0:EXPERT_ROW0 + N_EXPERTS, 0].set(b_router_expert)

    w_in_t = w_in.T
    h, a_low = _norm_in(x, norm_mix_w.reshape(1, D_MODEL), w_in_t)
    qkvg = _proj_qkvg(h, w_in_t)
    ob = _proj_conv(h, w_in_t, conv_w, conv_b.reshape(1, D_MODEL))
    gates = _proj_gates(h, w_in_t, b_merge.reshape(1, 2 * D_MODEL))
    oa = _gla(qkvg, a_low, w_up, b_up, gnw)
    mixed = _merge(oa, ob, w_gla_out, w_conv_out, gates)
    x1 = _mix(mixed, w_mix_out, x)

    hp, ids, wts = _route(x1, norm_ffn_w.reshape(1, D_MODEL), wr_t, br)
    dest, counts = _plan(ids)
    dest_flat = dest[1, 0:2].reshape(2 * t)
    tile_e, next_e, slot, n_act = _tile_table(counts[:, 0], n_tiles)
    xs = _dispatch(dest_flat, hp, n_rows)
    ys = _experts(tile_e, next_e, slot, n_act, xs, w_e_gate, w_e_up, w_e_down, n_tiles)
    return _combine_ple(dest_flat, ys, x1, wts.T, p,
                        ple_norm_w.reshape(1, D_MODEL), w_ple_gate.astype(BF16),
                        w_ple_proj.astype(BF16), final_norm_w.reshape(1, D_MODEL))


def kernel(x, p, norm_mix_w, w_in, b_merge, w_alpha_up, b_alpha_up, gla_norm_w, w_gla_out, conv_w, conv_b, w_conv_out, w_mix_out, norm_ffn_w, w_router_group, b_router_group, w_router_expert, b_router_expert, w_e_gate, w_e_up, w_e_down, ple_norm_w, w_ple_gate, w_ple_proj, final_norm_w):
    depth, batch = p.shape[0], x.shape[0]
    assert depth == 1 and batch == 1, "kernel is specialised to one layer and one sequence"
    out = _block(x[0], p[0, 0], norm_mix_w[0], w_in[0], b_merge[0], w_alpha_up[0], b_alpha_up[0],
                 gla_norm_w[0], w_gla_out[0], conv_w[0], conv_b[0], w_conv_out[0], w_mix_out[0],
                 norm_ffn_w[0], w_router_group[0], b_router_group[0], w_router_expert[0],
                 b_router_expert[0], w_e_gate[0], w_e_up[0], w_e_down[0], ple_norm_w[0],
                 w_ple_gate[0], w_ple_proj[0], final_norm_w)
    return out[None]
```

```python
import functools

import jax
import jax.numpy as jnp
from jax import lax
from jax.experimental import pallas as pl
from jax.experimental.pallas import tpu as pltpu

F32 = jnp.float32
BF16 = jnp.bfloat16
I32 = jnp.int32

D_MODEL = 2048
PLE_DIM = 256
EPS = 1e-6
LOG2_E = 1.4426950408889634
GLA_HEADS = 4
GLA_DK = 256
GLA_DV = 512
GLA_KEY = GLA_HEADS * GLA_DK
GLA_VAL = GLA_HEADS * GLA_DV
GLA_GATE_RANK = 16
GLA_GATE_NORM = 16.0
GLA_CHUNK = 64
CONV_K = 3
N_GROUPS = 4
EXPERTS_PER_GROUP = 8
N_EXPERTS = N_GROUPS * EXPERTS_PER_GROUP
D_FF = 512

QKVG_COLS = 2 * GLA_KEY + 2 * GLA_VAL
CONV_COL0 = QKVG_COLS + GLA_GATE_RANK

LANES = 128
SUBLANES = 8
ROW_TILES = D_MODEL // LANES

EXPERT_TILE = 256
DMA_UNROLL = 8
ROUTER_ROWS = 64
EXPERT_ROW0 = 8

MIB = 1024 * 1024


def _cp(sem, vmem_mib, **kw):
    return pltpu.CompilerParams(dimension_semantics=sem, vmem_limit_bytes=int(vmem_mib * MIB), **kw)


def _rms(x, w):
    return x * lax.rsqrt(jnp.mean(x * x, axis=-1, keepdims=True) + EPS) * w


def _dot(a, b):
    return jnp.dot(a, b, preferred_element_type=F32)


def _dot_nt(a, b):
    return lax.dot_general(a, b, (((1,), (1,)), ((), ())), preferred_element_type=F32)


def _dot_tn(a, b):
    return lax.dot_general(a, b, (((0,), (0,)), ((), ())), preferred_element_type=F32)


def _split_bf16(x):
    hi = x.astype(BF16)
    lo = (x - hi.astype(F32)).astype(BF16)
    return hi, lo


def _to_token_major(val):
    return val.reshape(val.shape[0], ROW_TILES, LANES)


def _from_token_major(val):
    return val.reshape(val.shape[0], D_MODEL)


def _norm_in_kernel(x_ref, w_ref, wal_ref, h_ref, al_ref, walbf_ref):
    @pl.when(pl.program_id(0) == 0)
    def _():
        walbf_ref[...] = wal_ref[...].astype(BF16)

    h = _rms(x_ref[...], w_ref[...]).astype(BF16)
    h_ref[...] = h
    al_ref[...] = _dot_nt(h, walbf_ref[...])


def _norm_in(x, w, w_in_t, tm=512):
    t = x.shape[0]
    return pl.pallas_call(
        _norm_in_kernel,
        grid=(t // tm,),
        in_specs=[pl.BlockSpec((tm, D_MODEL), lambda i: (i, 0)),
                  pl.BlockSpec((1, D_MODEL), lambda i: (0, 0)),
                  pl.BlockSpec((LANES, D_MODEL), lambda i: (QKVG_COLS // LANES, 0))],
        out_specs=[pl.BlockSpec((tm, D_MODEL), lambda i: (i, 0)),
                   pl.BlockSpec((tm, LANES), lambda i: (i, 0))],
        out_shape=[jax.ShapeDtypeStruct((t, D_MODEL), BF16),
                   jax.ShapeDtypeStruct((t, LANES), F32)],
        scratch_shapes=[pltpu.VMEM((LANES, D_MODEL), BF16)],
        compiler_params=_cp(("arbitrary",), 32),
        name="norm_in",
    )(x, w, w_in_t)


def _w_rows_spec(row0, tn):
    assert row0 % SUBLANES == 0 and tn % SUBLANES == 0
    return pl.BlockSpec((pl.Element(tn), pl.Element(D_MODEL)),
                        lambda n, m: (pl.multiple_of(row0 + n * tn, SUBLANES), 0))


def _proj_qkvg_kernel(h_ref, w_ref, o_ref, wbf_ref):
    @pl.when(pl.program_id(1) == 0)
    def _():
        wbf_ref[...] = w_ref[...].astype(BF16)

    o_ref[...] = _dot_nt(h_ref[...], wbf_ref[...]).astype(o_ref.dtype)


def _proj_qkvg(h, w_in_t, tm=2048, tn=1024):
    t = h.shape[0]
    return pl.pallas_call(
        _proj_qkvg_kernel,
        grid=(QKVG_COLS // tn, t // tm),
        in_specs=[pl.BlockSpec((tm, D_MODEL), lambda n, m: (m, 0)),
                  pl.BlockSpec((tn, D_MODEL), lambda n, m: (n, 0))],
        out_specs=pl.BlockSpec((tm, tn), lambda n, m: (m, n)),
        out_shape=jax.ShapeDtypeStruct((t, QKVG_COLS), BF16),
        scratch_shapes=[pltpu.VMEM((tn, D_MODEL), BF16)],
        compiler_params=_cp(("arbitrary", "arbitrary"), 56),
        name="proj_qkvg",
    )(h, w_in_t)


def _proj_conv_kernel(h_ref, wb_ref, wc_ref, wx_ref, cw_ref, cb_ref, o_ref,
                      wbbf_ref, wcbf_ref, wxbf_ref, prev_ref):
    m = pl.program_id(1)

    @pl.when(m == 0)
    def _():
        prev_ref[...] = jnp.zeros_like(prev_ref)
        wbbf_ref[...] = wb_ref[...].astype(BF16)
        wcbf_ref[...] = wc_ref[...].astype(BF16)
        wxbf_ref[...] = wx_ref[...].astype(BF16)

    h = h_ref[...]
    b = _dot_nt(h, wbbf_ref[...])
    s = _dot_nt(h, wcbf_ref[...]) * _dot_nt(h, wxbf_ref[...])
    tm = s.shape[0]
    row = lax.broadcasted_iota(I32, s.shape, 0)
    prev = prev_ref[...]
    p1 = prev[SUBLANES - 1:SUBLANES, :]
    p2 = prev[SUBLANES - 2:SUBLANES - 1, :]
    s1 = jnp.where(row == 0, p1, pltpu.roll(s, 1, 0))
    s2 = jnp.where(row == 0, p2, jnp.where(row == 1, p1, pltpu.roll(s, 2, 0)))
    cw = cw_ref[...]
    u = cw[2:3, :] * s + cw[1:2, :] * s1 + cw[0:1, :] * s2 + cb_ref[...]
    o_ref[...] = (b * u).astype(o_ref.dtype)
    prev_ref[...] = s[tm - SUBLANES:, :]


def _proj_conv(h, w_in_t, conv_w, conv_b, tm=1024, tn=512):
    t = h.shape[0]
    return pl.pallas_call(
        _proj_conv_kernel,
        grid=(D_MODEL // tn, t // tm),
        in_specs=[pl.BlockSpec((tm, D_MODEL), lambda n, m: (m, 0))] +
                 [_w_rows_spec(CONV_COL0 + seg * D_MODEL, tn) for seg in range(3)] +
                 [pl.BlockSpec((CONV_K, tn), lambda n, m: (0, n)),
                  pl.BlockSpec((1, tn), lambda n, m: (0, n))],
        out_specs=pl.BlockSpec((tm, tn), lambda n, m: (m, n)),
        out_shape=jax.ShapeDtypeStruct((t, D_MODEL), BF16),
        scratch_shapes=[pltpu.VMEM((tn, D_MODEL), BF16), pltpu.VMEM((tn, D_MODEL), BF16),
                        pltpu.VMEM((tn, D_MODEL), BF16), pltpu.VMEM((SUBLANES, tn), F32)],
        compiler_params=_cp(("arbitrary", "arbitrary"), 48),
        name="proj_conv",
    )(h, w_in_t, w_in_t, w_in_t, conv_w, conv_b)


def _proj_gates_kernel(h_ref, w_ref, b_ref, o_ref, wbf_ref):
    @pl.when(pl.program_id(1) == 0)
    def _():
        wbf_ref[...] = w_ref[...].astype(BF16)

    o_ref[...] = jax.nn.sigmoid(_dot_nt(h_ref[...], wbf_ref[...]) + b_ref[...]).astype(o_ref.dtype)


def _proj_gates(h, w_in_t, b_merge, tm=2048, tn=1024):
    t = h.shape[0]
    return pl.pallas_call(
        _proj_gates_kernel,
        grid=(2 * D_MODEL // tn, t // tm),
        in_specs=[pl.BlockSpec((tm, D_MODEL), lambda n, m: (m, 0)),
                  _w_rows_spec(CONV_COL0 + 3 * D_MODEL, tn),
                  pl.BlockSpec((1, tn), lambda n, m: (0, n))],
        out_specs=pl.BlockSpec((tm, tn), lambda n, m: (m, n)),
        out_shape=jax.ShapeDtypeStruct((t, 2 * D_MODEL), BF16),
        scratch_shapes=[pltpu.VMEM((tn, D_MODEL), BF16)],
        compiler_params=_cp(("arbitrary", "arbitrary"), 56),
        name="proj_gates",
    )(h, w_in_t, b_merge)


def _gla_kernel(q_ref, k_ref, v_ref, g_ref, al_ref, wup_ref, bup_ref, nw_ref, o_ref,
                st_ref, b_ref, bl_ref, qd_ref, ki_ref, kd_ref, oi_ref, u_ref, *, n_chunks):
    c_len = GLA_CHUNK

    @pl.when(pl.program_id(1) == 0)
    def _():
        st_ref[...] = jnp.zeros_like(st_ref)

    row = lax.broadcasted_iota(I32, (c_len, c_len), 0)
    col = lax.broadcasted_iota(I32, (c_len, c_len), 1)
    causal = col <= row
    tril = jnp.where(causal, 1.0, 0.0).astype(BF16)

    def decays(h):
        kc = slice(h * GLA_DK, (h + 1) * GLA_DK)
        z = _dot(al_ref[...].astype(BF16), wup_ref[h]) + bup_ref[h]
        la = (jnp.minimum(z, 0.0) - jnp.log1p(jnp.exp(-jnp.abs(z)))) * (LOG2_E / GLA_GATE_NORM)
        la_hi, la_lo = _split_bf16(la)
        for c in range(n_chunks):
            r0 = c * c_len
            b = _dot(tril, la_hi[r0:r0 + c_len]) + _dot(tril, la_lo[r0:r0 + c_len])
            b_ref[pl.ds(r0, c_len), kc] = b
            bl_ref[pl.ds(r0, c_len), kc] = jnp.broadcast_to(b[c_len - 1:c_len, :], b.shape)
        b = b_ref[:, kc]
        q = q_ref[:, kc].astype(F32)
        k = k_ref[:, kc].astype(F32)
        qd_ref[:, kc] = (q * jnp.exp2(b)).astype(BF16)
        ki_ref[:, kc] = (k * jnp.exp2(-b)).astype(BF16)
        kd_ref[:, kc] = (k * jnp.exp2(bl_ref[:, kc] - b)).astype(BF16)

    def local_products(h):
        kc = slice(h * GLA_DK, (h + 1) * GLA_DK)
        vc = slice(h * GLA_DV, (h + 1) * GLA_DV)
        for c in range(n_chunks):
            sl = pl.ds(c * c_len, c_len)
            v = v_ref[sl, vc]
            att = jnp.where(causal, _dot_nt(qd_ref[sl, kc], ki_ref[sl, kc]), 0.0).astype(BF16)
            oi_ref[sl, vc] = _dot(att, v)
            u_ref[h, c] = _dot_tn(v, kd_ref[sl, kc])

    def recurrence(h):
        kc = slice(h * GLA_DK, (h + 1) * GLA_DK)
        vc = slice(h * GLA_DV, (h + 1) * GLA_DV)
        for c in range(n_chunks):
            sl = pl.ds(c * c_len, c_len)
            st = st_ref[h]
            oi_ref[sl, vc] = oi_ref[sl, vc] + _dot_nt(qd_ref[sl, kc], st.astype(BF16))
            st_ref[h] = st * jnp.exp2(bl_ref[pl.ds(c * c_len, 1), kc]) + u_ref[h, c]
        o = oi_ref[:, vc]
        o = o * lax.rsqrt(jnp.mean(o * o, axis=-1, keepdims=True) + EPS * GLA_DK) * nw_ref[h]
        g = g_ref[:, vc].astype(F32)
        o_ref[:, vc] = (o * (g * jax.nn.sigmoid(g))).astype(o_ref.dtype)

    decays(0)
    local_products(0)
    decays(1)
    recurrence(0)
    local_products(1)
    recurrence(1)


def _gla(qkvg, a_low, w_up, b_up, norm_w, tb=512):
    t = qkvg.shape[0]
    hp = 2
    dk, dv = hp * GLA_DK, hp * GLA_DV
    kq = GLA_KEY // dk
    kv = 2 * GLA_KEY // dv
    kg = kv + GLA_VAL // dv
    n_chunks = tb // GLA_CHUNK
    kern = functools.partial(_gla_kernel, n_chunks=n_chunks)
    return pl.pallas_call(
        kern,
        grid=(GLA_HEADS // hp, t // tb),
        in_specs=[pl.BlockSpec((tb, dk), lambda h, i: (i, h)),
                  pl.BlockSpec((tb, dk), lambda h, i: (i, kq + h)),
                  pl.BlockSpec((tb, dv), lambda h, i: (i, kv + h)),
                  pl.BlockSpec((tb, dv), lambda h, i: (i, kg + h)),
                  pl.BlockSpec((tb, LANES), lambda h, i: (i, 0)),
                  pl.BlockSpec((hp, LANES, GLA_DK), lambda h, i: (h, 0, 0)),
                  pl.BlockSpec((hp, 1, GLA_DK), lambda h, i: (h, 0, 0)),
                  pl.BlockSpec((hp, 1, GLA_DV), lambda h, i: (h, 0, 0))],
        out_specs=pl.BlockSpec((tb, dv), lambda h, i: (i, h)),
        out_shape=jax.ShapeDtypeStruct((t, GLA_VAL), BF16),
        scratch_shapes=[pltpu.VMEM((hp, GLA_DV, GLA_DK), F32),
                        pltpu.VMEM((tb, dk), F32), pltpu.VMEM((tb, dk), F32),
                        pltpu.VMEM((tb, dk), BF16), pltpu.VMEM((tb, dk), BF16),
                        pltpu.VMEM((tb, dk), BF16),
                        pltpu.VMEM((tb, dv), F32),
                        pltpu.VMEM((hp, n_chunks, GLA_DV, GLA_DK), F32)],
        compiler_params=_cp(("arbitrary", "arbitrary"), 40),
        name="gla",
    )(qkvg, qkvg, qkvg, qkvg, a_low, w_up, b_up, norm_w)


def _merge_kernel(oa_ref, ob_ref, wa_ref, wb_ref, g0_ref, g1_ref, o_ref, wabf_ref, wbbf_ref):
    @pl.when(pl.program_id(1) == 0)
    def _():
        wabf_ref[...] = wa_ref[...].astype(BF16)
        wbbf_ref[...] = wb_ref[...].astype(BF16)

    a = _dot(oa_ref[...], wabf_ref[...])
    b = _dot(ob_ref[...], wbbf_ref[...])
    o_ref[...] = (g0_ref[...].astype(F32) * a + g1_ref[...].astype(F32) * b).astype(o_ref.dtype)


def _merge(oa, ob, w_a, w_b, gates, tm=1024, tn=512):
    t = oa.shape[0]
    nb = D_MODEL // tn
    return pl.pallas_call(
        _merge_kernel,
        grid=(nb, t // tm),
        in_specs=[pl.BlockSpec((tm, D_MODEL), lambda n, m: (m, 0)),
                  pl.BlockSpec((tm, D_MODEL), lambda n, m: (m, 0)),
                  pl.BlockSpec((D_MODEL, tn), lambda n, m: (0, n)),
                  pl.BlockSpec((D_MODEL, tn), lambda n, m: (0, n)),
                  pl.BlockSpec((tm, tn), lambda n, m: (m, n)),
                  pl.BlockSpec((tm, tn), lambda n, m: (m, nb + n))],
        out_specs=pl.BlockSpec((tm, tn), lambda n, m: (m, n)),
        out_shape=jax.ShapeDtypeStruct((t, D_MODEL), BF16),
        scratch_shapes=[pltpu.VMEM((D_MODEL, tn), BF16), pltpu.VMEM((D_MODEL, tn), BF16)],
        compiler_params=_cp(("arbitrary", "arbitrary"), 48),
        name="merge",
    )(oa, ob, w_a, w_b, gates, gates)


def _mix_kernel(a_ref, w_ref, x_ref, o_ref, wbf_ref):
    @pl.when(pl.program_id(1) == 0)
    def _():
        wbf_ref[...] = w_ref[...].astype(BF16)

    o_ref[...] = x_ref[...] + _dot(a_ref[...], wbf_ref[...])


def _mix(mixed, w, x, tm=1024, tn=1024):
    t = x.shape[0]
    return pl.pallas_call(
        _mix_kernel,
        grid=(D_MODEL // tn, t // tm),
        in_specs=[pl.BlockSpec((tm, D_MODEL), lambda n, m: (m, 0)),
                  pl.BlockSpec((D_MODEL, tn), lambda n, m: (0, n)),
                  pl.BlockSpec((tm, tn), lambda n, m: (m, n))],
        out_specs=pl.BlockSpec((tm, tn), lambda n, m: (m, n)),
        out_shape=jax.ShapeDtypeStruct((t, D_MODEL), F32),
        scratch_shapes=[pltpu.VMEM((D_MODEL, tn), BF16)],
        compiler_params=_cp(("arbitrary", "arbitrary"), 52),
        name="mix",
    )(mixed, w, x)


def _route_kernel(x_ref, nw_ref, wr_ref, br_ref, hp_ref, ids_ref, wts_ref):
    h = _rms(x_ref[...], nw_ref[...])
    hp_ref[...] = _to_token_major(h.astype(BF16))
    h_hi, h_lo = _split_bf16(h)
    w_hi, w_lo = _split_bf16(wr_ref[...])
    logits = _dot_nt(w_hi, h_hi) + _dot_nt(w_hi, h_lo) + _dot_nt(w_lo, h_hi) + br_ref[...]
    tm = logits.shape[1]

    best = logits[0:1, :]
    gidx = jnp.zeros((1, tm), I32)
    for i in range(1, N_GROUPS):
        li = logits[i:i + 1, :]
        take = li > best
        best = jnp.where(take, li, best)
        gidx = jnp.where(take, i, gidx)
    gsum = jnp.zeros((1, tm), F32)
    for i in range(N_GROUPS):
        gsum = gsum + jnp.exp(logits[i:i + 1, :] - best)
    g_p = 1.0 / gsum

    sel = logits[EXPERT_ROW0:EXPERT_ROW0 + EXPERTS_PER_GROUP, :]
    for g in range(1, N_GROUPS):
        r0 = EXPERT_ROW0 + g * EXPERTS_PER_GROUP
        sel = jnp.where(gidx == g, logits[r0:r0 + EXPERTS_PER_GROUP, :], sel)
    eio = lax.broadcasted_iota(I32, sel.shape, 0)
    m1 = jnp.max(sel, axis=0, keepdims=True)
    i1 = jnp.min(jnp.where(sel == m1, eio, EXPERTS_PER_GROUP), axis=0, keepdims=True)
    rest = jnp.where(eio == i1, -jnp.inf, sel)
    m2 = jnp.max(rest, axis=0, keepdims=True)
    i2 = jnp.min(jnp.where(rest == m2, eio, EXPERTS_PER_GROUP), axis=0, keepdims=True)
    p2 = jnp.exp(m2 - m1)
    w1 = g_p / (1.0 + p2)
    w2 = g_p * p2 / (1.0 + p2)
    e1 = gidx * EXPERTS_PER_GROUP + i1
    e2 = gidx * EXPERTS_PER_GROUP + i2
    rio = lax.broadcasted_iota(I32, (SUBLANES, tm), 0)
    ids_ref[...] = jnp.where(rio == 0, e1, jnp.where(rio == 1, e2, 0))
    wts_ref[...] = jnp.where(rio == 0, w1, jnp.where(rio == 1, w2, 0.0))


def _route(x1, norm_w, wr_t, br, tm=256):
    t = x1.shape[0]
    return pl.pallas_call(
        _route_kernel,
        grid=(t // tm,),
        in_specs=[pl.BlockSpec((tm, D_MODEL), lambda i: (i, 0)),
                  pl.BlockSpec((1, D_MODEL), lambda i: (0, 0)),
                  pl.BlockSpec((ROUTER_ROWS, D_MODEL), lambda i: (0, 0)),
                  pl.BlockSpec((ROUTER_ROWS, 1), lambda i: (0, 0))],
        out_specs=[pl.BlockSpec((tm, ROW_TILES, LANES), lambda i: (i, 0, 0)),
                   pl.BlockSpec((SUBLANES, tm), lambda i: (0, i)),
                   pl.BlockSpec((SUBLANES, tm), lambda i: (0, i))],
        out_shape=[jax.ShapeDtypeStruct((t, ROW_TILES, LANES), BF16),
                   jax.ShapeDtypeStruct((SUBLANES, t), I32),
                   jax.ShapeDtypeStruct((SUBLANES, t), F32)],
        compiler_params=_cp(("arbitrary",), 32),
        name="route",
    )(x1, norm_w, wr_t, br)


def _plan_kernel(ids_ref, dest_ref, cnt_ref, base_ref):
    phase = pl.program_id(0)
    step = pl.program_id(1)
    tm = ids_ref.shape[1]
    eio = lax.broadcasted_iota(I32, (N_EXPERTS, tm), 0)
    ids = ids_ref[...]
    oh = [jnp.where(eio == ids[k:k + 1, :], 1.0, 0.0) for k in range(2)]

    @pl.when((phase == 0) & (step == 0))
    def _():
        base_ref[...] = jnp.zeros_like(base_ref)

    @pl.when(phase == 0)
    def _():
        cnt = jnp.sum(oh[0] + oh[1], axis=1, keepdims=True)
        base_ref[...] = base_ref[...] + cnt
        dest_ref[0] = jnp.zeros(dest_ref.shape[1:], I32)
        cnt_ref[...] = base_ref[...]

    @pl.when((phase == 1) & (step == 0))
    def _():
        tiles = jnp.floor((base_ref[...] + (EXPERT_TILE - 1)) * (1.0 / EXPERT_TILE))
        r = lax.broadcasted_iota(I32, (N_EXPERTS, N_EXPERTS), 0)
        c = lax.broadcasted_iota(I32, (N_EXPERTS, N_EXPERTS), 1)
        lower = jnp.where(c < r, 1.0, 0.0).astype(BF16)
        base_ref[...] = _dot(lower, tiles.astype(BF16)) * float(EXPERT_TILE)

    @pl.when(phase == 1)
    def _():
        r = lax.broadcasted_iota(I32, (tm, tm), 0)
        c = lax.broadcasted_iota(I32, (tm, tm), 1)
        upper = jnp.where(r <= c, 1.0, 0.0).astype(BF16)
        base = base_ref[...][:, 0:1]
        rows = []
        for k in range(2):
            cum = _dot(oh[k].astype(BF16), upper)
            rows.append(jnp.sum(oh[k] * (cum - 1.0 + base), axis=0, keepdims=True))
            base = base + cum[:, tm - 1:tm]
        base_ref[...] = jnp.broadcast_to(base, base_ref.shape)
        rio = lax.broadcasted_iota(I32, (SUBLANES, tm), 0)
        d0 = rows[0].astype(I32)
        d1 = rows[1].astype(I32)
        dest_ref[0] = jnp.where(rio == 0, d0, jnp.where(rio == 1, d1, 0))


def _plan(ids, tm=512):
    t = ids.shape[1]
    return pl.pallas_call(
        _plan_kernel,
        grid=(2, t // tm),
        in_specs=[pl.BlockSpec((SUBLANES, tm), lambda p, i: (0, i))],
        out_specs=[pl.BlockSpec((1, SUBLANES, tm), lambda p, i: (p, 0, i)),
                   pl.BlockSpec((N_EXPERTS, LANES), lambda p, i: (0, 0))],
        out_shape=[jax.ShapeDtypeStruct((2, SUBLANES, t), I32),
                   jax.ShapeDtypeStruct((N_EXPERTS, LANES), F32)],
        scratch_shapes=[pltpu.VMEM((N_EXPERTS, LANES), F32)],
        compiler_params=_cp(("arbitrary", "arbitrary"), 32),
        name="plan",
    )(ids)


def _dispatch_kernel(dest_ref, hp_ref, zero_ref, xs_ref, sem, *, tm, t_total):
    del zero_ref
    base = pl.program_id(0) * tm

    def row_copy(k, tok):
        return pltpu.make_async_copy(hp_ref.at[tok], xs_ref.at[dest_ref[k * t_total + tok]], sem)

    def issue(j, c):
        for u in range(DMA_UNROLL):
            row_copy(0, base + j * DMA_UNROLL + u).start()
            row_copy(1, base + j * DMA_UNROLL + u).start()
        return c

    def drain(j, c):
        for u in range(DMA_UNROLL):
            row_copy(0, base + j * DMA_UNROLL + u).wait()
            row_copy(1, base + j * DMA_UNROLL + u).wait()
        return c

    lax.fori_loop(0, tm // DMA_UNROLL, issue, 0)
    lax.fori_loop(0, tm // DMA_UNROLL, drain, 0)


def _dispatch(dest_flat, hp3, n_rows, tm=512):
    t = hp3.shape[0]
    zeros = jnp.zeros((n_rows, ROW_TILES, LANES), BF16)
    kern = functools.partial(_dispatch_kernel, tm=tm, t_total=t)
    return pl.pallas_call(
        kern,
        grid_spec=pltpu.PrefetchScalarGridSpec(
            num_scalar_prefetch=1,
            grid=(t // tm,),
            in_specs=[pl.BlockSpec(memory_space=pl.ANY), pl.BlockSpec(memory_space=pl.ANY)],
            out_specs=pl.BlockSpec(memory_space=pl.ANY),
            scratch_shapes=[pltpu.SemaphoreType.DMA(())]),
        out_shape=jax.ShapeDtypeStruct((n_rows, ROW_TILES, LANES), BF16),
        input_output_aliases={2: 0},
        compiler_params=_cp(("arbitrary",), 16, has_side_effects=True),
        name="dispatch",
    )(dest_flat, hp3, zeros)


def _experts_kernel(te_ref, nx_ref, sl_ref, na_ref, xs_ref, wg_hbm, wu_hbm, wd_hbm, ys_ref,
                    wgf_ref, wuf_ref, wdf_ref, wgbf_ref, wubf_ref, wdbf_ref, sems):
    i = pl.program_id(0)
    active = i < na_ref[0]
    changed = (i == 0) | (te_ref[i] != te_ref[jnp.maximum(i - 1, 0)])

    def weight_copies(e, slot):
        return (pltpu.make_async_copy(wg_hbm.at[e], wgf_ref.at[slot], sems.at[slot, 0]),
                pltpu.make_async_copy(wu_hbm.at[e], wuf_ref.at[slot], sems.at[slot, 1]),
                pltpu.make_async_copy(wd_hbm.at[e], wdf_ref.at[slot], sems.at[slot, 2]))

    @pl.when(i == 0)
    def _():
        for cp in weight_copies(te_ref[0], 0):
            cp.start()

    @pl.when(active & changed)
    def _():
        slot = sl_ref[i]
        nxt = nx_ref[i]

        @pl.when(nxt >= 0)
        def _():
            for cp in weight_copies(nxt, 1 - slot):
                cp.start()

        for cp in weight_copies(te_ref[i], slot):
            cp.wait()
        wgbf_ref[...] = wgf_ref[slot].astype(BF16)
        wubf_ref[...] = wuf_ref[slot].astype(BF16)
        wdbf_ref[...] = wdf_ref[slot].astype(BF16)

    @pl.when(active)
    def _():
        x = _from_token_major(xs_ref[...])
        hg = _dot(x, wgbf_ref[...])
        hu = _dot(x, wubf_ref[...])
        act = (hg * jax.nn.sigmoid(hg) * hu).astype(BF16)
        ys_ref[...] = _to_token_major(_dot(act, wdbf_ref[...]).astype(BF16))


def _experts(tile_e, next_e, slot, n_act, xs, w_gate, w_up, w_down, n_tiles):
    def row_map(i, te, nx, sl, na):
        return (jnp.minimum(i, na[0] - 1), 0, 0)

    return pl.pallas_call(
        _experts_kernel,
        grid_spec=pltpu.PrefetchScalarGridSpec(
            num_scalar_prefetch=4,
            grid=(n_tiles,),
            in_specs=[pl.BlockSpec((EXPERT_TILE, ROW_TILES, LANES), row_map),
                      pl.BlockSpec(memory_space=pl.ANY),
                      pl.BlockSpec(memory_space=pl.ANY),
                      pl.BlockSpec(memory_space=pl.ANY)],
            out_specs=pl.BlockSpec((EXPERT_TILE, ROW_TILES, LANES), row_map),
            scratch_shapes=[pltpu.VMEM((2, D_MODEL, D_FF), F32),
                            pltpu.VMEM((2, D_MODEL, D_FF), F32),
                            pltpu.VMEM((2, D_FF, D_MODEL), F32),
                            pltpu.VMEM((D_MODEL, D_FF), BF16),
                            pltpu.VMEM((D_MODEL, D_FF), BF16),
                            pltpu.VMEM((D_FF, D_MODEL), BF16),
                            pltpu.SemaphoreType.DMA((2, 3))]),
        out_shape=jax.ShapeDtypeStruct(xs.shape, BF16),
        input_output_aliases={4: 0},
        compiler_params=_cp(("arbitrary",), 48),
        name="experts",
    )(tile_e, next_e, slot, n_act, xs, w_gate, w_up, w_down)


def _combine_kernel(dest_ref, ys_ref, x_ref, wt_ref, p_ref, nw_ref, wg_ref, wp_ref, fw_ref, o_ref,
                    buf0_ref, buf1_ref, sems, *, tm, t_total):
    step = pl.program_id(0)
    bufs = (buf0_ref, buf1_ref)

    def row_copy(tile, k, i):
        slot = tile % 2
        src = ys_ref.at[dest_ref[k * t_total + tile * tm + i]]
        return pltpu.make_async_copy(src, bufs[k].at[slot, i], sems.at[slot])

    def issue(tile):
        def body(j, c):
            for u in range(DMA_UNROLL):
                row_copy(tile, 0, j * DMA_UNROLL + u).start(priority=0)
                row_copy(tile, 1, j * DMA_UNROLL + u).start(priority=1)
            return c
        lax.fori_loop(0, tm // DMA_UNROLL, body, 0)

    def drain(tile):
        def body(j, c):
            for u in range(DMA_UNROLL):
                row_copy(tile, 0, j * DMA_UNROLL + u).wait()
                row_copy(tile, 1, j * DMA_UNROLL + u).wait()
            return c
        lax.fori_loop(0, tm // DMA_UNROLL, body, 0)

    @pl.when(step == 0)
    def _():
        issue(step)

    @pl.when(step + 1 < pl.num_programs(0))
    def _():
        issue(step + 1)

    drain(step)

    wt = wt_ref[...]
    slot = step % 2
    y0 = _from_token_major(buf0_ref[slot]).astype(F32)
    y1 = _from_token_major(buf1_ref[slot]).astype(F32)
    x2 = x_ref[...] + wt[:, 0:1] * y0 + wt[:, 1:2] * y1
    hn = _rms(x2, nw_ref[...]).astype(BF16)
    pg = jax.nn.sigmoid(_dot(hn, wg_ref[...]))
    x3 = x2 + pg * _dot(p_ref[...].astype(BF16), wp_ref[...])
    o_ref[...] = _rms(x3, fw_ref[...])


def _combine_ple(dest_flat, ys3, x1, wts_t, p, ple_norm_w, w_gate, w_proj, final_w, tm=256):
    t = x1.shape[0]
    kern = functools.partial(_combine_kernel, tm=tm, t_total=t)
    row = lambda i, d: (i, 0)
    fix = lambda i, d: (0, 0)
    return pl.pallas_call(
        kern,
        grid_spec=pltpu.PrefetchScalarGridSpec(
            num_scalar_prefetch=1,
            grid=(t // tm,),
            in_specs=[pl.BlockSpec(memory_space=pl.ANY),
                      pl.BlockSpec((tm, D_MODEL), row),
                      pl.BlockSpec((tm, SUBLANES), row),
                      pl.BlockSpec((tm, PLE_DIM), row),
                      pl.BlockSpec((1, D_MODEL), fix),
                      pl.BlockSpec((D_MODEL, D_MODEL), fix),
                      pl.BlockSpec((PLE_DIM, D_MODEL), fix),
                      pl.BlockSpec((1, D_MODEL), fix)],
            out_specs=pl.BlockSpec((tm, D_MODEL), row),
            scratch_shapes=[pltpu.VMEM((2, tm, ROW_TILES, LANES), BF16),
                            pltpu.VMEM((2, tm, ROW_TILES, LANES), BF16),
                            pltpu.SemaphoreType.DMA((2,))]),
        out_shape=jax.ShapeDtypeStruct((t, D_MODEL), F32),
        compiler_params=_cp(("arbitrary",), 48),
        name="combine_ple",
    )(dest_flat, ys3, x1, wts_t, p, ple_norm_w, w_gate, w_proj, final_w)


def _tile_table(counts, n_tiles):
    tiles = (counts.astype(I32) + (EXPERT_TILE - 1)) // EXPERT_TILE
    ends = jnp.cumsum(tiles)
    n_act = ends[-1]
    idx = jnp.minimum(jnp.arange(n_tiles, dtype=I32), n_act - 1)
    tile_e = jnp.sum((idx[:, None] >= ends[None, :]).astype(I32), axis=1).astype(I32)
    run_end = ends[tile_e]
    next_e = jnp.where(run_end < n_act, tile_e[jnp.minimum(run_end, n_tiles - 1)], -1).astype(I32)
    new_run = jnp.concatenate([jnp.ones((1,), I32), (tile_e[1:] != tile_e[:-1]).astype(I32)])
    slot = ((jnp.cumsum(new_run) - 1) % 2).astype(I32)
    return tile_e, next_e, slot, n_act.reshape(1).astype(I32)


def _block(x, p, norm_mix_w, w_in, b_merge, w_alpha_up, b_alpha_up, gla_norm_w, w_gla_out,
           conv_w, conv_b, w_conv_out, w_mix_out, norm_ffn_w, w_router_group, b_router_group,
           w_router_expert, b_router_expert, w_e_gate, w_e_up, w_e_down, ple_norm_w,
           w_ple_gate, w_ple_proj, final_norm_w):
    t = x.shape[0]
    n_tiles = (2 * t) // EXPERT_TILE + N_EXPERTS
    n_rows = n_tiles * EXPERT_TILE

    w_up = jnp.pad(w_alpha_up, ((0, LANES - GLA_GATE_RANK), (0, 0))).astype(BF16)
    w_up = w_up.reshape(LANES, GLA_HEADS, GLA_DK).transpose(1, 0, 2)
    b_up = b_alpha_up.reshape(GLA_HEADS, 1, GLA_DK)
    gnw = gla_norm_w.reshape(GLA_HEADS, 1, GLA_DV)
    wr_t = jnp.zeros((ROUTER_ROWS, D_MODEL), F32)
    wr_t = wr_t.at[0:N_GROUPS].set(w_router_group.T)
    wr_t = wr_t.at[EXPERT_ROW0:EXPERT_ROW0 + N_EXPERTS].set(w_router_expert.T)
    br = jnp.zeros((ROUTER_ROWS, 1), F32)
    br = br.at[0:N_GROUPS, 0].set(b_router_group)
    br = br.at[EXPERT_ROW0:EXPERT_ROW0 + N_EXPERTS, 0].set(b_router_expert)

    w_in_t = w_in.T
    h, a_low = _norm_in(x, norm_mix_w.reshape(1, D_MODEL), w_in_t)
    qkvg = _proj_qkvg(h, w_in_t)
    ob = _proj_conv(h, w_in_t, conv_w, conv_b.reshape(1, D_MODEL))
    gates = _proj_gates(h, w_in_t, b_merge.reshape(1, 2 * D_MODEL))
    oa = _gla(qkvg, a_low, w_up, b_up, gnw)
    mixed = _merge(oa, ob, w_gla_out, w_conv_out, gates)
    x1 = _mix(mixed, w_mix_out, x)

    hp, ids, wts = _route(x1, norm_ffn_w.reshape(1, D_MODEL), wr_t, br)
    dest, counts = _plan(ids)
    dest_flat = dest[1, 0:2].reshape(2 * t)
    tile_e, next_e, slot, n_act = _tile_table(counts[:, 0], n_tiles)
    xs = _dispatch(dest_flat, hp, n_rows)
    ys = _experts(tile_e, next_e, slot, n_act, xs, w_e_gate, w_e_up, w_e_down, n_tiles)
    return _combine_ple(dest_flat, ys, x1, wts.T, p,
                        ple_norm_w.reshape(1, D_MODEL), w_ple_gate.astype(BF16),
                        w_ple_proj.astype(BF16), final_norm_w.reshape(1, D_MODEL))


def kernel(x, p, norm_mix_w, w_in, b_merge, w_alpha_up, b_alpha_up, gla_norm_w, w_gla_out, conv_w, conv_b, w_conv_out, w_mix_out, norm_ffn_w, w_router_group, b_router_group, w_router_expert, b_router_expert, w_e_gate, w_e_up, w_e_down, ple_norm_w, w_ple_gate, w_ple_proj, final_norm_w):
    depth, batch = p.shape[0], x.shape[0]
    assert depth == 1 and batch == 1, "kernel is specialised to one layer and one sequence"
    out = _block(x[0], p[0, 0], norm_mix_w[0], w_in[0], b_merge[0], w_alpha_up[0], b_alpha_up[0],
                 gla_norm_w[0], w_gla_out[0], conv_w[0], conv_b[0], w_conv_out[0], w_mix_out[0],
                 norm_ffn_w[0], w_router_group[0], b_router_group[0], w_router_expert[0],
                 b_router_expert[0], w_e_gate[0], w_e_up[0], w_e_down[0], ple_norm_w[0],
                 w_ple_gate[0], w_ple_proj[0], final_norm_w)
    return out[None]
```

```python
import functools

import jax
import jax.numpy as jnp
from jax import lax
from jax.experimental import pallas as pl
from jax.experimental.pallas import tpu as pltpu

F32 = jnp.float32
BF16 = jnp.bfloat16
I32 = jnp.int32

D_MODEL = 2048
PLE_DIM = 256
EPS = 1e-6
LOG2_E = 1.4426950408889634
GLA_HEADS = 4
GLA_DK = 256
GLA_DV = 512
GLA_KEY = GLA_HEADS * GLA_DK
GLA_VAL = GLA_HEADS * GLA_DV
GLA_GATE_RANK = 16
GLA_GATE_NORM = 16.0
GLA_CHUNK = 64
CONV_K = 3
N_GROUPS = 4
EXPERTS_PER_GROUP = 8
N_EXPERTS = N_GROUPS * EXPERTS_PER_GROUP
D_FF = 512

QKVG_COLS = 2 * GLA_KEY + 2 * GLA_VAL
CONV_COL0 = QKVG_COLS + GLA_GATE_RANK

LANES = 128
SUBLANES = 8
MXU_COLS = 256
ROW_TILES = D_MODEL // LANES

EXPERT_TILE = 256
DMA_UNROLL = 8
ROUTER_ROWS = 64
EXPERT_ROW0 = 8

MIB = 1024 * 1024


def _cp(sem, vmem_mib, **kw):
    return pltpu.CompilerParams(dimension_semantics=sem, vmem_limit_bytes=int(vmem_mib * MIB), **kw)


def _rms(x, w):
    return x * lax.rsqrt(jnp.mean(x * x, axis=-1, keepdims=True) + EPS) * w


def _dot(a, b):
    return jnp.dot(a, b, preferred_element_type=F32)


def _dot_nt(a, b):
    return lax.dot_general(a, b, (((1,), (1,)), ((), ())), preferred_element_type=F32)


def _dot_tn(a, b):
    return lax.dot_general(a, b, (((0,), (0,)), ((), ())), preferred_element_type=F32)


def _split_bf16(x):
    hi = x.astype(BF16)
    lo = (x - hi.astype(F32)).astype(BF16)
    return hi, lo


def _to_token_major(val):
    return val.reshape(val.shape[0], ROW_TILES, LANES)


def _from_token_major(val):
    return val.reshape(val.shape[0], D_MODEL)


def _norm_in_kernel(x_ref, w_ref, wal_ref, h_ref, al_ref, walbf_ref):
    @pl.when(pl.program_id(0) == 0)
    def _():
        walbf_ref[...] = wal_ref[...].astype(BF16)

    h = _rms(x_ref[...], w_ref[...]).astype(BF16)
    h_ref[...] = h
    al_ref[...] = _dot_nt(h, walbf_ref[...])


def _norm_in(x, w, w_in_t, tm=512):
    t = x.shape[0]
    return pl.pallas_call(
        _norm_in_kernel,
        grid=(t // tm,),
        in_specs=[pl.BlockSpec((tm, D_MODEL), lambda i: (i, 0)),
                  pl.BlockSpec((1, D_MODEL), lambda i: (0, 0)),
                  pl.BlockSpec((LANES, D_MODEL), lambda i: (QKVG_COLS // LANES, 0))],
        out_specs=[pl.BlockSpec((tm, D_MODEL), lambda i: (i, 0)),
                   pl.BlockSpec((tm, LANES), lambda i: (i, 0))],
        out_shape=[jax.ShapeDtypeStruct((t, D_MODEL), BF16),
                   jax.ShapeDtypeStruct((t, LANES), F32)],
        scratch_shapes=[pltpu.VMEM((LANES, D_MODEL), BF16)],
        compiler_params=_cp(("arbitrary",), 32),
        name="norm_in",
    )(x, w, w_in_t)


def _w_rows_spec(row0, tn):
    assert row0 % SUBLANES == 0 and tn % SUBLANES == 0
    return pl.BlockSpec((pl.Element(tn), pl.Element(D_MODEL)),
                        lambda n, m: (pl.multiple_of(row0 + n * tn, SUBLANES), 0))


def _proj_qkvg_kernel(h_ref, w_ref, o_ref, wbf_ref):
    @pl.when(pl.program_id(1) == 0)
    def _():
        wbf_ref[...] = w_ref[...].astype(BF16)

    o_ref[...] = _dot_nt(h_ref[...], wbf_ref[...]).astype(o_ref.dtype)


def _proj_qkvg(h, w_in_t, tm=2048, tn=1024):
    t = h.shape[0]
    return pl.pallas_call(
        _proj_qkvg_kernel,
        grid=(QKVG_COLS // tn, t // tm),
        in_specs=[pl.BlockSpec((tm, D_MODEL), lambda n, m: (m, 0)),
                  pl.BlockSpec((tn, D_MODEL), lambda n, m: (n, 0))],
        out_specs=pl.BlockSpec((tm, tn), lambda n, m: (m, n)),
        out_shape=jax.ShapeDtypeStruct((t, QKVG_COLS), BF16),
        scratch_shapes=[pltpu.VMEM((tn, D_MODEL), BF16)],
        compiler_params=_cp(("arbitrary", "arbitrary"), 56),
        name="proj_qkvg",
    )(h, w_in_t)


def _proj_conv_kernel(h_ref, wb_ref, wc_ref, wx_ref, cw_ref, cb_ref, o_ref,
                      wbbf_ref, wcbf_ref, wxbf_ref, prev_ref):
    m = pl.program_id(1)

    @pl.when(m == 0)
    def _():
        prev_ref[...] = jnp.zeros_like(prev_ref)
        wbbf_ref[...] = wb_ref[...].astype(BF16)
        wcbf_ref[...] = wc_ref[...].astype(BF16)
        wxbf_ref[...] = wx_ref[...].astype(BF16)

    h = h_ref[...]
    tm = h.shape[0]
    row = lax.broadcasted_iota(I32, (tm, MXU_COLS), 0)
    for c in range(0, o_ref.shape[1], MXU_COLS):
        cols = slice(c, c + MXU_COLS)
        b = _dot_nt(h, wbbf_ref[cols, :])
        s = _dot_nt(h, wcbf_ref[cols, :]) * _dot_nt(h, wxbf_ref[cols, :])
        prev = prev_ref[:, cols]
        p1 = prev[SUBLANES - 1:SUBLANES, :]
        p2 = prev[SUBLANES - 2:SUBLANES - 1, :]
        s1 = jnp.where(row == 0, p1, pltpu.roll(s, 1, 0))
        s2 = jnp.where(row == 0, p2, jnp.where(row == 1, p1, pltpu.roll(s, 2, 0)))
        cw = cw_ref[:, cols]
        u = cw[2:3, :] * s + cw[1:2, :] * s1 + cw[0:1, :] * s2 + cb_ref[:, cols]
        o_ref[:, cols] = (b * u).astype(o_ref.dtype)
        prev_ref[:, cols] = s[tm - SUBLANES:, :]


def _proj_conv(h, w_in_t, conv_w, conv_b, tm=1024, tn=512):
    t = h.shape[0]
    return pl.pallas_call(
        _proj_conv_kernel,
        grid=(D_MODEL // tn, t // tm),
        in_specs=[pl.BlockSpec((tm, D_MODEL), lambda n, m: (m, 0))] +
                 [_w_rows_spec(CONV_COL0 + seg * D_MODEL, tn) for seg in range(3)] +
                 [pl.BlockSpec((CONV_K, tn), lambda n, m: (0, n)),
                  pl.BlockSpec((1, tn), lambda n, m: (0, n))],
        out_specs=pl.BlockSpec((tm, tn), lambda n, m: (m, n)),
        out_shape=jax.ShapeDtypeStruct((t, D_MODEL), BF16),
        scratch_shapes=[pltpu.VMEM((tn, D_MODEL), BF16), pltpu.VMEM((tn, D_MODEL), BF16),
                        pltpu.VMEM((tn, D_MODEL), BF16), pltpu.VMEM((SUBLANES, tn), F32)],
        compiler_params=_cp(("arbitrary", "arbitrary"), 48),
        name="proj_conv",
    )(h, w_in_t, w_in_t, w_in_t, conv_w, conv_b)


def _proj_gates_kernel(h_ref, w_ref, b_ref, o_ref, wbf_ref):
    @pl.when(pl.program_id(1) == 0)
    def _():
        wbf_ref[...] = w_ref[...].astype(BF16)

    h = h_ref[...]
    for c in range(0, o_ref.shape[1], MXU_COLS):
        cols = slice(c, c + MXU_COLS)
        o_ref[:, cols] = jax.nn.sigmoid(_dot_nt(h, wbf_ref[cols, :]) + b_ref[:, cols]).astype(o_ref.dtype)


def _proj_gates(h, w_in_t, b_merge, tm=2048, tn=1024):
    t = h.shape[0]
    return pl.pallas_call(
        _proj_gates_kernel,
        grid=(2 * D_MODEL // tn, t // tm),
        in_specs=[pl.BlockSpec((tm, D_MODEL), lambda n, m: (m, 0)),
                  _w_rows_spec(CONV_COL0 + 3 * D_MODEL, tn),
                  pl.BlockSpec((1, tn), lambda n, m: (0, n))],
        out_specs=pl.BlockSpec((tm, tn), lambda n, m: (m, n)),
        out_shape=jax.ShapeDtypeStruct((t, 2 * D_MODEL), BF16),
        scratch_shapes=[pltpu.VMEM((tn, D_MODEL), BF16)],
        compiler_params=_cp(("arbitrary", "arbitrary"), 56),
        name="proj_gates",
    )(h, w_in_t, b_merge)


def _gla_kernel(q_ref, k_ref, v_ref, g_ref, al_ref, wup_ref, bup_ref, nw_ref, o_ref,
                st_ref, b_ref, bl_ref, qd_ref, ki_ref, kd_ref, oi_ref, u_ref, *, n_chunks):
    c_len = GLA_CHUNK

    @pl.when(pl.program_id(1) == 0)
    def _():
        st_ref[...] = jnp.zeros_like(st_ref)

    row = lax.broadcasted_iota(I32, (c_len, c_len), 0)
    col = lax.broadcasted_iota(I32, (c_len, c_len), 1)
    causal = col <= row
    tril = jnp.where(causal, 1.0, 0.0).astype(BF16)

    def decays(h):
        kc = slice(h * GLA_DK, (h + 1) * GLA_DK)
        z = _dot(al_ref[...].astype(BF16), wup_ref[h]) + bup_ref[h]
        la = (jnp.minimum(z, 0.0) - jnp.log1p(jnp.exp(-jnp.abs(z)))) * (LOG2_E / GLA_GATE_NORM)
        la_hi, la_lo = _split_bf16(la)
        for c in range(n_chunks):
            r0 = c * c_len
            b = _dot(tril, la_hi[r0:r0 + c_len]) + _dot(tril, la_lo[r0:r0 + c_len])
            b_ref[pl.ds(r0, c_len), kc] = b
            bl_ref[pl.ds(r0, c_len), kc] = jnp.broadcast_to(b[c_len - 1:c_len, :], b.shape)
        b = b_ref[:, kc]
        q = q_ref[:, kc].astype(F32)
        k = k_ref[:, kc].astype(F32)
        qd_ref[:, kc] = (q * jnp.exp2(b)).astype(BF16)
        ki_ref[:, kc] = (k * jnp.exp2(-b)).astype(BF16)
        kd_ref[:, kc] = (k * jnp.exp2(bl_ref[:, kc] - b)).astype(BF16)

    def local_products(h):
        kc = slice(h * GLA_DK, (h + 1) * GLA_DK)
        vc = slice(h * GLA_DV, (h + 1) * GLA_DV)
        for c in range(n_chunks):
            sl = pl.ds(c * c_len, c_len)
            v = v_ref[sl, vc]
            att = jnp.where(causal, _dot_nt(qd_ref[sl, kc], ki_ref[sl, kc]), 0.0).astype(BF16)
            oi_ref[sl, vc] = _dot(att, v)
            u_ref[h, c] = _dot_tn(v, kd_ref[sl, kc])

    def recurrence(h):
        kc = slice(h * GLA_DK, (h + 1) * GLA_DK)
        vc = slice(h * GLA_DV, (h + 1) * GLA_DV)
        for c in range(n_chunks):
            sl = pl.ds(c * c_len, c_len)
            st = st_ref[h]
            oi_ref[sl, vc] = oi_ref[sl, vc] + _dot_nt(qd_ref[sl, kc], st.astype(BF16))
            st_ref[h] = st * jnp.exp2(bl_ref[pl.ds(c * c_len, 1), kc]) + u_ref[h, c]
        o = oi_ref[:, vc]
        o = o * lax.rsqrt(jnp.mean(o * o, axis=-1, keepdims=True) + EPS * GLA_DK) * nw_ref[h]
        g = g_ref[:, vc].astype(F32)
        o_ref[:, vc] = (o * (g * jax.nn.sigmoid(g))).astype(o_ref.dtype)

    decays(0)
    local_products(0)
    decays(1)
    recurrence(0)
    local_products(1)
    recurrence(1)


def _gla(qkvg, a_low, w_up, b_up, norm_w, tb=512):
    t = qkvg.shape[0]
    hp = 2
    dk, dv = hp * GLA_DK, hp * GLA_DV
    kq = GLA_KEY // dk
    kv = 2 * GLA_KEY // dv
    kg = kv + GLA_VAL // dv
    n_chunks = tb // GLA_CHUNK
    kern = functools.partial(_gla_kernel, n_chunks=n_chunks)
    return pl.pallas_call(
        kern,
        grid=(GLA_HEADS // hp, t // tb),
        in_specs=[pl.BlockSpec((tb, dk), lambda h, i: (i, h)),
                  pl.BlockSpec((tb, dk), lambda h, i: (i, kq + h)),
                  pl.BlockSpec((tb, dv), lambda h, i: (i, kv + h)),
                  pl.BlockSpec((tb, dv), lambda h, i: (i, kg + h)),
                  pl.BlockSpec((tb, LANES), lambda h, i: (i, 0)),
                  pl.BlockSpec((hp, LANES, GLA_DK), lambda h, i: (h, 0, 0)),
                  pl.BlockSpec((hp, 1, GLA_DK), lambda h, i: (h, 0, 0)),
                  pl.BlockSpec((hp, 1, GLA_DV), lambda h, i: (h, 0, 0))],
        out_specs=pl.BlockSpec((tb, dv), lambda h, i: (i, h)),
        out_shape=jax.ShapeDtypeStruct((t, GLA_VAL), BF16),
        scratch_shapes=[pltpu.VMEM((hp, GLA_DV, GLA_DK), F32),
                        pltpu.VMEM((tb, dk), F32), pltpu.VMEM((tb, dk), F32),
                        pltpu.VMEM((tb, dk), BF16), pltpu.VMEM((tb, dk), BF16),
                        pltpu.VMEM((tb, dk), BF16),
                        pltpu.VMEM((tb, dv), F32),
                        pltpu.VMEM((hp, n_chunks, GLA_DV, GLA_DK), F32)],
        compiler_params=_cp(("arbitrary", "arbitrary"), 40),
        name="gla",
    )(qkvg, qkvg, qkvg, qkvg, a_low, w_up, b_up, norm_w)


def _merge_kernel(oa_ref, ob_ref, wa_ref, wb_ref, g0_ref, g1_ref, o_ref, wabf_ref, wbbf_ref):
    @pl.when(pl.program_id(1) == 0)
    def _():
        wabf_ref[...] = wa_ref[...].astype(BF16)
        wbbf_ref[...] = wb_ref[...].astype(BF16)

    oa = oa_ref[...]
    ob = ob_ref[...]
    for c in range(0, o_ref.shape[1], MXU_COLS):
        cols = slice(c, c + MXU_COLS)
        a = _dot(oa, wabf_ref[:, cols])
        b = _dot(ob, wbbf_ref[:, cols])
        o_ref[:, cols] = (g0_ref[:, cols].astype(F32) * a + g1_ref[:, cols].astype(F32) * b).astype(o_ref.dtype)


def _merge(oa, ob, w_a, w_b, gates, tm=1024, tn=512):
    t = oa.shape[0]
    nb = D_MODEL // tn
    return pl.pallas_call(
        _merge_kernel,
        grid=(nb, t // tm),
        in_specs=[pl.BlockSpec((tm, D_MODEL), lambda n, m: (m, 0)),
                  pl.BlockSpec((tm, D_MODEL), lambda n, m: (m, 0)),
                  pl.BlockSpec((D_MODEL, tn), lambda n, m: (0, n)),
                  pl.BlockSpec((D_MODEL, tn), lambda n, m: (0, n)),
                  pl.BlockSpec((tm, tn), lambda n, m: (m, n)),
                  pl.BlockSpec((tm, tn), lambda n, m: (m, nb + n))],
        out_specs=pl.BlockSpec((tm, tn), lambda n, m: (m, n)),
        out_shape=jax.ShapeDtypeStruct((t, D_MODEL), BF16),
        scratch_shapes=[pltpu.VMEM((D_MODEL, tn), BF16), pltpu.VMEM((D_MODEL, tn), BF16)],
        compiler_params=_cp(("arbitrary", "arbitrary"), 56),
        name="merge",
    )(oa, ob, w_a, w_b, gates, gates)


def _mix_kernel(a_ref, w_ref, x_ref, o_ref, wbf_ref):
    @pl.when(pl.program_id(1) == 0)
    def _():
        wbf_ref[...] = w_ref[...].astype(BF16)

    a = a_ref[...]
    for c in range(0, o_ref.shape[1], MXU_COLS):
        cols = slice(c, c + MXU_COLS)
        o_ref[:, cols] = x_ref[:, cols] + _dot(a, wbf_ref[:, cols])


def _mix(mixed, w, x, tm=1024, tn=1024):
    t = x.shape[0]
    return pl.pallas_call(
        _mix_kernel,
        grid=(D_MODEL // tn, t // tm),
        in_specs=[pl.BlockSpec((tm, D_MODEL), lambda n, m: (m, 0)),
                  pl.BlockSpec((D_MODEL, tn), lambda n, m: (0, n)),
                  pl.BlockSpec((tm, tn), lambda n, m: (m, n))],
        out_specs=pl.BlockSpec((tm, tn), lambda n, m: (m, n)),
        out_shape=jax.ShapeDtypeStruct((t, D_MODEL), F32),
        scratch_shapes=[pltpu.VMEM((D_MODEL, tn), BF16)],
        compiler_params=_cp(("arbitrary", "arbitrary"), 52),
        name="mix",
    )(mixed, w, x)


def _route_kernel(x_ref, nw_ref, wr_ref, br_ref, hp_ref, ids_ref, wts_ref):
    h = _rms(x_ref[...], nw_ref[...])
    hp_ref[...] = _to_token_major(h.astype(BF16))
    h_hi, h_lo = _split_bf16(h)
    w_hi, w_lo = _split_bf16(wr_ref[...])
    logits = _dot_nt(w_hi, h_hi) + _dot_nt(w_hi, h_lo) + _dot_nt(w_lo, h_hi) + br_ref[...]
    tm = logits.shape[1]

    best = logits[0:1, :]
    gidx = jnp.zeros((1, tm), I32)
    for i in range(1, N_GROUPS):
        li = logits[i:i + 1, :]
        take = li > best
        best = jnp.where(take, li, best)
        gidx = jnp.where(take, i, gidx)
    gsum = jnp.zeros((1, tm), F32)
    for i in range(N_GROUPS):
        gsum = gsum + jnp.exp(logits[i:i + 1, :] - best)
    g_p = 1.0 / gsum

    sel = logits[EXPERT_ROW0:EXPERT_ROW0 + EXPERTS_PER_GROUP, :]
    for g in range(1, N_GROUPS):
        r0 = EXPERT_ROW0 + g * EXPERTS_PER_GROUP
        sel = jnp.where(gidx == g, logits[r0:r0 + EXPERTS_PER_GROUP, :], sel)
    eio = lax.broadcasted_iota(I32, sel.shape, 0)
    m1 = jnp.max(sel, axis=0, keepdims=True)
    i1 = jnp.min(jnp.where(sel == m1, eio, EXPERTS_PER_GROUP), axis=0, keepdims=True)
    rest = jnp.where(eio == i1, -jnp.inf, sel)
    m2 = jnp.max(rest, axis=0, keepdims=True)
    i2 = jnp.min(jnp.where(rest == m2, eio, EXPERTS_PER_GROUP), axis=0, keepdims=True)
    p2 = jnp.exp(m2 - m1)
    w1 = g_p / (1.0 + p2)
    w2 = g_p * p2 / (1.0 + p2)
    e1 = gidx * EXPERTS_PER_GROUP + i1
    e2 = gidx * EXPERTS_PER_GROUP + i2
    rio = lax.broadcasted_iota(I32, (SUBLANES, tm), 0)
    ids_ref[...] = jnp.where(rio == 0, e1, jnp.where(rio == 1, e2, 0))
    wts_ref[...] = jnp.where(rio == 0, w1, jnp.where(rio == 1, w2, 0.0))


def _route(x1, norm_w, wr_t, br, tm=256):
    t = x1.shape[0]
    return pl.pallas_call(
        _route_kernel,
        grid=(t // tm,),
        in_specs=[pl.BlockSpec((tm, D_MODEL), lambda i: (i, 0)),
                  pl.BlockSpec((1, D_MODEL), lambda i: (0, 0)),
                  pl.BlockSpec((ROUTER_ROWS, D_MODEL), lambda i: (0, 0)),
                  pl.BlockSpec((ROUTER_ROWS, 1), lambda i: (0, 0))],
        out_specs=[pl.BlockSpec((tm, ROW_TILES, LANES), lambda i: (i, 0, 0)),
                   pl.BlockSpec((SUBLANES, tm), lambda i: (0, i)),
                   pl.BlockSpec((SUBLANES, tm), lambda i: (0, i))],
        out_shape=[jax.ShapeDtypeStruct((t, ROW_TILES, LANES), BF16),
                   jax.ShapeDtypeStruct((SUBLANES, t), I32),
                   jax.ShapeDtypeStruct((SUBLANES, t), F32)],
        compiler_params=_cp(("arbitrary",), 32),
        name="route",
    )(x1, norm_w, wr_t, br)


def _plan_kernel(ids_ref, dest_ref, cnt_ref, base_ref):
    phase = pl.program_id(0)
    step = pl.program_id(1)
    tm = ids_ref.shape[1]
    eio = lax.broadcasted_iota(I32, (N_EXPERTS, tm), 0)
    ids = ids_ref[...]
    oh = [jnp.where(eio == ids[k:k + 1, :], 1.0, 0.0) for k in range(2)]

    @pl.when((phase == 0) & (step == 0))
    def _():
        base_ref[...] = jnp.zeros_like(base_ref)

    @pl.when(phase == 0)
    def _():
        cnt = jnp.sum(oh[0] + oh[1], axis=1, keepdims=True)
        base_ref[...] = base_ref[...] + cnt
        dest_ref[0] = jnp.zeros(dest_ref.shape[1:], I32)
        cnt_ref[...] = base_ref[...]

    @pl.when((phase == 1) & (step == 0))
    def _():
        tiles = jnp.floor((base_ref[...] + (EXPERT_TILE - 1)) * (1.0 / EXPERT_TILE))
        r = lax.broadcasted_iota(I32, (N_EXPERTS, N_EXPERTS), 0)
        c = lax.broadcasted_iota(I32, (N_EXPERTS, N_EXPERTS), 1)
        lower = jnp.where(c < r, 1.0, 0.0).astype(BF16)
        base_ref[...] = _dot(lower, tiles.astype(BF16)) * float(EXPERT_TILE)

    @pl.when(phase == 1)
    def _():
        r = lax.broadcasted_iota(I32, (tm, tm), 0)
        c = lax.broadcasted_iota(I32, (tm, tm), 1)
        upper = jnp.where(r <= c, 1.0, 0.0).astype(BF16)
        base = base_ref[...][:, 0:1]
        rows = []
        for k in range(2):
            cum = _dot(oh[k].astype(BF16), upper)
            rows.append(jnp.sum(oh[k] * (cum - 1.0 + base), axis=0, keepdims=True))
            base = base + cum[:, tm - 1:tm]
        base_ref[...] = jnp.broadcast_to(base, base_ref.shape)
        rio = lax.broadcasted_iota(I32, (SUBLANES, tm), 0)
        d0 = rows[0].astype(I32)
        d1 = rows[1].astype(I32)
        dest_ref[0] = jnp.where(rio == 0, d0, jnp.where(rio == 1, d1, 0))


def _plan(ids, tm=512):
    t = ids.shape[1]
    return pl.pallas_call(
        _plan_kernel,
        grid=(2, t // tm),
        in_specs=[pl.BlockSpec((SUBLANES, tm), lambda p, i: (0, i))],
        out_specs=[pl.BlockSpec((1, SUBLANES, tm), lambda p, i: (p, 0, i)),
                   pl.BlockSpec((N_EXPERTS, LANES), lambda p, i: (0, 0))],
        out_shape=[jax.ShapeDtypeStruct((2, SUBLANES, t), I32),
                   jax.ShapeDtypeStruct((N_EXPERTS, LANES), F32)],
        scratch_shapes=[pltpu.VMEM((N_EXPERTS, LANES), F32)],
        compiler_params=_cp(("arbitrary", "arbitrary"), 32),
        name="plan",
    )(ids)


def _dispatch_kernel(dest_ref, hp_ref, zero_ref, xs_ref, slot_ref, sem, *, tm, t_total):
    del zero_ref
    base = pl.program_id(0) * tm

    @pl.when(pl.program_id(0) == 0)
    def _():
        def fill(j, c):
            for u in range(DMA_UNROLL):
                slot_ref[j * DMA_UNROLL + u] = 2 * t_total
            return c
        lax.fori_loop(0, slot_ref.shape[0] // DMA_UNROLL, fill, 0)

    def row_copy(k, tok):
        return pltpu.make_async_copy(hp_ref.at[tok], xs_ref.at[dest_ref[k * t_total + tok]], sem)

    def issue(j, c):
        for u in range(DMA_UNROLL):
            tok = base + j * DMA_UNROLL + u
            for k in range(2):
                row_copy(k, tok).start()
                slot_ref[dest_ref[k * t_total + tok]] = k * t_total + tok
        return c

    def drain(j, c):
        for u in range(DMA_UNROLL):
            row_copy(0, base + j * DMA_UNROLL + u).wait()
            row_copy(1, base + j * DMA_UNROLL + u).wait()
        return c

    lax.fori_loop(0, tm // DMA_UNROLL, issue, 0)
    lax.fori_loop(0, tm // DMA_UNROLL, drain, 0)


def _dispatch(dest_flat, hp3, n_rows, tm=512):
    t = hp3.shape[0]
    zeros = jnp.zeros((n_rows, ROW_TILES, LANES), BF16)
    kern = functools.partial(_dispatch_kernel, tm=tm, t_total=t)
    return pl.pallas_call(
        kern,
        grid_spec=pltpu.PrefetchScalarGridSpec(
            num_scalar_prefetch=1,
            grid=(t // tm,),
            in_specs=[pl.BlockSpec(memory_space=pl.ANY), pl.BlockSpec(memory_space=pl.ANY)],
            out_specs=[pl.BlockSpec(memory_space=pl.ANY), pl.BlockSpec(memory_space=pltpu.SMEM)],
            scratch_shapes=[pltpu.SemaphoreType.DMA(())]),
        out_shape=[jax.ShapeDtypeStruct((n_rows, ROW_TILES, LANES), BF16),
                   jax.ShapeDtypeStruct((n_rows,), I32)],
        input_output_aliases={2: 0},
        compiler_params=_cp(("arbitrary",), 16, has_side_effects=True),
        name="dispatch",
    )(dest_flat, hp3, zeros)


def _experts_kernel(te_ref, nx_ref, sl_ref, vr_ref, rs_ref, na_ref, xs_ref, wg_hbm, wu_hbm, wd_hbm, yt_hbm,
                    wgf_ref, wuf_ref, wdf_ref, wgbf_ref, wubf_ref, wdbf_ref, ybuf_ref, sems, ysems):
    i = pl.program_id(0)
    n_act = na_ref[0]
    active = i < n_act
    changed = (i == 0) | (te_ref[i] != te_ref[jnp.maximum(i - 1, 0)])

    def weight_copies(e, slot):
        return (pltpu.make_async_copy(wg_hbm.at[e], wgf_ref.at[slot], sems.at[slot, 0]),
                pltpu.make_async_copy(wu_hbm.at[e], wuf_ref.at[slot], sems.at[slot, 1]),
                pltpu.make_async_copy(wd_hbm.at[e], wdf_ref.at[slot], sems.at[slot, 2]))

    def for_rows(tile, fn):
        slot = tile % 2
        n = vr_ref[tile]

        def row_copy(r):
            dst = yt_hbm.at[rs_ref[tile * EXPERT_TILE + r]]
            return pltpu.make_async_copy(ybuf_ref.at[slot, r], dst, ysems.at[slot])

        groups = lax.shift_right_logical(n, DMA_UNROLL.bit_length() - 1)

        def body(j, c):
            for u in range(DMA_UNROLL):
                fn(row_copy(j * DMA_UNROLL + u))
            return c

        def tail(r, c):
            fn(row_copy(r))
            return c

        lax.fori_loop(0, groups, body, 0)
        lax.fori_loop(groups * DMA_UNROLL, n, tail, 0)

    @pl.when(i == 0)
    def _():
        for cp in weight_copies(te_ref[0], 0):
            cp.start()

    @pl.when(active & changed)
    def _():
        slot = sl_ref[i]
        nxt = nx_ref[i]

        @pl.when(nxt >= 0)
        def _():
            for cp in weight_copies(nxt, 1 - slot):
                cp.start()

        for cp in weight_copies(te_ref[i], slot):
            cp.wait()
        wgbf_ref[...] = wgf_ref[slot].astype(BF16)
        wubf_ref[...] = wuf_ref[slot].astype(BF16)
        wdbf_ref[...] = wdf_ref[slot].astype(BF16)

    @pl.when(active)
    def _():
        x = _from_token_major(xs_ref[...])
        hg = _dot(x, wgbf_ref[...])
        hu = _dot(x, wubf_ref[...])
        act = (hg * jax.nn.sigmoid(hg) * hu).astype(BF16)
        ybuf_ref[i % 2] = _to_token_major(_dot(act, wdbf_ref[...]).astype(BF16))

    @pl.when((i >= 1) & (i - 1 < n_act))
    def _():
        for_rows(i - 1, lambda cp: cp.wait())

    @pl.when(active)
    def _():
        for_rows(i, lambda cp: cp.start())

    @pl.when(active & (i == pl.num_programs(0) - 1))
    def _():
        for_rows(i, lambda cp: cp.wait())


def _experts(tile_e, next_e, slot, valid, row_slot, n_act, xs, w_gate, w_up, w_down, n_tiles, n_slots):
    def row_map(i, te, nx, sl, vr, rs, na):
        return (jnp.minimum(i, na[0] - 1), 0, 0)

    return pl.pallas_call(
        _experts_kernel,
        grid_spec=pltpu.PrefetchScalarGridSpec(
            num_scalar_prefetch=6,
            grid=(n_tiles,),
            in_specs=[pl.BlockSpec((EXPERT_TILE, ROW_TILES, LANES), row_map),
                      pl.BlockSpec(memory_space=pl.ANY),
                      pl.BlockSpec(memory_space=pl.ANY),
                      pl.BlockSpec(memory_space=pl.ANY)],
            out_specs=pl.BlockSpec(memory_space=pl.ANY),
            scratch_shapes=[pltpu.VMEM((2, D_MODEL, D_FF), F32),
                            pltpu.VMEM((2, D_MODEL, D_FF), F32),
                            pltpu.VMEM((2, D_FF, D_MODEL), F32),
                            pltpu.VMEM((D_MODEL, D_FF), BF16),
                            pltpu.VMEM((D_MODEL, D_FF), BF16),
                            pltpu.VMEM((D_FF, D_MODEL), BF16),
                            pltpu.VMEM((2, EXPERT_TILE, ROW_TILES, LANES), BF16),
                            pltpu.SemaphoreType.DMA((2, 3)),
                            pltpu.SemaphoreType.DMA((2,))]),
        out_shape=jax.ShapeDtypeStruct((n_slots, ROW_TILES, LANES), BF16),
        compiler_params=_cp(("arbitrary",), 48, has_side_effects=True),
        name="experts",
    )(tile_e, next_e, slot, valid, row_slot, n_act, xs, w_gate, w_up, w_down)


def _combine_kernel(y0_ref, y1_ref, x_ref, wt_ref, p_ref, nw_ref, wg_ref, wp_ref, fw_ref, o_ref):
    wt = wt_ref[...]
    y0 = _from_token_major(y0_ref[...]).astype(F32)
    y1 = _from_token_major(y1_ref[...]).astype(F32)
    x2 = x_ref[...] + wt[:, 0:1] * y0 + wt[:, 1:2] * y1
    hn = _rms(x2, nw_ref[...]).astype(BF16)
    pg = jax.nn.sigmoid(_dot(hn, wg_ref[...]))
    x3 = x2 + pg * _dot(p_ref[...].astype(BF16), wp_ref[...])
    o_ref[...] = _rms(x3, fw_ref[...])


def _combine_ple(yt, x1, wts_t, p, ple_norm_w, w_gate, w_proj, final_w, tm=256):
    t = x1.shape[0]
    nb = t // tm
    row = lambda i: (i, 0)
    fix = lambda i: (0, 0)
    return pl.pallas_call(
        _combine_kernel,
        grid=(nb,),
        in_specs=[pl.BlockSpec((tm, ROW_TILES, LANES), lambda i: (i, 0, 0)),
                  pl.BlockSpec((tm, ROW_TILES, LANES), lambda i: (nb + i, 0, 0)),
                  pl.BlockSpec((tm, D_MODEL), row),
                  pl.BlockSpec((tm, SUBLANES), row),
                  pl.BlockSpec((tm, PLE_DIM), row),
                  pl.BlockSpec((1, D_MODEL), fix),
                  pl.BlockSpec((D_MODEL, D_MODEL), fix),
                  pl.BlockSpec((PLE_DIM, D_MODEL), fix),
                  pl.BlockSpec((1, D_MODEL), fix)],
        out_specs=pl.BlockSpec((tm, D_MODEL), row),
        out_shape=jax.ShapeDtypeStruct((t, D_MODEL), F32),
        compiler_params=_cp(("arbitrary",), 48),
        name="combine_ple",
    )(yt, yt, x1, wts_t, p, ple_norm_w, w_gate, w_proj, final_w)


def _tile_table(counts, n_tiles):
    tiles = (counts.astype(I32) + (EXPERT_TILE - 1)) // EXPERT_TILE
    ends = jnp.cumsum(tiles)
    n_act = ends[-1]
    idx = jnp.minimum(jnp.arange(n_tiles, dtype=I32), n_act - 1)
    tile_e = jnp.sum((idx[:, None] >= ends[None, :]).astype(I32), axis=1).astype(I32)
    run_end = ends[tile_e]
    next_e = jnp.where(run_end < n_act, tile_e[jnp.minimum(run_end, n_tiles - 1)], -1).astype(I32)
    new_run = jnp.concatenate([jnp.ones((1,), I32), (tile_e[1:] != tile_e[:-1]).astype(I32)])
    slot = ((jnp.cumsum(new_run) - 1) % 2).astype(I32)
    arange = jnp.arange(n_tiles, dtype=I32)
    last = arange == run_end - 1
    valid = jnp.where(last, counts.astype(I32)[tile_e] - (tiles[tile_e] - 1) * EXPERT_TILE, EXPERT_TILE)
    valid = jnp.where(arange < n_act, valid, 0).astype(I32)
    return tile_e, next_e, slot, valid, n_act.reshape(1).astype(I32)


def _block(x, p, norm_mix_w, w_in, b_merge, w_alpha_up, b_alpha_up, gla_norm_w, w_gla_out,
           conv_w, conv_b, w_conv_out, w_mix_out, norm_ffn_w, w_router_group, b_router_group,
           w_router_expert, b_router_expert, w_e_gate, w_e_up, w_e_down, ple_norm_w,
           w_ple_gate, w_ple_proj, final_norm_w):
    t = x.shape[0]
    n_tiles = (2 * t) // EXPERT_TILE + N_EXPERTS
    n_rows = n_tiles * EXPERT_TILE

    w_up = jnp.pad(w_alpha_up, ((0, LANES - GLA_GATE_RANK), (0, 0))).astype(BF16)
    w_up = w_up.reshape(LANES, GLA_HEADS, GLA_DK).transpose(1, 0, 2)
    b_up = b_alpha_up.reshape(GLA_HEADS, 1, GLA_DK)
    gnw = gla_norm_w.reshape(GLA_HEADS, 1, GLA_DV)
    wr_t = jnp.zeros((ROUTER_ROWS, D_MODEL), F32)
    wr_t = wr_t.at[0:N_GROUPS].set(w_router_group.T)
    wr_t = wr_t.at[EXPERT_ROW0:EXPERT_ROW0 + N_EXPERTS].set(w_router_expert.T)
    br = jnp.zeros((ROUTER_ROWS, 1), F32)
    br = br.at[0:N_GROUPS, 0].set(b_router_group)
    br = br.at[EXPERT_ROW0:EXPERT_ROW0 + N_EXPERTS, 0].set(b_router_expert)

    w_in_t = w_in.T
    h, a_low = _norm_in(x, norm_mix_w.reshape(1, D_MODEL), w_in_t)
    qkvg = _proj_qkvg(h, w_in_t)
    ob = _proj_conv(h, w_in_t, conv_w, conv_b.reshape(1, D_MODEL))
    gates = _proj_gates(h, w_in_t, b_merge.reshape(1, 2 * D_MODEL))
    oa = _gla(qkvg, a_low, w_up, b_up, gnw)
    mixed = _merge(oa, ob, w_gla_out, w_conv_out, gates)
    x1 = _mix(mixed, w_mix_out, x)

    hp, ids, wts = _route(x1, norm_ffn_w.reshape(1, D_MODEL), wr_t, br)
    dest, counts = _plan(ids)
    dest_flat = dest[1, 0:2].reshape(2 * t)
    tile_e, next_e, slot, valid, n_act = _tile_table(counts[:, 0], n_tiles)
    xs, row_slot = _dispatch(dest_flat, hp, n_rows)
    yt = _experts(tile_e, next_e, slot, valid, row_slot, n_act, xs, w_e_gate, w_e_up, w_e_down, n_tiles, 2 * t)
    return _combine_ple(yt, x1, wts.T, p,
                        ple_norm_w.reshape(1, D_MODEL), w_ple_gate.astype(BF16),
                        w_ple_proj.astype(BF16), final_norm_w.reshape(1, D_MODEL))


def kernel(x, p, norm_mix_w, w_in, b_merge, w_alpha_up, b_alpha_up, gla_norm_w, w_gla_out, conv_w, conv_b, w_conv_out, w_mix_out, norm_ffn_w, w_router_group, b_router_group, w_router_expert, b_router_expert, w_e_gate, w_e_up, w_e_down, ple_norm_w, w_ple_gate, w_ple_proj, final_norm_w):
    depth, batch = p.shape[0], x.shape[0]
    assert depth == 1 and batch == 1, "kernel is specialised to one layer and one sequence"
    out = _block(x[0], p[0, 0], norm_mix_w[0], w_in[0], b_merge[0], w_alpha_up[0], b_alpha_up[0],
                 gla_norm_w[0], w_gla_out[0], conv_w[0], conv_b[0], w_conv_out[0], w_mix_out[0],
                 norm_ffn_w[0], w_router_group[0], b_router_group[0], w_router_expert[0],
                 b_router_expert[0], w_e_gate[0], w_e_up[0], w_e_down[0], ple_norm_w[0],
                 w_ple_gate[0], w_ple_proj[0], final_norm_w)
    return out[None]
```

```python
import functools

import jax
import jax.numpy as jnp
from jax import lax
from jax.experimental import pallas as pl
from jax.experimental.pallas import tpu as pltpu

F32 = jnp.float32
BF16 = jnp.bfloat16
I32 = jnp.int32

D_MODEL = 2048
PLE_DIM = 256
EPS = 1e-6
LOG2_E = 1.4426950408889634
GLA_HEADS = 4
GLA_DK = 256
GLA_DV = 512
GLA_KEY = GLA_HEADS * GLA_DK
GLA_VAL = GLA_HEADS * GLA_DV
GLA_GATE_RANK = 16
GLA_GATE_NORM = 16.0
GLA_CHUNK = 64
CONV_K = 3
N_GROUPS = 4
EXPERTS_PER_GROUP = 8
N_EXPERTS = N_GROUPS * EXPERTS_PER_GROUP
D_FF = 512

QKVG_COLS = 2 * GLA_KEY + 2 * GLA_VAL
CONV_COL0 = QKVG_COLS + GLA_GATE_RANK

LANES = 128
SUBLANES = 8
MXU_COLS = 256
ROW_TILES = D_MODEL // LANES

EXPERT_TILE = 256
DMA_UNROLL = 8
FILL_UNROLL = 32
ROUTER_ROWS = 64
EXPERT_ROW0 = 8

MIB = 1024 * 1024


def _cp(sem, vmem_mib, **kw):
    return pltpu.CompilerParams(dimension_semantics=sem, vmem_limit_bytes=int(vmem_mib * MIB), **kw)


def _rms(x, w):
    return x * lax.rsqrt(jnp.mean(x * x, axis=-1, keepdims=True) + EPS) * w


def _dot(a, b):
    return jnp.dot(a, b, preferred_element_type=F32)


def _dot_nt(a, b):
    return lax.dot_general(a, b, (((1,), (1,)), ((), ())), preferred_element_type=F32)


def _dot_tn(a, b):
    return lax.dot_general(a, b, (((0,), (0,)), ((), ())), preferred_element_type=F32)


def _split_bf16(x):
    hi = x.astype(BF16)
    lo = (x - hi.astype(F32)).astype(BF16)
    return hi, lo


def _to_token_major(val):
    return val.reshape(val.shape[0], ROW_TILES, LANES)


def _from_token_major(val):
    return val.reshape(val.shape[0], D_MODEL)


def _norm_in_kernel(x_ref, w_ref, wal_ref, h_ref, al_ref, walbf_ref):
    @pl.when(pl.program_id(0) == 0)
    def _():
        walbf_ref[...] = wal_ref[...].astype(BF16)

    h = _rms(x_ref[...], w_ref[...]).astype(BF16)
    h_ref[...] = h
    al_ref[...] = _dot_nt(h, walbf_ref[...])


def _norm_in(x, w, w_in_t, tm=512):
    t = x.shape[0]
    return pl.pallas_call(
        _norm_in_kernel,
        grid=(t // tm,),
        in_specs=[pl.BlockSpec((tm, D_MODEL), lambda i: (i, 0)),
                  pl.BlockSpec((1, D_MODEL), lambda i: (0, 0)),
                  pl.BlockSpec((LANES, D_MODEL), lambda i: (QKVG_COLS // LANES, 0))],
        out_specs=[pl.BlockSpec((tm, D_MODEL), lambda i: (i, 0)),
                   pl.BlockSpec((tm, LANES), lambda i: (i, 0))],
        out_shape=[jax.ShapeDtypeStruct((t, D_MODEL), BF16),
                   jax.ShapeDtypeStruct((t, LANES), F32)],
        scratch_shapes=[pltpu.VMEM((LANES, D_MODEL), BF16)],
        compiler_params=_cp(("arbitrary",), 32),
        name="norm_in",
    )(x, w, w_in_t)


def _w_rows_spec(row0, tn):
    assert row0 % SUBLANES == 0 and tn % SUBLANES == 0
    return pl.BlockSpec((pl.Element(tn), pl.Element(D_MODEL)),
                        lambda n, m: (pl.multiple_of(row0 + n * tn, SUBLANES), 0))


def _proj_qkvg_kernel(h_ref, w_ref, o_ref, wbf_ref):
    @pl.when(pl.program_id(1) == 0)
    def _():
        wbf_ref[...] = w_ref[...].astype(BF16)

    o_ref[...] = _dot_nt(h_ref[...], wbf_ref[...]).astype(o_ref.dtype)


def _proj_qkvg(h, w_in_t, tm=2048, tn=1024):
    t = h.shape[0]
    return pl.pallas_call(
        _proj_qkvg_kernel,
        grid=(QKVG_COLS // tn, t // tm),
        in_specs=[pl.BlockSpec((tm, D_MODEL), lambda n, m: (m, 0)),
                  pl.BlockSpec((tn, D_MODEL), lambda n, m: (n, 0))],
        out_specs=pl.BlockSpec((tm, tn), lambda n, m: (m, n)),
        out_shape=jax.ShapeDtypeStruct((t, QKVG_COLS), BF16),
        scratch_shapes=[pltpu.VMEM((tn, D_MODEL), BF16)],
        compiler_params=_cp(("arbitrary", "arbitrary"), 56),
        name="proj_qkvg",
    )(h, w_in_t)


def _proj_conv_kernel(h_ref, wb_ref, wc_ref, wx_ref, cw_ref, cb_ref, o_ref,
                      wbbf_ref, wcbf_ref, wxbf_ref, prev_ref):
    m = pl.program_id(1)

    @pl.when(m == 0)
    def _():
        prev_ref[...] = jnp.zeros_like(prev_ref)
        wbbf_ref[...] = wb_ref[...].astype(BF16)
        wcbf_ref[...] = wc_ref[...].astype(BF16)
        wxbf_ref[...] = wx_ref[...].astype(BF16)

    h = h_ref[...]
    tm = h.shape[0]
    row = lax.broadcasted_iota(I32, (tm, MXU_COLS), 0)
    for c in range(0, o_ref.shape[1], MXU_COLS):
        cols = slice(c, c + MXU_COLS)
        b = _dot_nt(h, wbbf_ref[cols, :])
        s = _dot_nt(h, wcbf_ref[cols, :]) * _dot_nt(h, wxbf_ref[cols, :])
        prev = prev_ref[:, cols]
        p1 = prev[SUBLANES - 1:SUBLANES, :]
        p2 = prev[SUBLANES - 2:SUBLANES - 1, :]
        s1 = jnp.where(row == 0, p1, pltpu.roll(s, 1, 0))
        s2 = jnp.where(row == 0, p2, jnp.where(row == 1, p1, pltpu.roll(s, 2, 0)))
        cw = cw_ref[:, cols]
        u = cw[2:3, :] * s + cw[1:2, :] * s1 + cw[0:1, :] * s2 + cb_ref[:, cols]
        o_ref[:, cols] = (b * u).astype(o_ref.dtype)
        prev_ref[:, cols] = s[tm - SUBLANES:, :]


def _proj_conv(h, w_in_t, conv_w, conv_b, tm=1024, tn=512):
    t = h.shape[0]
    return pl.pallas_call(
        _proj_conv_kernel,
        grid=(D_MODEL // tn, t // tm),
        in_specs=[pl.BlockSpec((tm, D_MODEL), lambda n, m: (m, 0))] +
                 [_w_rows_spec(CONV_COL0 + seg * D_MODEL, tn) for seg in range(3)] +
                 [pl.BlockSpec((CONV_K, tn), lambda n, m: (0, n)),
                  pl.BlockSpec((1, tn), lambda n, m: (0, n))],
        out_specs=pl.BlockSpec((tm, tn), lambda n, m: (m, n)),
        out_shape=jax.ShapeDtypeStruct((t, D_MODEL), BF16),
        scratch_shapes=[pltpu.VMEM((tn, D_MODEL), BF16), pltpu.VMEM((tn, D_MODEL), BF16),
                        pltpu.VMEM((tn, D_MODEL), BF16), pltpu.VMEM((SUBLANES, tn), F32)],
        compiler_params=_cp(("arbitrary", "arbitrary"), 48),
        name="proj_conv",
    )(h, w_in_t, w_in_t, w_in_t, conv_w, conv_b)


def _proj_gates_kernel(h_ref, w_ref, b_ref, o_ref, wbf_ref):
    @pl.when(pl.program_id(1) == 0)
    def _():
        wbf_ref[...] = w_ref[...].astype(BF16)

    h = h_ref[...]
    for c in range(0, o_ref.shape[1], MXU_COLS):
        cols = slice(c, c + MXU_COLS)
        o_ref[:, cols] = jax.nn.sigmoid(_dot_nt(h, wbf_ref[cols, :]) + b_ref[:, cols]).astype(o_ref.dtype)


def _proj_gates(h, w_in_t, b_merge, tm=2048, tn=1024):
    t = h.shape[0]
    return pl.pallas_call(
        _proj_gates_kernel,
        grid=(2 * D_MODEL // tn, t // tm),
        in_specs=[pl.BlockSpec((tm, D_MODEL), lambda n, m: (m, 0)),
                  _w_rows_spec(CONV_COL0 + 3 * D_MODEL, tn),
                  pl.BlockSpec((1, tn), lambda n, m: (0, n))],
        out_specs=pl.BlockSpec((tm, tn), lambda n, m: (m, n)),
        out_shape=jax.ShapeDtypeStruct((t, 2 * D_MODEL), BF16),
        scratch_shapes=[pltpu.VMEM((tn, D_MODEL), BF16)],
        compiler_params=_cp(("arbitrary", "arbitrary"), 56),
        name="proj_gates",
    )(h, w_in_t, b_merge)


def _gla_kernel(q_ref, k_ref, v_ref, g_ref, al_ref, wup_ref, bup_ref, nw_ref, o_ref,
                st_ref, b_ref, bl_ref, qd_ref, ki_ref, kd_ref, oi_ref, u_ref, *, n_chunks):
    c_len = GLA_CHUNK

    @pl.when(pl.program_id(1) == 0)
    def _():
        st_ref[...] = jnp.zeros_like(st_ref)

    row = lax.broadcasted_iota(I32, (c_len, c_len), 0)
    col = lax.broadcasted_iota(I32, (c_len, c_len), 1)
    causal = col <= row
    tril = jnp.where(causal, 1.0, 0.0).astype(BF16)

    def decays(h):
        kc = slice(h * GLA_DK, (h + 1) * GLA_DK)
        z = _dot(al_ref[...].astype(BF16), wup_ref[h]) + bup_ref[h]
        la = (jnp.minimum(z, 0.0) - jnp.log1p(jnp.exp(-jnp.abs(z)))) * (LOG2_E / GLA_GATE_NORM)
        la_hi, la_lo = _split_bf16(la)
        for c in range(n_chunks):
            r0 = c * c_len
            b = _dot(tril, la_hi[r0:r0 + c_len]) + _dot(tril, la_lo[r0:r0 + c_len])
            b_ref[pl.ds(r0, c_len), kc] = b
            bl_ref[pl.ds(r0, c_len), kc] = jnp.broadcast_to(b[c_len - 1:c_len, :], b.shape)
        b = b_ref[:, kc]
        q = q_ref[:, kc].astype(F32)
        k = k_ref[:, kc].astype(F32)
        qd_ref[:, kc] = (q * jnp.exp2(b)).astype(BF16)
        ki_ref[:, kc] = (k * jnp.exp2(-b)).astype(BF16)
        kd_ref[:, kc] = (k * jnp.exp2(bl_ref[:, kc] - b)).astype(BF16)

    def local_products(h):
        kc = slice(h * GLA_DK, (h + 1) * GLA_DK)
        vc = slice(h * GLA_DV, (h + 1) * GLA_DV)
        for c in range(n_chunks):
            sl = pl.ds(c * c_len, c_len)
            v = v_ref[sl, vc]
            att = jnp.where(causal, _dot_nt(qd_ref[sl, kc], ki_ref[sl, kc]), 0.0).astype(BF16)
            oi_ref[sl, vc] = _dot(att, v)
            u_ref[h, c] = _dot_tn(v, kd_ref[sl, kc])

    def recurrence(h):
        kc = slice(h * GLA_DK, (h + 1) * GLA_DK)
        vc = slice(h * GLA_DV, (h + 1) * GLA_DV)
        for c in range(n_chunks):
            sl = pl.ds(c * c_len, c_len)
            st = st_ref[h]
            oi_ref[sl, vc] = oi_ref[sl, vc] + _dot_nt(qd_ref[sl, kc], st.astype(BF16))
            st_ref[h] = st * jnp.exp2(bl_ref[pl.ds(c * c_len, 1), kc]) + u_ref[h, c]
        o = oi_ref[:, vc]
        o = o * lax.rsqrt(jnp.mean(o * o, axis=-1, keepdims=True) + EPS * GLA_DK) * nw_ref[h]
        g = g_ref[:, vc].astype(F32)
        o_ref[:, vc] = (o * (g * jax.nn.sigmoid(g))).astype(o_ref.dtype)

    decays(0)
    local_products(0)
    decays(1)
    recurrence(0)
    local_products(1)
    recurrence(1)


def _gla(qkvg, a_low, w_up, b_up, norm_w, tb=512):
    t = qkvg.shape[0]
    hp = 2
    dk, dv = hp * GLA_DK, hp * GLA_DV
    kq = GLA_KEY // dk
    kv = 2 * GLA_KEY // dv
    kg = kv + GLA_VAL // dv
    n_chunks = tb // GLA_CHUNK
    kern = functools.partial(_gla_kernel, n_chunks=n_chunks)
    return pl.pallas_call(
        kern,
        grid=(GLA_HEADS // hp, t // tb),
        in_specs=[pl.BlockSpec((tb, dk), lambda h, i: (i, h)),
                  pl.BlockSpec((tb, dk), lambda h, i: (i, kq + h)),
                  pl.BlockSpec((tb, dv), lambda h, i: (i, kv + h)),
                  pl.BlockSpec((tb, dv), lambda h, i: (i, kg + h)),
                  pl.BlockSpec((tb, LANES), lambda h, i: (i, 0)),
                  pl.BlockSpec((hp, LANES, GLA_DK), lambda h, i: (h, 0, 0)),
                  pl.BlockSpec((hp, 1, GLA_DK), lambda h, i: (h, 0, 0)),
                  pl.BlockSpec((hp, 1, GLA_DV), lambda h, i: (h, 0, 0))],
        out_specs=pl.BlockSpec((tb, dv), lambda h, i: (i, h)),
        out_shape=jax.ShapeDtypeStruct((t, GLA_VAL), BF16),
        scratch_shapes=[pltpu.VMEM((hp, GLA_DV, GLA_DK), F32),
                        pltpu.VMEM((tb, dk), F32), pltpu.VMEM((tb, dk), F32),
                        pltpu.VMEM((tb, dk), BF16), pltpu.VMEM((tb, dk), BF16),
                        pltpu.VMEM((tb, dk), BF16),
                        pltpu.VMEM((tb, dv), F32),
                        pltpu.VMEM((hp, n_chunks, GLA_DV, GLA_DK), F32)],
        compiler_params=_cp(("arbitrary", "arbitrary"), 40),
        name="gla",
    )(qkvg, qkvg, qkvg, qkvg, a_low, w_up, b_up, norm_w)


def _merge_kernel(oa_ref, ob_ref, wa_ref, wb_ref, g0_ref, g1_ref, o_ref, wabf_ref, wbbf_ref):
    @pl.when(pl.program_id(1) == 0)
    def _():
        wabf_ref[...] = wa_ref[...].astype(BF16)
        wbbf_ref[...] = wb_ref[...].astype(BF16)

    oa = oa_ref[...]
    ob = ob_ref[...]
    for c in range(0, o_ref.shape[1], MXU_COLS):
        cols = slice(c, c + MXU_COLS)
        a = _dot(oa, wabf_ref[:, cols])
        b = _dot(ob, wbbf_ref[:, cols])
        o_ref[:, cols] = (g0_ref[:, cols].astype(F32) * a + g1_ref[:, cols].astype(F32) * b).astype(o_ref.dtype)


def _merge(oa, ob, w_a, w_b, gates, tm=1024, tn=512):
    t = oa.shape[0]
    nb = D_MODEL // tn
    return pl.pallas_call(
        _merge_kernel,
        grid=(nb, t // tm),
        in_specs=[pl.BlockSpec((tm, D_MODEL), lambda n, m: (m, 0)),
                  pl.BlockSpec((tm, D_MODEL), lambda n, m: (m, 0)),
                  pl.BlockSpec((D_MODEL, tn), lambda n, m: (0, n)),
                  pl.BlockSpec((D_MODEL, tn), lambda n, m: (0, n)),
                  pl.BlockSpec((tm, tn), lambda n, m: (m, n)),
                  pl.BlockSpec((tm, tn), lambda n, m: (m, nb + n))],
        out_specs=pl.BlockSpec((tm, tn), lambda n, m: (m, n)),
        out_shape=jax.ShapeDtypeStruct((t, D_MODEL), BF16),
        scratch_shapes=[pltpu.VMEM((D_MODEL, tn), BF16), pltpu.VMEM((D_MODEL, tn), BF16)],
        compiler_params=_cp(("arbitrary", "arbitrary"), 56),
        name="merge",
    )(oa, ob, w_a, w_b, gates, gates)


def _mix_kernel(a_ref, w_ref, x_ref, o_ref, wbf_ref):
    @pl.when(pl.program_id(1) == 0)
    def _():
        wbf_ref[...] = w_ref[...].astype(BF16)

    a = a_ref[...]
    for c in range(0, o_ref.shape[1], MXU_COLS):
        cols = slice(c, c + MXU_COLS)
        o_ref[:, cols] = x_ref[:, cols] + _dot(a, wbf_ref[:, cols])


def _mix(mixed, w, x, tm=1024, tn=1024):
    t = x.shape[0]
    return pl.pallas_call(
        _mix_kernel,
        grid=(D_MODEL // tn, t // tm),
        in_specs=[pl.BlockSpec((tm, D_MODEL), lambda n, m: (m, 0)),
                  pl.BlockSpec((D_MODEL, tn), lambda n, m: (0, n)),
                  pl.BlockSpec((tm, tn), lambda n, m: (m, n))],
        out_specs=pl.BlockSpec((tm, tn), lambda n, m: (m, n)),
        out_shape=jax.ShapeDtypeStruct((t, D_MODEL), F32),
        scratch_shapes=[pltpu.VMEM((D_MODEL, tn), BF16)],
        compiler_params=_cp(("arbitrary", "arbitrary"), 52),
        name="mix",
    )(mixed, w, x)


def _route_kernel(x_ref, nw_ref, wr_ref, br_ref, hp_ref, ids_ref, wts_ref):
    h = _rms(x_ref[...], nw_ref[...])
    hp_ref[...] = _to_token_major(h.astype(BF16))
    h_hi, h_lo = _split_bf16(h)
    w_hi, w_lo = _split_bf16(wr_ref[...])
    logits = _dot_nt(w_hi, h_hi) + _dot_nt(w_hi, h_lo) + _dot_nt(w_lo, h_hi) + br_ref[...]
    tm = logits.shape[1]

    best = logits[0:1, :]
    gidx = jnp.zeros((1, tm), I32)
    for i in range(1, N_GROUPS):
        li = logits[i:i + 1, :]
        take = li > best
        best = jnp.where(take, li, best)
        gidx = jnp.where(take, i, gidx)
    gsum = jnp.zeros((1, tm), F32)
    for i in range(N_GROUPS):
        gsum = gsum + jnp.exp(logits[i:i + 1, :] - best)
    g_p = 1.0 / gsum

    sel = logits[EXPERT_ROW0:EXPERT_ROW0 + EXPERTS_PER_GROUP, :]
    for g in range(1, N_GROUPS):
        r0 = EXPERT_ROW0 + g * EXPERTS_PER_GROUP
        sel = jnp.where(gidx == g, logits[r0:r0 + EXPERTS_PER_GROUP, :], sel)
    eio = lax.broadcasted_iota(I32, sel.shape, 0)
    m1 = jnp.max(sel, axis=0, keepdims=True)
    i1 = jnp.min(jnp.where(sel == m1, eio, EXPERTS_PER_GROUP), axis=0, keepdims=True)
    rest = jnp.where(eio == i1, -jnp.inf, sel)
    m2 = jnp.max(rest, axis=0, keepdims=True)
    i2 = jnp.min(jnp.where(rest == m2, eio, EXPERTS_PER_GROUP), axis=0, keepdims=True)
    p2 = jnp.exp(m2 - m1)
    w1 = g_p / (1.0 + p2)
    w2 = g_p * p2 / (1.0 + p2)
    e1 = gidx * EXPERTS_PER_GROUP + i1
    e2 = gidx * EXPERTS_PER_GROUP + i2
    rio = lax.broadcasted_iota(I32, (SUBLANES, tm), 0)
    ids_ref[...] = jnp.where(rio == 0, e1, jnp.where(rio == 1, e2, 0))
    wts_ref[...] = jnp.where(rio == 0, w1, jnp.where(rio == 1, w2, 0.0))


def _route(x1, norm_w, wr_t, br, tm=256):
    t = x1.shape[0]
    return pl.pallas_call(
        _route_kernel,
        grid=(t // tm,),
        in_specs=[pl.BlockSpec((tm, D_MODEL), lambda i: (i, 0)),
                  pl.BlockSpec((1, D_MODEL), lambda i: (0, 0)),
                  pl.BlockSpec((ROUTER_ROWS, D_MODEL), lambda i: (0, 0)),
                  pl.BlockSpec((ROUTER_ROWS, 1), lambda i: (0, 0))],
        out_specs=[pl.BlockSpec((tm, ROW_TILES, LANES), lambda i: (i, 0, 0)),
                   pl.BlockSpec((SUBLANES, tm), lambda i: (0, i)),
                   pl.BlockSpec((SUBLANES, tm), lambda i: (0, i))],
        out_shape=[jax.ShapeDtypeStruct((t, ROW_TILES, LANES), BF16),
                   jax.ShapeDtypeStruct((SUBLANES, t), I32),
                   jax.ShapeDtypeStruct((SUBLANES, t), F32)],
        compiler_params=_cp(("arbitrary",), 32),
        name="route",
    )(x1, norm_w, wr_t, br)


def _plan_kernel(ids_ref, dest_ref, cnt_ref, base_ref):
    phase = pl.program_id(0)
    step = pl.program_id(1)
    tm = ids_ref.shape[1]
    eio = lax.broadcasted_iota(I32, (N_EXPERTS, tm), 0)
    ids = ids_ref[...]
    oh = [jnp.where(eio == ids[k:k + 1, :], 1.0, 0.0) for k in range(2)]

    @pl.when((phase == 0) & (step == 0))
    def _():
        base_ref[...] = jnp.zeros_like(base_ref)

    @pl.when(phase == 0)
    def _():
        cnt = jnp.sum(oh[0] + oh[1], axis=1, keepdims=True)
        base_ref[...] = base_ref[...] + cnt
        dest_ref[0] = jnp.zeros(dest_ref.shape[1:], I32)
        cnt_ref[...] = base_ref[...]

    @pl.when((phase == 1) & (step == 0))
    def _():
        tiles = jnp.floor((base_ref[...] + (EXPERT_TILE - 1)) * (1.0 / EXPERT_TILE))
        r = lax.broadcasted_iota(I32, (N_EXPERTS, N_EXPERTS), 0)
        c = lax.broadcasted_iota(I32, (N_EXPERTS, N_EXPERTS), 1)
        lower = jnp.where(c < r, 1.0, 0.0).astype(BF16)
        base_ref[...] = _dot(lower, tiles.astype(BF16)) * float(EXPERT_TILE)

    @pl.when(phase == 1)
    def _():
        r = lax.broadcasted_iota(I32, (tm, tm), 0)
        c = lax.broadcasted_iota(I32, (tm, tm), 1)
        upper = jnp.where(r <= c, 1.0, 0.0).astype(BF16)
        base = base_ref[...][:, 0:1]
        rows = []
        for k in range(2):
            cum = _dot(oh[k].astype(BF16), upper)
            rows.append(jnp.sum(oh[k] * (cum - 1.0 + base), axis=0, keepdims=True))
            base = base + cum[:, tm - 1:tm]
        base_ref[...] = jnp.broadcast_to(base, base_ref.shape)
        rio = lax.broadcasted_iota(I32, (SUBLANES, tm), 0)
        d0 = rows[0].astype(I32)
        d1 = rows[1].astype(I32)
        dest_ref[0] = jnp.where(rio == 0, d0, jnp.where(rio == 1, d1, 0))


def _plan(ids, tm=512):
    t = ids.shape[1]
    return pl.pallas_call(
        _plan_kernel,
        grid=(2, t // tm),
        in_specs=[pl.BlockSpec((SUBLANES, tm), lambda p, i: (0, i))],
        out_specs=[pl.BlockSpec((1, SUBLANES, tm), lambda p, i: (p, 0, i)),
                   pl.BlockSpec((N_EXPERTS, LANES), lambda p, i: (0, 0))],
        out_shape=[jax.ShapeDtypeStruct((2, SUBLANES, t), I32),
                   jax.ShapeDtypeStruct((N_EXPERTS, LANES), F32)],
        scratch_shapes=[pltpu.VMEM((N_EXPERTS, LANES), F32)],
        compiler_params=_cp(("arbitrary", "arbitrary"), 32),
        name="plan",
    )(ids)


def _invert_kernel(dest_ref, slot_ref, *, n_slots):
    def fill(j, c):
        for u in range(FILL_UNROLL):
            slot_ref[j * FILL_UNROLL + u] = n_slots
        return c

    def scatter(j, c):
        for u in range(DMA_UNROLL):
            q = j * DMA_UNROLL + u
            slot_ref[dest_ref[q]] = q
        return c

    lax.fori_loop(0, slot_ref.shape[0] // FILL_UNROLL, fill, 0)
    lax.fori_loop(0, n_slots // DMA_UNROLL, scatter, 0)


def _invert(dest_flat, n_rows):
    n_slots = dest_flat.shape[0]
    assert n_rows % FILL_UNROLL == 0 and n_slots % DMA_UNROLL == 0
    return pl.pallas_call(
        functools.partial(_invert_kernel, n_slots=n_slots),
        grid_spec=pltpu.PrefetchScalarGridSpec(
            num_scalar_prefetch=1,
            grid=(1,),
            in_specs=[],
            out_specs=pl.BlockSpec(memory_space=pltpu.SMEM)),
        out_shape=jax.ShapeDtypeStruct((n_rows,), I32),
        compiler_params=_cp(("arbitrary",), 16),
        name="invert",
    )(dest_flat)


def _experts_kernel(te_ref, nx_ref, sl_ref, vr_ref, rs_ref, na_ref, hp_hbm, wg_hbm, wu_hbm, wd_hbm, yt_hbm,
                    wgf_ref, wuf_ref, wdf_ref, wgbf_ref, wubf_ref, wdbf_ref, xbuf_ref, ybuf_ref,
                    sems, xsems, ysems, *, t_total):
    i = pl.program_id(0)
    n_act = na_ref[0]
    active = i < n_act
    changed = (i == 0) | (te_ref[i] != te_ref[jnp.maximum(i - 1, 0)])

    def weight_copies(e, slot):
        return (pltpu.make_async_copy(wg_hbm.at[e], wgf_ref.at[slot], sems.at[slot, 0]),
                pltpu.make_async_copy(wu_hbm.at[e], wuf_ref.at[slot], sems.at[slot, 1]),
                pltpu.make_async_copy(wd_hbm.at[e], wdf_ref.at[slot], sems.at[slot, 2]))

    def in_copy(tile, r):
        tok = rs_ref[tile * EXPERT_TILE + r] & (t_total - 1)
        return pltpu.make_async_copy(hp_hbm.at[tok], xbuf_ref.at[tile % 2, r], xsems.at[tile % 2])

    def out_copy(tile, r):
        dst = yt_hbm.at[rs_ref[tile * EXPERT_TILE + r]]
        return pltpu.make_async_copy(ybuf_ref.at[tile % 2, r], dst, ysems.at[tile % 2])

    def for_rows(tile, make_copy, fn):
        n = vr_ref[tile]
        groups = lax.shift_right_logical(n, DMA_UNROLL.bit_length() - 1)

        def body(j, c):
            for u in range(DMA_UNROLL):
                fn(make_copy(tile, j * DMA_UNROLL + u))
            return c

        def tail(r, c):
            fn(make_copy(tile, r))
            return c

        lax.fori_loop(0, groups, body, 0)
        lax.fori_loop(groups * DMA_UNROLL, n, tail, 0)

    def wait_rows(tile, make_copy, whole_tile_copy):
        full = vr_ref[tile] == EXPERT_TILE

        @pl.when(full)
        def _():
            whole_tile_copy.wait()

        @pl.when(jnp.logical_not(full))
        def _():
            for_rows(tile, make_copy, lambda cp: cp.wait())

    def in_tile(tile):
        return pltpu.make_async_copy(hp_hbm.at[pl.ds(0, EXPERT_TILE)], xbuf_ref.at[tile % 2], xsems.at[tile % 2])

    def out_tile(tile):
        return pltpu.make_async_copy(ybuf_ref.at[tile % 2], yt_hbm.at[pl.ds(0, EXPERT_TILE)], ysems.at[tile % 2])

    @pl.when(i == 0)
    def _():
        xbuf_ref[...] = jnp.zeros_like(xbuf_ref)
        for cp in weight_copies(te_ref[0], 0):
            cp.start()
        for_rows(i, in_copy, lambda cp: cp.start())

    @pl.when(i + 1 < n_act)
    def _():
        for_rows(i + 1, in_copy, lambda cp: cp.start())

    @pl.when(active & changed)
    def _():
        slot = sl_ref[i]
        nxt = nx_ref[i]

        @pl.when(nxt >= 0)
        def _():
            for cp in weight_copies(nxt, 1 - slot):
                cp.start()

        for cp in weight_copies(te_ref[i], slot):
            cp.wait()
        wgbf_ref[...] = wgf_ref[slot].astype(BF16)
        wubf_ref[...] = wuf_ref[slot].astype(BF16)
        wdbf_ref[...] = wdf_ref[slot].astype(BF16)

    @pl.when(active)
    def _():
        wait_rows(i, in_copy, in_tile(i))
        x = _from_token_major(xbuf_ref[i % 2])
        hg = _dot(x, wgbf_ref[...])
        hu = _dot(x, wubf_ref[...])
        act = (hg * jax.nn.sigmoid(hg) * hu).astype(BF16)
        ybuf_ref[i % 2] = _to_token_major(_dot(act, wdbf_ref[...]).astype(BF16))

    @pl.when((i >= 1) & (i - 1 < n_act))
    def _():
        wait_rows(i - 1, out_copy, out_tile(i - 1))

    @pl.when(active)
    def _():
        for_rows(i, out_copy, lambda cp: cp.start())

    @pl.when(active & (i == pl.num_programs(0) - 1))
    def _():
        wait_rows(i, out_copy, out_tile(i))


def _experts(tile_e, next_e, slot, valid, row_slot, n_act, hp, w_gate, w_up, w_down, n_tiles):
    t = hp.shape[0]
    assert t & (t - 1) == 0, "the row map packs slot * T + token with T a power of two"
    any_spec = pl.BlockSpec(memory_space=pl.ANY)
    tile_buf = pltpu.VMEM((2, EXPERT_TILE, ROW_TILES, LANES), BF16)
    return pl.pallas_call(
        functools.partial(_experts_kernel, t_total=t),
        grid_spec=pltpu.PrefetchScalarGridSpec(
            num_scalar_prefetch=6,
            grid=(n_tiles,),
            in_specs=[any_spec, any_spec, any_spec, any_spec],
            out_specs=any_spec,
            scratch_shapes=[pltpu.VMEM((2, D_MODEL, D_FF), F32),
                            pltpu.VMEM((2, D_MODEL, D_FF), F32),
                            pltpu.VMEM((2, D_FF, D_MODEL), F32),
                            pltpu.VMEM((D_MODEL, D_FF), BF16),
                            pltpu.VMEM((D_MODEL, D_FF), BF16),
                            pltpu.VMEM((D_FF, D_MODEL), BF16),
                            tile_buf, tile_buf,
                            pltpu.SemaphoreType.DMA((2, 3)),
                            pltpu.SemaphoreType.DMA((2,)),
                            pltpu.SemaphoreType.DMA((2,))]),
        out_shape=jax.ShapeDtypeStruct((2 * t, ROW_TILES, LANES), BF16),
        compiler_params=_cp(("arbitrary",), 48, has_side_effects=True),
        name="experts",
    )(tile_e, next_e, slot, valid, row_slot, n_act, hp, w_gate, w_up, w_down)


def _combine_kernel(y0_ref, y1_ref, x_ref, wt_ref, p_ref, nw_ref, wg_ref, wp_ref, fw_ref, o_ref):
    wt = wt_ref[...]
    y0 = _from_token_major(y0_ref[...]).astype(F32)
    y1 = _from_token_major(y1_ref[...]).astype(F32)
    x2 = x_ref[...] + wt[:, 0:1] * y0 + wt[:, 1:2] * y1
    hn = _rms(x2, nw_ref[...]).astype(BF16)
    pg = jax.nn.sigmoid(_dot(hn, wg_ref[...]))
    x3 = x2 + pg * _dot(p_ref[...].astype(BF16), wp_ref[...])
    o_ref[...] = _rms(x3, fw_ref[...])


def _combine_ple(yt, x1, wts_t, p, ple_norm_w, w_gate, w_proj, final_w, tm=256):
    t = x1.shape[0]
    nb = t // tm
    row = lambda i: (i, 0)
    fix = lambda i: (0, 0)
    return pl.pallas_call(
        _combine_kernel,
        grid=(nb,),
        in_specs=[pl.BlockSpec((tm, ROW_TILES, LANES), lambda i: (i, 0, 0)),
                  pl.BlockSpec((tm, ROW_TILES, LANES), lambda i: (nb + i, 0, 0)),
                  pl.BlockSpec((tm, D_MODEL), row),
                  pl.BlockSpec((tm, SUBLANES), row),
                  pl.BlockSpec((tm, PLE_DIM), row),
                  pl.BlockSpec((1, D_MODEL), fix),
                  pl.BlockSpec((D_MODEL, D_MODEL), fix),
                  pl.BlockSpec((PLE_DIM, D_MODEL), fix),
                  pl.BlockSpec((1, D_MODEL), fix)],
        out_specs=pl.BlockSpec((tm, D_MODEL), row),
        out_shape=jax.ShapeDtypeStruct((t, D_MODEL), F32),
        compiler_params=_cp(("arbitrary",), 48),
        name="combine_ple",
    )(yt, yt, x1, wts_t, p, ple_norm_w, w_gate, w_proj, final_w)


def _tile_table(counts, n_tiles):
    tiles = (counts.astype(I32) + (EXPERT_TILE - 1)) // EXPERT_TILE
    ends = jnp.cumsum(tiles)
    n_act = ends[-1]
    idx = jnp.minimum(jnp.arange(n_tiles, dtype=I32), n_act - 1)
    tile_e = jnp.sum((idx[:, None] >= ends[None, :]).astype(I32), axis=1).astype(I32)
    run_end = ends[tile_e]
    next_e = jnp.where(run_end < n_act, tile_e[jnp.minimum(run_end, n_tiles - 1)], -1).astype(I32)
    new_run = jnp.concatenate([jnp.ones((1,), I32), (tile_e[1:] != tile_e[:-1]).astype(I32)])
    slot = ((jnp.cumsum(new_run) - 1) % 2).astype(I32)
    arange = jnp.arange(n_tiles, dtype=I32)
    last = arange == run_end - 1
    valid = jnp.where(last, counts.astype(I32)[tile_e] - (tiles[tile_e] - 1) * EXPERT_TILE, EXPERT_TILE)
    valid = jnp.where(arange < n_act, valid, 0).astype(I32)
    return tile_e, next_e, slot, valid, n_act.reshape(1).astype(I32)


def _block(x, p, norm_mix_w, w_in, b_merge, w_alpha_up, b_alpha_up, gla_norm_w, w_gla_out,
           conv_w, conv_b, w_conv_out, w_mix_out, norm_ffn_w, w_router_group, b_router_group,
           w_router_expert, b_router_expert, w_e_gate, w_e_up, w_e_down, ple_norm_w,
           w_ple_gate, w_ple_proj, final_norm_w):
    t = x.shape[0]
    n_tiles = (2 * t) // EXPERT_TILE + N_EXPERTS
    n_rows = n_tiles * EXPERT_TILE

    w_up = jnp.pad(w_alpha_up, ((0, LANES - GLA_GATE_RANK), (0, 0))).astype(BF16)
    w_up = w_up.reshape(LANES, GLA_HEADS, GLA_DK).transpose(1, 0, 2)
    b_up = b_alpha_up.reshape(GLA_HEADS, 1, GLA_DK)
    gnw = gla_norm_w.reshape(GLA_HEADS, 1, GLA_DV)
    wr_t = jnp.zeros((ROUTER_ROWS, D_MODEL), F32)
    wr_t = wr_t.at[0:N_GROUPS].set(w_router_group.T)
    wr_t = wr_t.at[EXPERT_ROW0:EXPERT_ROW0 + N_EXPERTS].set(w_router_expert.T)
    br = jnp.zeros((ROUTER_ROWS, 1), F32)
    br = br.at[0:N_GROUPS, 0].set(b_router_group)
    br = br.at[EXPERT_ROW0:EXPERT_ROW0 + N_EXPERTS, 0].set(b_router_expert)

    w_in_t = w_in.T
    h, a_low = _norm_in(x, norm_mix_w.reshape(1, D_MODEL), w_in_t)
    qkvg = _proj_qkvg(h, w_in_t)
    ob = _proj_conv(h, w_in_t, conv_w, conv_b.reshape(1, D_MODEL))
    gates = _proj_gates(h, w_in_t, b_merge.reshape(1, 2 * D_MODEL))
    oa = _gla(qkvg, a_low, w_up, b_up, gnw)
    mixed = _merge(oa, ob, w_gla_out, w_conv_out, gates)
    x1 = _mix(mixed, w_mix_out, x)

    hp, ids, wts = _route(x1, norm_ffn_w.reshape(1, D_MODEL), wr_t, br)
    dest, counts = _plan(ids)
    dest_flat = dest[1, 0:2].reshape(2 * t)
    tile_e, next_e, slot, valid, n_act = _tile_table(counts[:, 0], n_tiles)
    row_slot = _invert(dest_flat, n_rows)
    yt = _experts(tile_e, next_e, slot, valid, row_slot, n_act, hp, w_e_gate, w_e_up, w_e_down, n_tiles)
    return _combine_ple(yt, x1, wts.T, p,
                        ple_norm_w.reshape(1, D_MODEL), w_ple_gate.astype(BF16),
                        w_ple_proj.astype(BF16), final_norm_w.reshape(1, D_MODEL))


def kernel(x, p, norm_mix_w, w_in, b_merge, w_alpha_up, b_alpha_up, gla_norm_w, w_gla_out, conv_w, conv_b, w_conv_out, w_mix_out, norm_ffn_w, w_router_group, b_router_group, w_router_expert, b_router_expert, w_e_gate, w_e_up, w_e_down, ple_norm_w, w_ple_gate, w_ple_proj, final_norm_w):
    depth, batch = p.shape[0], x.shape[0]
    assert depth == 1 and batch == 1, "kernel is specialised to one layer and one sequence"
    out = _block(x[0], p[0, 0], norm_mix_w[0], w_in[0], b_merge[0], w_alpha_up[0], b_alpha_up[0],
                 gla_norm_w[0], w_gla_out[0], conv_w[0], conv_b[0], w_conv_out[0], w_mix_out[0],
                 norm_ffn_w[0], w_router_group[0], b_router_group[0], w_router_expert[0],
                 b_router_expert[0], w_e_gate[0], w_e_up[0], w_e_down[0], ple_norm_w[0],
                 w_ple_gate[0], w_ple_proj[0], final_norm_w)
    return out[None]
```

```python
import functools

import jax
import jax.numpy as jnp
from jax import lax
from jax.experimental import pallas as pl
from jax.experimental.pallas import tpu as pltpu

F32 = jnp.float32
BF16 = jnp.bfloat16
I32 = jnp.int32

D_MODEL = 2048
PLE_DIM = 256
EPS = 1e-6
LOG2_E = 1.4426950408889634
GLA_HEADS = 4
GLA_DK = 256
GLA_DV = 512
GLA_KEY = GLA_HEADS * GLA_DK
GLA_VAL = GLA_HEADS * GLA_DV
GLA_GATE_RANK = 16
GLA_GATE_NORM = 16.0
GLA_CHUNK = 64
CONV_K = 3
N_GROUPS = 4
EXPERTS_PER_GROUP = 8
N_EXPERTS = N_GROUPS * EXPERTS_PER_GROUP
D_FF = 512

QKVG_COLS = 2 * GLA_KEY + 2 * GLA_VAL
CONV_COL0 = QKVG_COLS + GLA_GATE_RANK

LANES = 128
SUBLANES = 8
MXU_COLS = 256
ROW_TILES = D_MODEL // LANES

EXPERT_TILE = 256
DMA_UNROLL = 8
ROUTER_ROWS = 64
EXPERT_ROW0 = 8

MIB = 1024 * 1024


def _cp(sem, vmem_mib, **kw):
    return pltpu.CompilerParams(dimension_semantics=sem, vmem_limit_bytes=int(vmem_mib * MIB), **kw)


def _rms(x, w):
    return x * lax.rsqrt(jnp.mean(x * x, axis=-1, keepdims=True) + EPS) * w


def _dot(a, b):
    return jnp.dot(a, b, preferred_element_type=F32)


def _dot_nt(a, b):
    return lax.dot_general(a, b, (((1,), (1,)), ((), ())), preferred_element_type=F32)


def _dot_tn(a, b):
    return lax.dot_general(a, b, (((0,), (0,)), ((), ())), preferred_element_type=F32)


def _split_bf16(x):
    hi = x.astype(BF16)
    lo = (x - hi.astype(F32)).astype(BF16)
    return hi, lo


def _to_token_major(val):
    return val.reshape(val.shape[0], ROW_TILES, LANES)


def _from_token_major(val):
    return val.reshape(val.shape[0], D_MODEL)


def _norm_in_kernel(x_ref, w_ref, wal_ref, h_ref, al_ref, walbf_ref):
    @pl.when(pl.program_id(0) == 0)
    def _():
        walbf_ref[...] = wal_ref[...].astype(BF16)

    h = _rms(x_ref[...], w_ref[...]).astype(BF16)
    h_ref[...] = h
    al_ref[...] = _dot_nt(h, walbf_ref[...])


def _norm_in(x, w, w_in_t, tm=512):
    t = x.shape[0]
    return pl.pallas_call(
        _norm_in_kernel,
        grid=(t // tm,),
        in_specs=[pl.BlockSpec((tm, D_MODEL), lambda i: (i, 0)),
                  pl.BlockSpec((1, D_MODEL), lambda i: (0, 0)),
                  pl.BlockSpec((LANES, D_MODEL), lambda i: (QKVG_COLS // LANES, 0))],
        out_specs=[pl.BlockSpec((tm, D_MODEL), lambda i: (i, 0)),
                   pl.BlockSpec((tm, LANES), lambda i: (i, 0))],
        out_shape=[jax.ShapeDtypeStruct((t, D_MODEL), BF16),
                   jax.ShapeDtypeStruct((t, LANES), F32)],
        scratch_shapes=[pltpu.VMEM((LANES, D_MODEL), BF16)],
        compiler_params=_cp(("arbitrary",), 32),
        name="norm_in",
    )(x, w, w_in_t)


def _w_rows_spec(row0, tn):
    assert row0 % SUBLANES == 0 and tn % SUBLANES == 0
    return pl.BlockSpec((pl.Element(tn), pl.Element(D_MODEL)),
                        lambda n, m: (pl.multiple_of(row0 + n * tn, SUBLANES), 0))


def _proj_qkvg_kernel(h_ref, w_ref, o_ref, wbf_ref):
    @pl.when(pl.program_id(1) == 0)
    def _():
        wbf_ref[...] = w_ref[...].astype(BF16)

    o_ref[...] = _dot_nt(h_ref[...], wbf_ref[...]).astype(o_ref.dtype)


def _proj_qkvg(h, w_in_t, tm=2048, tn=1024):
    t = h.shape[0]
    return pl.pallas_call(
        _proj_qkvg_kernel,
        grid=(QKVG_COLS // tn, t // tm),
        in_specs=[pl.BlockSpec((tm, D_MODEL), lambda n, m: (m, 0)),
                  pl.BlockSpec((tn, D_MODEL), lambda n, m: (n, 0))],
        out_specs=pl.BlockSpec((tm, tn), lambda n, m: (m, n)),
        out_shape=jax.ShapeDtypeStruct((t, QKVG_COLS), BF16),
        scratch_shapes=[pltpu.VMEM((tn, D_MODEL), BF16)],
        compiler_params=_cp(("arbitrary", "arbitrary"), 56),
        name="proj_qkvg",
    )(h, w_in_t)


def _proj_conv_kernel(h_ref, wb_ref, wc_ref, wx_ref, cw_ref, cb_ref, o_ref,
                      wbbf_ref, wcbf_ref, wxbf_ref, prev_ref):
    m = pl.program_id(1)

    @pl.when(m == 0)
    def _():
        prev_ref[...] = jnp.zeros_like(prev_ref)
        wbbf_ref[...] = wb_ref[...].astype(BF16)
        wcbf_ref[...] = wc_ref[...].astype(BF16)
        wxbf_ref[...] = wx_ref[...].astype(BF16)

    h = h_ref[...]
    tm = h.shape[0]
    row = lax.broadcasted_iota(I32, (tm, MXU_COLS), 0)
    for c in range(0, o_ref.shape[1], MXU_COLS):
        cols = slice(c, c + MXU_COLS)
        b = _dot_nt(h, wbbf_ref[cols, :])
        s = _dot_nt(h, wcbf_ref[cols, :]) * _dot_nt(h, wxbf_ref[cols, :])
        prev = prev_ref[:, cols]
        p1 = prev[SUBLANES - 1:SUBLANES, :]
        p2 = prev[SUBLANES - 2:SUBLANES - 1, :]
        s1 = jnp.where(row == 0, p1, pltpu.roll(s, 1, 0))
        s2 = jnp.where(row == 0, p2, jnp.where(row == 1, p1, pltpu.roll(s, 2, 0)))
        cw = cw_ref[:, cols]
        u = cw[2:3, :] * s + cw[1:2, :] * s1 + cw[0:1, :] * s2 + cb_ref[:, cols]
        o_ref[:, cols] = (b * u).astype(o_ref.dtype)
        prev_ref[:, cols] = s[tm - SUBLANES:, :]


def _proj_conv(h, w_in_t, conv_w, conv_b, tm=1024, tn=512):
    t = h.shape[0]
    return pl.pallas_call(
        _proj_conv_kernel,
        grid=(D_MODEL // tn, t // tm),
        in_specs=[pl.BlockSpec((tm, D_MODEL), lambda n, m: (m, 0))] +
                 [_w_rows_spec(CONV_COL0 + seg * D_MODEL, tn) for seg in range(3)] +
                 [pl.BlockSpec((CONV_K, tn), lambda n, m: (0, n)),
                  pl.BlockSpec((1, tn), lambda n, m: (0, n))],
        out_specs=pl.BlockSpec((tm, tn), lambda n, m: (m, n)),
        out_shape=jax.ShapeDtypeStruct((t, D_MODEL), BF16),
        scratch_shapes=[pltpu.VMEM((tn, D_MODEL), BF16), pltpu.VMEM((tn, D_MODEL), BF16),
                        pltpu.VMEM((tn, D_MODEL), BF16), pltpu.VMEM((SUBLANES, tn), F32)],
        compiler_params=_cp(("arbitrary", "arbitrary"), 48),
        name="proj_conv",
    )(h, w_in_t, w_in_t, w_in_t, conv_w, conv_b)


def _proj_gates_kernel(h_ref, w_ref, b_ref, o_ref, wbf_ref):
    @pl.when(pl.program_id(1) == 0)
    def _():
        wbf_ref[...] = w_ref[...].astype(BF16)

    h = h_ref[...]
    for c in range(0, o_ref.shape[1], MXU_COLS):
        cols = slice(c, c + MXU_COLS)
        o_ref[:, cols] = jax.nn.sigmoid(_dot_nt(h, wbf_ref[cols, :]) + b_ref[:, cols]).astype(o_ref.dtype)


def _proj_gates(h, w_in_t, b_merge, tm=2048, tn=1024):
    t = h.shape[0]
    return pl.pallas_call(
        _proj_gates_kernel,
        grid=(2 * D_MODEL // tn, t // tm),
        in_specs=[pl.BlockSpec((tm, D_MODEL), lambda n, m: (m, 0)),
                  _w_rows_spec(CONV_COL0 + 3 * D_MODEL, tn),
                  pl.BlockSpec((1, tn), lambda n, m: (0, n))],
        out_specs=pl.BlockSpec((tm, tn), lambda n, m: (m, n)),
        out_shape=jax.ShapeDtypeStruct((t, 2 * D_MODEL), BF16),
        scratch_shapes=[pltpu.VMEM((tn, D_MODEL), BF16)],
        compiler_params=_cp(("arbitrary", "arbitrary"), 56),
        name="proj_gates",
    )(h, w_in_t, b_merge)


def _gla_kernel(q_ref, k_ref, v_ref, g_ref, al_ref, wup_ref, bup_ref, nw_ref, o_ref,
                st_ref, b_ref, bl_ref, qd_ref, ki_ref, kd_ref, oi_ref, u_ref, *, n_chunks):
    c_len = GLA_CHUNK

    @pl.when(pl.program_id(1) == 0)
    def _():
        st_ref[...] = jnp.zeros_like(st_ref)

    row = lax.broadcasted_iota(I32, (c_len, c_len), 0)
    col = lax.broadcasted_iota(I32, (c_len, c_len), 1)
    causal = col <= row
    tril = jnp.where(causal, 1.0, 0.0).astype(BF16)

    def decays(h):
        kc = slice(h * GLA_DK, (h + 1) * GLA_DK)
        z = _dot(al_ref[...].astype(BF16), wup_ref[h]) + bup_ref[h]
        la = (jnp.minimum(z, 0.0) - jnp.log1p(jnp.exp(-jnp.abs(z)))) * (LOG2_E / GLA_GATE_NORM)
        la_hi, la_lo = _split_bf16(la)
        for c in range(n_chunks):
            r0 = c * c_len
            b = _dot(tril, la_hi[r0:r0 + c_len]) + _dot(tril, la_lo[r0:r0 + c_len])
            b_ref[pl.ds(r0, c_len), kc] = b
            bl_ref[pl.ds(r0, c_len), kc] = jnp.broadcast_to(b[c_len - 1:c_len, :], b.shape)
        b = b_ref[:, kc]
        q = q_ref[:, kc].astype(F32)
        k = k_ref[:, kc].astype(F32)
        qd_ref[:, kc] = (q * jnp.exp2(b)).astype(BF16)
        ki_ref[:, kc] = (k * jnp.exp2(-b)).astype(BF16)
        kd_ref[:, kc] = (k * jnp.exp2(bl_ref[:, kc] - b)).astype(BF16)

    def local_products(h):
        kc = slice(h * GLA_DK, (h + 1) * GLA_DK)
        vc = slice(h * GLA_DV, (h + 1) * GLA_DV)
        for c in range(n_chunks):
            sl = pl.ds(c * c_len, c_len)
            v = v_ref[sl, vc]
            att = jnp.where(causal, _dot_nt(qd_ref[sl, kc], ki_ref[sl, kc]), 0.0).astype(BF16)
            oi_ref[sl, vc] = _dot(att, v)
            u_ref[h, c] = _dot_tn(v, kd_ref[sl, kc])

    def recurrence(h):
        kc = slice(h * GLA_DK, (h + 1) * GLA_DK)
        vc = slice(h * GLA_DV, (h + 1) * GLA_DV)
        for c in range(n_chunks):
            sl = pl.ds(c * c_len, c_len)
            st = st_ref[h]
            oi_ref[sl, vc] = oi_ref[sl, vc] + _dot_nt(qd_ref[sl, kc], st.astype(BF16))
            st_ref[h] = st * jnp.exp2(bl_ref[pl.ds(c * c_len, 1), kc]) + u_ref[h, c]
        o = oi_ref[:, vc]
        o = o * lax.rsqrt(jnp.mean(o * o, axis=-1, keepdims=True) + EPS * GLA_DK) * nw_ref[h]
        g = g_ref[:, vc].astype(F32)
        o_ref[:, vc] = (o * (g * jax.nn.sigmoid(g))).astype(o_ref.dtype)

    decays(0)
    local_products(0)
    decays(1)
    recurrence(0)
    local_products(1)
    recurrence(1)


def _gla(qkvg, a_low, w_up, b_up, norm_w, tb=512):
    t = qkvg.shape[0]
    hp = 2
    dk, dv = hp * GLA_DK, hp * GLA_DV
    kq = GLA_KEY // dk
    kv = 2 * GLA_KEY // dv
    kg = kv + GLA_VAL // dv
    n_chunks = tb // GLA_CHUNK
    kern = functools.partial(_gla_kernel, n_chunks=n_chunks)
    return pl.pallas_call(
        kern,
        grid=(GLA_HEADS // hp, t // tb),
        in_specs=[pl.BlockSpec((tb, dk), lambda h, i: (i, h)),
                  pl.BlockSpec((tb, dk), lambda h, i: (i, kq + h)),
                  pl.BlockSpec((tb, dv), lambda h, i: (i, kv + h)),
                  pl.BlockSpec((tb, dv), lambda h, i: (i, kg + h)),
                  pl.BlockSpec((tb, LANES), lambda h, i: (i, 0)),
                  pl.BlockSpec((hp, LANES, GLA_DK), lambda h, i: (h, 0, 0)),
                  pl.BlockSpec((hp, 1, GLA_DK), lambda h, i: (h, 0, 0)),
                  pl.BlockSpec((hp, 1, GLA_DV), lambda h, i: (h, 0, 0))],
        out_specs=pl.BlockSpec((tb, dv), lambda h, i: (i, h)),
        out_shape=jax.ShapeDtypeStruct((t, GLA_VAL), BF16),
        scratch_shapes=[pltpu.VMEM((hp, GLA_DV, GLA_DK), F32),
                        pltpu.VMEM((tb, dk), F32), pltpu.VMEM((tb, dk), F32),
                        pltpu.VMEM((tb, dk), BF16), pltpu.VMEM((tb, dk), BF16),
                        pltpu.VMEM((tb, dk), BF16),
                        pltpu.VMEM((tb, dv), F32),
                        pltpu.VMEM((hp, n_chunks, GLA_DV, GLA_DK), F32)],
        compiler_params=_cp(("arbitrary", "arbitrary"), 40),
        name="gla",
    )(qkvg, qkvg, qkvg, qkvg, a_low, w_up, b_up, norm_w)


def _merge_kernel(oa_ref, ob_ref, wa_ref, wb_ref, g0_ref, g1_ref, o_ref, wabf_ref, wbbf_ref):
    @pl.when(pl.program_id(1) == 0)
    def _():
        wabf_ref[...] = wa_ref[...].astype(BF16)
        wbbf_ref[...] = wb_ref[...].astype(BF16)

    oa = oa_ref[...]
    ob = ob_ref[...]
    for c in range(0, o_ref.shape[1], MXU_COLS):
        cols = slice(c, c + MXU_COLS)
        a = _dot(oa, wabf_ref[:, cols])
        b = _dot(ob, wbbf_ref[:, cols])
        o_ref[:, cols] = (g0_ref[:, cols].astype(F32) * a + g1_ref[:, cols].astype(F32) * b).astype(o_ref.dtype)


def _merge(oa, ob, w_a, w_b, gates, tm=1024, tn=512):
    t = oa.shape[0]
    nb = D_MODEL // tn
    return pl.pallas_call(
        _merge_kernel,
        grid=(nb, t // tm),
        in_specs=[pl.BlockSpec((tm, D_MODEL), lambda n, m: (m, 0)),
                  pl.BlockSpec((tm, D_MODEL), lambda n, m: (m, 0)),
                  pl.BlockSpec((D_MODEL, tn), lambda n, m: (0, n)),
                  pl.BlockSpec((D_MODEL, tn), lambda n, m: (0, n)),
                  pl.BlockSpec((tm, tn), lambda n, m: (m, n)),
                  pl.BlockSpec((tm, tn), lambda n, m: (m, nb + n))],
        out_specs=pl.BlockSpec((tm, tn), lambda n, m: (m, n)),
        out_shape=jax.ShapeDtypeStruct((t, D_MODEL), BF16),
        scratch_shapes=[pltpu.VMEM((D_MODEL, tn), BF16), pltpu.VMEM((D_MODEL, tn), BF16)],
        compiler_params=_cp(("arbitrary", "arbitrary"), 56),
        name="merge",
    )(oa, ob, w_a, w_b, gates, gates)


def _mix_kernel(a_ref, w_ref, x_ref, o_ref, wbf_ref):
    @pl.when(pl.program_id(1) == 0)
    def _():
        wbf_ref[...] = w_ref[...].astype(BF16)

    a = a_ref[...]
    for c in range(0, o_ref.shape[1], MXU_COLS):
        cols = slice(c, c + MXU_COLS)
        o_ref[:, cols] = x_ref[:, cols] + _dot(a, wbf_ref[:, cols])


def _mix(mixed, w, x, tm=1024, tn=1024):
    t = x.shape[0]
    return pl.pallas_call(
        _mix_kernel,
        grid=(D_MODEL // tn, t // tm),
        in_specs=[pl.BlockSpec((tm, D_MODEL), lambda n, m: (m, 0)),
                  pl.BlockSpec((D_MODEL, tn), lambda n, m: (0, n)),
                  pl.BlockSpec((tm, tn), lambda n, m: (m, n))],
        out_specs=pl.BlockSpec((tm, tn), lambda n, m: (m, n)),
        out_shape=jax.ShapeDtypeStruct((t, D_MODEL), F32),
        scratch_shapes=[pltpu.VMEM((D_MODEL, tn), BF16)],
        compiler_params=_cp(("arbitrary", "arbitrary"), 52),
        name="mix",
    )(mixed, w, x)


def _route_kernel(x_ref, nw_ref, wr_ref, br_ref, hp_ref, ids_ref, wts_ref):
    h = _rms(x_ref[...], nw_ref[...])
    hp_ref[...] = _to_token_major(h.astype(BF16))
    h_hi, h_lo = _split_bf16(h)
    w_hi, w_lo = _split_bf16(wr_ref[...])
    logits = _dot_nt(w_hi, h_hi) + _dot_nt(w_hi, h_lo) + _dot_nt(w_lo, h_hi) + br_ref[...]
    tm = logits.shape[1]

    best = logits[0:1, :]
    gidx = jnp.zeros((1, tm), I32)
    for i in range(1, N_GROUPS):
        li = logits[i:i + 1, :]
        take = li > best
        best = jnp.where(take, li, best)
        gidx = jnp.where(take, i, gidx)
    gsum = jnp.zeros((1, tm), F32)
    for i in range(N_GROUPS):
        gsum = gsum + jnp.exp(logits[i:i + 1, :] - best)
    g_p = 1.0 / gsum

    sel = logits[EXPERT_ROW0:EXPERT_ROW0 + EXPERTS_PER_GROUP, :]
    for g in range(1, N_GROUPS):
        r0 = EXPERT_ROW0 + g * EXPERTS_PER_GROUP
        sel = jnp.where(gidx == g, logits[r0:r0 + EXPERTS_PER_GROUP, :], sel)
    eio = lax.broadcasted_iota(I32, sel.shape, 0)
    m1 = jnp.max(sel, axis=0, keepdims=True)
    i1 = jnp.min(jnp.where(sel == m1, eio, EXPERTS_PER_GROUP), axis=0, keepdims=True)
    rest = jnp.where(eio == i1, -jnp.inf, sel)
    m2 = jnp.max(rest, axis=0, keepdims=True)
    i2 = jnp.min(jnp.where(rest == m2, eio, EXPERTS_PER_GROUP), axis=0, keepdims=True)
    p2 = jnp.exp(m2 - m1)
    w1 = g_p / (1.0 + p2)
    w2 = g_p * p2 / (1.0 + p2)
    e1 = gidx * EXPERTS_PER_GROUP + i1
    e2 = gidx * EXPERTS_PER_GROUP + i2
    rio = lax.broadcasted_iota(I32, (SUBLANES, tm), 0)
    ids_ref[...] = jnp.where(rio == 0, e1, jnp.where(rio == 1, e2, 0))
    wts_ref[...] = jnp.where(rio == 0, w1, jnp.where(rio == 1, w2, 0.0))


def _route(x1, norm_w, wr_t, br, tm=256):
    t = x1.shape[0]
    return pl.pallas_call(
        _route_kernel,
        grid=(t // tm,),
        in_specs=[pl.BlockSpec((tm, D_MODEL), lambda i: (i, 0)),
                  pl.BlockSpec((1, D_MODEL), lambda i: (0, 0)),
                  pl.BlockSpec((ROUTER_ROWS, D_MODEL), lambda i: (0, 0)),
                  pl.BlockSpec((ROUTER_ROWS, 1), lambda i: (0, 0))],
        out_specs=[pl.BlockSpec((tm, ROW_TILES, LANES), lambda i: (i, 0, 0)),
                   pl.BlockSpec((SUBLANES, tm), lambda i: (0, i)),
                   pl.BlockSpec((SUBLANES, tm), lambda i: (0, i))],
        out_shape=[jax.ShapeDtypeStruct((t, ROW_TILES, LANES), BF16),
                   jax.ShapeDtypeStruct((SUBLANES, t), I32),
                   jax.ShapeDtypeStruct((SUBLANES, t), F32)],
        compiler_params=_cp(("arbitrary",), 32),
        name="route",
    )(x1, norm_w, wr_t, br)


def _plan_kernel(ids_ref, dest_ref, cnt_ref, base_ref):
    phase = pl.program_id(0)
    step = pl.program_id(1)
    tm = ids_ref.shape[1]
    eio = lax.broadcasted_iota(I32, (N_EXPERTS, tm), 0)
    ids = ids_ref[...]
    oh = [jnp.where(eio == ids[k:k + 1, :], 1.0, 0.0) for k in range(2)]

    @pl.when((phase == 0) & (step == 0))
    def _():
        base_ref[...] = jnp.zeros_like(base_ref)

    @pl.when(phase == 0)
    def _():
        cnt = jnp.sum(oh[0] + oh[1], axis=1, keepdims=True)
        base_ref[...] = base_ref[...] + cnt
        dest_ref[0] = jnp.zeros(dest_ref.shape[1:], I32)
        cnt_ref[...] = base_ref[...]

    @pl.when((phase == 1) & (step == 0))
    def _():
        tiles = jnp.floor((base_ref[...] + (EXPERT_TILE - 1)) * (1.0 / EXPERT_TILE))
        r = lax.broadcasted_iota(I32, (N_EXPERTS, N_EXPERTS), 0)
        c = lax.broadcasted_iota(I32, (N_EXPERTS, N_EXPERTS), 1)
        lower = jnp.where(c < r, 1.0, 0.0).astype(BF16)
        base_ref[...] = _dot(lower, tiles.astype(BF16)) * float(EXPERT_TILE)

    @pl.when(phase == 1)
    def _():
        r = lax.broadcasted_iota(I32, (tm, tm), 0)
        c = lax.broadcasted_iota(I32, (tm, tm), 1)
        upper = jnp.where(r <= c, 1.0, 0.0).astype(BF16)
        base = base_ref[...][:, 0:1]
        rows = []
        for k in range(2):
            cum = _dot(oh[k].astype(BF16), upper)
            rows.append(jnp.sum(oh[k] * (cum - 1.0 + base), axis=0, keepdims=True))
            base = base + cum[:, tm - 1:tm]
        base_ref[...] = jnp.broadcast_to(base, base_ref.shape)
        rio = lax.broadcasted_iota(I32, (SUBLANES, tm), 0)
        d0 = rows[0].astype(I32)
        d1 = rows[1].astype(I32)
        dest_ref[0] = jnp.where(rio == 0, d0, jnp.where(rio == 1, d1, 0))


def _plan(ids, tm=512):
    t = ids.shape[1]
    return pl.pallas_call(
        _plan_kernel,
        grid=(2, t // tm),
        in_specs=[pl.BlockSpec((SUBLANES, tm), lambda p, i: (0, i))],
        out_specs=[pl.BlockSpec((1, SUBLANES, tm), lambda p, i: (p, 0, i)),
                   pl.BlockSpec((N_EXPERTS, LANES), lambda p, i: (0, 0))],
        out_shape=[jax.ShapeDtypeStruct((2, SUBLANES, t), I32),
                   jax.ShapeDtypeStruct((N_EXPERTS, LANES), F32)],
        scratch_shapes=[pltpu.VMEM((N_EXPERTS, LANES), F32)],
        compiler_params=_cp(("arbitrary", "arbitrary"), 32),
        name="plan",
    )(ids)


def _invert_kernel(dest_ref, slot_ref, fill_ref, sem, *, n_slots):
    fill_ref[...] = jnp.full(fill_ref.shape, n_slots, I32)
    fill = pltpu.make_async_copy(fill_ref, slot_ref, sem)
    fill.start()
    fill.wait()

    def scatter(j, c):
        for u in range(DMA_UNROLL):
            q = j * DMA_UNROLL + u
            slot_ref[dest_ref[q]] = q
        return c

    lax.fori_loop(0, n_slots // DMA_UNROLL, scatter, 0)


def _invert(dest_flat, n_rows):
    n_slots = dest_flat.shape[0]
    assert n_slots % DMA_UNROLL == 0
    return pl.pallas_call(
        functools.partial(_invert_kernel, n_slots=n_slots),
        grid_spec=pltpu.PrefetchScalarGridSpec(
            num_scalar_prefetch=1,
            grid=(1,),
            in_specs=[],
            out_specs=pl.BlockSpec(memory_space=pltpu.SMEM),
            scratch_shapes=[pltpu.VMEM((n_rows,), I32), pltpu.SemaphoreType.DMA(())]),
        out_shape=jax.ShapeDtypeStruct((n_rows,), I32),
        compiler_params=_cp(("arbitrary",), 16),
        name="invert",
    )(dest_flat)


def _experts_kernel(te_ref, nx_ref, sl_ref, vr_ref, rs_ref, na_ref, hp_hbm, wg_hbm, wu_hbm, wd_hbm, yt_hbm,
                    wgf_ref, wuf_ref, wdf_ref, wgbf_ref, wubf_ref, wdbf_ref, xbuf_ref, ybuf_ref,
                    sems, xsems, ysems, *, t_total):
    i = pl.program_id(0)
    n_act = na_ref[0]
    active = i < n_act
    changed = (i == 0) | (te_ref[i] != te_ref[jnp.maximum(i - 1, 0)])

    def weight_copies(e, slot):
        return (pltpu.make_async_copy(wg_hbm.at[e], wgf_ref.at[slot], sems.at[slot, 0]),
                pltpu.make_async_copy(wu_hbm.at[e], wuf_ref.at[slot], sems.at[slot, 1]),
                pltpu.make_async_copy(wd_hbm.at[e], wdf_ref.at[slot], sems.at[slot, 2]))

    def in_copy(tile, r):
        tok = rs_ref[tile * EXPERT_TILE + r] & (t_total - 1)
        return pltpu.make_async_copy(hp_hbm.at[tok], xbuf_ref.at[tile % 2, r], xsems.at[tile % 2])

    def out_copy(tile, r):
        dst = yt_hbm.at[rs_ref[tile * EXPERT_TILE + r]]
        return pltpu.make_async_copy(ybuf_ref.at[tile % 2, r], dst, ysems.at[tile % 2])

    def for_rows(tile, make_copy, fn):
        n = vr_ref[tile]
        groups = lax.shift_right_logical(n, DMA_UNROLL.bit_length() - 1)

        def body(j, c):
            for u in range(DMA_UNROLL):
                fn(make_copy(tile, j * DMA_UNROLL + u))
            return c

        def tail(r, c):
            fn(make_copy(tile, r))
            return c

        lax.fori_loop(0, groups, body, 0)
        lax.fori_loop(groups * DMA_UNROLL, n, tail, 0)

    def wait_rows(tile, make_copy, whole_tile_copy):
        full = vr_ref[tile] == EXPERT_TILE

        @pl.when(full)
        def _():
            whole_tile_copy.wait()

        @pl.when(jnp.logical_not(full))
        def _():
            for_rows(tile, make_copy, lambda cp: cp.wait())

    def in_tile(tile):
        return pltpu.make_async_copy(hp_hbm.at[pl.ds(0, EXPERT_TILE)], xbuf_ref.at[tile % 2], xsems.at[tile % 2])

    def out_tile(tile):
        return pltpu.make_async_copy(ybuf_ref.at[tile % 2], yt_hbm.at[pl.ds(0, EXPERT_TILE)], ysems.at[tile % 2])

    @pl.when(i == 0)
    def _():
        xbuf_ref[...] = jnp.zeros_like(xbuf_ref)
        for cp in weight_copies(te_ref[0], 0):
            cp.start(priority=1)
        for_rows(i, in_copy, lambda cp: cp.start())

    @pl.when(i + 1 < n_act)
    def _():
        for_rows(i + 1, in_copy, lambda cp: cp.start())

    @pl.when(active & changed)
    def _():
        slot = sl_ref[i]
        nxt = nx_ref[i]

        @pl.when(nxt >= 0)
        def _():
            for cp in weight_copies(nxt, 1 - slot):
                cp.start(priority=1)

        for cp in weight_copies(te_ref[i], slot):
            cp.wait()
        wgbf_ref[...] = wgf_ref[slot].astype(BF16)
        wubf_ref[...] = wuf_ref[slot].astype(BF16)
        wdbf_ref[...] = wdf_ref[slot].astype(BF16)

    @pl.when(active)
    def _():
        wait_rows(i, in_copy, in_tile(i))
        x = _from_token_major(xbuf_ref[i % 2])
        hg = _dot(x, wgbf_ref[...])
        hu = _dot(x, wubf_ref[...])
        act = (hg * jax.nn.sigmoid(hg) * hu).astype(BF16)
        ybuf_ref[i % 2] = _to_token_major(_dot(act, wdbf_ref[...]).astype(BF16))

    @pl.when((i >= 1) & (i - 1 < n_act))
    def _():
        wait_rows(i - 1, out_copy, out_tile(i - 1))

    @pl.when(active)
    def _():
        for_rows(i, out_copy, lambda cp: cp.start())

    @pl.when(active & (i == pl.num_programs(0) - 1))
    def _():
        wait_rows(i, out_copy, out_tile(i))


def _experts(tile_e, next_e, slot, valid, row_slot, n_act, hp, w_gate, w_up, w_down, n_tiles):
    t = hp.shape[0]
    assert t & (t - 1) == 0, "the row map packs slot * T + token with T a power of two"
    any_spec = pl.BlockSpec(memory_space=pl.ANY)
    tile_buf = pltpu.VMEM((2, EXPERT_TILE, ROW_TILES, LANES), BF16)
    return pl.pallas_call(
        functools.partial(_experts_kernel, t_total=t),
        grid_spec=pltpu.PrefetchScalarGridSpec(
            num_scalar_prefetch=6,
            grid=(n_tiles,),
            in_specs=[any_spec, any_spec, any_spec, any_spec],
            out_specs=any_spec,
            scratch_shapes=[pltpu.VMEM((2, D_MODEL, D_FF), F32),
                            pltpu.VMEM((2, D_MODEL, D_FF), F32),
                            pltpu.VMEM((2, D_FF, D_MODEL), F32),
                            pltpu.VMEM((D_MODEL, D_FF), BF16),
                            pltpu.VMEM((D_MODEL, D_FF), BF16),
                            pltpu.VMEM((D_FF, D_MODEL), BF16),
                            tile_buf, tile_buf,
                            pltpu.SemaphoreType.DMA((2, 3)),
                            pltpu.SemaphoreType.DMA((2,)),
                            pltpu.SemaphoreType.DMA((2,))]),
        out_shape=jax.ShapeDtypeStruct((2 * t, ROW_TILES, LANES), BF16),
        compiler_params=_cp(("arbitrary",), 48, has_side_effects=True),
        name="experts",
    )(tile_e, next_e, slot, valid, row_slot, n_act, hp, w_gate, w_up, w_down)


def _combine_kernel(y0_ref, y1_ref, x_ref, wt_ref, p_ref, nw_ref, wg_ref, wp_ref, fw_ref, o_ref):
    wt = wt_ref[...]
    y0 = _from_token_major(y0_ref[...]).astype(F32)
    y1 = _from_token_major(y1_ref[...]).astype(F32)
    x2 = x_ref[...] + wt[:, 0:1] * y0 + wt[:, 1:2] * y1
    hn = _rms(x2, nw_ref[...]).astype(BF16)
    pg = jax.nn.sigmoid(_dot(hn, wg_ref[...]))
    x3 = x2 + pg * _dot(p_ref[...].astype(BF16), wp_ref[...])
    o_ref[...] = _rms(x3, fw_ref[...])


def _combine_ple(yt, x1, wts_t, p, ple_norm_w, w_gate, w_proj, final_w, tm=512):
    t = x1.shape[0]
    nb = t // tm
    row = lambda i: (i, 0)
    fix = lambda i: (0, 0)
    return pl.pallas_call(
        _combine_kernel,
        grid=(nb,),
        in_specs=[pl.BlockSpec((tm, ROW_TILES, LANES), lambda i: (i, 0, 0)),
                  pl.BlockSpec((tm, ROW_TILES, LANES), lambda i: (nb + i, 0, 0)),
                  pl.BlockSpec((tm, D_MODEL), row),
                  pl.BlockSpec((tm, SUBLANES), row),
                  pl.BlockSpec((tm, PLE_DIM), row),
                  pl.BlockSpec((1, D_MODEL), fix),
                  pl.BlockSpec((D_MODEL, D_MODEL), fix),
                  pl.BlockSpec((PLE_DIM, D_MODEL), fix),
                  pl.BlockSpec((1, D_MODEL), fix)],
        out_specs=pl.BlockSpec((tm, D_MODEL), row),
        out_shape=jax.ShapeDtypeStruct((t, D_MODEL), F32),
        compiler_params=_cp(("arbitrary",), 56),
        name="combine_ple",
    )(yt, yt, x1, wts_t, p, ple_norm_w, w_gate, w_proj, final_w)


def _tile_table(counts, n_tiles):
    tiles = (counts.astype(I32) + (EXPERT_TILE - 1)) // EXPERT_TILE
    ends = jnp.cumsum(tiles)
    n_act = ends[-1]
    idx = jnp.minimum(jnp.arange(n_tiles, dtype=I32), n_act - 1)
    tile_e = jnp.sum((idx[:, None] >= ends[None, :]).astype(I32), axis=1).astype(I32)
    run_end = ends[tile_e]
    next_e = jnp.where(run_end < n_act, tile_e[jnp.minimum(run_end, n_tiles - 1)], -1).astype(I32)
    new_run = jnp.concatenate([jnp.ones((1,), I32), (tile_e[1:] != tile_e[:-1]).astype(I32)])
    slot = ((jnp.cumsum(new_run) - 1) % 2).astype(I32)
    arange = jnp.arange(n_tiles, dtype=I32)
    last = arange == run_end - 1
    valid = jnp.where(last, counts.astype(I32)[tile_e] - (tiles[tile_e] - 1) * EXPERT_TILE, EXPERT_TILE)
    valid = jnp.where(arange < n_act, valid, 0).astype(I32)
    return tile_e, next_e, slot, valid, n_act.reshape(1).astype(I32)


def _block(x, p, norm_mix_w, w_in, b_merge, w_alpha_up, b_alpha_up, gla_norm_w, w_gla_out,
           conv_w, conv_b, w_conv_out, w_mix_out, norm_ffn_w, w_router_group, b_router_group,
           w_router_expert, b_router_expert, w_e_gate, w_e_up, w_e_down, ple_norm_w,
           w_ple_gate, w_ple_proj, final_norm_w):
    t = x.shape[0]
    n_tiles = (2 * t) // EXPERT_TILE + N_EXPERTS
    n_rows = n_tiles * EXPERT_TILE

    w_up = jnp.pad(w_alpha_up, ((0, LANES - GLA_GATE_RANK), (0, 0))).astype(BF16)
    w_up = w_up.reshape(LANES, GLA_HEADS, GLA_DK).transpose(1, 0, 2)
    b_up = b_alpha_up.reshape(GLA_HEADS, 1, GLA_DK)
    gnw = gla_norm_w.reshape(GLA_HEADS, 1, GLA_DV)
    wr_t = jnp.zeros((ROUTER_ROWS, D_MODEL), F32)
    wr_t = wr_t.at[0:N_GROUPS].set(w_router_group.T)
    wr_t = wr_t.at[EXPERT_ROW0:EXPERT_ROW0 + N_EXPERTS].set(w_router_expert.T)
    br = jnp.zeros((ROUTER_ROWS, 1), F32)
    br = br.at[0:N_GROUPS, 0].set(b_router_group)
    br = br.at[EXPERT_ROW0:EXPERT_ROW0 + N_EXPERTS, 0].set(b_router_expert)

    w_in_t = w_in.T
    h, a_low = _norm_in(x, norm_mix_w.reshape(1, D_MODEL), w_in_t)
    qkvg = _proj_qkvg(h, w_in_t)
    ob = _proj_conv(h, w_in_t, conv_w, conv_b.reshape(1, D_MODEL))
    gates = _proj_gates(h, w_in_t, b_merge.reshape(1, 2 * D_MODEL))
    oa = _gla(qkvg, a_low, w_up, b_up, gnw)
    mixed = _merge(oa, ob, w_gla_out, w_conv_out, gates)
    x1 = _mix(mixed, w_mix_out, x)

    hp, ids, wts = _route(x1, norm_ffn_w.reshape(1, D_MODEL), wr_t, br)
    dest, counts = _plan(ids)
    dest_flat = dest[1, 0:2].reshape(2 * t)
    tile_e, next_e, slot, valid, n_act = _tile_table(counts[:, 0], n_tiles)
    row_slot = _invert(dest_flat, n_rows)
    yt = _experts(tile_e, next_e, slot, valid, row_slot, n_act, hp, w_e_gate, w_e_up, w_e_down, n_tiles)
    return _combine_ple(yt, x1, wts.T, p,
                        ple_norm_w.reshape(1, D_MODEL), w_ple_gate.astype(BF16),
                        w_ple_proj.astype(BF16), final_norm_w.reshape(1, D_MODEL))


def kernel(x, p, norm_mix_w, w_in, b_merge, w_alpha_up, b_alpha_up, gla_norm_w, w_gla_out, conv_w, conv_b, w_conv_out, w_mix_out, norm_ffn_w, w_router_group, b_router_group, w_router_expert, b_router_expert, w_e_gate, w_e_up, w_e_down, ple_norm_w, w_ple_gate, w_ple_proj, final_norm_w):
    depth, batch = p.shape[0], x.shape[0]
    assert depth == 1 and batch == 1, "kernel is specialised to one layer and one sequence"
    out = _block(x[0], p[0, 0], norm_mix_w[0], w_in[0], b_merge[0], w_alpha_up[0], b_alpha_up[0],
                 gla_norm_w[0], w_gla_out[0], conv_w[0], conv_b[0], w_conv_out[0], w_mix_out[0],
                 norm_ffn_w[0], w_router_group[0], b_router_group[0], w_router_expert[0],
                 b_router_expert[0], w_e_gate[0], w_e_up[0], w_e_down[0], ple_norm_w[0],
                 w_ple_gate[0], w_ple_proj[0], final_norm_w)
    return out[None]
```

```python
import functools

import jax
import jax.numpy as jnp
from jax import lax
from jax.experimental import pallas as pl
from jax.experimental.pallas import tpu as pltpu

F32 = jnp.float32
BF16 = jnp.bfloat16
I32 = jnp.int32

D_MODEL = 2048
PLE_DIM = 256
EPS = 1e-6
LOG2_E = 1.4426950408889634
GLA_HEADS = 4
GLA_DK = 256
GLA_DV = 512
GLA_KEY = GLA_HEADS * GLA_DK
GLA_VAL = GLA_HEADS * GLA_DV
GLA_GATE_RANK = 16
GLA_GATE_NORM = 16.0
GLA_CHUNK = 64
CONV_K = 3
N_GROUPS = 4
EXPERTS_PER_GROUP = 8
N_EXPERTS = N_GROUPS * EXPERTS_PER_GROUP
D_FF = 512

QKVG_COLS = 2 * GLA_KEY + 2 * GLA_VAL
CONV_COL0 = QKVG_COLS + GLA_GATE_RANK

LANES = 128
SUBLANES = 8
MXU_COLS = 256
ROW_CHUNK = 256
ROW_TILES = D_MODEL // LANES

EXPERT_TILE = 256
DMA_UNROLL = 8
ROUTER_ROWS = 64
EXPERT_ROW0 = 8

MIB = 1024 * 1024


def _cp(sem, vmem_mib, **kw):
    return pltpu.CompilerParams(dimension_semantics=sem, vmem_limit_bytes=int(vmem_mib * MIB), **kw)


def _rms(x, w):
    return x * lax.rsqrt(jnp.mean(x * x, axis=-1, keepdims=True) + EPS) * w


def _dot(a, b):
    return jnp.dot(a, b, preferred_element_type=F32)


def _dot_nt(a, b):
    return lax.dot_general(a, b, (((1,), (1,)), ((), ())), preferred_element_type=F32)


def _dot_tn(a, b):
    return lax.dot_general(a, b, (((0,), (0,)), ((), ())), preferred_element_type=F32)


def _split_bf16(x):
    hi = x.astype(BF16)
    lo = (x - hi.astype(F32)).astype(BF16)
    return hi, lo


def _to_token_major(val):
    return val.reshape(val.shape[0], ROW_TILES, LANES)


def _from_token_major(val):
    return val.reshape(val.shape[0], D_MODEL)


def _norm_in_kernel(x_ref, w_ref, wal_ref, h_ref, al_ref, walbf_ref):
    @pl.when(pl.program_id(0) == 0)
    def _():
        walbf_ref[...] = wal_ref[...].astype(BF16)

    h = _rms(x_ref[...], w_ref[...]).astype(BF16)
    h_ref[...] = h
    al_ref[...] = _dot_nt(h, walbf_ref[...])


def _norm_in(x, w, w_in_t, tm=512):
    t = x.shape[0]
    return pl.pallas_call(
        _norm_in_kernel,
        grid=(t // tm,),
        in_specs=[pl.BlockSpec((tm, D_MODEL), lambda i: (i, 0)),
                  pl.BlockSpec((1, D_MODEL), lambda i: (0, 0)),
                  pl.BlockSpec((LANES, D_MODEL), lambda i: (QKVG_COLS // LANES, 0))],
        out_specs=[pl.BlockSpec((tm, D_MODEL), lambda i: (i, 0)),
                   pl.BlockSpec((tm, LANES), lambda i: (i, 0))],
        out_shape=[jax.ShapeDtypeStruct((t, D_MODEL), BF16),
                   jax.ShapeDtypeStruct((t, LANES), F32)],
        scratch_shapes=[pltpu.VMEM((LANES, D_MODEL), BF16)],
        compiler_params=_cp(("arbitrary",), 32),
        name="norm_in",
    )(x, w, w_in_t)


def _w_rows_spec(row0, tn):
    assert row0 % SUBLANES == 0 and tn % SUBLANES == 0
    return pl.BlockSpec((pl.Element(tn), pl.Element(D_MODEL)),
                        lambda n, m: (pl.multiple_of(row0 + n * tn, SUBLANES), 0))


def _proj_qkvg_kernel(h_ref, w_ref, o_ref, wbf_ref):
    @pl.when(pl.program_id(1) == 0)
    def _():
        wbf_ref[...] = w_ref[...].astype(BF16)

    o_ref[...] = _dot_nt(h_ref[...], wbf_ref[...]).astype(o_ref.dtype)


def _proj_qkvg(h, w_in_t, tm=2048, tn=1024):
    t = h.shape[0]
    return pl.pallas_call(
        _proj_qkvg_kernel,
        grid=(QKVG_COLS // tn, t // tm),
        in_specs=[pl.BlockSpec((tm, D_MODEL), lambda n, m: (m, 0)),
                  pl.BlockSpec((tn, D_MODEL), lambda n, m: (n, 0))],
        out_specs=pl.BlockSpec((tm, tn), lambda n, m: (m, n)),
        out_shape=jax.ShapeDtypeStruct((t, QKVG_COLS), BF16),
        scratch_shapes=[pltpu.VMEM((tn, D_MODEL), BF16)],
        compiler_params=_cp(("arbitrary", "arbitrary"), 56),
        name="proj_qkvg",
    )(h, w_in_t)


def _proj_conv_kernel(h_ref, wb_ref, wc_ref, wx_ref, cw_ref, cb_ref, o_ref,
                      wbbf_ref, wcbf_ref, wxbf_ref, prev_ref):
    m = pl.program_id(1)

    @pl.when(m == 0)
    def _():
        prev_ref[...] = jnp.zeros_like(prev_ref)
        wbbf_ref[...] = wb_ref[...].astype(BF16)
        wcbf_ref[...] = wc_ref[...].astype(BF16)
        wxbf_ref[...] = wx_ref[...].astype(BF16)

    h = h_ref[...]
    tm = h.shape[0]
    row = lax.broadcasted_iota(I32, (tm, MXU_COLS), 0)
    for c in range(0, o_ref.shape[1], MXU_COLS):
        cols = slice(c, c + MXU_COLS)
        b = _dot_nt(h, wbbf_ref[cols, :])
        s = _dot_nt(h, wcbf_ref[cols, :]) * _dot_nt(h, wxbf_ref[cols, :])
        prev = prev_ref[:, cols]
        p1 = prev[SUBLANES - 1:SUBLANES, :]
        p2 = prev[SUBLANES - 2:SUBLANES - 1, :]
        s1 = jnp.where(row == 0, p1, pltpu.roll(s, 1, 0))
        s2 = jnp.where(row == 0, p2, jnp.where(row == 1, p1, pltpu.roll(s, 2, 0)))
        cw = cw_ref[:, cols]
        u = cw[2:3, :] * s + cw[1:2, :] * s1 + cw[0:1, :] * s2 + cb_ref[:, cols]
        o_ref[:, cols] = (b * u).astype(o_ref.dtype)
        prev_ref[:, cols] = s[tm - SUBLANES:, :]


def _proj_conv(h, w_in_t, conv_w, conv_b, tm=1024, tn=512):
    t = h.shape[0]
    return pl.pallas_call(
        _proj_conv_kernel,
        grid=(D_MODEL // tn, t // tm),
        in_specs=[pl.BlockSpec((tm, D_MODEL), lambda n, m: (m, 0))] +
                 [_w_rows_spec(CONV_COL0 + seg * D_MODEL, tn) for seg in range(3)] +
                 [pl.BlockSpec((CONV_K, tn), lambda n, m: (0, n)),
                  pl.BlockSpec((1, tn), lambda n, m: (0, n))],
        out_specs=pl.BlockSpec((tm, tn), lambda n, m: (m, n)),
        out_shape=jax.ShapeDtypeStruct((t, D_MODEL), BF16),
        scratch_shapes=[pltpu.VMEM((tn, D_MODEL), BF16), pltpu.VMEM((tn, D_MODEL), BF16),
                        pltpu.VMEM((tn, D_MODEL), BF16), pltpu.VMEM((SUBLANES, tn), F32)],
        compiler_params=_cp(("arbitrary", "arbitrary"), 48),
        name="proj_conv",
    )(h, w_in_t, w_in_t, w_in_t, conv_w, conv_b)


def _proj_gates_kernel(h_ref, w_ref, b_ref, o_ref, wbf_ref):
    @pl.when(pl.program_id(1) == 0)
    def _():
        wbf_ref[...] = w_ref[...].astype(BF16)

    for c in range(0, o_ref.shape[1], MXU_COLS):
        cols = slice(c, c + MXU_COLS)
        for r in range(0, o_ref.shape[0], ROW_CHUNK):
            rows = slice(r, r + ROW_CHUNK)
            z = _dot_nt(h_ref[rows, :], wbf_ref[cols, :]) + b_ref[:, cols]
            o_ref[rows, cols] = jax.nn.sigmoid(z).astype(o_ref.dtype)


def _proj_gates(h, w_in_t, b_merge, tm=2048, tn=1024):
    t = h.shape[0]
    return pl.pallas_call(
        _proj_gates_kernel,
        grid=(2 * D_MODEL // tn, t // tm),
        in_specs=[pl.BlockSpec((tm, D_MODEL), lambda n, m: (m, 0)),
                  _w_rows_spec(CONV_COL0 + 3 * D_MODEL, tn),
                  pl.BlockSpec((1, tn), lambda n, m: (0, n))],
        out_specs=pl.BlockSpec((tm, tn), lambda n, m: (m, n)),
        out_shape=jax.ShapeDtypeStruct((t, 2 * D_MODEL), BF16),
        scratch_shapes=[pltpu.VMEM((tn, D_MODEL), BF16)],
        compiler_params=_cp(("arbitrary", "arbitrary"), 56),
        name="proj_gates",
    )(h, w_in_t, b_merge)


def _gla_kernel(q_ref, k_ref, v_ref, g_ref, al_ref, wup_ref, bup_ref, nw_ref, o_ref,
                st_ref, b_ref, bl_ref, qd_ref, ki_ref, kd_ref, oi_ref, u_ref, *, n_chunks):
    c_len = GLA_CHUNK

    @pl.when(pl.program_id(1) == 0)
    def _():
        st_ref[...] = jnp.zeros_like(st_ref)

    row = lax.broadcasted_iota(I32, (c_len, c_len), 0)
    col = lax.broadcasted_iota(I32, (c_len, c_len), 1)
    causal = col <= row
    tril = jnp.where(causal, 1.0, 0.0).astype(BF16)

    def decays(h):
        kc = slice(h * GLA_DK, (h + 1) * GLA_DK)
        z = _dot(al_ref[...].astype(BF16), wup_ref[h]) + bup_ref[h]
        la = (jnp.minimum(z, 0.0) - jnp.log1p(jnp.exp(-jnp.abs(z)))) * (LOG2_E / GLA_GATE_NORM)
        la_hi, la_lo = _split_bf16(la)
        for c in range(n_chunks):
            r0 = c * c_len
            b = _dot(tril, la_hi[r0:r0 + c_len]) + _dot(tril, la_lo[r0:r0 + c_len])
            b_ref[pl.ds(r0, c_len), kc] = b
            bl_ref[pl.ds(r0, c_len), kc] = jnp.broadcast_to(b[c_len - 1:c_len, :], b.shape)
        b = b_ref[:, kc]
        q = q_ref[:, kc].astype(F32)
        k = k_ref[:, kc].astype(F32)
        qd_ref[:, kc] = (q * jnp.exp2(b)).astype(BF16)
        ki_ref[:, kc] = (k * jnp.exp2(-b)).astype(BF16)
        kd_ref[:, kc] = (k * jnp.exp2(bl_ref[:, kc] - b)).astype(BF16)

    def local_products(h):
        kc = slice(h * GLA_DK, (h + 1) * GLA_DK)
        vc = slice(h * GLA_DV, (h + 1) * GLA_DV)
        for c in range(n_chunks):
            sl = pl.ds(c * c_len, c_len)
            v = v_ref[sl, vc]
            att = jnp.where(causal, _dot_nt(qd_ref[sl, kc], ki_ref[sl, kc]), 0.0).astype(BF16)
            oi_ref[sl, vc] = _dot(att, v)
            u_ref[h, c] = _dot_tn(v, kd_ref[sl, kc])

    def recurrence(h):
        kc = slice(h * GLA_DK, (h + 1) * GLA_DK)
        vc = slice(h * GLA_DV, (h + 1) * GLA_DV)
        for c in range(n_chunks):
            sl = pl.ds(c * c_len, c_len)
            st = st_ref[h]
            oi_ref[sl, vc] = oi_ref[sl, vc] + _dot_nt(qd_ref[sl, kc], st.astype(BF16))
            st_ref[h] = st * jnp.exp2(bl_ref[pl.ds(c * c_len, 1), kc]) + u_ref[h, c]
        o = oi_ref[:, vc]
        o = o * lax.rsqrt(jnp.mean(o * o, axis=-1, keepdims=True) + EPS * GLA_DK) * nw_ref[h]
        g = g_ref[:, vc].astype(F32)
        o_ref[:, vc] = (o * (g * jax.nn.sigmoid(g))).astype(o_ref.dtype)

    decays(0)
    local_products(0)
    decays(1)
    recurrence(0)
    local_products(1)
    recurrence(1)


def _gla(qkvg, a_low, w_up, b_up, norm_w, tb=512):
    t = qkvg.shape[0]
    hp = 2
    dk, dv = hp * GLA_DK, hp * GLA_DV
    kq = GLA_KEY // dk
    kv = 2 * GLA_KEY // dv
    kg = kv + GLA_VAL // dv
    n_chunks = tb // GLA_CHUNK
    kern = functools.partial(_gla_kernel, n_chunks=n_chunks)
    return pl.pallas_call(
        kern,
        grid=(GLA_HEADS // hp, t // tb),
        in_specs=[pl.BlockSpec((tb, dk), lambda h, i: (i, h)),
                  pl.BlockSpec((tb, dk), lambda h, i: (i, kq + h)),
                  pl.BlockSpec((tb, dv), lambda h, i: (i, kv + h)),
                  pl.BlockSpec((tb, dv), lambda h, i: (i, kg + h)),
                  pl.BlockSpec((tb, LANES), lambda h, i: (i, 0)),
                  pl.BlockSpec((hp, LANES, GLA_DK), lambda h, i: (h, 0, 0)),
                  pl.BlockSpec((hp, 1, GLA_DK), lambda h, i: (h, 0, 0)),
                  pl.BlockSpec((hp, 1, GLA_DV), lambda h, i: (h, 0, 0))],
        out_specs=pl.BlockSpec((tb, dv), lambda h, i: (i, h)),
        out_shape=jax.ShapeDtypeStruct((t, GLA_VAL), BF16),
        scratch_shapes=[pltpu.VMEM((hp, GLA_DV, GLA_DK), F32),
                        pltpu.VMEM((tb, dk), F32), pltpu.VMEM((tb, dk), F32),
                        pltpu.VMEM((tb, dk), BF16), pltpu.VMEM((tb, dk), BF16),
                        pltpu.VMEM((tb, dk), BF16),
                        pltpu.VMEM((tb, dv), F32),
                        pltpu.VMEM((hp, n_chunks, GLA_DV, GLA_DK), F32)],
        compiler_params=_cp(("arbitrary", "arbitrary"), 40),
        name="gla",
    )(qkvg, qkvg, qkvg, qkvg, a_low, w_up, b_up, norm_w)


def _merge_kernel(oa_ref, ob_ref, wa_ref, wb_ref, g0_ref, g1_ref, o_ref, wabf_ref, wbbf_ref):
    @pl.when(pl.program_id(1) == 0)
    def _():
        wabf_ref[...] = wa_ref[...].astype(BF16)
        wbbf_ref[...] = wb_ref[...].astype(BF16)

    oa = oa_ref[...]
    ob = ob_ref[...]
    for c in range(0, o_ref.shape[1], MXU_COLS):
        cols = slice(c, c + MXU_COLS)
        a = _dot(oa, wabf_ref[:, cols])
        b = _dot(ob, wbbf_ref[:, cols])
        o_ref[:, cols] = (g0_ref[:, cols].astype(F32) * a + g1_ref[:, cols].astype(F32) * b).astype(o_ref.dtype)


def _merge(oa, ob, w_a, w_b, gates, tm=1024, tn=512):
    t = oa.shape[0]
    nb = D_MODEL // tn
    return pl.pallas_call(
        _merge_kernel,
        grid=(nb, t // tm),
        in_specs=[pl.BlockSpec((tm, D_MODEL), lambda n, m: (m, 0)),
                  pl.BlockSpec((tm, D_MODEL), lambda n, m: (m, 0)),
                  pl.BlockSpec((D_MODEL, tn), lambda n, m: (0, n)),
                  pl.BlockSpec((D_MODEL, tn), lambda n, m: (0, n)),
                  pl.BlockSpec((tm, tn), lambda n, m: (m, n)),
                  pl.BlockSpec((tm, tn), lambda n, m: (m, nb + n))],
        out_specs=pl.BlockSpec((tm, tn), lambda n, m: (m, n)),
        out_shape=jax.ShapeDtypeStruct((t, D_MODEL), BF16),
        scratch_shapes=[pltpu.VMEM((D_MODEL, tn), BF16), pltpu.VMEM((D_MODEL, tn), BF16)],
        compiler_params=_cp(("arbitrary", "arbitrary"), 56),
        name="merge",
    )(oa, ob, w_a, w_b, gates, gates)


def _mix_kernel(a_ref, w_ref, x_ref, o_ref, wbf_ref):
    @pl.when(pl.program_id(1) == 0)
    def _():
        wbf_ref[...] = w_ref[...].astype(BF16)

    a = a_ref[...]
    for c in range(0, o_ref.shape[1], MXU_COLS):
        cols = slice(c, c + MXU_COLS)
        o_ref[:, cols] = x_ref[:, cols] + _dot(a, wbf_ref[:, cols])


def _mix(mixed, w, x, tm=2048, tn=512):
    t = x.shape[0]
    return pl.pallas_call(
        _mix_kernel,
        grid=(D_MODEL // tn, t // tm),
        in_specs=[pl.BlockSpec((tm, D_MODEL), lambda n, m: (m, 0)),
                  pl.BlockSpec((D_MODEL, tn), lambda n, m: (0, n)),
                  pl.BlockSpec((tm, tn), lambda n, m: (m, n))],
        out_specs=pl.BlockSpec((tm, tn), lambda n, m: (m, n)),
        out_shape=jax.ShapeDtypeStruct((t, D_MODEL), F32),
        scratch_shapes=[pltpu.VMEM((D_MODEL, tn), BF16)],
        compiler_params=_cp(("arbitrary", "arbitrary"), 52),
        name="mix",
    )(mixed, w, x)


def _route_kernel(x_ref, nw_ref, wr_ref, br_ref, hp_ref, ids_ref, wts_ref):
    h = _rms(x_ref[...], nw_ref[...])
    hp_ref[...] = _to_token_major(h.astype(BF16))
    h_hi, h_lo = _split_bf16(h)
    w_hi, w_lo = _split_bf16(wr_ref[...])
    logits = _dot_nt(w_hi, h_hi) + _dot_nt(w_hi, h_lo) + _dot_nt(w_lo, h_hi) + br_ref[...]
    tm = logits.shape[1]

    best = logits[0:1, :]
    gidx = jnp.zeros((1, tm), I32)
    for i in range(1, N_GROUPS):
        li = logits[i:i + 1, :]
        take = li > best
        best = jnp.where(take, li, best)
        gidx = jnp.where(take, i, gidx)
    gsum = jnp.zeros((1, tm), F32)
    for i in range(N_GROUPS):
        gsum = gsum + jnp.exp(logits[i:i + 1, :] - best)
    g_p = 1.0 / gsum

    sel = logits[EXPERT_ROW0:EXPERT_ROW0 + EXPERTS_PER_GROUP, :]
    for g in range(1, N_GROUPS):
        r0 = EXPERT_ROW0 + g * EXPERTS_PER_GROUP
        sel = jnp.where(gidx == g, logits[r0:r0 + EXPERTS_PER_GROUP, :], sel)
    eio = lax.broadcasted_iota(I32, sel.shape, 0)
    m1 = jnp.max(sel, axis=0, keepdims=True)
    i1 = jnp.min(jnp.where(sel == m1, eio, EXPERTS_PER_GROUP), axis=0, keepdims=True)
    rest = jnp.where(eio == i1, -jnp.inf, sel)
    m2 = jnp.max(rest, axis=0, keepdims=True)
    i2 = jnp.min(jnp.where(rest == m2, eio, EXPERTS_PER_GROUP), axis=0, keepdims=True)
    p2 = jnp.exp(m2 - m1)
    w1 = g_p / (1.0 + p2)
    w2 = g_p * p2 / (1.0 + p2)
    e1 = gidx * EXPERTS_PER_GROUP + i1
    e2 = gidx * EXPERTS_PER_GROUP + i2
    rio = lax.broadcasted_iota(I32, (SUBLANES, tm), 0)
    ids_ref[...] = jnp.where(rio == 0, e1, jnp.where(rio == 1, e2, 0))
    wts_ref[...] = jnp.where(rio == 0, w1, jnp.where(rio == 1, w2, 0.0))


def _route(x1, norm_w, wr_t, br, tm=512):
    t = x1.shape[0]
    return pl.pallas_call(
        _route_kernel,
        grid=(t // tm,),
        in_specs=[pl.BlockSpec((tm, D_MODEL), lambda i: (i, 0)),
                  pl.BlockSpec((1, D_MODEL), lambda i: (0, 0)),
                  pl.BlockSpec((ROUTER_ROWS, D_MODEL), lambda i: (0, 0)),
                  pl.BlockSpec((ROUTER_ROWS, 1), lambda i: (0, 0))],
        out_specs=[pl.BlockSpec((tm, ROW_TILES, LANES), lambda i: (i, 0, 0)),
                   pl.BlockSpec((SUBLANES, tm), lambda i: (0, i)),
                   pl.BlockSpec((SUBLANES, tm), lambda i: (0, i))],
        out_shape=[jax.ShapeDtypeStruct((t, ROW_TILES, LANES), BF16),
                   jax.ShapeDtypeStruct((SUBLANES, t), I32),
                   jax.ShapeDtypeStruct((SUBLANES, t), F32)],
        compiler_params=_cp(("arbitrary",), 32),
        name="route",
    )(x1, norm_w, wr_t, br)


def _plan_kernel(ids_ref, dest_ref, cnt_ref, base_ref):
    phase = pl.program_id(0)
    step = pl.program_id(1)
    tm = ids_ref.shape[1]
    eio = lax.broadcasted_iota(I32, (N_EXPERTS, tm), 0)
    ids = ids_ref[...]
    oh = [jnp.where(eio == ids[k:k + 1, :], 1.0, 0.0) for k in range(2)]

    @pl.when((phase == 0) & (step == 0))
    def _():
        base_ref[...] = jnp.zeros_like(base_ref)

    @pl.when(phase == 0)
    def _():
        cnt = jnp.sum(oh[0] + oh[1], axis=1, keepdims=True)
        base_ref[...] = base_ref[...] + cnt
        dest_ref[0] = jnp.zeros(dest_ref.shape[1:], I32)
        cnt_ref[...] = base_ref[...]

    @pl.when((phase == 1) & (step == 0))
    def _():
        tiles = jnp.floor((base_ref[...] + (EXPERT_TILE - 1)) * (1.0 / EXPERT_TILE))
        r = lax.broadcasted_iota(I32, (N_EXPERTS, N_EXPERTS), 0)
        c = lax.broadcasted_iota(I32, (N_EXPERTS, N_EXPERTS), 1)
        lower = jnp.where(c < r, 1.0, 0.0).astype(BF16)
        base_ref[...] = _dot(lower, tiles.astype(BF16)) * float(EXPERT_TILE)

    @pl.when(phase == 1)
    def _():
        r = lax.broadcasted_iota(I32, (tm, tm), 0)
        c = lax.broadcasted_iota(I32, (tm, tm), 1)
        upper = jnp.where(r <= c, 1.0, 0.0).astype(BF16)
        base = base_ref[...][:, 0:1]
        rows = []
        for k in range(2):
            cum = _dot(oh[k].astype(BF16), upper)
            rows.append(jnp.sum(oh[k] * (cum - 1.0 + base), axis=0, keepdims=True))
            base = base + cum[:, tm - 1:tm]
        base_ref[...] = jnp.broadcast_to(base, base_ref.shape)
        rio = lax.broadcasted_iota(I32, (SUBLANES, tm), 0)
        d0 = rows[0].astype(I32)
        d1 = rows[1].astype(I32)
        dest_ref[0] = jnp.where(rio == 0, d0, jnp.where(rio == 1, d1, 0))


def _plan(ids, tm=512):
    t = ids.shape[1]
    return pl.pallas_call(
        _plan_kernel,
        grid=(2, t // tm),
        in_specs=[pl.BlockSpec((SUBLANES, tm), lambda p, i: (0, i))],
        out_specs=[pl.BlockSpec((1, SUBLANES, tm), lambda p, i: (p, 0, i)),
                   pl.BlockSpec((N_EXPERTS, LANES), lambda p, i: (0, 0))],
        out_shape=[jax.ShapeDtypeStruct((2, SUBLANES, t), I32),
                   jax.ShapeDtypeStruct((N_EXPERTS, LANES), F32)],
        scratch_shapes=[pltpu.VMEM((N_EXPERTS, LANES), F32)],
        compiler_params=_cp(("arbitrary", "arbitrary"), 32),
        name="plan",
    )(ids)


def _invert_kernel(dest_ref, slot_ref, fill_ref, sem, *, n_slots):
    fill_ref[...] = jnp.full(fill_ref.shape, n_slots, I32)
    fill = pltpu.make_async_copy(fill_ref, slot_ref, sem)
    fill.start()
    fill.wait()

    def scatter(j, c):
        for u in range(DMA_UNROLL):
            q = j * DMA_UNROLL + u
            slot_ref[dest_ref[q]] = q
        return c

    lax.fori_loop(0, n_slots // DMA_UNROLL, scatter, 0)


def _invert(dest_flat, n_rows):
    n_slots = dest_flat.shape[0]
    assert n_slots % DMA_UNROLL == 0
    return pl.pallas_call(
        functools.partial(_invert_kernel, n_slots=n_slots),
        grid_spec=pltpu.PrefetchScalarGridSpec(
            num_scalar_prefetch=1,
            grid=(1,),
            in_specs=[],
            out_specs=pl.BlockSpec(memory_space=pltpu.SMEM),
            scratch_shapes=[pltpu.VMEM((n_rows,), I32), pltpu.SemaphoreType.DMA(())]),
        out_shape=jax.ShapeDtypeStruct((n_rows,), I32),
        compiler_params=_cp(("arbitrary",), 16),
        name="invert",
    )(dest_flat)


def _experts_kernel(te_ref, nx_ref, sl_ref, vr_ref, rs_ref, na_ref, hp_hbm, wg_hbm, wu_hbm, wd_hbm, yt_hbm,
                    wgf_ref, wuf_ref, wdf_ref, wgbf_ref, wubf_ref, wdbf_ref, xbuf_ref, ybuf_ref,
                    sems, xsems, ysems, *, t_total):
    i = pl.program_id(0)
    n_act = na_ref[0]
    active = i < n_act
    changed = (i == 0) | (te_ref[i] != te_ref[jnp.maximum(i - 1, 0)])

    def weight_copies(e, slot):
        return (pltpu.make_async_copy(wg_hbm.at[e], wgf_ref.at[slot], sems.at[slot, 0]),
                pltpu.make_async_copy(wu_hbm.at[e], wuf_ref.at[slot], sems.at[slot, 1]),
                pltpu.make_async_copy(wd_hbm.at[e], wdf_ref.at[slot], sems.at[slot, 2]))

    def in_copy(tile, r):
        tok = rs_ref[tile * EXPERT_TILE + r] & (t_total - 1)
        return pltpu.make_async_copy(hp_hbm.at[tok], xbuf_ref.at[tile % 2, r], xsems.at[tile % 2])

    def out_copy(tile, r):
        dst = yt_hbm.at[rs_ref[tile * EXPERT_TILE + r]]
        return pltpu.make_async_copy(ybuf_ref.at[tile % 2, r], dst, ysems.at[tile % 2])

    def for_rows(tile, make_copy, fn):
        n = vr_ref[tile]
        groups = lax.shift_right_logical(n, DMA_UNROLL.bit_length() - 1)

        def body(j, c):
            for u in range(DMA_UNROLL):
                fn(make_copy(tile, j * DMA_UNROLL + u))
            return c

        def tail(r, c):
            fn(make_copy(tile, r))
            return c

        lax.fori_loop(0, groups, body, 0)
        lax.fori_loop(groups * DMA_UNROLL, n, tail, 0)

    def wait_rows(tile, make_copy, whole_tile_copy):
        full = vr_ref[tile] == EXPERT_TILE

        @pl.when(full)
        def _():
            whole_tile_copy.wait()

        @pl.when(jnp.logical_not(full))
        def _():
            for_rows(tile, make_copy, lambda cp: cp.wait())

    def in_tile(tile):
        return pltpu.make_async_copy(hp_hbm.at[pl.ds(0, EXPERT_TILE)], xbuf_ref.at[tile % 2], xsems.at[tile % 2])

    def out_tile(tile):
        return pltpu.make_async_copy(ybuf_ref.at[tile % 2], yt_hbm.at[pl.ds(0, EXPERT_TILE)], ysems.at[tile % 2])

    @pl.when(i == 0)
    def _():
        xbuf_ref[...] = jnp.zeros_like(xbuf_ref)
        for cp in weight_copies(te_ref[0], 0):
            cp.start(priority=1)
        for_rows(i, in_copy, lambda cp: cp.start())

    @pl.when(i + 1 < n_act)
    def _():
        for_rows(i + 1, in_copy, lambda cp: cp.start())

    @pl.when(active & changed)
    def _():
        slot = sl_ref[i]
        nxt = nx_ref[i]

        @pl.when(nxt >= 0)
        def _():
            for cp in weight_copies(nxt, 1 - slot):
                cp.start(priority=1)

        for cp in weight_copies(te_ref[i], slot):
            cp.wait()
        wgbf_ref[...] = wgf_ref[slot].astype(BF16)
        wubf_ref[...] = wuf_ref[slot].astype(BF16)
        wdbf_ref[...] = wdf_ref[slot].astype(BF16)

    @pl.when(active)
    def _():
        wait_rows(i, in_copy, in_tile(i))
        x = _from_token_major(xbuf_ref[i % 2])
        hg = _dot(x, wgbf_ref[...])
        hu = _dot(x, wubf_ref[...])
        act = (hg * jax.nn.sigmoid(hg) * hu).astype(BF16)
        ybuf_ref[i % 2] = _to_token_major(_dot(act, wdbf_ref[...]).astype(BF16))

    @pl.when((i >= 1) & (i - 1 < n_act))
    def _():
        wait_rows(i - 1, out_copy, out_tile(i - 1))

    @pl.when(active)
    def _():
        for_rows(i, out_copy, lambda cp: cp.start())

    @pl.when(active & (i == pl.num_programs(0) - 1))
    def _():
        wait_rows(i, out_copy, out_tile(i))


def _experts(tile_e, next_e, slot, valid, row_slot, n_act, hp, w_gate, w_up, w_down, n_tiles):
    t = hp.shape[0]
    assert t & (t - 1) == 0, "the row map packs slot * T + token with T a power of two"
    any_spec = pl.BlockSpec(memory_space=pl.ANY)
    tile_buf = pltpu.VMEM((2, EXPERT_TILE, ROW_TILES, LANES), BF16)
    return pl.pallas_call(
        functools.partial(_experts_kernel, t_total=t),
        grid_spec=pltpu.PrefetchScalarGridSpec(
            num_scalar_prefetch=6,
            grid=(n_tiles,),
            in_specs=[any_spec, any_spec, any_spec, any_spec],
            out_specs=any_spec,
            scratch_shapes=[pltpu.VMEM((2, D_MODEL, D_FF), F32),
                            pltpu.VMEM((2, D_MODEL, D_FF), F32),
                            pltpu.VMEM((2, D_FF, D_MODEL), F32),
                            pltpu.VMEM((D_MODEL, D_FF), BF16),
                            pltpu.VMEM((D_MODEL, D_FF), BF16),
                            pltpu.VMEM((D_FF, D_MODEL), BF16),
                            tile_buf, tile_buf,
                            pltpu.SemaphoreType.DMA((2, 3)),
                            pltpu.SemaphoreType.DMA((2,)),
                            pltpu.SemaphoreType.DMA((2,))]),
        out_shape=jax.ShapeDtypeStruct((2 * t, ROW_TILES, LANES), BF16),
        compiler_params=_cp(("arbitrary",), 48, has_side_effects=True),
        name="experts",
    )(tile_e, next_e, slot, valid, row_slot, n_act, hp, w_gate, w_up, w_down)


def _combine_kernel(y0_ref, y1_ref, x_ref, wt_ref, p_ref, nw_ref, wg_ref, wp_ref, fw_ref, o_ref):
    wt = wt_ref[...]
    y0 = _from_token_major(y0_ref[...]).astype(F32)
    y1 = _from_token_major(y1_ref[...]).astype(F32)
    x2 = x_ref[...] + wt[:, 0:1] * y0 + wt[:, 1:2] * y1
    hn = _rms(x2, nw_ref[...]).astype(BF16)
    pg = jax.nn.sigmoid(_dot(hn, wg_ref[...]))
    x3 = x2 + pg * _dot(p_ref[...].astype(BF16), wp_ref[...])
    o_ref[...] = _rms(x3, fw_ref[...])


def _combine_ple(yt, x1, wts_t, p, ple_norm_w, w_gate, w_proj, final_w, tm=512):
    t = x1.shape[0]
    nb = t // tm
    row = lambda i: (i, 0)
    fix = lambda i: (0, 0)
    return pl.pallas_call(
        _combine_kernel,
        grid=(nb,),
        in_specs=[pl.BlockSpec((tm, ROW_TILES, LANES), lambda i: (i, 0, 0)),
                  pl.BlockSpec((tm, ROW_TILES, LANES), lambda i: (nb + i, 0, 0)),
                  pl.BlockSpec((tm, D_MODEL), row),
                  pl.BlockSpec((tm, SUBLANES), row),
                  pl.BlockSpec((tm, PLE_DIM), row),
                  pl.BlockSpec((1, D_MODEL), fix),
                  pl.BlockSpec((D_MODEL, D_MODEL), fix),
                  pl.BlockSpec((PLE_DIM, D_MODEL), fix),
                  pl.BlockSpec((1, D_MODEL), fix)],
        out_specs=pl.BlockSpec((tm, D_MODEL), row),
        out_shape=jax.ShapeDtypeStruct((t, D_MODEL), F32),
        compiler_params=_cp(("arbitrary",), 56),
        name="combine_ple",
    )(yt, yt, x1, wts_t, p, ple_norm_w, w_gate, w_proj, final_w)


def _tile_table(counts, n_tiles):
    tiles = (counts.astype(I32) + (EXPERT_TILE - 1)) // EXPERT_TILE
    ends = jnp.cumsum(tiles)
    n_act = ends[-1]
    idx = jnp.minimum(jnp.arange(n_tiles, dtype=I32), n_act - 1)
    tile_e = jnp.sum((idx[:, None] >= ends[None, :]).astype(I32), axis=1).astype(I32)
    run_end = ends[tile_e]
    next_e = jnp.where(run_end < n_act, tile_e[jnp.minimum(run_end, n_tiles - 1)], -1).astype(I32)
    new_run = jnp.concatenate([jnp.ones((1,), I32), (tile_e[1:] != tile_e[:-1]).astype(I32)])
    slot = ((jnp.cumsum(new_run) - 1) % 2).astype(I32)
    arange = jnp.arange(n_tiles, dtype=I32)
    last = arange == run_end - 1
    valid = jnp.where(last, counts.astype(I32)[tile_e] - (tiles[tile_e] - 1) * EXPERT_TILE, EXPERT_TILE)
    valid = jnp.where(arange < n_act, valid, 0).astype(I32)
    return tile_e, next_e, slot, valid, n_act.reshape(1).astype(I32)


def _block(x, p, norm_mix_w, w_in, b_merge, w_alpha_up, b_alpha_up, gla_norm_w, w_gla_out,
           conv_w, conv_b, w_conv_out, w_mix_out, norm_ffn_w, w_router_group, b_router_group,
           w_router_expert, b_router_expert, w_e_gate, w_e_up, w_e_down, ple_norm_w,
           w_ple_gate, w_ple_proj, final_norm_w):
    t = x.shape[0]
    n_tiles = (2 * t) // EXPERT_TILE + N_EXPERTS
    n_rows = n_tiles * EXPERT_TILE

    w_up = jnp.pad(w_alpha_up, ((0, LANES - GLA_GATE_RANK), (0, 0))).astype(BF16)
    w_up = w_up.reshape(LANES, GLA_HEADS, GLA_DK).transpose(1, 0, 2)
    b_up = b_alpha_up.reshape(GLA_HEADS, 1, GLA_DK)
    gnw = gla_norm_w.reshape(GLA_HEADS, 1, GLA_DV)
    wr_t = jnp.zeros((ROUTER_ROWS, D_MODEL), F32)
    wr_t = wr_t.at[0:N_GROUPS].set(w_router_group.T)
    wr_t = wr_t.at[EXPERT_ROW0:EXPERT_ROW0 + N_EXPERTS].set(w_router_expert.T)
    br = jnp.zeros((ROUTER_ROWS, 1), F32)
    br = br.at[0:N_GROUPS, 0].set(b_router_group)
    br = br.at[EXPERT_ROW0:EXPERT_ROW0 + N_EXPERTS, 0].set(b_router_expert)

    w_in_t = w_in.T
    h, a_low = _norm_in(x, norm_mix_w.reshape(1, D_MODEL), w_in_t)
    qkvg = _proj_qkvg(h, w_in_t)
    ob = _proj_conv(h, w_in_t, conv_w, conv_b.reshape(1, D_MODEL))
    gates = _proj_gates(h, w_in_t, b_merge.reshape(1, 2 * D_MODEL))
    oa = _gla(qkvg, a_low, w_up, b_up, gnw)
    mixed = _merge(oa, ob, w_gla_out, w_conv_out, gates)
    x1 = _mix(mixed, w_mix_out, x)

    hp, ids, wts = _route(x1, norm_ffn_w.reshape(1, D_MODEL), wr_t, br)
    dest, counts = _plan(ids)
    dest_flat = dest[1, 0:2].reshape(2 * t)
    tile_e, next_e, slot, valid, n_act = _tile_table(counts[:, 0], n_tiles)
    row_slot = _invert(dest_flat, n_rows)
    yt = _experts(tile_e, next_e, slot, valid, row_slot, n_act, hp, w_e_gate, w_e_up, w_e_down, n_tiles)
    return _combine_ple(yt, x1, wts.T, p,
                        ple_norm_w.reshape(1, D_MODEL), w_ple_gate.astype(BF16),
                        w_ple_proj.astype(BF16), final_norm_w.reshape(1, D_MODEL))


def kernel(x, p, norm_mix_w, w_in, b_merge, w_alpha_up, b_alpha_up, gla_norm_w, w_gla_out, conv_w, conv_b, w_conv_out, w_mix_out, norm_ffn_w, w_router_group, b_router_group, w_router_expert, b_router_expert, w_e_gate, w_e_up, w_e_down, ple_norm_w, w_ple_gate, w_ple_proj, final_norm_w):
    depth, batch = p.shape[0], x.shape[0]
    assert depth == 1 and batch == 1, "kernel is specialised to one layer and one sequence"
    out = _block(x[0], p[0, 0], norm_mix_w[0], w_in[0], b_merge[0], w_alpha_up[0], b_alpha_up[0],
                 gla_norm_w[0], w_gla_out[0], conv_w[0], conv_b[0], w_conv_out[0], w_mix_out[0],
                 norm_ffn_w[0], w_router_group[0], b_router_group[0], w_router_expert[0],
                 b_router_expert[0], w_e_gate[0], w_e_up[0], w_e_down[0], ple_norm_w[0],
                 w_ple_gate[0], w_ple_proj[0], final_norm_w)
    return out[None]
```

```python
import functools

import jax
import jax.numpy as jnp
from jax import lax
from jax.experimental import pallas as pl
from jax.experimental.pallas import tpu as pltpu

F32 = jnp.float32
BF16 = jnp.bfloat16
I32 = jnp.int32

D_MODEL = 2048
PLE_DIM = 256
EPS = 1e-6
LOG2_E = 1.4426950408889634
GLA_HEADS = 4
GLA_DK = 256
GLA_DV = 512
GLA_KEY = GLA_HEADS * GLA_DK
GLA_VAL = GLA_HEADS * GLA_DV
GLA_GATE_RANK = 16
GLA_GATE_NORM = 16.0
GLA_CHUNK = 64
CONV_K = 3
N_GROUPS = 4
EXPERTS_PER_GROUP = 8
N_EXPERTS = N_GROUPS * EXPERTS_PER_GROUP
D_FF = 512

QKVG_COLS = 2 * GLA_KEY + 2 * GLA_VAL
CONV_COL0 = QKVG_COLS + GLA_GATE_RANK

LANES = 128
SUBLANES = 8
MXU_COLS = 256
ROW_CHUNK = 256
ROW_TILES = D_MODEL // LANES

EXPERT_TILE = 256
DMA_UNROLL = 8
ROUTER_ROWS = 64
EXPERT_ROW0 = 8

MIB = 1024 * 1024


def _cp(sem, vmem_mib, **kw):
    return pltpu.CompilerParams(dimension_semantics=sem, vmem_limit_bytes=int(vmem_mib * MIB), **kw)


def _rms(x, w):
    return x * lax.rsqrt(jnp.mean(x * x, axis=-1, keepdims=True) + EPS) * w


def _dot(a, b):
    return jnp.dot(a, b, preferred_element_type=F32)


def _dot_nt(a, b):
    return lax.dot_general(a, b, (((1,), (1,)), ((), ())), preferred_element_type=F32)


def _dot_tn(a, b):
    return lax.dot_general(a, b, (((0,), (0,)), ((), ())), preferred_element_type=F32)


def _split_bf16(x):
    hi = x.astype(BF16)
    lo = (x - hi.astype(F32)).astype(BF16)
    return hi, lo


def _to_token_major(val):
    return val.reshape(val.shape[0], ROW_TILES, LANES)


def _from_token_major(val):
    return val.reshape(val.shape[0], D_MODEL)


def _norm_in_kernel(x_ref, w_ref, wal_ref, h_ref, al_ref, walbf_ref):
    @pl.when(pl.program_id(0) == 0)
    def _():
        walbf_ref[...] = wal_ref[...].astype(BF16)

    h = _rms(x_ref[...], w_ref[...]).astype(BF16)
    h_ref[...] = h
    al_ref[...] = _dot_nt(h, walbf_ref[...])


def _norm_in(x, w, w_in_t, tm=512):
    t = x.shape[0]
    return pl.pallas_call(
        _norm_in_kernel,
        grid=(t // tm,),
        in_specs=[pl.BlockSpec((tm, D_MODEL), lambda i: (i, 0)),
                  pl.BlockSpec((1, D_MODEL), lambda i: (0, 0)),
                  pl.BlockSpec((LANES, D_MODEL), lambda i: (QKVG_COLS // LANES, 0))],
        out_specs=[pl.BlockSpec((tm, D_MODEL), lambda i: (i, 0)),
                   pl.BlockSpec((tm, LANES), lambda i: (i, 0))],
        out_shape=[jax.ShapeDtypeStruct((t, D_MODEL), BF16),
                   jax.ShapeDtypeStruct((t, LANES), F32)],
        scratch_shapes=[pltpu.VMEM((LANES, D_MODEL), BF16)],
        compiler_params=_cp(("arbitrary",), 32),
        name="norm_in",
    )(x, w, w_in_t)


def _w_rows_spec(row0, tn):
    assert row0 % SUBLANES == 0 and tn % SUBLANES == 0
    return pl.BlockSpec((pl.Element(tn), pl.Element(D_MODEL)),
                        lambda n, m: (pl.multiple_of(row0 + n * tn, SUBLANES), 0))


def _proj_qkvg_kernel(h_ref, w_ref, o_ref, wbf_ref):
    @pl.when(pl.program_id(1) == 0)
    def _():
        wbf_ref[...] = w_ref[...].astype(BF16)

    o_ref[...] = _dot_nt(h_ref[...], wbf_ref[...]).astype(o_ref.dtype)


def _proj_qkvg(h, w_in_t, tm=2048, tn=1024):
    t = h.shape[0]
    return pl.pallas_call(
        _proj_qkvg_kernel,
        grid=(QKVG_COLS // tn, t // tm),
        in_specs=[pl.BlockSpec((tm, D_MODEL), lambda n, m: (m, 0)),
                  pl.BlockSpec((tn, D_MODEL), lambda n, m: (n, 0))],
        out_specs=pl.BlockSpec((tm, tn), lambda n, m: (m, n)),
        out_shape=jax.ShapeDtypeStruct((t, QKVG_COLS), BF16),
        scratch_shapes=[pltpu.VMEM((tn, D_MODEL), BF16)],
        compiler_params=_cp(("arbitrary", "arbitrary"), 56),
        name="proj_qkvg",
    )(h, w_in_t)


def _proj_conv_kernel(h_ref, wb_ref, wc_ref, wx_ref, cw_ref, cb_ref, o_ref,
                      wbbf_ref, wcbf_ref, wxbf_ref, prev_ref):
    m = pl.program_id(1)

    @pl.when(m == 0)
    def _():
        prev_ref[...] = jnp.zeros_like(prev_ref)
        wbbf_ref[...] = wb_ref[...].astype(BF16)
        wcbf_ref[...] = wc_ref[...].astype(BF16)
        wxbf_ref[...] = wx_ref[...].astype(BF16)

    h = h_ref[...]
    tm = h.shape[0]
    row = lax.broadcasted_iota(I32, (tm, MXU_COLS), 0)
    for c in range(0, o_ref.shape[1], MXU_COLS):
        cols = slice(c, c + MXU_COLS)
        b = _dot_nt(h, wbbf_ref[cols, :])
        s = _dot_nt(h, wcbf_ref[cols, :]) * _dot_nt(h, wxbf_ref[cols, :])
        prev = prev_ref[:, cols]
        p1 = prev[SUBLANES - 1:SUBLANES, :]
        p2 = prev[SUBLANES - 2:SUBLANES - 1, :]
        s1 = jnp.where(row == 0, p1, pltpu.roll(s, 1, 0))
        s2 = jnp.where(row == 0, p2, jnp.where(row == 1, p1, pltpu.roll(s, 2, 0)))
        cw = cw_ref[:, cols]
        u = cw[2:3, :] * s + cw[1:2, :] * s1 + cw[0:1, :] * s2 + cb_ref[:, cols]
        o_ref[:, cols] = (b * u).astype(o_ref.dtype)
        prev_ref[:, cols] = s[tm - SUBLANES:, :]


def _proj_conv(h, w_in_t, conv_w, conv_b, tm=1024, tn=512):
    t = h.shape[0]
    return pl.pallas_call(
        _proj_conv_kernel,
        grid=(D_MODEL // tn, t // tm),
        in_specs=[pl.BlockSpec((tm, D_MODEL), lambda n, m: (m, 0))] +
                 [_w_rows_spec(CONV_COL0 + seg * D_MODEL, tn) for seg in range(3)] +
                 [pl.BlockSpec((CONV_K, tn), lambda n, m: (0, n)),
                  pl.BlockSpec((1, tn), lambda n, m: (0, n))],
        out_specs=pl.BlockSpec((tm, tn), lambda n, m: (m, n)),
        out_shape=jax.ShapeDtypeStruct((t, D_MODEL), BF16),
        scratch_shapes=[pltpu.VMEM((tn, D_MODEL), BF16), pltpu.VMEM((tn, D_MODEL), BF16),
                        pltpu.VMEM((tn, D_MODEL), BF16), pltpu.VMEM((SUBLANES, tn), F32)],
        compiler_params=_cp(("arbitrary", "arbitrary"), 48),
        name="proj_conv",
    )(h, w_in_t, w_in_t, w_in_t, conv_w, conv_b)


def _proj_gates_kernel(h_ref, w_ref, b_ref, o_ref, wbf_ref):
    @pl.when(pl.program_id(1) == 0)
    def _():
        wbf_ref[...] = w_ref[...].astype(BF16)

    for c in range(0, o_ref.shape[1], MXU_COLS):
        cols = slice(c, c + MXU_COLS)
        for r in range(0, o_ref.shape[0], ROW_CHUNK):
            rows = slice(r, r + ROW_CHUNK)
            z = _dot_nt(h_ref[rows, :], wbf_ref[cols, :]) + b_ref[:, cols]
            o_ref[rows, cols] = jax.nn.sigmoid(z).astype(o_ref.dtype)


def _proj_gates(h, w_in_t, b_merge, tm=2048, tn=1024):
    t = h.shape[0]
    return pl.pallas_call(
        _proj_gates_kernel,
        grid=(2 * D_MODEL // tn, t // tm),
        in_specs=[pl.BlockSpec((tm, D_MODEL), lambda n, m: (m, 0)),
                  _w_rows_spec(CONV_COL0 + 3 * D_MODEL, tn),
                  pl.BlockSpec((1, tn), lambda n, m: (0, n))],
        out_specs=pl.BlockSpec((tm, tn), lambda n, m: (m, n)),
        out_shape=jax.ShapeDtypeStruct((t, 2 * D_MODEL), BF16),
        scratch_shapes=[pltpu.VMEM((tn, D_MODEL), BF16)],
        compiler_params=_cp(("arbitrary", "arbitrary"), 56),
        name="proj_gates",
    )(h, w_in_t, b_merge)


def _gla_kernel(q_ref, k_ref, v_ref, g_ref, al_ref, wup_ref, bup_ref, nw_ref, o_ref,
                st_ref, b_ref, bl_ref, qd_ref, ki_ref, kd_ref, oi_ref, u_ref, *, n_chunks):
    c_len = GLA_CHUNK

    @pl.when(pl.program_id(1) == 0)
    def _():
        st_ref[...] = jnp.zeros_like(st_ref)

    row = lax.broadcasted_iota(I32, (c_len, c_len), 0)
    col = lax.broadcasted_iota(I32, (c_len, c_len), 1)
    causal = col <= row
    tril = jnp.where(causal, 1.0, 0.0).astype(BF16)

    def decays(h):
        kc = slice(h * GLA_DK, (h + 1) * GLA_DK)
        z = _dot(al_ref[...].astype(BF16), wup_ref[h]) + bup_ref[h]
        la = (jnp.minimum(z, 0.0) - jnp.log1p(jnp.exp(-jnp.abs(z)))) * (LOG2_E / GLA_GATE_NORM)
        la_hi, la_lo = _split_bf16(la)
        for c in range(n_chunks):
            r0 = c * c_len
            b = _dot(tril, la_hi[r0:r0 + c_len]) + _dot(tril, la_lo[r0:r0 + c_len])
            b_ref[pl.ds(r0, c_len), kc] = b
            bl_ref[pl.ds(r0, c_len), kc] = jnp.broadcast_to(b[c_len - 1:c_len, :], b.shape)
        b = b_ref[:, kc]
        q = q_ref[:, kc].astype(F32)
        k = k_ref[:, kc].astype(F32)
        qd_ref[:, kc] = (q * jnp.exp2(b)).astype(BF16)
        ki_ref[:, kc] = (k * jnp.exp2(-b)).astype(BF16)
        kd_ref[:, kc] = (k * jnp.exp2(bl_ref[:, kc] - b)).astype(BF16)

    def local_products(h):
        kc = slice(h * GLA_DK, (h + 1) * GLA_DK)
        vc = slice(h * GLA_DV, (h + 1) * GLA_DV)
        for c in range(n_chunks):
            sl = pl.ds(c * c_len, c_len)
            v = v_ref[sl, vc]
            att = jnp.where(causal, _dot_nt(qd_ref[sl, kc], ki_ref[sl, kc]), 0.0).astype(BF16)
            oi_ref[sl, vc] = _dot(att, v)
            u_ref[h, c] = _dot_tn(v, kd_ref[sl, kc])

    def recurrence(h):
        kc = slice(h * GLA_DK, (h + 1) * GLA_DK)
        vc = slice(h * GLA_DV, (h + 1) * GLA_DV)
        for c in range(n_chunks):
            sl = pl.ds(c * c_len, c_len)
            st = st_ref[h]
            oi_ref[sl, vc] = oi_ref[sl, vc] + _dot_nt(qd_ref[sl, kc], st.astype(BF16))
            st_ref[h] = st * jnp.exp2(bl_ref[pl.ds(c * c_len, 1), kc]) + u_ref[h, c]
        o = oi_ref[:, vc]
        o = o * lax.rsqrt(jnp.mean(o * o, axis=-1, keepdims=True) + EPS * GLA_DK) * nw_ref[h]
        g = g_ref[:, vc].astype(F32)
        o_ref[:, vc] = (o * (g * jax.nn.sigmoid(g))).astype(o_ref.dtype)

    decays(0)
    local_products(0)
    decays(1)
    recurrence(0)
    local_products(1)
    recurrence(1)


def _gla(qkvg, a_low, w_up, b_up, norm_w, tb=512):
    t = qkvg.shape[0]
    hp = 2
    dk, dv = hp * GLA_DK, hp * GLA_DV
    kq = GLA_KEY // dk
    kv = 2 * GLA_KEY // dv
    kg = kv + GLA_VAL // dv
    n_chunks = tb // GLA_CHUNK
    kern = functools.partial(_gla_kernel, n_chunks=n_chunks)
    return pl.pallas_call(
        kern,
        grid=(GLA_HEADS // hp, t // tb),
        in_specs=[pl.BlockSpec((tb, dk), lambda h, i: (i, h)),
                  pl.BlockSpec((tb, dk), lambda h, i: (i, kq + h)),
                  pl.BlockSpec((tb, dv), lambda h, i: (i, kv + h)),
                  pl.BlockSpec((tb, dv), lambda h, i: (i, kg + h)),
                  pl.BlockSpec((tb, LANES), lambda h, i: (i, 0)),
                  pl.BlockSpec((hp, LANES, GLA_DK), lambda h, i: (h, 0, 0)),
                  pl.BlockSpec((hp, 1, GLA_DK), lambda h, i: (h, 0, 0)),
                  pl.BlockSpec((hp, 1, GLA_DV), lambda h, i: (h, 0, 0))],
        out_specs=pl.BlockSpec((tb, dv), lambda h, i: (i, h)),
        out_shape=jax.ShapeDtypeStruct((t, GLA_VAL), BF16),
        scratch_shapes=[pltpu.VMEM((hp, GLA_DV, GLA_DK), F32),
                        pltpu.VMEM((tb, dk), F32), pltpu.VMEM((tb, dk), F32),
                        pltpu.VMEM((tb, dk), BF16), pltpu.VMEM((tb, dk), BF16),
                        pltpu.VMEM((tb, dk), BF16),
                        pltpu.VMEM((tb, dv), F32),
                        pltpu.VMEM((hp, n_chunks, GLA_DV, GLA_DK), F32)],
        compiler_params=_cp(("arbitrary", "arbitrary"), 40),
        name="gla",
    )(qkvg, qkvg, qkvg, qkvg, a_low, w_up, b_up, norm_w)


def _merge_kernel(oa_ref, ob_ref, wa_ref, wb_ref, g0_ref, g1_ref, o_ref, wabf_ref, wbbf_ref):
    @pl.when(pl.program_id(1) == 0)
    def _():
        wabf_ref[...] = wa_ref[...].astype(BF16)
        wbbf_ref[...] = wb_ref[...].astype(BF16)

    oa = oa_ref[...]
    ob = ob_ref[...]
    for c in range(0, o_ref.shape[1], MXU_COLS):
        cols = slice(c, c + MXU_COLS)
        a = _dot(oa, wabf_ref[:, cols])
        b = _dot(ob, wbbf_ref[:, cols])
        o_ref[:, cols] = (g0_ref[:, cols].astype(F32) * a + g1_ref[:, cols].astype(F32) * b).astype(o_ref.dtype)


def _merge(oa, ob, w_a, w_b, gates, tm=1024, tn=512):
    t = oa.shape[0]
    nb = D_MODEL // tn
    return pl.pallas_call(
        _merge_kernel,
        grid=(nb, t // tm),
        in_specs=[pl.BlockSpec((tm, D_MODEL), lambda n, m: (m, 0)),
                  pl.BlockSpec((tm, D_MODEL), lambda n, m: (m, 0)),
                  pl.BlockSpec((D_MODEL, tn), lambda n, m: (0, n)),
                  pl.BlockSpec((D_MODEL, tn), lambda n, m: (0, n)),
                  pl.BlockSpec((tm, tn), lambda n, m: (m, n)),
                  pl.BlockSpec((tm, tn), lambda n, m: (m, nb + n))],
        out_specs=pl.BlockSpec((tm, tn), lambda n, m: (m, n)),
        out_shape=jax.ShapeDtypeStruct((t, D_MODEL), BF16),
        scratch_shapes=[pltpu.VMEM((D_MODEL, tn), BF16), pltpu.VMEM((D_MODEL, tn), BF16)],
        compiler_params=_cp(("arbitrary", "arbitrary"), 56),
        name="merge",
    )(oa, ob, w_a, w_b, gates, gates)


def _mix_kernel(a_ref, w_ref, x_ref, o_ref, wbf_ref):
    @pl.when(pl.program_id(1) == 0)
    def _():
        wbf_ref[...] = w_ref[...].astype(BF16)

    a = a_ref[...]
    for c in range(0, o_ref.shape[1], MXU_COLS):
        cols = slice(c, c + MXU_COLS)
        o_ref[:, cols] = x_ref[:, cols] + _dot(a, wbf_ref[:, cols])


def _mix(mixed, w, x, tm=1024, tn=1024):
    t = x.shape[0]
    return pl.pallas_call(
        _mix_kernel,
        grid=(D_MODEL // tn, t // tm),
        in_specs=[pl.BlockSpec((tm, D_MODEL), lambda n, m: (m, 0)),
                  pl.BlockSpec((D_MODEL, tn), lambda n, m: (0, n)),
                  pl.BlockSpec((tm, tn), lambda n, m: (m, n))],
        out_specs=pl.BlockSpec((tm, tn), lambda n, m: (m, n)),
        out_shape=jax.ShapeDtypeStruct((t, D_MODEL), F32),
        scratch_shapes=[pltpu.VMEM((D_MODEL, tn), BF16)],
        compiler_params=_cp(("arbitrary", "arbitrary"), 52),
        name="mix",
    )(mixed, w, x)


def _route_kernel(x_ref, nw_ref, wr_ref, br_ref, hp_ref, ids_ref, wts_ref):
    h = _rms(x_ref[...], nw_ref[...])
    hp_ref[...] = _to_token_major(h.astype(BF16))
    h_hi, h_lo = _split_bf16(h)
    w_hi, w_lo = _split_bf16(wr_ref[...])
    logits = _dot_nt(w_hi, h_hi) + _dot_nt(w_hi, h_lo) + _dot_nt(w_lo, h_hi) + br_ref[...]
    tm = logits.shape[1]

    best = logits[0:1, :]
    gidx = jnp.zeros((1, tm), I32)
    for i in range(1, N_GROUPS):
        li = logits[i:i + 1, :]
        take = li > best
        best = jnp.where(take, li, best)
        gidx = jnp.where(take, i, gidx)
    gsum = jnp.zeros((1, tm), F32)
    for i in range(N_GROUPS):
        gsum = gsum + jnp.exp(logits[i:i + 1, :] - best)
    g_p = 1.0 / gsum

    sel = logits[EXPERT_ROW0:EXPERT_ROW0 + EXPERTS_PER_GROUP, :]
    for g in range(1, N_GROUPS):
        r0 = EXPERT_ROW0 + g * EXPERTS_PER_GROUP
        sel = jnp.where(gidx == g, logits[r0:r0 + EXPERTS_PER_GROUP, :], sel)
    eio = lax.broadcasted_iota(I32, sel.shape, 0)
    m1 = jnp.max(sel, axis=0, keepdims=True)
    i1 = jnp.min(jnp.where(sel == m1, eio, EXPERTS_PER_GROUP), axis=0, keepdims=True)
    rest = jnp.where(eio == i1, -jnp.inf, sel)
    m2 = jnp.max(rest, axis=0, keepdims=True)
    i2 = jnp.min(jnp.where(rest == m2, eio, EXPERTS_PER_GROUP), axis=0, keepdims=True)
    p2 = jnp.exp(m2 - m1)
    w1 = g_p / (1.0 + p2)
    w2 = g_p * p2 / (1.0 + p2)
    e1 = gidx * EXPERTS_PER_GROUP + i1
    e2 = gidx * EXPERTS_PER_GROUP + i2
    rio = lax.broadcasted_iota(I32, (SUBLANES, tm), 0)
    ids_ref[...] = jnp.where(rio == 0, e1, jnp.where(rio == 1, e2, 0))
    wts_ref[...] = jnp.where(rio == 0, w1, jnp.where(rio == 1, w2, 0.0))


def _route(x1, norm_w, wr_t, br, tm=512):
    t = x1.shape[0]
    return pl.pallas_call(
        _route_kernel,
        grid=(t // tm,),
        in_specs=[pl.BlockSpec((tm, D_MODEL), lambda i: (i, 0)),
                  pl.BlockSpec((1, D_MODEL), lambda i: (0, 0)),
                  pl.BlockSpec((ROUTER_ROWS, D_MODEL), lambda i: (0, 0)),
                  pl.BlockSpec((ROUTER_ROWS, 1), lambda i: (0, 0))],
        out_specs=[pl.BlockSpec((tm, ROW_TILES, LANES), lambda i: (i, 0, 0)),
                   pl.BlockSpec((SUBLANES, tm), lambda i: (0, i)),
                   pl.BlockSpec((SUBLANES, tm), lambda i: (0, i))],
        out_shape=[jax.ShapeDtypeStruct((t, ROW_TILES, LANES), BF16),
                   jax.ShapeDtypeStruct((SUBLANES, t), I32),
                   jax.ShapeDtypeStruct((SUBLANES, t), F32)],
        compiler_params=_cp(("arbitrary",), 32),
        name="route",
    )(x1, norm_w, wr_t, br)


def _plan_kernel(ids_ref, dest_ref, cnt_ref, base_ref):
    phase = pl.program_id(0)
    step = pl.program_id(1)
    tm = ids_ref.shape[1]
    eio = lax.broadcasted_iota(I32, (N_EXPERTS, tm), 0)
    ids = ids_ref[...]
    oh = [jnp.where(eio == ids[k:k + 1, :], 1.0, 0.0) for k in range(2)]

    @pl.when((phase == 0) & (step == 0))
    def _():
        base_ref[...] = jnp.zeros_like(base_ref)

    @pl.when(phase == 0)
    def _():
        cnt = jnp.sum(oh[0] + oh[1], axis=1, keepdims=True)
        base_ref[...] = base_ref[...] + cnt
        dest_ref[0] = jnp.zeros(dest_ref.shape[1:], I32)
        cnt_ref[...] = base_ref[...]

    @pl.when((phase == 1) & (step == 0))
    def _():
        tiles = jnp.floor((base_ref[...] + (EXPERT_TILE - 1)) * (1.0 / EXPERT_TILE))
        r = lax.broadcasted_iota(I32, (N_EXPERTS, N_EXPERTS), 0)
        c = lax.broadcasted_iota(I32, (N_EXPERTS, N_EXPERTS), 1)
        lower = jnp.where(c < r, 1.0, 0.0).astype(BF16)
        base_ref[...] = _dot(lower, tiles.astype(BF16)) * float(EXPERT_TILE)

    @pl.when(phase == 1)
    def _():
        r = lax.broadcasted_iota(I32, (tm, tm), 0)
        c = lax.broadcasted_iota(I32, (tm, tm), 1)
        upper = jnp.where(r <= c, 1.0, 0.0).astype(BF16)
        base = base_ref[...][:, 0:1]
        rows = []
        for k in range(2):
            cum = _dot(oh[k].astype(BF16), upper)
            rows.append(jnp.sum(oh[k] * (cum - 1.0 + base), axis=0, keepdims=True))
            base = base + cum[:, tm - 1:tm]
        base_ref[...] = jnp.broadcast_to(base, base_ref.shape)
        rio = lax.broadcasted_iota(I32, (SUBLANES, tm), 0)
        d0 = rows[0].astype(I32)
        d1 = rows[1].astype(I32)
        dest_ref[0] = jnp.where(rio == 0, d0, jnp.where(rio == 1, d1, 0))


def _plan(ids, tm=512):
    t = ids.shape[1]
    return pl.pallas_call(
        _plan_kernel,
        grid=(2, t // tm),
        in_specs=[pl.BlockSpec((SUBLANES, tm), lambda p, i: (0, i))],
        out_specs=[pl.BlockSpec((1, SUBLANES, tm), lambda p, i: (p, 0, i)),
                   pl.BlockSpec((N_EXPERTS, LANES), lambda p, i: (0, 0))],
        out_shape=[jax.ShapeDtypeStruct((2, SUBLANES, t), I32),
                   jax.ShapeDtypeStruct((N_EXPERTS, LANES), F32)],
        scratch_shapes=[pltpu.VMEM((N_EXPERTS, LANES), F32)],
        compiler_params=_cp(("arbitrary", "arbitrary"), 32),
        name="plan",
    )(ids)


def _invert_kernel(dest_ref, slot_ref, fill_ref, sem, *, n_slots):
    fill_ref[...] = jnp.full(fill_ref.shape, n_slots, I32)
    fill = pltpu.make_async_copy(fill_ref, slot_ref, sem)
    fill.start()
    fill.wait()

    def scatter(j, c):
        for u in range(DMA_UNROLL):
            q = j * DMA_UNROLL + u
            slot_ref[dest_ref[q]] = q
        return c

    lax.fori_loop(0, n_slots // DMA_UNROLL, scatter, 0)


def _invert(dest_flat, n_rows):
    n_slots = dest_flat.shape[0]
    assert n_slots % DMA_UNROLL == 0
    return pl.pallas_call(
        functools.partial(_invert_kernel, n_slots=n_slots),
        grid_spec=pltpu.PrefetchScalarGridSpec(
            num_scalar_prefetch=1,
            grid=(1,),
            in_specs=[],
            out_specs=pl.BlockSpec(memory_space=pltpu.SMEM),
            scratch_shapes=[pltpu.VMEM((n_rows,), I32), pltpu.SemaphoreType.DMA(())]),
        out_shape=jax.ShapeDtypeStruct((n_rows,), I32),
        compiler_params=_cp(("arbitrary",), 16),
        name="invert",
    )(dest_flat)


def _experts_kernel(te_ref, nx_ref, sl_ref, vr_ref, rs_ref, na_ref, hp_hbm, wg_hbm, wu_hbm, wd_hbm, yt_hbm,
                    wgf_ref, wuf_ref, wdf_ref, wgbf_ref, wubf_ref, wdbf_ref, xbuf_ref, ybuf_ref,
                    sems, xsems, ysems, *, t_total):
    i = pl.program_id(0)
    n_act = na_ref[0]
    active = i < n_act
    changed = (i == 0) | (te_ref[i] != te_ref[jnp.maximum(i - 1, 0)])

    def weight_copies(e, slot):
        return (pltpu.make_async_copy(wg_hbm.at[e], wgf_ref.at[slot], sems.at[slot, 0]),
                pltpu.make_async_copy(wu_hbm.at[e], wuf_ref.at[slot], sems.at[slot, 1]),
                pltpu.make_async_copy(wd_hbm.at[e], wdf_ref.at[slot], sems.at[slot, 2]))

    def in_copy(tile, r):
        tok = rs_ref[tile * EXPERT_TILE + r] & (t_total - 1)
        return pltpu.make_async_copy(hp_hbm.at[tok], xbuf_ref.at[tile % 2, r], xsems.at[tile % 2])

    def out_copy(tile, r):
        dst = yt_hbm.at[rs_ref[tile * EXPERT_TILE + r]]
        return pltpu.make_async_copy(ybuf_ref.at[tile % 2, r], dst, ysems.at[tile % 2])

    def for_rows(tile, make_copy, fn):
        n = vr_ref[tile]
        groups = lax.shift_right_logical(n, DMA_UNROLL.bit_length() - 1)

        def body(j, c):
            for u in range(DMA_UNROLL):
                fn(make_copy(tile, j * DMA_UNROLL + u))
            return c

        def tail(r, c):
            fn(make_copy(tile, r))
            return c

        lax.fori_loop(0, groups, body, 0)
        lax.fori_loop(groups * DMA_UNROLL, n, tail, 0)

    def wait_rows(tile, make_copy, whole_tile_copy):
        full = vr_ref[tile] == EXPERT_TILE

        @pl.when(full)
        def _():
            whole_tile_copy.wait()

        @pl.when(jnp.logical_not(full))
        def _():
            for_rows(tile, make_copy, lambda cp: cp.wait())

    def in_tile(tile):
        return pltpu.make_async_copy(hp_hbm.at[pl.ds(0, EXPERT_TILE)], xbuf_ref.at[tile % 2], xsems.at[tile % 2])

    def out_tile(tile):
        return pltpu.make_async_copy(ybuf_ref.at[tile % 2], yt_hbm.at[pl.ds(0, EXPERT_TILE)], ysems.at[tile % 2])

    @pl.when(i == 0)
    def _():
        xbuf_ref[...] = jnp.zeros_like(xbuf_ref)
        for cp in weight_copies(te_ref[0], 0):
            cp.start(priority=1)
        for_rows(i, in_copy, lambda cp: cp.start())

    @pl.when(i + 1 < n_act)
    def _():
        for_rows(i + 1, in_copy, lambda cp: cp.start())

    @pl.when(active & changed)
    def _():
        slot = sl_ref[i]
        nxt = nx_ref[i]

        @pl.when(nxt >= 0)
        def _():
            for cp in weight_copies(nxt, 1 - slot):
                cp.start(priority=1)

        for cp in weight_copies(te_ref[i], slot):
            cp.wait()
        wgbf_ref[...] = wgf_ref[slot].astype(BF16)
        wubf_ref[...] = wuf_ref[slot].astype(BF16)
        wdbf_ref[...] = wdf_ref[slot].astype(BF16)

    @pl.when(active)
    def _():
        wait_rows(i, in_copy, in_tile(i))
        x = _from_token_major(xbuf_ref[i % 2])
        hg = _dot(x, wgbf_ref[...])
        hu = _dot(x, wubf_ref[...])
        act = (hg * jax.nn.sigmoid(hg) * hu).astype(BF16)
        ybuf_ref[i % 2] = _to_token_major(_dot(act, wdbf_ref[...]).astype(BF16))

    @pl.when((i >= 1) & (i - 1 < n_act))
    def _():
        wait_rows(i - 1, out_copy, out_tile(i - 1))

    @pl.when(active)
    def _():
        for_rows(i, out_copy, lambda cp: cp.start())

    @pl.when(active & (i == pl.num_programs(0) - 1))
    def _():
        wait_rows(i, out_copy, out_tile(i))


def _experts(tile_e, next_e, slot, valid, row_slot, n_act, hp, w_gate, w_up, w_down, n_tiles):
    t = hp.shape[0]
    assert t & (t - 1) == 0, "the row map packs slot * T + token with T a power of two"
    any_spec = pl.BlockSpec(memory_space=pl.ANY)
    tile_buf = pltpu.VMEM((2, EXPERT_TILE, ROW_TILES, LANES), BF16)
    return pl.pallas_call(
        functools.partial(_experts_kernel, t_total=t),
        grid_spec=pltpu.PrefetchScalarGridSpec(
            num_scalar_prefetch=6,
            grid=(n_tiles,),
            in_specs=[any_spec, any_spec, any_spec, any_spec],
            out_specs=any_spec,
            scratch_shapes=[pltpu.VMEM((2, D_MODEL, D_FF), F32),
                            pltpu.VMEM((2, D_MODEL, D_FF), F32),
                            pltpu.VMEM((2, D_FF, D_MODEL), F32),
                            pltpu.VMEM((D_MODEL, D_FF), BF16),
                            pltpu.VMEM((D_MODEL, D_FF), BF16),
                            pltpu.VMEM((D_FF, D_MODEL), BF16),
                            tile_buf, tile_buf,
                            pltpu.SemaphoreType.DMA((2, 3)),
                            pltpu.SemaphoreType.DMA((2,)),
                            pltpu.SemaphoreType.DMA((2,))]),
        out_shape=jax.ShapeDtypeStruct((2 * t, ROW_TILES, LANES), BF16),
        compiler_params=_cp(("arbitrary",), 48, has_side_effects=True),
        name="experts",
    )(tile_e, next_e, slot, valid, row_slot, n_act, hp, w_gate, w_up, w_down)


def _combine_kernel(y0_ref, y1_ref, x_ref, wt_ref, p_ref, nw_ref, wg_ref, wp_ref, fw_ref, o_ref):
    wt = wt_ref[...]
    y0 = _from_token_major(y0_ref[...]).astype(F32)
    y1 = _from_token_major(y1_ref[...]).astype(F32)
    x2 = x_ref[...] + wt[:, 0:1] * y0 + wt[:, 1:2] * y1
    hn = _rms(x2, nw_ref[...]).astype(BF16)
    pg = jax.nn.sigmoid(_dot(hn, wg_ref[...]))
    x3 = x2 + pg * _dot(p_ref[...].astype(BF16), wp_ref[...])
    o_ref[...] = _rms(x3, fw_ref[...])


def _combine_ple(yt, x1, wts_t, p, ple_norm_w, w_gate, w_proj, final_w, tm=512):
    t = x1.shape[0]
    nb = t // tm
    row = lambda i: (i, 0)
    fix = lambda i: (0, 0)
    return pl.pallas_call(
        _combine_kernel,
        grid=(nb,),
        in_specs=[pl.BlockSpec((tm, ROW_TILES, LANES), lambda i: (i, 0, 0)),
                  pl.BlockSpec((tm, ROW_TILES, LANES), lambda i: (nb + i, 0, 0)),
                  pl.BlockSpec((tm, D_MODEL), row),
                  pl.BlockSpec((tm, SUBLANES), row),
                  pl.BlockSpec((tm, PLE_DIM), row),
                  pl.BlockSpec((1, D_MODEL), fix),
                  pl.BlockSpec((D_MODEL, D_MODEL), fix),
                  pl.BlockSpec((PLE_DIM, D_MODEL), fix),
                  pl.BlockSpec((1, D_MODEL), fix)],
        out_specs=pl.BlockSpec((tm, D_MODEL), row),
        out_shape=jax.ShapeDtypeStruct((t, D_MODEL), F32),
        compiler_params=_cp(("arbitrary",), 56),
        name="combine_ple",
    )(yt, yt, x1, wts_t, p, ple_norm_w, w_gate, w_proj, final_w)


def _tile_table(counts, n_tiles):
    tiles = (counts.astype(I32) + (EXPERT_TILE - 1)) // EXPERT_TILE
    ends = jnp.cumsum(tiles)
    n_act = ends[-1]
    idx = jnp.minimum(jnp.arange(n_tiles, dtype=I32), n_act - 1)
    tile_e = jnp.sum((idx[:, None] >= ends[None, :]).astype(I32), axis=1).astype(I32)
    run_end = ends[tile_e]
    next_e = jnp.where(run_end < n_act, tile_e[jnp.minimum(run_end, n_tiles - 1)], -1).astype(I32)
    new_run = jnp.concatenate([jnp.ones((1,), I32), (tile_e[1:] != tile_e[:-1]).astype(I32)])
    slot = ((jnp.cumsum(new_run) - 1) % 2).astype(I32)
    arange = jnp.arange(n_tiles, dtype=I32)
    last = arange == run_end - 1
    valid = jnp.where(last, counts.astype(I32)[tile_e] - (tiles[tile_e] - 1) * EXPERT_TILE, EXPERT_TILE)
    valid = jnp.where(arange < n_act, valid, 0).astype(I32)
    return tile_e, next_e, slot, valid, n_act.reshape(1).astype(I32)


def _block(x, p, norm_mix_w, w_in, b_merge, w_alpha_up, b_alpha_up, gla_norm_w, w_gla_out,
           conv_w, conv_b, w_conv_out, w_mix_out, norm_ffn_w, w_router_group, b_router_group,
           w_router_expert, b_router_expert, w_e_gate, w_e_up, w_e_down, ple_norm_w,
           w_ple_gate, w_ple_proj, final_norm_w):
    t = x.shape[0]
    n_tiles = (2 * t) // EXPERT_TILE + N_EXPERTS
    n_rows = n_tiles * EXPERT_TILE

    w_up = jnp.pad(w_alpha_up, ((0, LANES - GLA_GATE_RANK), (0, 0))).astype(BF16)
    w_up = w_up.reshape(LANES, GLA_HEADS, GLA_DK).transpose(1, 0, 2)
    b_up = b_alpha_up.reshape(GLA_HEADS, 1, GLA_DK)
    gnw = gla_norm_w.reshape(GLA_HEADS, 1, GLA_DV)
    wr_t = jnp.zeros((ROUTER_ROWS, D_MODEL), F32)
    wr_t = wr_t.at[0:N_GROUPS].set(w_router_group.T)
    wr_t = wr_t.at[EXPERT_ROW0:EXPERT_ROW0 + N_EXPERTS].set(w_router_expert.T)
    br = jnp.zeros((ROUTER_ROWS, 1), F32)
    br = br.at[0:N_GROUPS, 0].set(b_router_group)
    br = br.at[EXPERT_ROW0:EXPERT_ROW0 + N_EXPERTS, 0].set(b_router_expert)

    w_in_t = w_in.T
    h, a_low = _norm_in(x, norm_mix_w.reshape(1, D_MODEL), w_in_t)
    qkvg = _proj_qkvg(h, w_in_t)
    ob = _proj_conv(h, w_in_t, conv_w, conv_b.reshape(1, D_MODEL))
    gates = _proj_gates(h, w_in_t, b_merge.reshape(1, 2 * D_MODEL))
    oa = _gla(qkvg, a_low, w_up, b_up, gnw)
    mixed = _merge(oa, ob, w_gla_out, w_conv_out, gates)
    x1 = _mix(mixed, w_mix_out, x)

    hp, ids, wts = _route(x1, norm_ffn_w.reshape(1, D_MODEL), wr_t, br)
    dest, counts = _plan(ids)
    dest_flat = dest[1, 0:2].reshape(2 * t)
    tile_e, next_e, slot, valid, n_act = _tile_table(counts[:, 0], n_tiles)
    row_slot = _invert(dest_flat, n_rows)
    yt = _experts(tile_e, next_e, slot, valid, row_slot, n_act, hp, w_e_gate, w_e_up, w_e_down, n_tiles)
    return _combine_ple(yt, x1, wts.T, p,
                        ple_norm_w.reshape(1, D_MODEL), w_ple_gate.astype(BF16),
                        w_ple_proj.astype(BF16), final_norm_w.reshape(1, D_MODEL))


def kernel(x, p, norm_mix_w, w_in, b_merge, w_alpha_up, b_alpha_up, gla_norm_w, w_gla_out, conv_w, conv_b, w_conv_out, w_mix_out, norm_ffn_w, w_router_group, b_router_group, w_router_expert, b_router_expert, w_e_gate, w_e_up, w_e_down, ple_norm_w, w_ple_gate, w_ple_proj, final_norm_w):
    depth, batch = p.shape[0], x.shape[0]
    assert depth == 1 and batch == 1, "kernel is specialised to one layer and one sequence"
    out = _block(x[0], p[0, 0], norm_mix_w[0], w_in[0], b_merge[0], w_alpha_up[0], b_alpha_up[0],
                 gla_norm_w[0], w_gla_out[0], conv_w[0], conv_b[0], w_conv_out[0], w_mix_out[0],
                 norm_ffn_w[0], w_router_group[0], b_router_group[0], w_router_expert[0],
                 b_router_expert[0], w_e_gate[0], w_e_up[0], w_e_down[0], ple_norm_w[0],
                 w_ple_gate[0], w_ple_proj[0], final_norm_w)
    return out[None]
```

```python
import functools

import jax
import jax.numpy as jnp
from jax import lax
from jax.experimental import pallas as pl
from jax.experimental.pallas import tpu as pltpu

F32 = jnp.float32
BF16 = jnp.bfloat16
I32 = jnp.int32

D_MODEL = 2048
PLE_DIM = 256
EPS = 1e-6
LOG2_E = 1.4426950408889634
GLA_HEADS = 4
GLA_DK = 256
GLA_DV = 512
GLA_KEY = GLA_HEADS * GLA_DK
GLA_VAL = GLA_HEADS * GLA_DV
GLA_GATE_RANK = 16
GLA_GATE_NORM = 16.0
GLA_CHUNK = 64
CONV_K = 3
N_GROUPS = 4
EXPERTS_PER_GROUP = 8
N_EXPERTS = N_GROUPS * EXPERTS_PER_GROUP
D_FF = 512

QKVG_COLS = 2 * GLA_KEY + 2 * GLA_VAL
CONV_COL0 = QKVG_COLS + GLA_GATE_RANK

LANES = 128
SUBLANES = 8
MXU_COLS = 256
ROW_CHUNK = 256
ROW_TILES = D_MODEL // LANES

EXPERT_TILE = 256
DMA_UNROLL = 8
ROUTER_ROWS = 64
EXPERT_ROW0 = 8

MIB = 1024 * 1024


def _cp(sem, vmem_mib, **kw):
    return pltpu.CompilerParams(dimension_semantics=sem, vmem_limit_bytes=int(vmem_mib * MIB), **kw)


def _rms(x, w):
    return x * lax.rsqrt(jnp.mean(x * x, axis=-1, keepdims=True) + EPS) * w


def _dot(a, b):
    return jnp.dot(a, b, preferred_element_type=F32)


def _dot_nt(a, b):
    return lax.dot_general(a, b, (((1,), (1,)), ((), ())), preferred_element_type=F32)


def _dot_tn(a, b):
    return lax.dot_general(a, b, (((0,), (0,)), ((), ())), preferred_element_type=F32)


def _split_bf16(x):
    hi = x.astype(BF16)
    lo = (x - hi.astype(F32)).astype(BF16)
    return hi, lo


def _to_token_major(val):
    return val.reshape(val.shape[0], ROW_TILES, LANES)


def _from_token_major(val):
    return val.reshape(val.shape[0], D_MODEL)


def _norm_in_kernel(x_ref, w_ref, wal_ref, h_ref, al_ref, walbf_ref):
    @pl.when(pl.program_id(0) == 0)
    def _():
        walbf_ref[...] = wal_ref[...].astype(BF16)

    h = _rms(x_ref[...], w_ref[...]).astype(BF16)
    h_ref[...] = h
    al_ref[...] = _dot_nt(h, walbf_ref[...])


def _norm_in(x, w, w_in_t, tm=512):
    t = x.shape[0]
    return pl.pallas_call(
        _norm_in_kernel,
        grid=(t // tm,),
        in_specs=[pl.BlockSpec((tm, D_MODEL), lambda i: (i, 0)),
                  pl.BlockSpec((1, D_MODEL), lambda i: (0, 0)),
                  pl.BlockSpec((LANES, D_MODEL), lambda i: (QKVG_COLS // LANES, 0))],
        out_specs=[pl.BlockSpec((tm, D_MODEL), lambda i: (i, 0)),
                   pl.BlockSpec((tm, LANES), lambda i: (i, 0))],
        out_shape=[jax.ShapeDtypeStruct((t, D_MODEL), BF16),
                   jax.ShapeDtypeStruct((t, LANES), F32)],
        scratch_shapes=[pltpu.VMEM((LANES, D_MODEL), BF16)],
        compiler_params=_cp(("arbitrary",), 32),
        name="norm_in",
    )(x, w, w_in_t)


def _w_rows_spec(row0, tn):
    assert row0 % SUBLANES == 0 and tn % SUBLANES == 0
    return pl.BlockSpec((pl.Element(tn), pl.Element(D_MODEL)),
                        lambda n, m: (pl.multiple_of(row0 + n * tn, SUBLANES), 0))


def _proj_qkvg_kernel(h_ref, w_ref, o_ref, wbf_ref):
    @pl.when(pl.program_id(1) == 0)
    def _():
        wbf_ref[...] = w_ref[...].astype(BF16)

    o_ref[...] = _dot_nt(h_ref[...], wbf_ref[...]).astype(o_ref.dtype)


def _proj_qkvg(h, w_in_t, tm=2048, tn=1024):
    t = h.shape[0]
    return pl.pallas_call(
        _proj_qkvg_kernel,
        grid=(QKVG_COLS // tn, t // tm),
        in_specs=[pl.BlockSpec((tm, D_MODEL), lambda n, m: (m, 0)),
                  pl.BlockSpec((tn, D_MODEL), lambda n, m: (n, 0))],
        out_specs=pl.BlockSpec((tm, tn), lambda n, m: (m, n)),
        out_shape=jax.ShapeDtypeStruct((t, QKVG_COLS), BF16),
        scratch_shapes=[pltpu.VMEM((tn, D_MODEL), BF16)],
        compiler_params=_cp(("arbitrary", "arbitrary"), 56),
        name="proj_qkvg",
    )(h, w_in_t)


def _proj_conv_kernel(h_ref, wb_ref, wc_ref, wx_ref, cw_ref, cb_ref, o_ref,
                      wbbf_ref, wcbf_ref, wxbf_ref, prev_ref):
    m = pl.program_id(1)

    @pl.when(m == 0)
    def _():
        prev_ref[...] = jnp.zeros_like(prev_ref)
        wbbf_ref[...] = wb_ref[...].astype(BF16)
        wcbf_ref[...] = wc_ref[...].astype(BF16)
        wxbf_ref[...] = wx_ref[...].astype(BF16)

    h = h_ref[...]
    tm = h.shape[0]
    row = lax.broadcasted_iota(I32, (tm, MXU_COLS), 0)
    for c in range(0, o_ref.shape[1], MXU_COLS):
        cols = slice(c, c + MXU_COLS)
        b = _dot_nt(h, wbbf_ref[cols, :])
        s = _dot_nt(h, wcbf_ref[cols, :]) * _dot_nt(h, wxbf_ref[cols, :])
        prev = prev_ref[:, cols]
        p1 = prev[SUBLANES - 1:SUBLANES, :]
        p2 = prev[SUBLANES - 2:SUBLANES - 1, :]
        s1 = jnp.where(row == 0, p1, pltpu.roll(s, 1, 0))
        s2 = jnp.where(row == 0, p2, jnp.where(row == 1, p1, pltpu.roll(s, 2, 0)))
        cw = cw_ref[:, cols]
        u = cw[2:3, :] * s + cw[1:2, :] * s1 + cw[0:1, :] * s2 + cb_ref[:, cols]
        o_ref[:, cols] = (b * u).astype(o_ref.dtype)
        prev_ref[:, cols] = s[tm - SUBLANES:, :]


def _proj_conv(h, w_in_t, conv_w, conv_b, tm=1024, tn=512):
    t = h.shape[0]
    return pl.pallas_call(
        _proj_conv_kernel,
        grid=(D_MODEL // tn, t // tm),
        in_specs=[pl.BlockSpec((tm, D_MODEL), lambda n, m: (m, 0))] +
                 [_w_rows_spec(CONV_COL0 + seg * D_MODEL, tn) for seg in range(3)] +
                 [pl.BlockSpec((CONV_K, tn), lambda n, m: (0, n)),
                  pl.BlockSpec((1, tn), lambda n, m: (0, n))],
        out_specs=pl.BlockSpec((tm, tn), lambda n, m: (m, n)),
        out_shape=jax.ShapeDtypeStruct((t, D_MODEL), BF16),
        scratch_shapes=[pltpu.VMEM((tn, D_MODEL), BF16), pltpu.VMEM((tn, D_MODEL), BF16),
                        pltpu.VMEM((tn, D_MODEL), BF16), pltpu.VMEM((SUBLANES, tn), F32)],
        compiler_params=_cp(("arbitrary", "arbitrary"), 48),
        name="proj_conv",
    )(h, w_in_t, w_in_t, w_in_t, conv_w, conv_b)


def _proj_gates_kernel(h_ref, w_ref, b_ref, o_ref, wbf_ref):
    @pl.when(pl.program_id(1) == 0)
    def _():
        wbf_ref[...] = w_ref[...].astype(BF16)

    for c in range(0, o_ref.shape[1], MXU_COLS):
        cols = slice(c, c + MXU_COLS)
        for r in range(0, o_ref.shape[0], ROW_CHUNK):
            rows = slice(r, r + ROW_CHUNK)
            z = _dot_nt(h_ref[rows, :], wbf_ref[cols, :]) + b_ref[:, cols]
            o_ref[rows, cols] = jax.nn.sigmoid(z).astype(o_ref.dtype)


def _proj_gates(h, w_in_t, b_merge, tm=2048, tn=1024):
    t = h.shape[0]
    return pl.pallas_call(
        _proj_gates_kernel,
        grid=(2 * D_MODEL // tn, t // tm),
        in_specs=[pl.BlockSpec((tm, D_MODEL), lambda n, m: (m, 0)),
                  _w_rows_spec(CONV_COL0 + 3 * D_MODEL, tn),
                  pl.BlockSpec((1, tn), lambda n, m: (0, n))],
        out_specs=pl.BlockSpec((tm, tn), lambda n, m: (m, n)),
        out_shape=jax.ShapeDtypeStruct((t, 2 * D_MODEL), BF16),
        scratch_shapes=[pltpu.VMEM((tn, D_MODEL), BF16)],
        compiler_params=_cp(("arbitrary", "arbitrary"), 56),
        name="proj_gates",
    )(h, w_in_t, b_merge)


def _gla_kernel(q_ref, k_ref, v_ref, g_ref, al_ref, wup_ref, bup_ref, nw_ref, o_ref,
                st_ref, b_ref, bl_ref, qd_ref, ki_ref, kd_ref, oi_ref, u_ref, *, n_chunks):
    c_len = GLA_CHUNK

    @pl.when(pl.program_id(1) == 0)
    def _():
        st_ref[...] = jnp.zeros_like(st_ref)

    row = lax.broadcasted_iota(I32, (c_len, c_len), 0)
    col = lax.broadcasted_iota(I32, (c_len, c_len), 1)
    causal = col <= row
    tril = jnp.where(causal, 1.0, 0.0).astype(BF16)

    def decays(h):
        kc = slice(h * GLA_DK, (h + 1) * GLA_DK)
        z = _dot(al_ref[...].astype(BF16), wup_ref[h]) + bup_ref[h]
        la = (jnp.minimum(z, 0.0) - jnp.log1p(jnp.exp(-jnp.abs(z)))) * (LOG2_E / GLA_GATE_NORM)
        la_hi, la_lo = _split_bf16(la)
        for c in range(n_chunks):
            r0 = c * c_len
            b = _dot(tril, la_hi[r0:r0 + c_len]) + _dot(tril, la_lo[r0:r0 + c_len])
            b_ref[pl.ds(r0, c_len), kc] = b
            bl_ref[pl.ds(r0, c_len), kc] = jnp.broadcast_to(b[c_len - 1:c_len, :], b.shape)
        b = b_ref[:, kc]
        q = q_ref[:, kc].astype(F32)
        k = k_ref[:, kc].astype(F32)
        qd_ref[:, kc] = (q * jnp.exp2(b)).astype(BF16)
        ki_ref[:, kc] = (k * jnp.exp2(-b)).astype(BF16)
        kd_ref[:, kc] = (k * jnp.exp2(bl_ref[:, kc] - b)).astype(BF16)

    def local_products(h):
        kc = slice(h * GLA_DK, (h + 1) * GLA_DK)
        vc = slice(h * GLA_DV, (h + 1) * GLA_DV)
        for c in range(n_chunks):
            sl = pl.ds(c * c_len, c_len)
            v = v_ref[sl, vc]
            att = jnp.where(causal, _dot_nt(qd_ref[sl, kc], ki_ref[sl, kc]), 0.0).astype(BF16)
            oi_ref[sl, vc] = _dot(att, v)
            u_ref[h, c] = _dot_tn(v, kd_ref[sl, kc])

    def recurrence(h):
        kc = slice(h * GLA_DK, (h + 1) * GLA_DK)
        vc = slice(h * GLA_DV, (h + 1) * GLA_DV)
        for c in range(n_chunks):
            sl = pl.ds(c * c_len, c_len)
            st = st_ref[h]
            oi_ref[sl, vc] = oi_ref[sl, vc] + _dot_nt(qd_ref[sl, kc], st.astype(BF16))
            st_ref[h] = st * jnp.exp2(bl_ref[pl.ds(c * c_len, 1), kc]) + u_ref[h, c]
        o = oi_ref[:, vc]
        o = o * lax.rsqrt(jnp.mean(o * o, axis=-1, keepdims=True) + EPS * GLA_DK) * nw_ref[h]
        g = g_ref[:, vc].astype(F32)
        o_ref[:, vc] = (o * (g * jax.nn.sigmoid(g))).astype(o_ref.dtype)

    decays(0)
    local_products(0)
    decays(1)
    recurrence(0)
    local_products(1)
    recurrence(1)


def _gla(qkvg, a_low, w_up, b_up, norm_w, tb=1024):
    t = qkvg.shape[0]
    hp = 2
    dk, dv = hp * GLA_DK, hp * GLA_DV
    kq = GLA_KEY // dk
    kv = 2 * GLA_KEY // dv
    kg = kv + GLA_VAL // dv
    n_chunks = tb // GLA_CHUNK
    kern = functools.partial(_gla_kernel, n_chunks=n_chunks)
    return pl.pallas_call(
        kern,
        grid=(GLA_HEADS // hp, t // tb),
        in_specs=[pl.BlockSpec((tb, dk), lambda h, i: (i, h)),
                  pl.BlockSpec((tb, dk), lambda h, i: (i, kq + h)),
                  pl.BlockSpec((tb, dv), lambda h, i: (i, kv + h)),
                  pl.BlockSpec((tb, dv), lambda h, i: (i, kg + h)),
                  pl.BlockSpec((tb, LANES), lambda h, i: (i, 0)),
                  pl.BlockSpec((hp, LANES, GLA_DK), lambda h, i: (h, 0, 0)),
                  pl.BlockSpec((hp, 1, GLA_DK), lambda h, i: (h, 0, 0)),
                  pl.BlockSpec((hp, 1, GLA_DV), lambda h, i: (h, 0, 0))],
        out_specs=pl.BlockSpec((tb, dv), lambda h, i: (i, h)),
        out_shape=jax.ShapeDtypeStruct((t, GLA_VAL), BF16),
        scratch_shapes=[pltpu.VMEM((hp, GLA_DV, GLA_DK), F32),
                        pltpu.VMEM((tb, dk), F32), pltpu.VMEM((tb, dk), F32),
                        pltpu.VMEM((tb, dk), BF16), pltpu.VMEM((tb, dk), BF16),
                        pltpu.VMEM((tb, dk), BF16),
                        pltpu.VMEM((tb, dv), F32),
                        pltpu.VMEM((hp, n_chunks, GLA_DV, GLA_DK), F32)],
        compiler_params=_cp(("arbitrary", "arbitrary"), 56),
        name="gla",
    )(qkvg, qkvg, qkvg, qkvg, a_low, w_up, b_up, norm_w)


def _merge_kernel(oa_ref, ob_ref, wa_ref, wb_ref, g0_ref, g1_ref, o_ref, wabf_ref, wbbf_ref):
    @pl.when(pl.program_id(1) == 0)
    def _():
        wabf_ref[...] = wa_ref[...].astype(BF16)
        wbbf_ref[...] = wb_ref[...].astype(BF16)

    oa = oa_ref[...]
    ob = ob_ref[...]
    for c in range(0, o_ref.shape[1], MXU_COLS):
        cols = slice(c, c + MXU_COLS)
        a = _dot(oa, wabf_ref[:, cols])
        b = _dot(ob, wbbf_ref[:, cols])
        o_ref[:, cols] = (g0_ref[:, cols].astype(F32) * a + g1_ref[:, cols].astype(F32) * b).astype(o_ref.dtype)


def _merge(oa, ob, w_a, w_b, gates, tm=1024, tn=512):
    t = oa.shape[0]
    nb = D_MODEL // tn
    return pl.pallas_call(
        _merge_kernel,
        grid=(nb, t // tm),
        in_specs=[pl.BlockSpec((tm, D_MODEL), lambda n, m: (m, 0)),
                  pl.BlockSpec((tm, D_MODEL), lambda n, m: (m, 0)),
                  pl.BlockSpec((D_MODEL, tn), lambda n, m: (0, n)),
                  pl.BlockSpec((D_MODEL, tn), lambda n, m: (0, n)),
                  pl.BlockSpec((tm, tn), lambda n, m: (m, n)),
                  pl.BlockSpec((tm, tn), lambda n, m: (m, nb + n))],
        out_specs=pl.BlockSpec((tm, tn), lambda n, m: (m, n)),
        out_shape=jax.ShapeDtypeStruct((t, D_MODEL), BF16),
        scratch_shapes=[pltpu.VMEM((D_MODEL, tn), BF16), pltpu.VMEM((D_MODEL, tn), BF16)],
        compiler_params=_cp(("arbitrary", "arbitrary"), 56),
        name="merge",
    )(oa, ob, w_a, w_b, gates, gates)


def _mix_kernel(a_ref, w_ref, x_ref, o_ref, wbf_ref):
    @pl.when(pl.program_id(1) == 0)
    def _():
        wbf_ref[...] = w_ref[...].astype(BF16)

    a = a_ref[...]
    for c in range(0, o_ref.shape[1], MXU_COLS):
        cols = slice(c, c + MXU_COLS)
        o_ref[:, cols] = x_ref[:, cols] + _dot(a, wbf_ref[:, cols])


def _mix(mixed, w, x, tm=1024, tn=1024):
    t = x.shape[0]
    return pl.pallas_call(
        _mix_kernel,
        grid=(D_MODEL // tn, t // tm),
        in_specs=[pl.BlockSpec((tm, D_MODEL), lambda n, m: (m, 0)),
                  pl.BlockSpec((D_MODEL, tn), lambda n, m: (0, n)),
                  pl.BlockSpec((tm, tn), lambda n, m: (m, n))],
        out_specs=pl.BlockSpec((tm, tn), lambda n, m: (m, n)),
        out_shape=jax.ShapeDtypeStruct((t, D_MODEL), F32),
        scratch_shapes=[pltpu.VMEM((D_MODEL, tn), BF16)],
        compiler_params=_cp(("arbitrary", "arbitrary"), 52),
        name="mix",
    )(mixed, w, x)


def _route_kernel(x_ref, nw_ref, wr_ref, br_ref, hp_ref, ids_ref, wts_ref):
    h = _rms(x_ref[...], nw_ref[...])
    hp_ref[...] = _to_token_major(h.astype(BF16))
    h_hi, h_lo = _split_bf16(h)
    w_hi, w_lo = _split_bf16(wr_ref[...])
    logits = _dot_nt(w_hi, h_hi) + _dot_nt(w_hi, h_lo) + _dot_nt(w_lo, h_hi) + br_ref[...]
    tm = logits.shape[1]

    best = logits[0:1, :]
    gidx = jnp.zeros((1, tm), I32)
    for i in range(1, N_GROUPS):
        li = logits[i:i + 1, :]
        take = li > best
        best = jnp.where(take, li, best)
        gidx = jnp.where(take, i, gidx)
    gsum = jnp.zeros((1, tm), F32)
    for i in range(N_GROUPS):
        gsum = gsum + jnp.exp(logits[i:i + 1, :] - best)
    g_p = 1.0 / gsum

    sel = logits[EXPERT_ROW0:EXPERT_ROW0 + EXPERTS_PER_GROUP, :]
    for g in range(1, N_GROUPS):
        r0 = EXPERT_ROW0 + g * EXPERTS_PER_GROUP
        sel = jnp.where(gidx == g, logits[r0:r0 + EXPERTS_PER_GROUP, :], sel)
    eio = lax.broadcasted_iota(I32, sel.shape, 0)
    m1 = jnp.max(sel, axis=0, keepdims=True)
    i1 = jnp.min(jnp.where(sel == m1, eio, EXPERTS_PER_GROUP), axis=0, keepdims=True)
    rest = jnp.where(eio == i1, -jnp.inf, sel)
    m2 = jnp.max(rest, axis=0, keepdims=True)
    i2 = jnp.min(jnp.where(rest == m2, eio, EXPERTS_PER_GROUP), axis=0, keepdims=True)
    p2 = jnp.exp(m2 - m1)
    w1 = g_p / (1.0 + p2)
    w2 = g_p * p2 / (1.0 + p2)
    e1 = gidx * EXPERTS_PER_GROUP + i1
    e2 = gidx * EXPERTS_PER_GROUP + i2
    rio = lax.broadcasted_iota(I32, (SUBLANES, tm), 0)
    ids_ref[...] = jnp.where(rio == 0, e1, jnp.where(rio == 1, e2, 0))
    wts_ref[...] = jnp.where(rio == 0, w1, jnp.where(rio == 1, w2, 0.0))


def _route(x1, norm_w, wr_t, br, tm=512):
    t = x1.shape[0]
    return pl.pallas_call(
        _route_kernel,
        grid=(t // tm,),
        in_specs=[pl.BlockSpec((tm, D_MODEL), lambda i: (i, 0)),
                  pl.BlockSpec((1, D_MODEL), lambda i: (0, 0)),
                  pl.BlockSpec((ROUTER_ROWS, D_MODEL), lambda i: (0, 0)),
                  pl.BlockSpec((ROUTER_ROWS, 1), lambda i: (0, 0))],
        out_specs=[pl.BlockSpec((tm, ROW_TILES, LANES), lambda i: (i, 0, 0)),
                   pl.BlockSpec((SUBLANES, tm), lambda i: (0, i)),
                   pl.BlockSpec((SUBLANES, tm), lambda i: (0, i))],
        out_shape=[jax.ShapeDtypeStruct((t, ROW_TILES, LANES), BF16),
                   jax.ShapeDtypeStruct((SUBLANES, t), I32),
                   jax.ShapeDtypeStruct((SUBLANES, t), F32)],
        compiler_params=_cp(("arbitrary",), 32),
        name="route",
    )(x1, norm_w, wr_t, br)


def _plan_kernel(ids_ref, dest_ref, cnt_ref, base_ref):
    phase = pl.program_id(0)
    step = pl.program_id(1)
    tm = ids_ref.shape[1]
    eio = lax.broadcasted_iota(I32, (N_EXPERTS, tm), 0)
    ids = ids_ref[...]
    oh = [jnp.where(eio == ids[k:k + 1, :], 1.0, 0.0) for k in range(2)]

    @pl.when((phase == 0) & (step == 0))
    def _():
        base_ref[...] = jnp.zeros_like(base_ref)

    @pl.when(phase == 0)
    def _():
        cnt = jnp.sum(oh[0] + oh[1], axis=1, keepdims=True)
        base_ref[...] = base_ref[...] + cnt
        dest_ref[0] = jnp.zeros(dest_ref.shape[1:], I32)
        cnt_ref[...] = base_ref[...]

    @pl.when((phase == 1) & (step == 0))
    def _():
        tiles = jnp.floor((base_ref[...] + (EXPERT_TILE - 1)) * (1.0 / EXPERT_TILE))
        r = lax.broadcasted_iota(I32, (N_EXPERTS, N_EXPERTS), 0)
        c = lax.broadcasted_iota(I32, (N_EXPERTS, N_EXPERTS), 1)
        lower = jnp.where(c < r, 1.0, 0.0).astype(BF16)
        base_ref[...] = _dot(lower, tiles.astype(BF16)) * float(EXPERT_TILE)

    @pl.when(phase == 1)
    def _():
        r = lax.broadcasted_iota(I32, (tm, tm), 0)
        c = lax.broadcasted_iota(I32, (tm, tm), 1)
        upper = jnp.where(r <= c, 1.0, 0.0).astype(BF16)
        base = base_ref[...][:, 0:1]
        rows = []
        for k in range(2):
            cum = _dot(oh[k].astype(BF16), upper)
            rows.append(jnp.sum(oh[k] * (cum - 1.0 + base), axis=0, keepdims=True))
            base = base + cum[:, tm - 1:tm]
        base_ref[...] = jnp.broadcast_to(base, base_ref.shape)
        rio = lax.broadcasted_iota(I32, (SUBLANES, tm), 0)
        d0 = rows[0].astype(I32)
        d1 = rows[1].astype(I32)
        dest_ref[0] = jnp.where(rio == 0, d0, jnp.where(rio == 1, d1, 0))


def _plan(ids, tm=512):
    t = ids.shape[1]
    return pl.pallas_call(
        _plan_kernel,
        grid=(2, t // tm),
        in_specs=[pl.BlockSpec((SUBLANES, tm), lambda p, i: (0, i))],
        out_specs=[pl.BlockSpec((1, SUBLANES, tm), lambda p, i: (p, 0, i)),
                   pl.BlockSpec((N_EXPERTS, LANES), lambda p, i: (0, 0))],
        out_shape=[jax.ShapeDtypeStruct((2, SUBLANES, t), I32),
                   jax.ShapeDtypeStruct((N_EXPERTS, LANES), F32)],
        scratch_shapes=[pltpu.VMEM((N_EXPERTS, LANES), F32)],
        compiler_params=_cp(("arbitrary", "arbitrary"), 32),
        name="plan",
    )(ids)


def _invert_kernel(dest_ref, slot_ref, fill_ref, sem, *, n_slots):
    fill_ref[...] = jnp.full(fill_ref.shape, n_slots, I32)
    fill = pltpu.make_async_copy(fill_ref, slot_ref, sem)
    fill.start()
    fill.wait()

    def scatter(j, c):
        for u in range(DMA_UNROLL):
            q = j * DMA_UNROLL + u
            slot_ref[dest_ref[q]] = q
        return c

    lax.fori_loop(0, n_slots // DMA_UNROLL, scatter, 0)


def _invert(dest_flat, n_rows):
    n_slots = dest_flat.shape[0]
    assert n_slots % DMA_UNROLL == 0
    return pl.pallas_call(
        functools.partial(_invert_kernel, n_slots=n_slots),
        grid_spec=pltpu.PrefetchScalarGridSpec(
            num_scalar_prefetch=1,
            grid=(1,),
            in_specs=[],
            out_specs=pl.BlockSpec(memory_space=pltpu.SMEM),
            scratch_shapes=[pltpu.VMEM((n_rows,), I32), pltpu.SemaphoreType.DMA(())]),
        out_shape=jax.ShapeDtypeStruct((n_rows,), I32),
        compiler_params=_cp(("arbitrary",), 16),
        name="invert",
    )(dest_flat)


def _experts_kernel(te_ref, nx_ref, sl_ref, vr_ref, rs_ref, na_ref, hp_hbm, wg_hbm, wu_hbm, wd_hbm, yt_hbm,
                    wgf_ref, wuf_ref, wdf_ref, wgbf_ref, wubf_ref, wdbf_ref, xbuf_ref, ybuf_ref,
                    sems, xsems, ysems, *, t_total):
    i = pl.program_id(0)
    n_act = na_ref[0]
    active = i < n_act
    changed = (i == 0) | (te_ref[i] != te_ref[jnp.maximum(i - 1, 0)])

    def weight_copies(e, slot):
        return (pltpu.make_async_copy(wg_hbm.at[e], wgf_ref.at[slot], sems.at[slot, 0]),
                pltpu.make_async_copy(wu_hbm.at[e], wuf_ref.at[slot], sems.at[slot, 1]),
                pltpu.make_async_copy(wd_hbm.at[e], wdf_ref.at[slot], sems.at[slot, 2]))

    def in_copy(tile, r):
        tok = rs_ref[tile * EXPERT_TILE + r] & (t_total - 1)
        return pltpu.make_async_copy(hp_hbm.at[tok], xbuf_ref.at[tile % 2, r], xsems.at[tile % 2])

    def out_copy(tile, r):
        dst = yt_hbm.at[rs_ref[tile * EXPERT_TILE + r]]
        return pltpu.make_async_copy(ybuf_ref.at[tile % 2, r], dst, ysems.at[tile % 2])

    def for_rows(tile, make_copy, fn):
        n = vr_ref[tile]
        groups = lax.shift_right_logical(n, DMA_UNROLL.bit_length() - 1)

        def body(j, c):
            for u in range(DMA_UNROLL):
                fn(make_copy(tile, j * DMA_UNROLL + u))
            return c

        def tail(r, c):
            fn(make_copy(tile, r))
            return c

        lax.fori_loop(0, groups, body, 0)
        lax.fori_loop(groups * DMA_UNROLL, n, tail, 0)

    def wait_rows(tile, make_copy, whole_tile_copy):
        full = vr_ref[tile] == EXPERT_TILE

        @pl.when(full)
        def _():
            whole_tile_copy.wait()

        @pl.when(jnp.logical_not(full))
        def _():
            for_rows(tile, make_copy, lambda cp: cp.wait())

    def in_tile(tile):
        return pltpu.make_async_copy(hp_hbm.at[pl.ds(0, EXPERT_TILE)], xbuf_ref.at[tile % 2], xsems.at[tile % 2])

    def out_tile(tile):
        return pltpu.make_async_copy(ybuf_ref.at[tile % 2], yt_hbm.at[pl.ds(0, EXPERT_TILE)], ysems.at[tile % 2])

    @pl.when(i == 0)
    def _():
        xbuf_ref[...] = jnp.zeros_like(xbuf_ref)
        for cp in weight_copies(te_ref[0], 0):
            cp.start(priority=1)
        for_rows(i, in_copy, lambda cp: cp.start())

    @pl.when(i + 1 < n_act)
    def _():
        for_rows(i + 1, in_copy, lambda cp: cp.start())

    @pl.when(active & changed)
    def _():
        slot = sl_ref[i]
        nxt = nx_ref[i]

        @pl.when(nxt >= 0)
        def _():
            for cp in weight_copies(nxt, 1 - slot):
                cp.start(priority=1)

        for cp in weight_copies(te_ref[i], slot):
            cp.wait()
        wgbf_ref[...] = wgf_ref[slot].astype(BF16)
        wubf_ref[...] = wuf_ref[slot].astype(BF16)
        wdbf_ref[...] = wdf_ref[slot].astype(BF16)

    @pl.when(active)
    def _():
        wait_rows(i, in_copy, in_tile(i))
        x = _from_token_major(xbuf_ref[i % 2])
        hg = _dot(x, wgbf_ref[...])
        hu = _dot(x, wubf_ref[...])
        act = (hg * jax.nn.sigmoid(hg) * hu).astype(BF16)
        ybuf_ref[i % 2] = _to_token_major(_dot(act, wdbf_ref[...]).astype(BF16))

    @pl.when((i >= 1) & (i - 1 < n_act))
    def _():
        wait_rows(i - 1, out_copy, out_tile(i - 1))

    @pl.when(active)
    def _():
        for_rows(i, out_copy, lambda cp: cp.start())

    @pl.when(active & (i == pl.num_programs(0) - 1))
    def _():
        wait_rows(i, out_copy, out_tile(i))


def _experts(tile_e, next_e, slot, valid, row_slot, n_act, hp, w_gate, w_up, w_down, n_tiles):
    t = hp.shape[0]
    assert t & (t - 1) == 0, "the row map packs slot * T + token with T a power of two"
    any_spec = pl.BlockSpec(memory_space=pl.ANY)
    tile_buf = pltpu.VMEM((2, EXPERT_TILE, ROW_TILES, LANES), BF16)
    return pl.pallas_call(
        functools.partial(_experts_kernel, t_total=t),
        grid_spec=pltpu.PrefetchScalarGridSpec(
            num_scalar_prefetch=6,
            grid=(n_tiles,),
            in_specs=[any_spec, any_spec, any_spec, any_spec],
            out_specs=any_spec,
            scratch_shapes=[pltpu.VMEM((2, D_MODEL, D_FF), F32),
                            pltpu.VMEM((2, D_MODEL, D_FF), F32),
                            pltpu.VMEM((2, D_FF, D_MODEL), F32),
                            pltpu.VMEM((D_MODEL, D_FF), BF16),
                            pltpu.VMEM((D_MODEL, D_FF), BF16),
                            pltpu.VMEM((D_FF, D_MODEL), BF16),
                            tile_buf, tile_buf,
                            pltpu.SemaphoreType.DMA((2, 3)),
                            pltpu.SemaphoreType.DMA((2,)),
                            pltpu.SemaphoreType.DMA((2,))]),
        out_shape=jax.ShapeDtypeStruct((2 * t, ROW_TILES, LANES), BF16),
        compiler_params=_cp(("arbitrary",), 48, has_side_effects=True),
        name="experts",
    )(tile_e, next_e, slot, valid, row_slot, n_act, hp, w_gate, w_up, w_down)


def _combine_kernel(y0_ref, y1_ref, x_ref, wt_ref, p_ref, nw_ref, wg_ref, wp_ref, fw_ref, o_ref):
    wt = wt_ref[...]
    y0 = _from_token_major(y0_ref[...]).astype(F32)
    y1 = _from_token_major(y1_ref[...]).astype(F32)
    x2 = x_ref[...] + wt[:, 0:1] * y0 + wt[:, 1:2] * y1
    hn = _rms(x2, nw_ref[...]).astype(BF16)
    pg = jax.nn.sigmoid(_dot(hn, wg_ref[...]))
    x3 = x2 + pg * _dot(p_ref[...].astype(BF16), wp_ref[...])
    o_ref[...] = _rms(x3, fw_ref[...])


def _combine_ple(yt, x1, wts_t, p, ple_norm_w, w_gate, w_proj, final_w, tm=512):
    t = x1.shape[0]
    nb = t // tm
    row = lambda i: (i, 0)
    fix = lambda i: (0, 0)
    return pl.pallas_call(
        _combine_kernel,
        grid=(nb,),
        in_specs=[pl.BlockSpec((tm, ROW_TILES, LANES), lambda i: (i, 0, 0)),
                  pl.BlockSpec((tm, ROW_TILES, LANES), lambda i: (nb + i, 0, 0)),
                  pl.BlockSpec((tm, D_MODEL), row),
                  pl.BlockSpec((tm, SUBLANES), row),
                  pl.BlockSpec((tm, PLE_DIM), row),
                  pl.BlockSpec((1, D_MODEL), fix),
                  pl.BlockSpec((D_MODEL, D_MODEL), fix),
                  pl.BlockSpec((PLE_DIM, D_MODEL), fix),
                  pl.BlockSpec((1, D_MODEL), fix)],
        out_specs=pl.BlockSpec((tm, D_MODEL), row),
        out_shape=jax.ShapeDtypeStruct((t, D_MODEL), F32),
        compiler_params=_cp(("arbitrary",), 56),
        name="combine_ple",
    )(yt, yt, x1, wts_t, p, ple_norm_w, w_gate, w_proj, final_w)


def _tile_table(counts, n_tiles):
    tiles = (counts.astype(I32) + (EXPERT_TILE - 1)) // EXPERT_TILE
    ends = jnp.cumsum(tiles)
    n_act = ends[-1]
    idx = jnp.minimum(jnp.arange(n_tiles, dtype=I32), n_act - 1)
    tile_e = jnp.sum((idx[:, None] >= ends[None, :]).astype(I32), axis=1).astype(I32)
    run_end = ends[tile_e]
    next_e = jnp.where(run_end < n_act, tile_e[jnp.minimum(run_end, n_tiles - 1)], -1).astype(I32)
    new_run = jnp.concatenate([jnp.ones((1,), I32), (tile_e[1:] != tile_e[:-1]).astype(I32)])
    slot = ((jnp.cumsum(new_run) - 1) % 2).astype(I32)
    arange = jnp.arange(n_tiles, dtype=I32)
    last = arange == run_end - 1
    valid = jnp.where(last, counts.astype(I32)[tile_e] - (tiles[tile_e] - 1) * EXPERT_TILE, EXPERT_TILE)
    valid = jnp.where(arange < n_act, valid, 0).astype(I32)
    return tile_e, next_e, slot, valid, n_act.reshape(1).astype(I32)


def _block(x, p, norm_mix_w, w_in, b_merge, w_alpha_up, b_alpha_up, gla_norm_w, w_gla_out,
           conv_w, conv_b, w_conv_out, w_mix_out, norm_ffn_w, w_router_group, b_router_group,
           w_router_expert, b_router_expert, w_e_gate, w_e_up, w_e_down, ple_norm_w,
           w_ple_gate, w_ple_proj, final_norm_w):
    t = x.shape[0]
    n_tiles = (2 * t) // EXPERT_TILE + N_EXPERTS
    n_rows = n_tiles * EXPERT_TILE

    w_up = jnp.pad(w_alpha_up, ((0, LANES - GLA_GATE_RANK), (0, 0))).astype(BF16)
    w_up = w_up.reshape(LANES, GLA_HEADS, GLA_DK).transpose(1, 0, 2)
    b_up = b_alpha_up.reshape(GLA_HEADS, 1, GLA_DK)
    gnw = gla_norm_w.reshape(GLA_HEADS, 1, GLA_DV)
    wr_t = jnp.zeros((ROUTER_ROWS, D_MODEL), F32)
    wr_t = wr_t.at[0:N_GROUPS].set(w_router_group.T)
    wr_t = wr_t.at[EXPERT_ROW0:EXPERT_ROW0 + N_EXPERTS].set(w_router_expert.T)
    br = jnp.zeros((ROUTER_ROWS, 1), F32)
    br = br.at[0:N_GROUPS, 0].set(b_router_group)
    br = br.at[EXPERT_ROW0:EXPERT_ROW0 + N_EXPERTS, 0].set(b_router_expert)

    w_in_t = w_in.T
    h, a_low = _norm_in(x, norm_mix_w.reshape(1, D_MODEL), w_in_t)
    qkvg = _proj_qkvg(h, w_in_t)
    ob = _proj_conv(h, w_in_t, conv_w, conv_b.reshape(1, D_MODEL))
    gates = _proj_gates(h, w_in_t, b_merge.reshape(1, 2 * D_MODEL))
    oa = _gla(qkvg, a_low, w_up, b_up, gnw)
    mixed = _merge(oa, ob, w_gla_out, w_conv_out, gates)
    x1 = _mix(mixed, w_mix_out, x)

    hp, ids, wts = _route(x1, norm_ffn_w.reshape(1, D_MODEL), wr_t, br)
    dest, counts = _plan(ids)
    dest_flat = dest[1, 0:2].reshape(2 * t)
    tile_e, next_e, slot, valid, n_act = _tile_table(counts[:, 0], n_tiles)
    row_slot = _invert(dest_flat, n_rows)
    yt = _experts(tile_e, next_e, slot, valid, row_slot, n_act, hp, w_e_gate, w_e_up, w_e_down, n_tiles)
    return _combine_ple(yt, x1, wts.T, p,
                        ple_norm_w.reshape(1, D_MODEL), w_ple_gate.astype(BF16),
                        w_ple_proj.astype(BF16), final_norm_w.reshape(1, D_MODEL))


def kernel(x, p, norm_mix_w, w_in, b_merge, w_alpha_up, b_alpha_up, gla_norm_w, w_gla_out, conv_w, conv_b, w_conv_out, w_mix_out, norm_ffn_w, w_router_group, b_router_group, w_router_expert, b_router_expert, w_e_gate, w_e_up, w_e_down, ple_norm_w, w_ple_gate, w_ple_proj, final_norm_w):
    depth, batch = p.shape[0], x.shape[0]
    assert depth == 1 and batch == 1, "kernel is specialised to one layer and one sequence"
    out = _block(x[0], p[0, 0], norm_mix_w[0], w_in[0], b_merge[0], w_alpha_up[0], b_alpha_up[0],
                 gla_norm_w[0], w_gla_out[0], conv_w[0], conv_b[0], w_conv_out[0], w_mix_out[0],
                 norm_ffn_w[0], w_router_group[0], b_router_group[0], w_router_expert[0],
                 b_router_expert[0], w_e_gate[0], w_e_up[0], w_e_down[0], ple_norm_w[0],
                 w_ple_gate[0], w_ple_proj[0], final_norm_w)
    return out[None]
```

```python
import functools

import jax
import jax.numpy as jnp
from jax import lax
from jax.experimental import pallas as pl
from jax.experimental.pallas import tpu as pltpu

F32 = jnp.float32
BF16 = jnp.bfloat16
I32 = jnp.int32

D_MODEL = 2048
PLE_DIM = 256
EPS = 1e-6
LOG2_E = 1.4426950408889634
GLA_HEADS = 4
GLA_DK = 256
GLA_DV = 512
GLA_KEY = GLA_HEADS * GLA_DK
GLA_VAL = GLA_HEADS * GLA_DV
GLA_GATE_RANK = 16
GLA_GATE_NORM = 16.0
GLA_CHUNK = 64
CONV_K = 3
N_GROUPS = 4
EXPERTS_PER_GROUP = 8
N_EXPERTS = N_GROUPS * EXPERTS_PER_GROUP
D_FF = 512

QKVG_COLS = 2 * GLA_KEY + 2 * GLA_VAL
CONV_COL0 = QKVG_COLS + GLA_GATE_RANK

LANES = 128
SUBLANES = 8
MXU_COLS = 256
ROW_CHUNK = 256
ROW_TILES = D_MODEL // LANES

EXPERT_TILE = 256
SORT_BLOCK = 512
DMA_UNROLL = 8
ROUTER_ROWS = 64
EXPERT_ROW0 = 8

MIB = 1024 * 1024


def _cp(sem, vmem_mib, **kw):
    return pltpu.CompilerParams(dimension_semantics=sem, vmem_limit_bytes=int(vmem_mib * MIB), **kw)


def _rms(x, w):
    return x * lax.rsqrt(jnp.mean(x * x, axis=-1, keepdims=True) + EPS) * w


def _dot(a, b):
    return jnp.dot(a, b, preferred_element_type=F32)


def _dot_nt(a, b):
    return lax.dot_general(a, b, (((1,), (1,)), ((), ())), preferred_element_type=F32)


def _dot_tn(a, b):
    return lax.dot_general(a, b, (((0,), (0,)), ((), ())), preferred_element_type=F32)


def _split_bf16(x):
    hi = x.astype(BF16)
    lo = (x - hi.astype(F32)).astype(BF16)
    return hi, lo


def _excl_prefix_rows(col):
    n = col.shape[0]
    r = lax.broadcasted_iota(I32, (n, n), 0)
    c = lax.broadcasted_iota(I32, (n, n), 1)
    lower = jnp.where(c < r, 1.0, 0.0).astype(BF16)
    hi, lo = _split_bf16(jnp.broadcast_to(col, (n, LANES)))
    return (_dot(lower, hi) + _dot(lower, lo))[:, 0:1]


def _to_token_major(val):
    return val.reshape(val.shape[0], ROW_TILES, LANES)


def _from_token_major(val):
    return val.reshape(val.shape[0], D_MODEL)


def _norm_in_kernel(x_ref, w_ref, wal_ref, h_ref, al_ref, walbf_ref):
    @pl.when(pl.program_id(0) == 0)
    def _():
        walbf_ref[...] = wal_ref[...].astype(BF16)

    h = _rms(x_ref[...], w_ref[...]).astype(BF16)
    h_ref[...] = h
    al_ref[...] = _dot_nt(h, walbf_ref[...])


def _norm_in(x, w, w_in_t, tm=512):
    t = x.shape[0]
    return pl.pallas_call(
        _norm_in_kernel,
        grid=(t // tm,),
        in_specs=[pl.BlockSpec((tm, D_MODEL), lambda i: (i, 0)),
                  pl.BlockSpec((1, D_MODEL), lambda i: (0, 0)),
                  pl.BlockSpec((LANES, D_MODEL), lambda i: (QKVG_COLS // LANES, 0))],
        out_specs=[pl.BlockSpec((tm, D_MODEL), lambda i: (i, 0)),
                   pl.BlockSpec((tm, LANES), lambda i: (i, 0))],
        out_shape=[jax.ShapeDtypeStruct((t, D_MODEL), BF16),
                   jax.ShapeDtypeStruct((t, LANES), F32)],
        scratch_shapes=[pltpu.VMEM((LANES, D_MODEL), BF16)],
        compiler_params=_cp(("arbitrary",), 32),
        name="norm_in",
    )(x, w, w_in_t)


def _w_rows_spec(row0, tn):
    assert row0 % SUBLANES == 0 and tn % SUBLANES == 0
    return pl.BlockSpec((pl.Element(tn), pl.Element(D_MODEL)),
                        lambda n, m: (pl.multiple_of(row0 + n * tn, SUBLANES), 0))


def _proj_qkvg_kernel(h_ref, w_ref, o_ref, wbf_ref):
    @pl.when(pl.program_id(1) == 0)
    def _():
        wbf_ref[...] = w_ref[...].astype(BF16)

    o_ref[...] = _dot_nt(h_ref[...], wbf_ref[...]).astype(o_ref.dtype)


def _proj_qkvg(h, w_in_t, tm=2048, tn=1024):
    t = h.shape[0]
    return pl.pallas_call(
        _proj_qkvg_kernel,
        grid=(QKVG_COLS // tn, t // tm),
        in_specs=[pl.BlockSpec((tm, D_MODEL), lambda n, m: (m, 0)),
                  pl.BlockSpec((tn, D_MODEL), lambda n, m: (n, 0))],
        out_specs=pl.BlockSpec((tm, tn), lambda n, m: (m, n)),
        out_shape=jax.ShapeDtypeStruct((t, QKVG_COLS), BF16),
        scratch_shapes=[pltpu.VMEM((tn, D_MODEL), BF16)],
        compiler_params=_cp(("arbitrary", "arbitrary"), 56),
        name="proj_qkvg",
    )(h, w_in_t)


def _proj_conv_kernel(h_ref, wb_ref, wc_ref, wx_ref, cw_ref, cb_ref, o_ref,
                      wbbf_ref, wcbf_ref, wxbf_ref, prev_ref):
    m = pl.program_id(1)

    @pl.when(m == 0)
    def _():
        prev_ref[...] = jnp.zeros_like(prev_ref)
        wbbf_ref[...] = wb_ref[...].astype(BF16)
        wcbf_ref[...] = wc_ref[...].astype(BF16)
        wxbf_ref[...] = wx_ref[...].astype(BF16)

    h = h_ref[...]
    tm = h.shape[0]
    row = lax.broadcasted_iota(I32, (tm, MXU_COLS), 0)
    for c in range(0, o_ref.shape[1], MXU_COLS):
        cols = slice(c, c + MXU_COLS)
        b = _dot_nt(h, wbbf_ref[cols, :])
        s = _dot_nt(h, wcbf_ref[cols, :]) * _dot_nt(h, wxbf_ref[cols, :])
        prev = prev_ref[:, cols]
        p1 = prev[SUBLANES - 1:SUBLANES, :]
        p2 = prev[SUBLANES - 2:SUBLANES - 1, :]
        s1 = jnp.where(row == 0, p1, pltpu.roll(s, 1, 0))
        s2 = jnp.where(row == 0, p2, jnp.where(row == 1, p1, pltpu.roll(s, 2, 0)))
        cw = cw_ref[:, cols]
        u = cw[2:3, :] * s + cw[1:2, :] * s1 + cw[0:1, :] * s2 + cb_ref[:, cols]
        o_ref[:, cols] = (b * u).astype(o_ref.dtype)
        prev_ref[:, cols] = s[tm - SUBLANES:, :]


def _proj_conv(h, w_in_t, conv_w, conv_b, tm=1024, tn=512):
    t = h.shape[0]
    return pl.pallas_call(
        _proj_conv_kernel,
        grid=(D_MODEL // tn, t // tm),
        in_specs=[pl.BlockSpec((tm, D_MODEL), lambda n, m: (m, 0))] +
                 [_w_rows_spec(CONV_COL0 + seg * D_MODEL, tn) for seg in range(3)] +
                 [pl.BlockSpec((CONV_K, tn), lambda n, m: (0, n)),
                  pl.BlockSpec((1, tn), lambda n, m: (0, n))],
        out_specs=pl.BlockSpec((tm, tn), lambda n, m: (m, n)),
        out_shape=jax.ShapeDtypeStruct((t, D_MODEL), BF16),
        scratch_shapes=[pltpu.VMEM((tn, D_MODEL), BF16), pltpu.VMEM((tn, D_MODEL), BF16),
                        pltpu.VMEM((tn, D_MODEL), BF16), pltpu.VMEM((SUBLANES, tn), F32)],
        compiler_params=_cp(("arbitrary", "arbitrary"), 48),
        name="proj_conv",
    )(h, w_in_t, w_in_t, w_in_t, conv_w, conv_b)


def _proj_gates_kernel(h_ref, w_ref, b_ref, o_ref, wbf_ref):
    @pl.when(pl.program_id(1) == 0)
    def _():
        wbf_ref[...] = w_ref[...].astype(BF16)

    for c in range(0, o_ref.shape[1], MXU_COLS):
        cols = slice(c, c + MXU_COLS)
        for r in range(0, o_ref.shape[0], ROW_CHUNK):
            rows = slice(r, r + ROW_CHUNK)
            z = _dot_nt(h_ref[rows, :], wbf_ref[cols, :]) + b_ref[:, cols]
            o_ref[rows, cols] = jax.nn.sigmoid(z).astype(o_ref.dtype)


def _proj_gates(h, w_in_t, b_merge, tm=2048, tn=1024):
    t = h.shape[0]
    return pl.pallas_call(
        _proj_gates_kernel,
        grid=(2 * D_MODEL // tn, t // tm),
        in_specs=[pl.BlockSpec((tm, D_MODEL), lambda n, m: (m, 0)),
                  _w_rows_spec(CONV_COL0 + 3 * D_MODEL, tn),
                  pl.BlockSpec((1, tn), lambda n, m: (0, n))],
        out_specs=pl.BlockSpec((tm, tn), lambda n, m: (m, n)),
        out_shape=jax.ShapeDtypeStruct((t, 2 * D_MODEL), BF16),
        scratch_shapes=[pltpu.VMEM((tn, D_MODEL), BF16)],
        compiler_params=_cp(("arbitrary", "arbitrary"), 56),
        name="proj_gates",
    )(h, w_in_t, b_merge)


def _gla_kernel(q_ref, k_ref, v_ref, g_ref, al_ref, wup_ref, bup_ref, nw_ref, o_ref,
                st_ref, b_ref, bl_ref, qd_ref, ki_ref, kd_ref, oi_ref, u_ref, *, n_chunks):
    c_len = GLA_CHUNK

    @pl.when(pl.program_id(1) == 0)
    def _():
        st_ref[...] = jnp.zeros_like(st_ref)

    row = lax.broadcasted_iota(I32, (c_len, c_len), 0)
    col = lax.broadcasted_iota(I32, (c_len, c_len), 1)
    causal = col <= row
    tril = jnp.where(causal, 1.0, 0.0).astype(BF16)

    def decays(h):
        kc = slice(h * GLA_DK, (h + 1) * GLA_DK)
        z = _dot(al_ref[...].astype(BF16), wup_ref[h]) + bup_ref[h]
        la = (jnp.minimum(z, 0.0) - jnp.log1p(jnp.exp(-jnp.abs(z)))) * (LOG2_E / GLA_GATE_NORM)
        la_hi, la_lo = _split_bf16(la)
        for c in range(n_chunks):
            r0 = c * c_len
            b = _dot(tril, la_hi[r0:r0 + c_len]) + _dot(tril, la_lo[r0:r0 + c_len])
            b_ref[pl.ds(r0, c_len), kc] = b
            bl_ref[pl.ds(r0, c_len), kc] = jnp.broadcast_to(b[c_len - 1:c_len, :], b.shape)
        b = b_ref[:, kc]
        q = q_ref[:, kc].astype(F32)
        k = k_ref[:, kc].astype(F32)
        qd_ref[:, kc] = (q * jnp.exp2(b)).astype(BF16)
        ki_ref[:, kc] = (k * jnp.exp2(-b)).astype(BF16)
        kd_ref[:, kc] = (k * jnp.exp2(bl_ref[:, kc] - b)).astype(BF16)

    def local_products(h):
        kc = slice(h * GLA_DK, (h + 1) * GLA_DK)
        vc = slice(h * GLA_DV, (h + 1) * GLA_DV)
        for c in range(n_chunks):
            sl = pl.ds(c * c_len, c_len)
            v = v_ref[sl, vc]
            att = jnp.where(causal, _dot_nt(qd_ref[sl, kc], ki_ref[sl, kc]), 0.0).astype(BF16)
            oi_ref[sl, vc] = _dot(att, v)
            u_ref[h, c] = _dot_tn(v, kd_ref[sl, kc])

    def recurrence(h):
        kc = slice(h * GLA_DK, (h + 1) * GLA_DK)
        vc = slice(h * GLA_DV, (h + 1) * GLA_DV)
        for c in range(n_chunks):
            sl = pl.ds(c * c_len, c_len)
            st = st_ref[h]
            oi_ref[sl, vc] = oi_ref[sl, vc] + _dot_nt(qd_ref[sl, kc], st.astype(BF16))
            st_ref[h] = st * jnp.exp2(bl_ref[pl.ds(c * c_len, 1), kc]) + u_ref[h, c]
        o = oi_ref[:, vc]
        o = o * lax.rsqrt(jnp.mean(o * o, axis=-1, keepdims=True) + EPS * GLA_DK) * nw_ref[h]
        g = g_ref[:, vc].astype(F32)
        o_ref[:, vc] = (o * (g * jax.nn.sigmoid(g))).astype(o_ref.dtype)

    decays(0)
    local_products(0)
    decays(1)
    recurrence(0)
    local_products(1)
    recurrence(1)


def _gla(qkvg, a_low, w_up, b_up, norm_w, tb=1024):
    t = qkvg.shape[0]
    hp = 2
    dk, dv = hp * GLA_DK, hp * GLA_DV
    kq = GLA_KEY // dk
    kv = 2 * GLA_KEY // dv
    kg = kv + GLA_VAL // dv
    n_chunks = tb // GLA_CHUNK
    kern = functools.partial(_gla_kernel, n_chunks=n_chunks)
    return pl.pallas_call(
        kern,
        grid=(GLA_HEADS // hp, t // tb),
        in_specs=[pl.BlockSpec((tb, dk), lambda h, i: (i, h)),
                  pl.BlockSpec((tb, dk), lambda h, i: (i, kq + h)),
                  pl.BlockSpec((tb, dv), lambda h, i: (i, kv + h)),
                  pl.BlockSpec((tb, dv), lambda h, i: (i, kg + h)),
                  pl.BlockSpec((tb, LANES), lambda h, i: (i, 0)),
                  pl.BlockSpec((hp, LANES, GLA_DK), lambda h, i: (h, 0, 0)),
                  pl.BlockSpec((hp, 1, GLA_DK), lambda h, i: (h, 0, 0)),
                  pl.BlockSpec((hp, 1, GLA_DV), lambda h, i: (h, 0, 0))],
        out_specs=pl.BlockSpec((tb, dv), lambda h, i: (i, h)),
        out_shape=jax.ShapeDtypeStruct((t, GLA_VAL), BF16),
        scratch_shapes=[pltpu.VMEM((hp, GLA_DV, GLA_DK), F32),
                        pltpu.VMEM((tb, dk), F32), pltpu.VMEM((tb, dk), F32),
                        pltpu.VMEM((tb, dk), BF16), pltpu.VMEM((tb, dk), BF16),
                        pltpu.VMEM((tb, dk), BF16),
                        pltpu.VMEM((tb, dv), F32),
                        pltpu.VMEM((hp, n_chunks, GLA_DV, GLA_DK), F32)],
        compiler_params=_cp(("arbitrary", "arbitrary"), 56),
        name="gla",
    )(qkvg, qkvg, qkvg, qkvg, a_low, w_up, b_up, norm_w)


def _merge_kernel(oa_ref, ob_ref, wa_ref, wb_ref, g0_ref, g1_ref, o_ref, wabf_ref, wbbf_ref):
    @pl.when(pl.program_id(1) == 0)
    def _():
        wabf_ref[...] = wa_ref[...].astype(BF16)
        wbbf_ref[...] = wb_ref[...].astype(BF16)

    oa = oa_ref[...]
    ob = ob_ref[...]
    for c in range(0, o_ref.shape[1], MXU_COLS):
        cols = slice(c, c + MXU_COLS)
        a = _dot(oa, wabf_ref[:, cols])
        b = _dot(ob, wbbf_ref[:, cols])
        o_ref[:, cols] = (g0_ref[:, cols].astype(F32) * a + g1_ref[:, cols].astype(F32) * b).astype(o_ref.dtype)


def _merge(oa, ob, w_a, w_b, gates, tm=1024, tn=512):
    t = oa.shape[0]
    nb = D_MODEL // tn
    return pl.pallas_call(
        _merge_kernel,
        grid=(nb, t // tm),
        in_specs=[pl.BlockSpec((tm, D_MODEL), lambda n, m: (m, 0)),
                  pl.BlockSpec((tm, D_MODEL), lambda n, m: (m, 0)),
                  pl.BlockSpec((D_MODEL, tn), lambda n, m: (0, n)),
                  pl.BlockSpec((D_MODEL, tn), lambda n, m: (0, n)),
                  pl.BlockSpec((tm, tn), lambda n, m: (m, n)),
                  pl.BlockSpec((tm, tn), lambda n, m: (m, nb + n))],
        out_specs=pl.BlockSpec((tm, tn), lambda n, m: (m, n)),
        out_shape=jax.ShapeDtypeStruct((t, D_MODEL), BF16),
        scratch_shapes=[pltpu.VMEM((D_MODEL, tn), BF16), pltpu.VMEM((D_MODEL, tn), BF16)],
        compiler_params=_cp(("arbitrary", "arbitrary"), 56),
        name="merge",
    )(oa, ob, w_a, w_b, gates, gates)


def _mix_kernel(a_ref, w_ref, x_ref, o_ref, wbf_ref):
    @pl.when(pl.program_id(1) == 0)
    def _():
        wbf_ref[...] = w_ref[...].astype(BF16)

    a = a_ref[...]
    for c in range(0, o_ref.shape[1], MXU_COLS):
        cols = slice(c, c + MXU_COLS)
        o_ref[:, cols] = x_ref[:, cols] + _dot(a, wbf_ref[:, cols])


def _mix(mixed, w, x, tm=1024, tn=1024):
    t = x.shape[0]
    return pl.pallas_call(
        _mix_kernel,
        grid=(D_MODEL // tn, t // tm),
        in_specs=[pl.BlockSpec((tm, D_MODEL), lambda n, m: (m, 0)),
                  pl.BlockSpec((D_MODEL, tn), lambda n, m: (0, n)),
                  pl.BlockSpec((tm, tn), lambda n, m: (m, n))],
        out_specs=pl.BlockSpec((tm, tn), lambda n, m: (m, n)),
        out_shape=jax.ShapeDtypeStruct((t, D_MODEL), F32),
        scratch_shapes=[pltpu.VMEM((D_MODEL, tn), BF16)],
        compiler_params=_cp(("arbitrary", "arbitrary"), 52),
        name="mix",
    )(mixed, w, x)


def _route_kernel(x_ref, nw_ref, wr_ref, br_ref, hp_ref, ids_ref, wts_ref):
    h = _rms(x_ref[...], nw_ref[...])
    h_hi, h_lo = _split_bf16(h)
    w_hi, w_lo = _split_bf16(wr_ref[...])
    logits = _dot_nt(w_hi, h_hi) + _dot_nt(w_hi, h_lo) + _dot_nt(w_lo, h_hi) + br_ref[...]
    tm = logits.shape[1]

    best = logits[0:1, :]
    gidx = jnp.zeros((1, tm), I32)
    for i in range(1, N_GROUPS):
        li = logits[i:i + 1, :]
        take = li > best
        best = jnp.where(take, li, best)
        gidx = jnp.where(take, i, gidx)
    gsum = jnp.zeros((1, tm), F32)
    for i in range(N_GROUPS):
        gsum = gsum + jnp.exp(logits[i:i + 1, :] - best)
    g_p = 1.0 / gsum

    sel = logits[EXPERT_ROW0:EXPERT_ROW0 + EXPERTS_PER_GROUP, :]
    for g in range(1, N_GROUPS):
        r0 = EXPERT_ROW0 + g * EXPERTS_PER_GROUP
        sel = jnp.where(gidx == g, logits[r0:r0 + EXPERTS_PER_GROUP, :], sel)
    eio = lax.broadcasted_iota(I32, sel.shape, 0)
    m1 = jnp.max(sel, axis=0, keepdims=True)
    i1 = jnp.min(jnp.where(sel == m1, eio, EXPERTS_PER_GROUP), axis=0, keepdims=True)
    rest = jnp.where(eio == i1, -jnp.inf, sel)
    m2 = jnp.max(rest, axis=0, keepdims=True)
    i2 = jnp.min(jnp.where(rest == m2, eio, EXPERTS_PER_GROUP), axis=0, keepdims=True)
    p2 = jnp.exp(m2 - m1)
    w1 = g_p / (1.0 + p2)
    w2 = g_p * p2 / (1.0 + p2)
    e1 = gidx * EXPERTS_PER_GROUP + i1
    e2 = gidx * EXPERTS_PER_GROUP + i2
    rio = lax.broadcasted_iota(I32, (SUBLANES, tm), 0)
    ids_ref[...] = jnp.where(rio == 0, e1, jnp.where(rio == 1, e2, 0))
    wts_ref[...] = jnp.where(rio == 0, w1, jnp.where(rio == 1, w2, 0.0))

    eio32 = lax.broadcasted_iota(I32, (N_EXPERTS, tm), 0)
    oh = [jnp.where(eio32 == e, 1.0, 0.0) for e in (e1, e2)]
    r = lax.broadcasted_iota(I32, (tm, tm), 0)
    c = lax.broadcasted_iota(I32, (tm, tm), 1)
    upper = jnp.where(r <= c, 1.0, 0.0).astype(BF16)
    cum = [_dot(o.astype(BF16), upper) for o in oh]
    cnt0 = cum[0][:, tm - 1:tm]
    first = _excl_prefix_rows(cnt0 + cum[1][:, tm - 1:tm])
    lpos0 = jnp.sum(oh[0] * (cum[0] - 1.0 + first), axis=0, keepdims=True).astype(I32)
    lpos1 = jnp.sum(oh[1] * (cum[1] - 1.0 + first + cnt0), axis=0, keepdims=True).astype(I32)
    jio = lax.broadcasted_iota(I32, (2 * tm, tm), 0)
    perm = jnp.where((jio == lpos0) | (jio == lpos1), 1.0, 0.0).astype(BF16)
    hp_ref[...] = _to_token_major(_dot(perm, h_hi).astype(BF16))


def _route(x1, norm_w, wr_t, br, tm=SORT_BLOCK):
    t = x1.shape[0]
    return pl.pallas_call(
        _route_kernel,
        grid=(t // tm,),
        in_specs=[pl.BlockSpec((tm, D_MODEL), lambda i: (i, 0)),
                  pl.BlockSpec((1, D_MODEL), lambda i: (0, 0)),
                  pl.BlockSpec((ROUTER_ROWS, D_MODEL), lambda i: (0, 0)),
                  pl.BlockSpec((ROUTER_ROWS, 1), lambda i: (0, 0))],
        out_specs=[pl.BlockSpec((2 * tm, ROW_TILES, LANES), lambda i: (i, 0, 0)),
                   pl.BlockSpec((SUBLANES, tm), lambda i: (0, i)),
                   pl.BlockSpec((SUBLANES, tm), lambda i: (0, i))],
        out_shape=[jax.ShapeDtypeStruct((2 * t, ROW_TILES, LANES), BF16),
                   jax.ShapeDtypeStruct((SUBLANES, t), I32),
                   jax.ShapeDtypeStruct((SUBLANES, t), F32)],
        compiler_params=_cp(("arbitrary",), 48),
        name="route",
    )(x1, norm_w, wr_t, br)


def _plan_kernel(ids_ref, dest_ref, cnt_ref, runs_ref, base_ref):
    phase = pl.program_id(0)
    step = pl.program_id(1)
    tm = ids_ref.shape[1]
    eio = lax.broadcasted_iota(I32, (N_EXPERTS, tm), 0)
    ids = ids_ref[...]
    oh = [jnp.where(eio == ids[k:k + 1, :], 1.0, 0.0) for k in range(2)]

    @pl.when((phase == 0) & (step == 0))
    def _():
        base_ref[...] = jnp.zeros_like(base_ref)

    @pl.when(phase == 0)
    def _():
        cnt = jnp.sum(oh[0] + oh[1], axis=1, keepdims=True)
        base_ref[...] = base_ref[...] + cnt
        dest_ref[0] = jnp.zeros(dest_ref.shape[1:], I32)
        runs_ref[0, 0] = jnp.zeros(runs_ref.shape[2:], F32)
        cnt_ref[...] = base_ref[...]

    @pl.when((phase == 1) & (step == 0))
    def _():
        tiles = jnp.floor((base_ref[...][:, 0:1] + (EXPERT_TILE - 1)) * (1.0 / EXPERT_TILE))
        base_ref[...] = jnp.broadcast_to(_excl_prefix_rows(tiles) * float(EXPERT_TILE), base_ref.shape)

    @pl.when(phase == 1)
    def _():
        r = lax.broadcasted_iota(I32, (tm, tm), 0)
        c = lax.broadcasted_iota(I32, (tm, tm), 1)
        upper = jnp.where(r <= c, 1.0, 0.0).astype(BF16)
        base = base_ref[...][:, 0:1]
        start = base
        rows = []
        for k in range(2):
            cum = _dot(oh[k].astype(BF16), upper)
            rows.append(jnp.sum(oh[k] * (cum - 1.0 + base), axis=0, keepdims=True))
            base = base + cum[:, tm - 1:tm]
        base_ref[...] = jnp.broadcast_to(base, base_ref.shape)
        length = base - start
        runs_ref[0, 0] = jnp.concatenate(
            [jnp.broadcast_to(v, (N_EXPERTS, LANES)) for v in (start, length, _excl_prefix_rows(length))], axis=0)
        rio = lax.broadcasted_iota(I32, (SUBLANES, tm), 0)
        d0 = rows[0].astype(I32)
        d1 = rows[1].astype(I32)
        dest_ref[0] = jnp.where(rio == 0, d0, jnp.where(rio == 1, d1, 0))


def _plan(ids, tm=SORT_BLOCK):
    t = ids.shape[1]
    return pl.pallas_call(
        _plan_kernel,
        grid=(2, t // tm),
        in_specs=[pl.BlockSpec((SUBLANES, tm), lambda p, i: (0, i))],
        out_specs=[pl.BlockSpec((1, SUBLANES, tm), lambda p, i: (p, 0, i)),
                   pl.BlockSpec((N_EXPERTS, LANES), lambda p, i: (0, 0)),
                   pl.BlockSpec((1, 1, 3 * N_EXPERTS, LANES), lambda p, i: (p, i, 0, 0))],
        out_shape=[jax.ShapeDtypeStruct((2, SUBLANES, t), I32),
                   jax.ShapeDtypeStruct((N_EXPERTS, LANES), F32),
                   jax.ShapeDtypeStruct((2, t // tm, 3 * N_EXPERTS, LANES), F32)],
        scratch_shapes=[pltpu.VMEM((N_EXPERTS, LANES), F32)],
        compiler_params=_cp(("arbitrary", "arbitrary"), 32),
        name="plan",
    )(ids)


def _invert_kernel(dest_ref, slot_ref, fill_ref, sem, *, n_slots):
    fill_ref[...] = jnp.full(fill_ref.shape, n_slots, I32)
    fill = pltpu.make_async_copy(fill_ref, slot_ref, sem)
    fill.start()
    fill.wait()

    def scatter(j, c):
        for u in range(DMA_UNROLL):
            q = j * DMA_UNROLL + u
            slot_ref[dest_ref[q]] = q
        return c

    lax.fori_loop(0, n_slots // DMA_UNROLL, scatter, 0)


def _invert(dest_flat, n_rows):
    n_slots = dest_flat.shape[0]
    assert n_slots % DMA_UNROLL == 0
    return pl.pallas_call(
        functools.partial(_invert_kernel, n_slots=n_slots),
        grid_spec=pltpu.PrefetchScalarGridSpec(
            num_scalar_prefetch=1,
            grid=(1,),
            in_specs=[],
            out_specs=pl.BlockSpec(memory_space=pltpu.SMEM),
            scratch_shapes=[pltpu.VMEM((n_rows,), I32), pltpu.SemaphoreType.DMA(())]),
        out_shape=jax.ShapeDtypeStruct((n_rows,), I32),
        compiler_params=_cp(("arbitrary",), 16),
        name="invert",
    )(dest_flat)


def _experts_kernel(te_ref, nx_ref, sl_ref, vr_ref, rs_ref, rd_ref, rc_ref, rl_ref, na_ref,
                    hp_hbm, wg_hbm, wu_hbm, wd_hbm, yt_hbm,
                    wgf_ref, wuf_ref, wdf_ref, wgbf_ref, wubf_ref, wdbf_ref, xbuf_ref, ybuf_ref,
                    sems, xsems, ysems, *, n_blocks):
    i = pl.program_id(0)
    n_act = na_ref[0]
    active = i < n_act
    changed = (i == 0) | (te_ref[i] != te_ref[jnp.maximum(i - 1, 0)])

    def weight_copies(e, slot):
        return (pltpu.make_async_copy(wg_hbm.at[e], wgf_ref.at[slot], sems.at[slot, 0]),
                pltpu.make_async_copy(wu_hbm.at[e], wuf_ref.at[slot], sems.at[slot, 1]),
                pltpu.make_async_copy(wd_hbm.at[e], wdf_ref.at[slot], sems.at[slot, 2]))

    def in_piece(tile, src, dst, rows):
        return pltpu.make_async_copy(hp_hbm.at[pl.ds(src, rows)], xbuf_ref.at[tile % 2, pl.ds(dst, rows)],
                                     xsems.at[tile % 2])

    def for_pieces(n, fn):
        for bit in reversed(range(EXPERT_TILE.bit_length())):
            rows = 1 << bit
            offset = lax.shift_left(lax.shift_right_logical(n, bit + 1), bit + 1)

            @pl.when((n > 0) & ((n & rows) != 0))
            def _():
                fn(offset, rows)

    def start_inputs(tile):
        e = te_ref[tile]
        r0 = tile * EXPERT_TILE
        r1 = r0 + vr_ref[tile]

        def per_block(b, c):
            run = b * N_EXPERTS + e
            first = rd_ref[run]
            lo = jnp.maximum(first, r0)
            n = jnp.minimum(first + rc_ref[run], r1) - lo
            src = b * (2 * SORT_BLOCK) + rl_ref[run] + (lo - first)
            for_pieces(n, lambda off, rows: in_piece(tile, src + off, lo - r0 + off, rows).start())
            return c

        lax.fori_loop(0, n_blocks, per_block, 0)

    def wait_inputs(tile):
        for_pieces(vr_ref[tile], lambda off, rows: in_piece(tile, off, off, rows).wait())

    def out_copy(tile, r):
        dst = yt_hbm.at[rs_ref[tile * EXPERT_TILE + r]]
        return pltpu.make_async_copy(ybuf_ref.at[tile % 2, r], dst, ysems.at[tile % 2])

    def for_rows(tile, make_copy, fn):
        n = vr_ref[tile]
        groups = lax.shift_right_logical(n, DMA_UNROLL.bit_length() - 1)

        def body(j, c):
            for u in range(DMA_UNROLL):
                fn(make_copy(tile, j * DMA_UNROLL + u))
            return c

        def tail(r, c):
            fn(make_copy(tile, r))
            return c

        lax.fori_loop(0, groups, body, 0)
        lax.fori_loop(groups * DMA_UNROLL, n, tail, 0)

    def wait_rows(tile, make_copy, whole_tile_copy):
        full = vr_ref[tile] == EXPERT_TILE

        @pl.when(full)
        def _():
            whole_tile_copy.wait()

        @pl.when(jnp.logical_not(full))
        def _():
            for_rows(tile, make_copy, lambda cp: cp.wait())

    def out_tile(tile):
        return pltpu.make_async_copy(ybuf_ref.at[tile % 2], yt_hbm.at[pl.ds(0, EXPERT_TILE)], ysems.at[tile % 2])

    @pl.when(i == 0)
    def _():
        xbuf_ref[...] = jnp.zeros_like(xbuf_ref)
        for cp in weight_copies(te_ref[0], 0):
            cp.start(priority=1)
        start_inputs(i)

    @pl.when(i + 1 < n_act)
    def _():
        start_inputs(i + 1)

    @pl.when(active & changed)
    def _():
        slot = sl_ref[i]
        nxt = nx_ref[i]

        @pl.when(nxt >= 0)
        def _():
            for cp in weight_copies(nxt, 1 - slot):
                cp.start(priority=1)

        for cp in weight_copies(te_ref[i], slot):
            cp.wait()
        wgbf_ref[...] = wgf_ref[slot].astype(BF16)
        wubf_ref[...] = wuf_ref[slot].astype(BF16)
        wdbf_ref[...] = wdf_ref[slot].astype(BF16)

    @pl.when(active)
    def _():
        wait_inputs(i)
        x = _from_token_major(xbuf_ref[i % 2])
        hg = _dot(x, wgbf_ref[...])
        hu = _dot(x, wubf_ref[...])
        act = (hg * jax.nn.sigmoid(hg) * hu).astype(BF16)
        ybuf_ref[i % 2] = _to_token_major(_dot(act, wdbf_ref[...]).astype(BF16))

    @pl.when((i >= 1) & (i - 1 < n_act))
    def _():
        wait_rows(i - 1, out_copy, out_tile(i - 1))

    @pl.when(active)
    def _():
        for_rows(i, out_copy, lambda cp: cp.start())

    @pl.when(active & (i == pl.num_programs(0) - 1))
    def _():
        wait_rows(i, out_copy, out_tile(i))


def _experts(tile_e, next_e, slot, valid, row_slot, run_first, run_len, run_local, n_act, hp,
             w_gate, w_up, w_down, n_tiles):
    n_slots = hp.shape[0]
    any_spec = pl.BlockSpec(memory_space=pl.ANY)
    tile_buf = pltpu.VMEM((2, EXPERT_TILE, ROW_TILES, LANES), BF16)
    return pl.pallas_call(
        functools.partial(_experts_kernel, n_blocks=n_slots // (2 * SORT_BLOCK)),
        grid_spec=pltpu.PrefetchScalarGridSpec(
            num_scalar_prefetch=9,
            grid=(n_tiles,),
            in_specs=[any_spec, any_spec, any_spec, any_spec],
            out_specs=any_spec,
            scratch_shapes=[pltpu.VMEM((2, D_MODEL, D_FF), F32),
                            pltpu.VMEM((2, D_MODEL, D_FF), F32),
                            pltpu.VMEM((2, D_FF, D_MODEL), F32),
                            pltpu.VMEM((D_MODEL, D_FF), BF16),
                            pltpu.VMEM((D_MODEL, D_FF), BF16),
                            pltpu.VMEM((D_FF, D_MODEL), BF16),
                            tile_buf, tile_buf,
                            pltpu.SemaphoreType.DMA((2, 3)),
                            pltpu.SemaphoreType.DMA((2,)),
                            pltpu.SemaphoreType.DMA((2,))]),
        out_shape=jax.ShapeDtypeStruct((n_slots, ROW_TILES, LANES), BF16),
        compiler_params=_cp(("arbitrary",), 48, has_side_effects=True),
        name="experts",
    )(tile_e, next_e, slot, valid, row_slot, run_first, run_len, run_local, n_act, hp, w_gate, w_up, w_down)


def _combine_kernel(y0_ref, y1_ref, x_ref, wt_ref, p_ref, nw_ref, wg_ref, wp_ref, fw_ref, o_ref):
    wt = wt_ref[...]
    y0 = _from_token_major(y0_ref[...]).astype(F32)
    y1 = _from_token_major(y1_ref[...]).astype(F32)
    x2 = x_ref[...] + wt[:, 0:1] * y0 + wt[:, 1:2] * y1
    hn = _rms(x2, nw_ref[...]).astype(BF16)
    pg = jax.nn.sigmoid(_dot(hn, wg_ref[...]))
    x3 = x2 + pg * _dot(p_ref[...].astype(BF16), wp_ref[...])
    o_ref[...] = _rms(x3, fw_ref[...])


def _combine_ple(yt, x1, wts_t, p, ple_norm_w, w_gate, w_proj, final_w, tm=512):
    t = x1.shape[0]
    nb = t // tm
    row = lambda i: (i, 0)
    fix = lambda i: (0, 0)
    return pl.pallas_call(
        _combine_kernel,
        grid=(nb,),
        in_specs=[pl.BlockSpec((tm, ROW_TILES, LANES), lambda i: (i, 0, 0)),
                  pl.BlockSpec((tm, ROW_TILES, LANES), lambda i: (nb + i, 0, 0)),
                  pl.BlockSpec((tm, D_MODEL), row),
                  pl.BlockSpec((tm, SUBLANES), row),
                  pl.BlockSpec((tm, PLE_DIM), row),
                  pl.BlockSpec((1, D_MODEL), fix),
                  pl.BlockSpec((D_MODEL, D_MODEL), fix),
                  pl.BlockSpec((PLE_DIM, D_MODEL), fix),
                  pl.BlockSpec((1, D_MODEL), fix)],
        out_specs=pl.BlockSpec((tm, D_MODEL), row),
        out_shape=jax.ShapeDtypeStruct((t, D_MODEL), F32),
        compiler_params=_cp(("arbitrary",), 56),
        name="combine_ple",
    )(yt, yt, x1, wts_t, p, ple_norm_w, w_gate, w_proj, final_w)


def _tile_table(counts, n_tiles):
    tiles = (counts.astype(I32) + (EXPERT_TILE - 1)) // EXPERT_TILE
    ends = jnp.cumsum(tiles)
    n_act = ends[-1]
    idx = jnp.minimum(jnp.arange(n_tiles, dtype=I32), n_act - 1)
    tile_e = jnp.sum((idx[:, None] >= ends[None, :]).astype(I32), axis=1).astype(I32)
    run_end = ends[tile_e]
    next_e = jnp.where(run_end < n_act, tile_e[jnp.minimum(run_end, n_tiles - 1)], -1).astype(I32)
    new_run = jnp.concatenate([jnp.ones((1,), I32), (tile_e[1:] != tile_e[:-1]).astype(I32)])
    slot = ((jnp.cumsum(new_run) - 1) % 2).astype(I32)
    arange = jnp.arange(n_tiles, dtype=I32)
    last = arange == run_end - 1
    valid = jnp.where(last, counts.astype(I32)[tile_e] - (tiles[tile_e] - 1) * EXPERT_TILE, EXPERT_TILE)
    valid = jnp.where(arange < n_act, valid, 0).astype(I32)
    return tile_e, next_e, slot, valid, n_act.reshape(1).astype(I32)


def _block(x, p, norm_mix_w, w_in, b_merge, w_alpha_up, b_alpha_up, gla_norm_w, w_gla_out,
           conv_w, conv_b, w_conv_out, w_mix_out, norm_ffn_w, w_router_group, b_router_group,
           w_router_expert, b_router_expert, w_e_gate, w_e_up, w_e_down, ple_norm_w,
           w_ple_gate, w_ple_proj, final_norm_w):
    t = x.shape[0]
    n_tiles = (2 * t) // EXPERT_TILE + N_EXPERTS
    n_rows = n_tiles * EXPERT_TILE

    w_up = jnp.pad(w_alpha_up, ((0, LANES - GLA_GATE_RANK), (0, 0))).astype(BF16)
    w_up = w_up.reshape(LANES, GLA_HEADS, GLA_DK).transpose(1, 0, 2)
    b_up = b_alpha_up.reshape(GLA_HEADS, 1, GLA_DK)
    gnw = gla_norm_w.reshape(GLA_HEADS, 1, GLA_DV)
    wr_t = jnp.zeros((ROUTER_ROWS, D_MODEL), F32)
    wr_t = wr_t.at[0:N_GROUPS].set(w_router_group.T)
    wr_t = wr_t.at[EXPERT_ROW0:EXPERT_ROW0 + N_EXPERTS].set(w_router_expert.T)
    br = jnp.zeros((ROUTER_ROWS, 1), F32)
    br = br.at[0:N_GROUPS, 0].set(b_router_group)
    br = br.at[EXPERT_ROW0:EXPERT_ROW0 + N_EXPERTS, 0].set(b_router_expert)

    w_in_t = w_in.T
    h, a_low = _norm_in(x, norm_mix_w.reshape(1, D_MODEL), w_in_t)
    qkvg = _proj_qkvg(h, w_in_t)
    ob = _proj_conv(h, w_in_t, conv_w, conv_b.reshape(1, D_MODEL))
    gates = _proj_gates(h, w_in_t, b_merge.reshape(1, 2 * D_MODEL))
    oa = _gla(qkvg, a_low, w_up, b_up, gnw)
    mixed = _merge(oa, ob, w_gla_out, w_conv_out, gates)
    x1 = _mix(mixed, w_mix_out, x)

    hp, ids, wts = _route(x1, norm_ffn_w.reshape(1, D_MODEL), wr_t, br)
    dest, counts, runs = _plan(ids)
    runs = runs[1, :, :, 0].astype(I32)
    run_first, run_len, run_local = (runs[:, k * N_EXPERTS:(k + 1) * N_EXPERTS].reshape(-1) for k in range(3))
    dest_flat = dest[1, 0:2].reshape(2 * t)
    tile_e, next_e, slot, valid, n_act = _tile_table(counts[:, 0], n_tiles)
    row_slot = _invert(dest_flat, n_rows)
    yt = _experts(tile_e, next_e, slot, valid, row_slot, run_first, run_len, run_local, n_act, hp,
                  w_e_gate, w_e_up, w_e_down, n_tiles)
    return _combine_ple(yt, x1, wts.T, p,
                        ple_norm_w.reshape(1, D_MODEL), w_ple_gate.astype(BF16),
                        w_ple_proj.astype(BF16), final_norm_w.reshape(1, D_MODEL))


def kernel(x, p, norm_mix_w, w_in, b_merge, w_alpha_up, b_alpha_up, gla_norm_w, w_gla_out, conv_w, conv_b, w_conv_out, w_mix_out, norm_ffn_w, w_router_group, b_router_group, w_router_expert, b_router_expert, w_e_gate, w_e_up, w_e_down, ple_norm_w, w_ple_gate, w_ple_proj, final_norm_w):
    depth, batch = p.shape[0], x.shape[0]
    assert depth == 1 and batch == 1, "kernel is specialised to one layer and one sequence"
    out = _block(x[0], p[0, 0], norm_mix_w[0], w_in[0], b_merge[0], w_alpha_up[0], b_alpha_up[0],
                 gla_norm_w[0], w_gla_out[0], conv_w[0], conv_b[0], w_conv_out[0], w_mix_out[0],
                 norm_ffn_w[0], w_router_group[0], b_router_group[0], w_router_expert[0],
                 b_router_expert[0], w_e_gate[0], w_e_up[0], w_e_down[0], ple_norm_w[0],
                 w_ple_gate[0], w_ple_proj[0], final_norm_w)
    return out[None]
```

```python
import functools

import jax
import jax.numpy as jnp
from jax import lax
from jax.experimental import pallas as pl
from jax.experimental.pallas import tpu as pltpu

F32 = jnp.float32
BF16 = jnp.bfloat16
I32 = jnp.int32

D_MODEL = 2048
PLE_DIM = 256
EPS = 1e-6
LOG2_E = 1.4426950408889634
GLA_HEADS = 4
GLA_DK = 256
GLA_DV = 512
GLA_KEY = GLA_HEADS * GLA_DK
GLA_VAL = GLA_HEADS * GLA_DV
GLA_GATE_RANK = 16
GLA_GATE_NORM = 16.0
GLA_CHUNK = 64
CONV_K = 3
N_GROUPS = 4
EXPERTS_PER_GROUP = 8
N_EXPERTS = N_GROUPS * EXPERTS_PER_GROUP
D_FF = 512

QKVG_COLS = 2 * GLA_KEY + 2 * GLA_VAL
CONV_COL0 = QKVG_COLS + GLA_GATE_RANK

LANES = 128
SUBLANES = 8
MXU_COLS = 256
ROW_CHUNK = 256
ROW_TILES = D_MODEL // LANES

EXPERT_TILE = 256
DMA_UNROLL = 8
ROUTER_ROWS = 64
EXPERT_ROW0 = 8

MIB = 1024 * 1024


def _cp(sem, vmem_mib, **kw):
    return pltpu.CompilerParams(dimension_semantics=sem, vmem_limit_bytes=int(vmem_mib * MIB), **kw)


def _rms(x, w):
    return x * lax.rsqrt(jnp.mean(x * x, axis=-1, keepdims=True) + EPS) * w


def _dot(a, b):
    return jnp.dot(a, b, preferred_element_type=F32)


def _dot_nt(a, b):
    return lax.dot_general(a, b, (((1,), (1,)), ((), ())), preferred_element_type=F32)


def _dot_tn(a, b):
    return lax.dot_general(a, b, (((0,), (0,)), ((), ())), preferred_element_type=F32)


def _split_bf16(x):
    hi = x.astype(BF16)
    lo = (x - hi.astype(F32)).astype(BF16)
    return hi, lo


def _to_token_major(val):
    return val.reshape(val.shape[0], ROW_TILES, LANES)


def _from_token_major(val):
    return val.reshape(val.shape[0], D_MODEL)


def _norm_in_kernel(x_ref, w_ref, wal_ref, h_ref, al_ref, walbf_ref):
    @pl.when(pl.program_id(0) == 0)
    def _():
        walbf_ref[...] = wal_ref[...].astype(BF16)

    h = _rms(x_ref[...], w_ref[...]).astype(BF16)
    h_ref[...] = h
    al_ref[...] = _dot_nt(h, walbf_ref[...])


def _norm_in(x, w, w_in_t, tm=512):
    t = x.shape[0]
    return pl.pallas_call(
        _norm_in_kernel,
        grid=(t // tm,),
        in_specs=[pl.BlockSpec((tm, D_MODEL), lambda i: (i, 0)),
                  pl.BlockSpec((1, D_MODEL), lambda i: (0, 0)),
                  pl.BlockSpec((LANES, D_MODEL), lambda i: (QKVG_COLS // LANES, 0))],
        out_specs=[pl.BlockSpec((tm, D_MODEL), lambda i: (i, 0)),
                   pl.BlockSpec((tm, LANES), lambda i: (i, 0))],
        out_shape=[jax.ShapeDtypeStruct((t, D_MODEL), BF16),
                   jax.ShapeDtypeStruct((t, LANES), F32)],
        scratch_shapes=[pltpu.VMEM((LANES, D_MODEL), BF16)],
        compiler_params=_cp(("arbitrary",), 32),
        name="norm_in",
    )(x, w, w_in_t)


def _w_rows_spec(row0, tn):
    assert row0 % SUBLANES == 0 and tn % SUBLANES == 0
    return pl.BlockSpec((pl.Element(tn), pl.Element(D_MODEL)),
                        lambda n, m: (pl.multiple_of(row0 + n * tn, SUBLANES), 0))


def _proj_qkvg_kernel(h_ref, w_ref, o_ref, wbf_ref):
    @pl.when(pl.program_id(1) == 0)
    def _():
        wbf_ref[...] = w_ref[...].astype(BF16)

    o_ref[...] = _dot_nt(h_ref[...], wbf_ref[...]).astype(o_ref.dtype)


def _proj_qkvg(h, w_in_t, tm=2048, tn=1024):
    t = h.shape[0]
    return pl.pallas_call(
        _proj_qkvg_kernel,
        grid=(QKVG_COLS // tn, t // tm),
        in_specs=[pl.BlockSpec((tm, D_MODEL), lambda n, m: (m, 0)),
                  pl.BlockSpec((tn, D_MODEL), lambda n, m: (n, 0))],
        out_specs=pl.BlockSpec((tm, tn), lambda n, m: (m, n)),
        out_shape=jax.ShapeDtypeStruct((t, QKVG_COLS), BF16),
        scratch_shapes=[pltpu.VMEM((tn, D_MODEL), BF16)],
        compiler_params=_cp(("arbitrary", "arbitrary"), 56),
        name="proj_qkvg",
    )(h, w_in_t)


def _proj_conv_kernel(h_ref, wb_ref, wc_ref, wx_ref, cw_ref, cb_ref, o_ref,
                      wbbf_ref, wcbf_ref, wxbf_ref, prev_ref):
    m = pl.program_id(1)

    @pl.when(m == 0)
    def _():
        prev_ref[...] = jnp.zeros_like(prev_ref)
        wbbf_ref[...] = wb_ref[...].astype(BF16)
        wcbf_ref[...] = wc_ref[...].astype(BF16)
        wxbf_ref[...] = wx_ref[...].astype(BF16)

    h = h_ref[...]
    tm = h.shape[0]
    row = lax.broadcasted_iota(I32, (tm, MXU_COLS), 0)
    for c in range(0, o_ref.shape[1], MXU_COLS):
        cols = slice(c, c + MXU_COLS)
        b = _dot_nt(h, wbbf_ref[cols, :])
        s = _dot_nt(h, wcbf_ref[cols, :]) * _dot_nt(h, wxbf_ref[cols, :])
        prev = prev_ref[:, cols]
        p1 = prev[SUBLANES - 1:SUBLANES, :]
        p2 = prev[SUBLANES - 2:SUBLANES - 1, :]
        s1 = jnp.where(row == 0, p1, pltpu.roll(s, 1, 0))
        s2 = jnp.where(row == 0, p2, jnp.where(row == 1, p1, pltpu.roll(s, 2, 0)))
        cw = cw_ref[:, cols]
        u = cw[2:3, :] * s + cw[1:2, :] * s1 + cw[0:1, :] * s2 + cb_ref[:, cols]
        o_ref[:, cols] = (b * u).astype(o_ref.dtype)
        prev_ref[:, cols] = s[tm - SUBLANES:, :]


def _proj_conv(h, w_in_t, conv_w, conv_b, tm=1024, tn=512):
    t = h.shape[0]
    return pl.pallas_call(
        _proj_conv_kernel,
        grid=(D_MODEL // tn, t // tm),
        in_specs=[pl.BlockSpec((tm, D_MODEL), lambda n, m: (m, 0))] +
                 [_w_rows_spec(CONV_COL0 + seg * D_MODEL, tn) for seg in range(3)] +
                 [pl.BlockSpec((CONV_K, tn), lambda n, m: (0, n)),
                  pl.BlockSpec((1, tn), lambda n, m: (0, n))],
        out_specs=pl.BlockSpec((tm, tn), lambda n, m: (m, n)),
        out_shape=jax.ShapeDtypeStruct((t, D_MODEL), BF16),
        scratch_shapes=[pltpu.VMEM((tn, D_MODEL), BF16), pltpu.VMEM((tn, D_MODEL), BF16),
                        pltpu.VMEM((tn, D_MODEL), BF16), pltpu.VMEM((SUBLANES, tn), F32)],
        compiler_params=_cp(("arbitrary", "arbitrary"), 48),
        name="proj_conv",
    )(h, w_in_t, w_in_t, w_in_t, conv_w, conv_b)


def _proj_gates_kernel(h_ref, w_ref, b_ref, o_ref, wbf_ref):
    @pl.when(pl.program_id(1) == 0)
    def _():
        wbf_ref[...] = w_ref[...].astype(BF16)

    for c in range(0, o_ref.shape[1], MXU_COLS):
        cols = slice(c, c + MXU_COLS)
        for r in range(0, o_ref.shape[0], ROW_CHUNK):
            rows = slice(r, r + ROW_CHUNK)
            z = _dot_nt(h_ref[rows, :], wbf_ref[cols, :]) + b_ref[:, cols]
            o_ref[rows, cols] = jax.nn.sigmoid(z).astype(o_ref.dtype)


def _proj_gates(h, w_in_t, b_merge, tm=2048, tn=1024):
    t = h.shape[0]
    return pl.pallas_call(
        _proj_gates_kernel,
        grid=(2 * D_MODEL // tn, t // tm),
        in_specs=[pl.BlockSpec((tm, D_MODEL), lambda n, m: (m, 0)),
                  _w_rows_spec(CONV_COL0 + 3 * D_MODEL, tn),
                  pl.BlockSpec((1, tn), lambda n, m: (0, n))],
        out_specs=pl.BlockSpec((tm, tn), lambda n, m: (m, n)),
        out_shape=jax.ShapeDtypeStruct((t, 2 * D_MODEL), BF16),
        scratch_shapes=[pltpu.VMEM((tn, D_MODEL), BF16)],
        compiler_params=_cp(("arbitrary", "arbitrary"), 56),
        name="proj_gates",
    )(h, w_in_t, b_merge)


def _gla_kernel(q_ref, k_ref, v_ref, g_ref, al_ref, wup_ref, bup_ref, nw_ref, o_ref,
                st_ref, b_ref, bl_ref, qd_ref, ki_ref, kd_ref, oi_ref, u_ref, *, n_chunks):
    c_len = GLA_CHUNK

    @pl.when(pl.program_id(1) == 0)
    def _():
        st_ref[...] = jnp.zeros_like(st_ref)

    row = lax.broadcasted_iota(I32, (c_len, c_len), 0)
    col = lax.broadcasted_iota(I32, (c_len, c_len), 1)
    causal = col <= row
    tril = jnp.where(causal, 1.0, 0.0).astype(BF16)

    def decays(h):
        kc = slice(h * GLA_DK, (h + 1) * GLA_DK)
        z = _dot(al_ref[...].astype(BF16), wup_ref[h]) + bup_ref[h]
        la = (jnp.minimum(z, 0.0) - jnp.log1p(jnp.exp(-jnp.abs(z)))) * (LOG2_E / GLA_GATE_NORM)
        la_hi, la_lo = _split_bf16(la)
        for c in range(n_chunks):
            r0 = c * c_len
            b = _dot(tril, la_hi[r0:r0 + c_len]) + _dot(tril, la_lo[r0:r0 + c_len])
            b_ref[pl.ds(r0, c_len), kc] = b
            bl_ref[pl.ds(r0, c_len), kc] = jnp.broadcast_to(b[c_len - 1:c_len, :], b.shape)
        b = b_ref[:, kc]
        q = q_ref[:, kc].astype(F32)
        k = k_ref[:, kc].astype(F32)
        qd_ref[:, kc] = (q * jnp.exp2(b)).astype(BF16)
        ki_ref[:, kc] = (k * jnp.exp2(-b)).astype(BF16)
        kd_ref[:, kc] = (k * jnp.exp2(bl_ref[:, kc] - b)).astype(BF16)

    def local_products(h):
        kc = slice(h * GLA_DK, (h + 1) * GLA_DK)
        vc = slice(h * GLA_DV, (h + 1) * GLA_DV)
        for c in range(n_chunks):
            sl = pl.ds(c * c_len, c_len)
            v = v_ref[sl, vc]
            att = jnp.where(causal, _dot_nt(qd_ref[sl, kc], ki_ref[sl, kc]), 0.0).astype(BF16)
            oi_ref[sl, vc] = _dot(att, v)
            u_ref[h, c] = _dot_tn(v, kd_ref[sl, kc])

    def recurrence(h):
        kc = slice(h * GLA_DK, (h + 1) * GLA_DK)
        vc = slice(h * GLA_DV, (h + 1) * GLA_DV)
        for c in range(n_chunks):
            sl = pl.ds(c * c_len, c_len)
            st = st_ref[h]
            oi_ref[sl, vc] = oi_ref[sl, vc] + _dot_nt(qd_ref[sl, kc], st.astype(BF16))
            st_ref[h] = st * jnp.exp2(bl_ref[pl.ds(c * c_len, 1), kc]) + u_ref[h, c]
        o = oi_ref[:, vc]
        o = o * lax.rsqrt(jnp.mean(o * o, axis=-1, keepdims=True) + EPS * GLA_DK) * nw_ref[h]
        g = g_ref[:, vc].astype(F32)
        o_ref[:, vc] = (o * (g * jax.nn.sigmoid(g))).astype(o_ref.dtype)

    decays(0)
    local_products(0)
    decays(1)
    recurrence(0)
    local_products(1)
    recurrence(1)


def _gla(qkvg, a_low, w_up, b_up, norm_w, tb=1024):
    t = qkvg.shape[0]
    hp = 2
    dk, dv = hp * GLA_DK, hp * GLA_DV
    kq = GLA_KEY // dk
    kv = 2 * GLA_KEY // dv
    kg = kv + GLA_VAL // dv
    n_chunks = tb // GLA_CHUNK
    kern = functools.partial(_gla_kernel, n_chunks=n_chunks)
    return pl.pallas_call(
        kern,
        grid=(GLA_HEADS // hp, t // tb),
        in_specs=[pl.BlockSpec((tb, dk), lambda h, i: (i, h)),
                  pl.BlockSpec((tb, dk), lambda h, i: (i, kq + h)),
                  pl.BlockSpec((tb, dv), lambda h, i: (i, kv + h)),
                  pl.BlockSpec((tb, dv), lambda h, i: (i, kg + h)),
                  pl.BlockSpec((tb, LANES), lambda h, i: (i, 0)),
                  pl.BlockSpec((hp, LANES, GLA_DK), lambda h, i: (h, 0, 0)),
                  pl.BlockSpec((hp, 1, GLA_DK), lambda h, i: (h, 0, 0)),
                  pl.BlockSpec((hp, 1, GLA_DV), lambda h, i: (h, 0, 0))],
        out_specs=pl.BlockSpec((tb, dv), lambda h, i: (i, h)),
        out_shape=jax.ShapeDtypeStruct((t, GLA_VAL), BF16),
        scratch_shapes=[pltpu.VMEM((hp, GLA_DV, GLA_DK), F32),
                        pltpu.VMEM((tb, dk), F32), pltpu.VMEM((tb, dk), F32),
                        pltpu.VMEM((tb, dk), BF16), pltpu.VMEM((tb, dk), BF16),
                        pltpu.VMEM((tb, dk), BF16),
                        pltpu.VMEM((tb, dv), F32),
                        pltpu.VMEM((hp, n_chunks, GLA_DV, GLA_DK), F32)],
        compiler_params=_cp(("arbitrary", "arbitrary"), 56),
        name="gla",
    )(qkvg, qkvg, qkvg, qkvg, a_low, w_up, b_up, norm_w)


def _merge_kernel(oa_ref, ob_ref, wa_ref, wb_ref, g0_ref, g1_ref, o_ref, wabf_ref, wbbf_ref):
    @pl.when(pl.program_id(1) == 0)
    def _():
        wabf_ref[...] = wa_ref[...].astype(BF16)
        wbbf_ref[...] = wb_ref[...].astype(BF16)

    oa = oa_ref[...]
    ob = ob_ref[...]
    for c in range(0, o_ref.shape[1], MXU_COLS):
        cols = slice(c, c + MXU_COLS)
        a = _dot(oa, wabf_ref[:, cols])
        b = _dot(ob, wbbf_ref[:, cols])
        o_ref[:, cols] = (g0_ref[:, cols].astype(F32) * a + g1_ref[:, cols].astype(F32) * b).astype(o_ref.dtype)


def _merge(oa, ob, w_a, w_b, gates, tm=1024, tn=512):
    t = oa.shape[0]
    nb = D_MODEL // tn
    return pl.pallas_call(
        _merge_kernel,
        grid=(nb, t // tm),
        in_specs=[pl.BlockSpec((tm, D_MODEL), lambda n, m: (m, 0)),
                  pl.BlockSpec((tm, D_MODEL), lambda n, m: (m, 0)),
                  pl.BlockSpec((D_MODEL, tn), lambda n, m: (0, n)),
                  pl.BlockSpec((D_MODEL, tn), lambda n, m: (0, n)),
                  pl.BlockSpec((tm, tn), lambda n, m: (m, n)),
                  pl.BlockSpec((tm, tn), lambda n, m: (m, nb + n))],
        out_specs=pl.BlockSpec((tm, tn), lambda n, m: (m, n)),
        out_shape=jax.ShapeDtypeStruct((t, D_MODEL), BF16),
        scratch_shapes=[pltpu.VMEM((D_MODEL, tn), BF16), pltpu.VMEM((D_MODEL, tn), BF16)],
        compiler_params=_cp(("arbitrary", "arbitrary"), 56),
        name="merge",
    )(oa, ob, w_a, w_b, gates, gates)


def _mix_kernel(a_ref, w_ref, x_ref, o_ref, wbf_ref):
    @pl.when(pl.program_id(1) == 0)
    def _():
        wbf_ref[...] = w_ref[...].astype(BF16)

    a = a_ref[...]
    for c in range(0, o_ref.shape[1], MXU_COLS):
        cols = slice(c, c + MXU_COLS)
        o_ref[:, cols] = x_ref[:, cols] + _dot(a, wbf_ref[:, cols])


def _mix(mixed, w, x, tm=1024, tn=1024):
    t = x.shape[0]
    return pl.pallas_call(
        _mix_kernel,
        grid=(D_MODEL // tn, t // tm),
        in_specs=[pl.BlockSpec((tm, D_MODEL), lambda n, m: (m, 0)),
                  pl.BlockSpec((D_MODEL, tn), lambda n, m: (0, n)),
                  pl.BlockSpec((tm, tn), lambda n, m: (m, n))],
        out_specs=pl.BlockSpec((tm, tn), lambda n, m: (m, n)),
        out_shape=jax.ShapeDtypeStruct((t, D_MODEL), F32),
        scratch_shapes=[pltpu.VMEM((D_MODEL, tn), BF16)],
        compiler_params=_cp(("arbitrary", "arbitrary"), 52),
        name="mix",
    )(mixed, w, x)


def _route_kernel(x_ref, nw_ref, wr_ref, br_ref, hp_ref, ids_ref, wts_ref):
    h = _rms(x_ref[...], nw_ref[...])
    hp_ref[...] = _to_token_major(h.astype(BF16))
    h_hi, h_lo = _split_bf16(h)
    w_hi, w_lo = _split_bf16(wr_ref[...])
    logits = _dot_nt(w_hi, h_hi) + _dot_nt(w_hi, h_lo) + _dot_nt(w_lo, h_hi) + br_ref[...]
    tm = logits.shape[1]

    best = logits[0:1, :]
    gidx = jnp.zeros((1, tm), I32)
    for i in range(1, N_GROUPS):
        li = logits[i:i + 1, :]
        take = li > best
        best = jnp.where(take, li, best)
        gidx = jnp.where(take, i, gidx)
    gsum = jnp.zeros((1, tm), F32)
    for i in range(N_GROUPS):
        gsum = gsum + jnp.exp(logits[i:i + 1, :] - best)
    g_p = 1.0 / gsum

    sel = logits[EXPERT_ROW0:EXPERT_ROW0 + EXPERTS_PER_GROUP, :]
    for g in range(1, N_GROUPS):
        r0 = EXPERT_ROW0 + g * EXPERTS_PER_GROUP
        sel = jnp.where(gidx == g, logits[r0:r0 + EXPERTS_PER_GROUP, :], sel)
    eio = lax.broadcasted_iota(I32, sel.shape, 0)
    m1 = jnp.max(sel, axis=0, keepdims=True)
    i1 = jnp.min(jnp.where(sel == m1, eio, EXPERTS_PER_GROUP), axis=0, keepdims=True)
    rest = jnp.where(eio == i1, -jnp.inf, sel)
    m2 = jnp.max(rest, axis=0, keepdims=True)
    i2 = jnp.min(jnp.where(rest == m2, eio, EXPERTS_PER_GROUP), axis=0, keepdims=True)
    p2 = jnp.exp(m2 - m1)
    w1 = g_p / (1.0 + p2)
    w2 = g_p * p2 / (1.0 + p2)
    e1 = gidx * EXPERTS_PER_GROUP + i1
    e2 = gidx * EXPERTS_PER_GROUP + i2
    rio = lax.broadcasted_iota(I32, (SUBLANES, tm), 0)
    ids_ref[...] = jnp.where(rio == 0, e1, jnp.where(rio == 1, e2, 0))
    wts_ref[...] = jnp.where(rio == 0, w1, jnp.where(rio == 1, w2, 0.0))


def _route(x1, norm_w, wr_t, br, tm=512):
    t = x1.shape[0]
    return pl.pallas_call(
        _route_kernel,
        grid=(t // tm,),
        in_specs=[pl.BlockSpec((tm, D_MODEL), lambda i: (i, 0)),
                  pl.BlockSpec((1, D_MODEL), lambda i: (0, 0)),
                  pl.BlockSpec((ROUTER_ROWS, D_MODEL), lambda i: (0, 0)),
                  pl.BlockSpec((ROUTER_ROWS, 1), lambda i: (0, 0))],
        out_specs=[pl.BlockSpec((tm, ROW_TILES, LANES), lambda i: (i, 0, 0)),
                   pl.BlockSpec((SUBLANES, tm), lambda i: (0, i)),
                   pl.BlockSpec((SUBLANES, tm), lambda i: (0, i))],
        out_shape=[jax.ShapeDtypeStruct((t, ROW_TILES, LANES), BF16),
                   jax.ShapeDtypeStruct((SUBLANES, t), I32),
                   jax.ShapeDtypeStruct((SUBLANES, t), F32)],
        compiler_params=_cp(("arbitrary",), 32),
        name="route",
    )(x1, norm_w, wr_t, br)


def _plan_kernel(ids_ref, dest_ref, cnt_ref, base_ref):
    phase = pl.program_id(0)
    step = pl.program_id(1)
    tm = ids_ref.shape[1]
    eio = lax.broadcasted_iota(I32, (N_EXPERTS, tm), 0)
    ids = ids_ref[...]
    oh = [jnp.where(eio == ids[k:k + 1, :], 1.0, 0.0) for k in range(2)]

    @pl.when((phase == 0) & (step == 0))
    def _():
        base_ref[...] = jnp.zeros_like(base_ref)

    @pl.when(phase == 0)
    def _():
        cnt = jnp.sum(oh[0] + oh[1], axis=1, keepdims=True)
        base_ref[...] = base_ref[...] + cnt
        dest_ref[0] = jnp.zeros(dest_ref.shape[1:], I32)
        cnt_ref[...] = base_ref[...]

    @pl.when((phase == 1) & (step == 0))
    def _():
        tiles = jnp.floor((base_ref[...] + (EXPERT_TILE - 1)) * (1.0 / EXPERT_TILE))
        r = lax.broadcasted_iota(I32, (N_EXPERTS, N_EXPERTS), 0)
        c = lax.broadcasted_iota(I32, (N_EXPERTS, N_EXPERTS), 1)
        lower = jnp.where(c < r, 1.0, 0.0).astype(BF16)
        base_ref[...] = _dot(lower, tiles.astype(BF16)) * float(EXPERT_TILE)

    @pl.when(phase == 1)
    def _():
        r = lax.broadcasted_iota(I32, (tm, tm), 0)
        c = lax.broadcasted_iota(I32, (tm, tm), 1)
        upper = jnp.where(r <= c, 1.0, 0.0).astype(BF16)
        base = base_ref[...][:, 0:1]
        rows = []
        for k in range(2):
            cum = _dot(oh[k].astype(BF16), upper)
            rows.append(jnp.sum(oh[k] * (cum - 1.0 + base), axis=0, keepdims=True))
            base = base + cum[:, tm - 1:tm]
        base_ref[...] = jnp.broadcast_to(base, base_ref.shape)
        rio = lax.broadcasted_iota(I32, (SUBLANES, tm), 0)
        d0 = rows[0].astype(I32)
        d1 = rows[1].astype(I32)
        dest_ref[0] = jnp.where(rio == 0, d0, jnp.where(rio == 1, d1, 0))


def _plan(ids, tm=512):
    t = ids.shape[1]
    return pl.pallas_call(
        _plan_kernel,
        grid=(2, t // tm),
        in_specs=[pl.BlockSpec((SUBLANES, tm), lambda p, i: (0, i))],
        out_specs=[pl.BlockSpec((1, SUBLANES, tm), lambda p, i: (p, 0, i)),
                   pl.BlockSpec((N_EXPERTS, LANES), lambda p, i: (0, 0))],
        out_shape=[jax.ShapeDtypeStruct((2, SUBLANES, t), I32),
                   jax.ShapeDtypeStruct((N_EXPERTS, LANES), F32)],
        scratch_shapes=[pltpu.VMEM((N_EXPERTS, LANES), F32)],
        compiler_params=_cp(("arbitrary", "arbitrary"), 32),
        name="plan",
    )(ids)


def _invert_kernel(dest_ref, slot_ref, fill_ref, sem, *, n_slots):
    fill_ref[...] = jnp.full(fill_ref.shape, n_slots, I32)
    fill = pltpu.make_async_copy(fill_ref, slot_ref, sem)
    fill.start()
    fill.wait()

    def scatter(j, c):
        for u in range(DMA_UNROLL):
            q = j * DMA_UNROLL + u
            slot_ref[dest_ref[q]] = q
        return c

    lax.fori_loop(0, n_slots // DMA_UNROLL, scatter, 0)


def _invert(dest_flat, n_rows):
    n_slots = dest_flat.shape[0]
    assert n_slots % DMA_UNROLL == 0
    return pl.pallas_call(
        functools.partial(_invert_kernel, n_slots=n_slots),
        grid_spec=pltpu.PrefetchScalarGridSpec(
            num_scalar_prefetch=1,
            grid=(1,),
            in_specs=[],
            out_specs=pl.BlockSpec(memory_space=pltpu.SMEM),
            scratch_shapes=[pltpu.VMEM((n_rows,), I32), pltpu.SemaphoreType.DMA(())]),
        out_shape=jax.ShapeDtypeStruct((n_rows,), I32),
        compiler_params=_cp(("arbitrary",), 16),
        name="invert",
    )(dest_flat)


def _experts_kernel(te_ref, nx_ref, sl_ref, vr_ref, rs_ref, na_ref, hp_hbm, wg_hbm, wu_hbm, wd_hbm, yt_hbm,
                    wgf_ref, wuf_ref, wdf_ref, wgbf_ref, wubf_ref, wdbf_ref, xbuf_ref, ybuf_ref,
                    sems, xsems, ysems, *, t_total):
    i = pl.program_id(0)
    n_act = na_ref[0]
    active = i < n_act
    changed = (i == 0) | (te_ref[i] != te_ref[jnp.maximum(i - 1, 0)])

    def weight_copies(e, slot):
        return (pltpu.make_async_copy(wg_hbm.at[e], wgf_ref.at[slot], sems.at[slot, 0]),
                pltpu.make_async_copy(wu_hbm.at[e], wuf_ref.at[slot], sems.at[slot, 1]),
                pltpu.make_async_copy(wd_hbm.at[e], wdf_ref.at[slot], sems.at[slot, 2]))

    def in_copy(tile, r):
        tok = rs_ref[tile * EXPERT_TILE + r] & (t_total - 1)
        return pltpu.make_async_copy(hp_hbm.at[tok], xbuf_ref.at[tile % 2, r], xsems.at[tile % 2])

    def out_copy(tile, r):
        dst = yt_hbm.at[rs_ref[tile * EXPERT_TILE + r]]
        return pltpu.make_async_copy(ybuf_ref.at[tile % 2, r], dst, ysems.at[tile % 2])

    def for_rows(tile, make_copy, fn):
        n = vr_ref[tile]
        groups = lax.shift_right_logical(n, DMA_UNROLL.bit_length() - 1)

        def body(j, c):
            for u in range(DMA_UNROLL):
                fn(make_copy(tile, j * DMA_UNROLL + u))
            return c

        def tail(r, c):
            fn(make_copy(tile, r))
            return c

        lax.fori_loop(0, groups, body, 0)
        lax.fori_loop(groups * DMA_UNROLL, n, tail, 0)

    def wait_rows(tile, make_copy, whole_tile_copy):
        full = vr_ref[tile] == EXPERT_TILE

        @pl.when(full)
        def _():
            whole_tile_copy.wait()

        @pl.when(jnp.logical_not(full))
        def _():
            for_rows(tile, make_copy, lambda cp: cp.wait())

    def in_tile(tile):
        return pltpu.make_async_copy(hp_hbm.at[pl.ds(0, EXPERT_TILE)], xbuf_ref.at[tile % 2], xsems.at[tile % 2])

    def out_tile(tile):
        return pltpu.make_async_copy(ybuf_ref.at[tile % 2], yt_hbm.at[pl.ds(0, EXPERT_TILE)], ysems.at[tile % 2])

    @pl.when(i == 0)
    def _():
        xbuf_ref[...] = jnp.zeros_like(xbuf_ref)
        for cp in weight_copies(te_ref[0], 0):
            cp.start(priority=1)
        for_rows(i, in_copy, lambda cp: cp.start())

    last = pl.num_programs(0) - 1
    nxt_active = i + 1 < n_act
    nxt_full = nxt_active & (vr_ref[jnp.minimum(i + 1, last)] == EXPERT_TILE)
    prev_active = (i >= 1) & (i - 1 < n_act)
    prev_full = prev_active & (vr_ref[jnp.maximum(i - 1, 0)] == EXPERT_TILE)

    @pl.when((i >= 2) & (i - 2 < n_act))
    def _():
        wait_rows(i - 2, out_copy, out_tile(i - 2))

    @pl.when(nxt_active & jnp.logical_not(nxt_full))
    def _():
        for_rows(i + 1, in_copy, lambda cp: cp.start())

    @pl.when(prev_active & jnp.logical_not(active & prev_full))
    def _():
        for_rows(i - 1, out_copy, lambda cp: cp.start())

    @pl.when(active & changed)
    def _():
        slot = sl_ref[i]
        nxt = nx_ref[i]

        @pl.when(nxt >= 0)
        def _():
            for cp in weight_copies(nxt, 1 - slot):
                cp.start(priority=1)

        for cp in weight_copies(te_ref[i], slot):
            cp.wait()
        wgbf_ref[...] = wgf_ref[slot].astype(BF16)
        wubf_ref[...] = wuf_ref[slot].astype(BF16)
        wdbf_ref[...] = wdf_ref[slot].astype(BF16)

    def compute(fetch_next, send_prev):
        def row_copies(g):
            for r in range(g * (EXPERT_TILE // 4), (g + 1) * (EXPERT_TILE // 4)):
                if fetch_next:
                    in_copy(i + 1, r).start()
                if send_prev:
                    out_copy(i - 1, r).start()

        wait_rows(i, in_copy, in_tile(i))
        x = _from_token_major(xbuf_ref[i % 2])
        row_copies(0)
        hg = _dot(x, wgbf_ref[...])
        row_copies(1)
        hu = _dot(x, wubf_ref[...])
        row_copies(2)
        act = (hg * jax.nn.sigmoid(hg) * hu).astype(BF16)
        y = _dot(act, wdbf_ref[...])
        row_copies(3)
        ybuf_ref[i % 2] = _to_token_major(y.astype(BF16))

    for fetch_next in (False, True):
        for send_prev in (False, True):
            @pl.when(active & (nxt_full == fetch_next) & (prev_full == send_prev))
            def _(fetch_next=fetch_next, send_prev=send_prev):
                compute(fetch_next, send_prev)

    @pl.when(i == last)
    def _():
        @pl.when(active)
        def _():
            for_rows(i, out_copy, lambda cp: cp.start())
            wait_rows(i, out_copy, out_tile(i))

        @pl.when(prev_active)
        def _():
            wait_rows(i - 1, out_copy, out_tile(i - 1))


def _experts(tile_e, next_e, slot, valid, row_slot, n_act, hp, w_gate, w_up, w_down, n_tiles):
    t = hp.shape[0]
    assert t & (t - 1) == 0, "the row map packs slot * T + token with T a power of two"
    any_spec = pl.BlockSpec(memory_space=pl.ANY)
    tile_buf = pltpu.VMEM((2, EXPERT_TILE, ROW_TILES, LANES), BF16)
    return pl.pallas_call(
        functools.partial(_experts_kernel, t_total=t),
        grid_spec=pltpu.PrefetchScalarGridSpec(
            num_scalar_prefetch=6,
            grid=(n_tiles,),
            in_specs=[any_spec, any_spec, any_spec, any_spec],
            out_specs=any_spec,
            scratch_shapes=[pltpu.VMEM((2, D_MODEL, D_FF), F32),
                            pltpu.VMEM((2, D_MODEL, D_FF), F32),
                            pltpu.VMEM((2, D_FF, D_MODEL), F32),
                            pltpu.VMEM((D_MODEL, D_FF), BF16),
                            pltpu.VMEM((D_MODEL, D_FF), BF16),
                            pltpu.VMEM((D_FF, D_MODEL), BF16),
                            tile_buf, tile_buf,
                            pltpu.SemaphoreType.DMA((2, 3)),
                            pltpu.SemaphoreType.DMA((2,)),
                            pltpu.SemaphoreType.DMA((2,))]),
        out_shape=jax.ShapeDtypeStruct((2 * t, ROW_TILES, LANES), BF16),
        compiler_params=_cp(("arbitrary",), 48, has_side_effects=True),
        name="experts",
    )(tile_e, next_e, slot, valid, row_slot, n_act, hp, w_gate, w_up, w_down)


def _combine_kernel(y0_ref, y1_ref, x_ref, wt_ref, p_ref, nw_ref, wg_ref, wp_ref, fw_ref, o_ref):
    wt = wt_ref[...]
    y0 = _from_token_major(y0_ref[...]).astype(F32)
    y1 = _from_token_major(y1_ref[...]).astype(F32)
    x2 = x_ref[...] + wt[:, 0:1] * y0 + wt[:, 1:2] * y1
    hn = _rms(x2, nw_ref[...]).astype(BF16)
    pg = jax.nn.sigmoid(_dot(hn, wg_ref[...]))
    x3 = x2 + pg * _dot(p_ref[...].astype(BF16), wp_ref[...])
    o_ref[...] = _rms(x3, fw_ref[...])


def _combine_ple(yt, x1, wts_t, p, ple_norm_w, w_gate, w_proj, final_w, tm=512):
    t = x1.shape[0]
    nb = t // tm
    row = lambda i: (i, 0)
    fix = lambda i: (0, 0)
    return pl.pallas_call(
        _combine_kernel,
        grid=(nb,),
        in_specs=[pl.BlockSpec((tm, ROW_TILES, LANES), lambda i: (i, 0, 0)),
                  pl.BlockSpec((tm, ROW_TILES, LANES), lambda i: (nb + i, 0, 0)),
                  pl.BlockSpec((tm, D_MODEL), row),
                  pl.BlockSpec((tm, SUBLANES), row),
                  pl.BlockSpec((tm, PLE_DIM), row),
                  pl.BlockSpec((1, D_MODEL), fix),
                  pl.BlockSpec((D_MODEL, D_MODEL), fix),
                  pl.BlockSpec((PLE_DIM, D_MODEL), fix),
                  pl.BlockSpec((1, D_MODEL), fix)],
        out_specs=pl.BlockSpec((tm, D_MODEL), row),
        out_shape=jax.ShapeDtypeStruct((t, D_MODEL), F32),
        compiler_params=_cp(("arbitrary",), 56),
        name="combine_ple",
    )(yt, yt, x1, wts_t, p, ple_norm_w, w_gate, w_proj, final_w)


def _tile_table(counts, n_tiles):
    tiles = (counts.astype(I32) + (EXPERT_TILE - 1)) // EXPERT_TILE
    ends = jnp.cumsum(tiles)
    n_act = ends[-1]
    idx = jnp.minimum(jnp.arange(n_tiles, dtype=I32), n_act - 1)
    tile_e = jnp.sum((idx[:, None] >= ends[None, :]).astype(I32), axis=1).astype(I32)
    run_end = ends[tile_e]
    next_e = jnp.where(run_end < n_act, tile_e[jnp.minimum(run_end, n_tiles - 1)], -1).astype(I32)
    new_run = jnp.concatenate([jnp.ones((1,), I32), (tile_e[1:] != tile_e[:-1]).astype(I32)])
    slot = ((jnp.cumsum(new_run) - 1) % 2).astype(I32)
    arange = jnp.arange(n_tiles, dtype=I32)
    last = arange == run_end - 1
    valid = jnp.where(last, counts.astype(I32)[tile_e] - (tiles[tile_e] - 1) * EXPERT_TILE, EXPERT_TILE)
    valid = jnp.where(arange < n_act, valid, 0).astype(I32)
    return tile_e, next_e, slot, valid, n_act.reshape(1).astype(I32)


def _block(x, p, norm_mix_w, w_in, b_merge, w_alpha_up, b_alpha_up, gla_norm_w, w_gla_out,
           conv_w, conv_b, w_conv_out, w_mix_out, norm_ffn_w, w_router_group, b_router_group,
           w_router_expert, b_router_expert, w_e_gate, w_e_up, w_e_down, ple_norm_w,
           w_ple_gate, w_ple_proj, final_norm_w):
    t = x.shape[0]
    n_tiles = (2 * t) // EXPERT_TILE + N_EXPERTS
    n_rows = n_tiles * EXPERT_TILE

    w_up = jnp.pad(w_alpha_up, ((0, LANES - GLA_GATE_RANK), (0, 0))).astype(BF16)
    w_up = w_up.reshape(LANES, GLA_HEADS, GLA_DK).transpose(1, 0, 2)
    b_up = b_alpha_up.reshape(GLA_HEADS, 1, GLA_DK)
    gnw = gla_norm_w.reshape(GLA_HEADS, 1, GLA_DV)
    wr_t = jnp.zeros((ROUTER_ROWS, D_MODEL), F32)
    wr_t = wr_t.at[0:N_GROUPS].set(w_router_group.T)
    wr_t = wr_t.at[EXPERT_ROW0:EXPERT_ROW0 + N_EXPERTS].set(w_router_expert.T)
    br = jnp.zeros((ROUTER_ROWS, 1), F32)
    br = br.at[0:N_GROUPS, 0].set(b_router_group)
    br = br.at[EXPERT_ROW0:EXPERT_ROW0 + N_EXPERTS, 0].set(b_router_expert)

    w_in_t = w_in.T
    h, a_low = _norm_in(x, norm_mix_w.reshape(1, D_MODEL), w_in_t)
    qkvg = _proj_qkvg(h, w_in_t)
    ob = _proj_conv(h, w_in_t, conv_w, conv_b.reshape(1, D_MODEL))
    gates = _proj_gates(h, w_in_t, b_merge.reshape(1, 2 * D_MODEL))
    oa = _gla(qkvg, a_low, w_up, b_up, gnw)
    mixed = _merge(oa, ob, w_gla_out, w_conv_out, gates)
    x1 = _mix(mixed, w_mix_out, x)

    hp, ids, wts = _route(x1, norm_ffn_w.reshape(1, D_MODEL), wr_t, br)
    dest, counts = _plan(ids)
    dest_flat = dest[1, 0:2].reshape(2 * t)
    tile_e, next_e, slot, valid, n_act = _tile_table(counts[:, 0], n_tiles)
    row_slot = _invert(dest_flat, n_rows)
    yt = _experts(tile_e, next_e, slot, valid, row_slot, n_act, hp, w_e_gate, w_e_up, w_e_down, n_tiles)
    return _combine_ple(yt, x1, wts.T, p,
                        ple_norm_w.reshape(1, D_MODEL), w_ple_gate.astype(BF16),
                        w_ple_proj.astype(BF16), final_norm_w.reshape(1, D_MODEL))


def kernel(x, p, norm_mix_w, w_in, b_merge, w_alpha_up, b_alpha_up, gla_norm_w, w_gla_out, conv_w, conv_b, w_conv_out, w_mix_out, norm_ffn_w, w_router_group, b_router_group, w_router_expert, b_router_expert, w_e_gate, w_e_up, w_e_down, ple_norm_w, w_ple_gate, w_ple_proj, final_norm_w):
    depth, batch = p.shape[0], x.shape[0]
    assert depth == 1 and batch == 1, "kernel is specialised to one layer and one sequence"
    out = _block(x[0], p[0, 0], norm_mix_w[0], w_in[0], b_merge[0], w_alpha_up[0], b_alpha_up[0],
                 gla_norm_w[0], w_gla_out[0], conv_w[0], conv_b[0], w_conv_out[0], w_mix_out[0],
                 norm_ffn_w[0], w_router_group[0], b_router_group[0], w_router_expert[0],
                 b_router_expert[0], w_e_gate[0], w_e_up[0], w_e_down[0], ple_norm_w[0],
                 w_ple_gate[0], w_ple_proj[0], final_norm_w)
    return out[None]
```

```python
import functools

import jax
import jax.numpy as jnp
from jax import lax
from jax.experimental import pallas as pl
from jax.experimental.pallas import tpu as pltpu

F32 = jnp.float32
BF16 = jnp.bfloat16
I32 = jnp.int32

D_MODEL = 2048
PLE_DIM = 256
EPS = 1e-6
LOG2_E = 1.4426950408889634
GLA_HEADS = 4
GLA_DK = 256
GLA_DV = 512
GLA_KEY = GLA_HEADS * GLA_DK
GLA_VAL = GLA_HEADS * GLA_DV
GLA_GATE_RANK = 16
GLA_GATE_NORM = 16.0
GLA_CHUNK = 64
CONV_K = 3
N_GROUPS = 4
EXPERTS_PER_GROUP = 8
N_EXPERTS = N_GROUPS * EXPERTS_PER_GROUP
D_FF = 512

QKVG_COLS = 2 * GLA_KEY + 2 * GLA_VAL
CONV_COL0 = QKVG_COLS + GLA_GATE_RANK

LANES = 128
SUBLANES = 8
MXU_COLS = 256
ROW_CHUNK = 256
ROW_TILES = D_MODEL // LANES

EXPERT_TILE = 256
DMA_UNROLL = 8
ROUTER_ROWS = 64
EXPERT_ROW0 = 8

MIB = 1024 * 1024


def _cp(sem, vmem_mib, **kw):
    return pltpu.CompilerParams(dimension_semantics=sem, vmem_limit_bytes=int(vmem_mib * MIB), **kw)


def _rms(x, w):
    return x * lax.rsqrt(jnp.mean(x * x, axis=-1, keepdims=True) + EPS) * w


def _dot(a, b):
    return jnp.dot(a, b, preferred_element_type=F32)


def _dot_nt(a, b):
    return lax.dot_general(a, b, (((1,), (1,)), ((), ())), preferred_element_type=F32)


def _dot_tn(a, b):
    return lax.dot_general(a, b, (((0,), (0,)), ((), ())), preferred_element_type=F32)


def _split_bf16(x):
    hi = x.astype(BF16)
    lo = (x - hi.astype(F32)).astype(BF16)
    return hi, lo


def _to_token_major(val):
    return val.reshape(val.shape[0], ROW_TILES, LANES)


def _from_token_major(val):
    return val.reshape(val.shape[0], D_MODEL)


def _norm_in_kernel(x_ref, w_ref, wal_ref, h_ref, al_ref, walbf_ref):
    @pl.when(pl.program_id(0) == 0)
    def _():
        walbf_ref[...] = wal_ref[...].astype(BF16)

    h = _rms(x_ref[...], w_ref[...]).astype(BF16)
    h_ref[...] = h
    al_ref[...] = _dot_nt(h, walbf_ref[...])


def _norm_in(x, w, w_in_t, tm=512):
    t = x.shape[0]
    return pl.pallas_call(
        _norm_in_kernel,
        grid=(t // tm,),
        in_specs=[pl.BlockSpec((tm, D_MODEL), lambda i: (i, 0)),
                  pl.BlockSpec((1, D_MODEL), lambda i: (0, 0)),
                  pl.BlockSpec((LANES, D_MODEL), lambda i: (QKVG_COLS // LANES, 0))],
        out_specs=[pl.BlockSpec((tm, D_MODEL), lambda i: (i, 0)),
                   pl.BlockSpec((tm, LANES), lambda i: (i, 0))],
        out_shape=[jax.ShapeDtypeStruct((t, D_MODEL), BF16),
                   jax.ShapeDtypeStruct((t, LANES), F32)],
        scratch_shapes=[pltpu.VMEM((LANES, D_MODEL), BF16)],
        compiler_params=_cp(("arbitrary",), 32),
        name="norm_in",
    )(x, w, w_in_t)


def _w_rows_spec(row0, tn):
    assert row0 % SUBLANES == 0 and tn % SUBLANES == 0
    return pl.BlockSpec((pl.Element(tn), pl.Element(D_MODEL)),
                        lambda n, m: (pl.multiple_of(row0 + n * tn, SUBLANES), 0))


def _proj_qkvg_kernel(h_ref, w_ref, o_ref, wbf_ref):
    @pl.when(pl.program_id(1) == 0)
    def _():
        wbf_ref[...] = w_ref[...].astype(BF16)

    o_ref[...] = _dot_nt(h_ref[...], wbf_ref[...]).astype(o_ref.dtype)


def _proj_qkvg(h, w_in_t, tm=2048, tn=1024):
    t = h.shape[0]
    return pl.pallas_call(
        _proj_qkvg_kernel,
        grid=(QKVG_COLS // tn, t // tm),
        in_specs=[pl.BlockSpec((tm, D_MODEL), lambda n, m: (m, 0)),
                  pl.BlockSpec((tn, D_MODEL), lambda n, m: (n, 0))],
        out_specs=pl.BlockSpec((tm, tn), lambda n, m: (m, n)),
        out_shape=jax.ShapeDtypeStruct((t, QKVG_COLS), BF16),
        scratch_shapes=[pltpu.VMEM((tn, D_MODEL), BF16)],
        compiler_params=_cp(("arbitrary", "arbitrary"), 56),
        name="proj_qkvg",
    )(h, w_in_t)


def _proj_conv_kernel(h_ref, wb_ref, wc_ref, wx_ref, cw_ref, cb_ref, o_ref,
                      wbbf_ref, wcbf_ref, wxbf_ref, prev_ref):
    m = pl.program_id(1)

    @pl.when(m == 0)
    def _():
        prev_ref[...] = jnp.zeros_like(prev_ref)
        wbbf_ref[...] = wb_ref[...].astype(BF16)
        wcbf_ref[...] = wc_ref[...].astype(BF16)
        wxbf_ref[...] = wx_ref[...].astype(BF16)

    h = h_ref[...]
    tm = h.shape[0]
    row = lax.broadcasted_iota(I32, (tm, MXU_COLS), 0)
    for c in range(0, o_ref.shape[1], MXU_COLS):
        cols = slice(c, c + MXU_COLS)
        b = _dot_nt(h, wbbf_ref[cols, :])
        s = _dot_nt(h, wcbf_ref[cols, :]) * _dot_nt(h, wxbf_ref[cols, :])
        prev = prev_ref[:, cols]
        p1 = prev[SUBLANES - 1:SUBLANES, :]
        p2 = prev[SUBLANES - 2:SUBLANES - 1, :]
        s1 = jnp.where(row == 0, p1, pltpu.roll(s, 1, 0))
        s2 = jnp.where(row == 0, p2, jnp.where(row == 1, p1, pltpu.roll(s, 2, 0)))
        cw = cw_ref[:, cols]
        u = cw[2:3, :] * s + cw[1:2, :] * s1 + cw[0:1, :] * s2 + cb_ref[:, cols]
        o_ref[:, cols] = (b * u).astype(o_ref.dtype)
        prev_ref[:, cols] = s[tm - SUBLANES:, :]


def _proj_conv(h, w_in_t, conv_w, conv_b, tm=1024, tn=512):
    t = h.shape[0]
    return pl.pallas_call(
        _proj_conv_kernel,
        grid=(D_MODEL // tn, t // tm),
        in_specs=[pl.BlockSpec((tm, D_MODEL), lambda n, m: (m, 0))] +
                 [_w_rows_spec(CONV_COL0 + seg * D_MODEL, tn) for seg in range(3)] +
                 [pl.BlockSpec((CONV_K, tn), lambda n, m: (0, n)),
                  pl.BlockSpec((1, tn), lambda n, m: (0, n))],
        out_specs=pl.BlockSpec((tm, tn), lambda n, m: (m, n)),
        out_shape=jax.ShapeDtypeStruct((t, D_MODEL), BF16),
        scratch_shapes=[pltpu.VMEM((tn, D_MODEL), BF16), pltpu.VMEM((tn, D_MODEL), BF16),
                        pltpu.VMEM((tn, D_MODEL), BF16), pltpu.VMEM((SUBLANES, tn), F32)],
        compiler_params=_cp(("arbitrary", "arbitrary"), 48),
        name="proj_conv",
    )(h, w_in_t, w_in_t, w_in_t, conv_w, conv_b)


def _proj_gates_kernel(h_ref, w_ref, b_ref, o_ref, wbf_ref):
    @pl.when(pl.program_id(1) == 0)
    def _():
        wbf_ref[...] = w_ref[...].astype(BF16)

    for c in range(0, o_ref.shape[1], MXU_COLS):
        cols = slice(c, c + MXU_COLS)
        for r in range(0, o_ref.shape[0], ROW_CHUNK):
            rows = slice(r, r + ROW_CHUNK)
            z = _dot_nt(h_ref[rows, :], wbf_ref[cols, :]) + b_ref[:, cols]
            o_ref[rows, cols] = jax.nn.sigmoid(z).astype(o_ref.dtype)


def _proj_gates(h, w_in_t, b_merge, tm=2048, tn=1024):
    t = h.shape[0]
    return pl.pallas_call(
        _proj_gates_kernel,
        grid=(2 * D_MODEL // tn, t // tm),
        in_specs=[pl.BlockSpec((tm, D_MODEL), lambda n, m: (m, 0)),
                  _w_rows_spec(CONV_COL0 + 3 * D_MODEL, tn),
                  pl.BlockSpec((1, tn), lambda n, m: (0, n))],
        out_specs=pl.BlockSpec((tm, tn), lambda n, m: (m, n)),
        out_shape=jax.ShapeDtypeStruct((t, 2 * D_MODEL), BF16),
        scratch_shapes=[pltpu.VMEM((tn, D_MODEL), BF16)],
        compiler_params=_cp(("arbitrary", "arbitrary"), 56),
        name="proj_gates",
    )(h, w_in_t, b_merge)


def _gla_kernel(q_ref, k_ref, v_ref, g_ref, al_ref, wup_ref, bup_ref, nw_ref, o_ref,
                st_ref, b_ref, bl_ref, qd_ref, ki_ref, kd_ref, oi_ref, u_ref, *, n_chunks):
    c_len = GLA_CHUNK

    @pl.when(pl.program_id(1) == 0)
    def _():
        st_ref[...] = jnp.zeros_like(st_ref)

    row = lax.broadcasted_iota(I32, (c_len, c_len), 0)
    col = lax.broadcasted_iota(I32, (c_len, c_len), 1)
    causal = col <= row
    tril = jnp.where(causal, 1.0, 0.0).astype(BF16)

    def decays(h):
        kc = slice(h * GLA_DK, (h + 1) * GLA_DK)
        z = _dot(al_ref[...].astype(BF16), wup_ref[h]) + bup_ref[h]
        la = (jnp.minimum(z, 0.0) - jnp.log1p(jnp.exp(-jnp.abs(z)))) * (LOG2_E / GLA_GATE_NORM)
        la_hi, la_lo = _split_bf16(la)
        for c in range(n_chunks):
            r0 = c * c_len
            b = _dot(tril, la_hi[r0:r0 + c_len]) + _dot(tril, la_lo[r0:r0 + c_len])
            b_ref[pl.ds(r0, c_len), kc] = b
            bl_ref[pl.ds(r0, c_len), kc] = jnp.broadcast_to(b[c_len - 1:c_len, :], b.shape)
        b = b_ref[:, kc]
        q = q_ref[:, kc].astype(F32)
        k = k_ref[:, kc].astype(F32)
        qd_ref[:, kc] = (q * jnp.exp2(b)).astype(BF16)
        ki_ref[:, kc] = (k * jnp.exp2(-b)).astype(BF16)
        kd_ref[:, kc] = (k * jnp.exp2(bl_ref[:, kc] - b)).astype(BF16)

    def local_products(h):
        kc = slice(h * GLA_DK, (h + 1) * GLA_DK)
        vc = slice(h * GLA_DV, (h + 1) * GLA_DV)
        for c in range(n_chunks):
            sl = pl.ds(c * c_len, c_len)
            v = v_ref[sl, vc]
            att = jnp.where(causal, _dot_nt(qd_ref[sl, kc], ki_ref[sl, kc]), 0.0).astype(BF16)
            oi_ref[sl, vc] = _dot(att, v)
            u_ref[h, c] = _dot_tn(v, kd_ref[sl, kc])

    def recurrence(h):
        kc = slice(h * GLA_DK, (h + 1) * GLA_DK)
        vc = slice(h * GLA_DV, (h + 1) * GLA_DV)
        for c in range(n_chunks):
            sl = pl.ds(c * c_len, c_len)
            st = st_ref[h]
            oi_ref[sl, vc] = oi_ref[sl, vc] + _dot_nt(qd_ref[sl, kc], st.astype(BF16))
            st_ref[h] = st * jnp.exp2(bl_ref[pl.ds(c * c_len, 1), kc]) + u_ref[h, c]
        o = oi_ref[:, vc]
        o = o * lax.rsqrt(jnp.mean(o * o, axis=-1, keepdims=True) + EPS * GLA_DK) * nw_ref[h]
        g = g_ref[:, vc].astype(F32)
        o_ref[:, vc] = (o * (g * jax.nn.sigmoid(g))).astype(o_ref.dtype)

    decays(0)
    local_products(0)
    decays(1)
    recurrence(0)
    local_products(1)
    recurrence(1)


def _gla(qkvg, a_low, w_up, b_up, norm_w, tb=1024):
    t = qkvg.shape[0]
    hp = 2
    dk, dv = hp * GLA_DK, hp * GLA_DV
    kq = GLA_KEY // dk
    kv = 2 * GLA_KEY // dv
    kg = kv + GLA_VAL // dv
    n_chunks = tb // GLA_CHUNK
    kern = functools.partial(_gla_kernel, n_chunks=n_chunks)
    return pl.pallas_call(
        kern,
        grid=(GLA_HEADS // hp, t // tb),
        in_specs=[pl.BlockSpec((tb, dk), lambda h, i: (i, h)),
                  pl.BlockSpec((tb, dk), lambda h, i: (i, kq + h)),
                  pl.BlockSpec((tb, dv), lambda h, i: (i, kv + h)),
                  pl.BlockSpec((tb, dv), lambda h, i: (i, kg + h)),
                  pl.BlockSpec((tb, LANES), lambda h, i: (i, 0)),
                  pl.BlockSpec((hp, LANES, GLA_DK), lambda h, i: (h, 0, 0)),
                  pl.BlockSpec((hp, 1, GLA_DK), lambda h, i: (h, 0, 0)),
                  pl.BlockSpec((hp, 1, GLA_DV), lambda h, i: (h, 0, 0))],
        out_specs=pl.BlockSpec((tb, dv), lambda h, i: (i, h)),
        out_shape=jax.ShapeDtypeStruct((t, GLA_VAL), BF16),
        scratch_shapes=[pltpu.VMEM((hp, GLA_DV, GLA_DK), F32),
                        pltpu.VMEM((tb, dk), F32), pltpu.VMEM((tb, dk), F32),
                        pltpu.VMEM((tb, dk), BF16), pltpu.VMEM((tb, dk), BF16),
                        pltpu.VMEM((tb, dk), BF16),
                        pltpu.VMEM((tb, dv), F32),
                        pltpu.VMEM((hp, n_chunks, GLA_DV, GLA_DK), F32)],
        compiler_params=_cp(("arbitrary", "arbitrary"), 56),
        name="gla",
    )(qkvg, qkvg, qkvg, qkvg, a_low, w_up, b_up, norm_w)


def _merge_kernel(oa_ref, ob_ref, wa_ref, wb_ref, g0_ref, g1_ref, o_ref, wabf_ref, wbbf_ref):
    @pl.when(pl.program_id(1) == 0)
    def _():
        wabf_ref[...] = wa_ref[...].astype(BF16)
        wbbf_ref[...] = wb_ref[...].astype(BF16)

    oa = oa_ref[...]
    ob = ob_ref[...]
    for c in range(0, o_ref.shape[1], MXU_COLS):
        cols = slice(c, c + MXU_COLS)
        a = _dot(oa, wabf_ref[:, cols])
        b = _dot(ob, wbbf_ref[:, cols])
        o_ref[:, cols] = (g0_ref[:, cols].astype(F32) * a + g1_ref[:, cols].astype(F32) * b).astype(o_ref.dtype)


def _merge(oa, ob, w_a, w_b, gates, tm=1024, tn=512):
    t = oa.shape[0]
    nb = D_MODEL // tn
    return pl.pallas_call(
        _merge_kernel,
        grid=(nb, t // tm),
        in_specs=[pl.BlockSpec((tm, D_MODEL), lambda n, m: (m, 0)),
                  pl.BlockSpec((tm, D_MODEL), lambda n, m: (m, 0)),
                  pl.BlockSpec((D_MODEL, tn), lambda n, m: (0, n)),
                  pl.BlockSpec((D_MODEL, tn), lambda n, m: (0, n)),
                  pl.BlockSpec((tm, tn), lambda n, m: (m, n)),
                  pl.BlockSpec((tm, tn), lambda n, m: (m, nb + n))],
        out_specs=pl.BlockSpec((tm, tn), lambda n, m: (m, n)),
        out_shape=jax.ShapeDtypeStruct((t, D_MODEL), BF16),
        scratch_shapes=[pltpu.VMEM((D_MODEL, tn), BF16), pltpu.VMEM((D_MODEL, tn), BF16)],
        compiler_params=_cp(("arbitrary", "arbitrary"), 56),
        name="merge",
    )(oa, ob, w_a, w_b, gates, gates)


def _mix_kernel(a_ref, w_ref, x_ref, o_ref, wbf_ref):
    @pl.when(pl.program_id(1) == 0)
    def _():
        wbf_ref[...] = w_ref[...].astype(BF16)

    a = a_ref[...]
    for c in range(0, o_ref.shape[1], MXU_COLS):
        cols = slice(c, c + MXU_COLS)
        o_ref[:, cols] = x_ref[:, cols] + _dot(a, wbf_ref[:, cols])


def _mix(mixed, w, x, tm=1024, tn=1024):
    t = x.shape[0]
    return pl.pallas_call(
        _mix_kernel,
        grid=(D_MODEL // tn, t // tm),
        in_specs=[pl.BlockSpec((tm, D_MODEL), lambda n, m: (m, 0)),
                  pl.BlockSpec((D_MODEL, tn), lambda n, m: (0, n)),
                  pl.BlockSpec((tm, tn), lambda n, m: (m, n))],
        out_specs=pl.BlockSpec((tm, tn), lambda n, m: (m, n)),
        out_shape=jax.ShapeDtypeStruct((t, D_MODEL), F32),
        scratch_shapes=[pltpu.VMEM((D_MODEL, tn), BF16)],
        compiler_params=_cp(("arbitrary", "arbitrary"), 52),
        name="mix",
    )(mixed, w, x)


def _route_kernel(x_ref, nw_ref, wr_ref, br_ref, hp_ref, ids_ref, wts_ref):
    h = _rms(x_ref[...], nw_ref[...])
    hp_ref[...] = _to_token_major(h.astype(BF16))
    h_hi, h_lo = _split_bf16(h)
    w_hi, w_lo = _split_bf16(wr_ref[...])
    logits = _dot_nt(w_hi, h_hi) + _dot_nt(w_hi, h_lo) + _dot_nt(w_lo, h_hi) + br_ref[...]
    tm = logits.shape[1]

    best = logits[0:1, :]
    gidx = jnp.zeros((1, tm), I32)
    for i in range(1, N_GROUPS):
        li = logits[i:i + 1, :]
        take = li > best
        best = jnp.where(take, li, best)
        gidx = jnp.where(take, i, gidx)
    gsum = jnp.zeros((1, tm), F32)
    for i in range(N_GROUPS):
        gsum = gsum + jnp.exp(logits[i:i + 1, :] - best)
    g_p = 1.0 / gsum

    sel = logits[EXPERT_ROW0:EXPERT_ROW0 + EXPERTS_PER_GROUP, :]
    for g in range(1, N_GROUPS):
        r0 = EXPERT_ROW0 + g * EXPERTS_PER_GROUP
        sel = jnp.where(gidx == g, logits[r0:r0 + EXPERTS_PER_GROUP, :], sel)
    eio = lax.broadcasted_iota(I32, sel.shape, 0)
    m1 = jnp.max(sel, axis=0, keepdims=True)
    i1 = jnp.min(jnp.where(sel == m1, eio, EXPERTS_PER_GROUP), axis=0, keepdims=True)
    rest = jnp.where(eio == i1, -jnp.inf, sel)
    m2 = jnp.max(rest, axis=0, keepdims=True)
    i2 = jnp.min(jnp.where(rest == m2, eio, EXPERTS_PER_GROUP), axis=0, keepdims=True)
    p2 = jnp.exp(m2 - m1)
    w1 = g_p / (1.0 + p2)
    w2 = g_p * p2 / (1.0 + p2)
    e1 = gidx * EXPERTS_PER_GROUP + i1
    e2 = gidx * EXPERTS_PER_GROUP + i2
    rio = lax.broadcasted_iota(I32, (SUBLANES, tm), 0)
    ids_ref[...] = jnp.where(rio == 0, e1, jnp.where(rio == 1, e2, 0))
    wts_ref[...] = jnp.where(rio == 0, w1, jnp.where(rio == 1, w2, 0.0))


def _route(x1, norm_w, wr_t, br, tm=512):
    t = x1.shape[0]
    return pl.pallas_call(
        _route_kernel,
        grid=(t // tm,),
        in_specs=[pl.BlockSpec((tm, D_MODEL), lambda i: (i, 0)),
                  pl.BlockSpec((1, D_MODEL), lambda i: (0, 0)),
                  pl.BlockSpec((ROUTER_ROWS, D_MODEL), lambda i: (0, 0)),
                  pl.BlockSpec((ROUTER_ROWS, 1), lambda i: (0, 0))],
        out_specs=[pl.BlockSpec((tm, ROW_TILES, LANES), lambda i: (i, 0, 0)),
                   pl.BlockSpec((SUBLANES, tm), lambda i: (0, i)),
                   pl.BlockSpec((SUBLANES, tm), lambda i: (0, i))],
        out_shape=[jax.ShapeDtypeStruct((t, ROW_TILES, LANES), BF16),
                   jax.ShapeDtypeStruct((SUBLANES, t), I32),
                   jax.ShapeDtypeStruct((SUBLANES, t), F32)],
        compiler_params=_cp(("arbitrary",), 32),
        name="route",
    )(x1, norm_w, wr_t, br)


def _plan_kernel(ids_ref, dest_ref, cnt_ref, base_ref):
    phase = pl.program_id(0)
    step = pl.program_id(1)
    tm = ids_ref.shape[1]
    eio = lax.broadcasted_iota(I32, (N_EXPERTS, tm), 0)
    ids = ids_ref[...]
    oh = [jnp.where(eio == ids[k:k + 1, :], 1.0, 0.0) for k in range(2)]

    @pl.when((phase == 0) & (step == 0))
    def _():
        base_ref[...] = jnp.zeros_like(base_ref)

    @pl.when(phase == 0)
    def _():
        cnt = jnp.sum(oh[0] + oh[1], axis=1, keepdims=True)
        base_ref[...] = base_ref[...] + cnt
        dest_ref[0] = jnp.zeros(dest_ref.shape[1:], I32)
        cnt_ref[...] = base_ref[...]

    @pl.when((phase == 1) & (step == 0))
    def _():
        tiles = jnp.floor((base_ref[...] + (EXPERT_TILE - 1)) * (1.0 / EXPERT_TILE))
        r = lax.broadcasted_iota(I32, (N_EXPERTS, N_EXPERTS), 0)
        c = lax.broadcasted_iota(I32, (N_EXPERTS, N_EXPERTS), 1)
        lower = jnp.where(c < r, 1.0, 0.0).astype(BF16)
        base_ref[...] = _dot(lower, tiles.astype(BF16)) * float(EXPERT_TILE)

    @pl.when(phase == 1)
    def _():
        r = lax.broadcasted_iota(I32, (tm, tm), 0)
        c = lax.broadcasted_iota(I32, (tm, tm), 1)
        upper = jnp.where(r <= c, 1.0, 0.0).astype(BF16)
        base = base_ref[...][:, 0:1]
        rows = []
        for k in range(2):
            cum = _dot(oh[k].astype(BF16), upper)
            rows.append(jnp.sum(oh[k] * (cum - 1.0 + base), axis=0, keepdims=True))
            base = base + cum[:, tm - 1:tm]
        base_ref[...] = jnp.broadcast_to(base, base_ref.shape)
        rio = lax.broadcasted_iota(I32, (SUBLANES, tm), 0)
        d0 = rows[0].astype(I32)
        d1 = rows[1].astype(I32)
        dest_ref[0] = jnp.where(rio == 0, d0, jnp.where(rio == 1, d1, 0))


def _plan(ids, tm=512):
    t = ids.shape[1]
    return pl.pallas_call(
        _plan_kernel,
        grid=(2, t // tm),
        in_specs=[pl.BlockSpec((SUBLANES, tm), lambda p, i: (0, i))],
        out_specs=[pl.BlockSpec((1, SUBLANES, tm), lambda p, i: (p, 0, i)),
                   pl.BlockSpec((N_EXPERTS, LANES), lambda p, i: (0, 0))],
        out_shape=[jax.ShapeDtypeStruct((2, SUBLANES, t), I32),
                   jax.ShapeDtypeStruct((N_EXPERTS, LANES), F32)],
        scratch_shapes=[pltpu.VMEM((N_EXPERTS, LANES), F32)],
        compiler_params=_cp(("arbitrary", "arbitrary"), 32),
        name="plan",
    )(ids)


def _invert_kernel(dest_ref, slot_ref, fill_ref, sem, *, n_slots):
    fill_ref[...] = jnp.full(fill_ref.shape, n_slots, I32)
    fill = pltpu.make_async_copy(fill_ref, slot_ref, sem)
    fill.start()
    fill.wait()

    def scatter(j, c):
        for u in range(DMA_UNROLL):
            q = j * DMA_UNROLL + u
            slot_ref[dest_ref[q]] = q
        return c

    lax.fori_loop(0, n_slots // DMA_UNROLL, scatter, 0)


def _invert(dest_flat, n_rows):
    n_slots = dest_flat.shape[0]
    assert n_slots % DMA_UNROLL == 0
    return pl.pallas_call(
        functools.partial(_invert_kernel, n_slots=n_slots),
        grid_spec=pltpu.PrefetchScalarGridSpec(
            num_scalar_prefetch=1,
            grid=(1,),
            in_specs=[],
            out_specs=pl.BlockSpec(memory_space=pltpu.SMEM),
            scratch_shapes=[pltpu.VMEM((n_rows,), I32), pltpu.SemaphoreType.DMA(())]),
        out_shape=jax.ShapeDtypeStruct((n_rows,), I32),
        compiler_params=_cp(("arbitrary",), 16),
        name="invert",
    )(dest_flat)


def _experts_kernel(te_ref, nx_ref, sl_ref, vr_ref, rs_ref, na_ref, hp_hbm, wg_hbm, wu_hbm, wd_hbm, yt_hbm,
                    wgf_ref, wuf_ref, wdf_ref, wgbf_ref, wubf_ref, wdbf_ref, xbuf_ref, ybuf_ref,
                    sems, xsems, ysems, *, t_total):
    i = pl.program_id(0)
    n_act = na_ref[0]
    active = i < n_act
    changed = (i == 0) | (te_ref[i] != te_ref[jnp.maximum(i - 1, 0)])

    def weight_copies(e, slot):
        return (pltpu.make_async_copy(wg_hbm.at[e], wgf_ref.at[slot], sems.at[slot, 0]),
                pltpu.make_async_copy(wu_hbm.at[e], wuf_ref.at[slot], sems.at[slot, 1]),
                pltpu.make_async_copy(wd_hbm.at[e], wdf_ref.at[slot], sems.at[slot, 2]))

    def in_copy(tile, r):
        tok = rs_ref[tile * EXPERT_TILE + r] & (t_total - 1)
        return pltpu.make_async_copy(hp_hbm.at[tok], xbuf_ref.at[tile % 2, r], xsems.at[tile % 2])

    def out_copy(tile, r):
        dst = yt_hbm.at[rs_ref[tile * EXPERT_TILE + r]]
        return pltpu.make_async_copy(ybuf_ref.at[tile % 2, r], dst, ysems.at[tile % 2])

    def for_rows(tile, make_copy, fn):
        n = vr_ref[tile]
        groups = lax.shift_right_logical(n, DMA_UNROLL.bit_length() - 1)

        def body(j, c):
            for u in range(DMA_UNROLL):
                fn(make_copy(tile, j * DMA_UNROLL + u))
            return c

        def tail(r, c):
            fn(make_copy(tile, r))
            return c

        lax.fori_loop(0, groups, body, 0)
        lax.fori_loop(groups * DMA_UNROLL, n, tail, 0)

    def wait_rows(tile, make_copy, whole_tile_copy):
        full = vr_ref[tile] == EXPERT_TILE

        @pl.when(full)
        def _():
            whole_tile_copy.wait()

        @pl.when(jnp.logical_not(full))
        def _():
            for_rows(tile, make_copy, lambda cp: cp.wait())

    def in_tile(tile):
        return pltpu.make_async_copy(hp_hbm.at[pl.ds(0, EXPERT_TILE)], xbuf_ref.at[tile % 2], xsems.at[tile % 2])

    def out_tile(tile):
        return pltpu.make_async_copy(ybuf_ref.at[tile % 2], yt_hbm.at[pl.ds(0, EXPERT_TILE)], ysems.at[tile % 2])

    @pl.when(i == 0)
    def _():
        for cp in weight_copies(te_ref[0], 0):
            cp.start(priority=1)

        def first_tile(j, c):
            for u in range(DMA_UNROLL):
                in_copy(i, j * DMA_UNROLL + u).start()
            return c
        lax.fori_loop(0, EXPERT_TILE // DMA_UNROLL, first_tile, 0)

    last = pl.num_programs(0) - 1
    nxt_active = i + 1 < n_act
    prev_active = (i >= 1) & (i - 1 < n_act)
    prev_full = prev_active & (vr_ref[jnp.maximum(i - 1, 0)] == EXPERT_TILE)

    @pl.when((i >= 2) & (i - 2 < n_act))
    def _():
        wait_rows(i - 2, out_copy, out_tile(i - 2))

    @pl.when(prev_active & jnp.logical_not(active & prev_full))
    def _():
        for_rows(i - 1, out_copy, lambda cp: cp.start())

    @pl.when(active & changed)
    def _():
        slot = sl_ref[i]
        nxt = nx_ref[i]

        @pl.when(nxt >= 0)
        def _():
            for cp in weight_copies(nxt, 1 - slot):
                cp.start(priority=1)

        for cp in weight_copies(te_ref[i], slot):
            cp.wait()
        wgbf_ref[...] = wgf_ref[slot].astype(BF16)
        wubf_ref[...] = wuf_ref[slot].astype(BF16)
        wdbf_ref[...] = wdf_ref[slot].astype(BF16)

    def compute(fetch_next, send_prev):
        def row_copies(g):
            for r in range(g * (EXPERT_TILE // 4), (g + 1) * (EXPERT_TILE // 4)):
                if fetch_next:
                    in_copy(i + 1, r).start()
                if send_prev:
                    out_copy(i - 1, r).start()

        in_tile(i).wait()
        x = _from_token_major(xbuf_ref[i % 2])
        row_copies(0)
        hg = _dot(x, wgbf_ref[...])
        row_copies(1)
        hu = _dot(x, wubf_ref[...])
        row_copies(2)
        act = (hg * jax.nn.sigmoid(hg) * hu).astype(BF16)
        y = _dot(act, wdbf_ref[...])
        row_copies(3)
        ybuf_ref[i % 2] = _to_token_major(y.astype(BF16))

    for fetch_next in (False, True):
        for send_prev in (False, True):
            @pl.when(active & (nxt_active == fetch_next) & (prev_full == send_prev))
            def _(fetch_next=fetch_next, send_prev=send_prev):
                compute(fetch_next, send_prev)

    @pl.when(i == last)
    def _():
        @pl.when(active)
        def _():
            for_rows(i, out_copy, lambda cp: cp.start())
            wait_rows(i, out_copy, out_tile(i))

        @pl.when(prev_active)
        def _():
            wait_rows(i - 1, out_copy, out_tile(i - 1))


def _experts(tile_e, next_e, slot, valid, row_slot, n_act, hp, w_gate, w_up, w_down, n_tiles):
    t = hp.shape[0]
    assert t & (t - 1) == 0, "the row map packs slot * T + token with T a power of two"
    any_spec = pl.BlockSpec(memory_space=pl.ANY)
    tile_buf = pltpu.VMEM((2, EXPERT_TILE, ROW_TILES, LANES), BF16)
    return pl.pallas_call(
        functools.partial(_experts_kernel, t_total=t),
        grid_spec=pltpu.PrefetchScalarGridSpec(
            num_scalar_prefetch=6,
            grid=(n_tiles,),
            in_specs=[any_spec, any_spec, any_spec, any_spec],
            out_specs=any_spec,
            scratch_shapes=[pltpu.VMEM((2, D_MODEL, D_FF), F32),
                            pltpu.VMEM((2, D_MODEL, D_FF), F32),
                            pltpu.VMEM((2, D_FF, D_MODEL), F32),
                            pltpu.VMEM((D_MODEL, D_FF), BF16),
                            pltpu.VMEM((D_MODEL, D_FF), BF16),
                            pltpu.VMEM((D_FF, D_MODEL), BF16),
                            tile_buf, tile_buf,
                            pltpu.SemaphoreType.DMA((2, 3)),
                            pltpu.SemaphoreType.DMA((2,)),
                            pltpu.SemaphoreType.DMA((2,))]),
        out_shape=jax.ShapeDtypeStruct((2 * t, ROW_TILES, LANES), BF16),
        compiler_params=_cp(("arbitrary",), 48, has_side_effects=True),
        name="experts",
    )(tile_e, next_e, slot, valid, row_slot, n_act, hp, w_gate, w_up, w_down)


def _combine_kernel(y0_ref, y1_ref, x_ref, wt_ref, p_ref, nw_ref, wg_ref, wp_ref, fw_ref, o_ref):
    wt = wt_ref[...]
    y0 = _from_token_major(y0_ref[...]).astype(F32)
    y1 = _from_token_major(y1_ref[...]).astype(F32)
    x2 = x_ref[...] + wt[:, 0:1] * y0 + wt[:, 1:2] * y1
    hn = _rms(x2, nw_ref[...]).astype(BF16)
    pg = jax.nn.sigmoid(_dot(hn, wg_ref[...]))
    x3 = x2 + pg * _dot(p_ref[...].astype(BF16), wp_ref[...])
    o_ref[...] = _rms(x3, fw_ref[...])


def _combine_ple(yt, x1, wts_t, p, ple_norm_w, w_gate, w_proj, final_w, tm=512):
    t = x1.shape[0]
    nb = t // tm
    row = lambda i: (i, 0)
    fix = lambda i: (0, 0)
    return pl.pallas_call(
        _combine_kernel,
        grid=(nb,),
        in_specs=[pl.BlockSpec((tm, ROW_TILES, LANES), lambda i: (i, 0, 0)),
                  pl.BlockSpec((tm, ROW_TILES, LANES), lambda i: (nb + i, 0, 0)),
                  pl.BlockSpec((tm, D_MODEL), row),
                  pl.BlockSpec((tm, SUBLANES), row),
                  pl.BlockSpec((tm, PLE_DIM), row),
                  pl.BlockSpec((1, D_MODEL), fix),
                  pl.BlockSpec((D_MODEL, D_MODEL), fix),
                  pl.BlockSpec((PLE_DIM, D_MODEL), fix),
                  pl.BlockSpec((1, D_MODEL), fix)],
        out_specs=pl.BlockSpec((tm, D_MODEL), row),
        out_shape=jax.ShapeDtypeStruct((t, D_MODEL), F32),
        compiler_params=_cp(("arbitrary",), 56),
        name="combine_ple",
    )(yt, yt, x1, wts_t, p, ple_norm_w, w_gate, w_proj, final_w)


def _tile_table(counts, n_tiles):
    tiles = (counts.astype(I32) + (EXPERT_TILE - 1)) // EXPERT_TILE
    ends = jnp.cumsum(tiles)
    n_act = ends[-1]
    idx = jnp.minimum(jnp.arange(n_tiles, dtype=I32), n_act - 1)
    tile_e = jnp.sum((idx[:, None] >= ends[None, :]).astype(I32), axis=1).astype(I32)
    run_end = ends[tile_e]
    next_e = jnp.where(run_end < n_act, tile_e[jnp.minimum(run_end, n_tiles - 1)], -1).astype(I32)
    new_run = jnp.concatenate([jnp.ones((1,), I32), (tile_e[1:] != tile_e[:-1]).astype(I32)])
    slot = ((jnp.cumsum(new_run) - 1) % 2).astype(I32)
    arange = jnp.arange(n_tiles, dtype=I32)
    last = arange == run_end - 1
    valid = jnp.where(last, counts.astype(I32)[tile_e] - (tiles[tile_e] - 1) * EXPERT_TILE, EXPERT_TILE)
    valid = jnp.where(arange < n_act, valid, 0).astype(I32)
    return tile_e, next_e, slot, valid, n_act.reshape(1).astype(I32)


def _block(x, p, norm_mix_w, w_in, b_merge, w_alpha_up, b_alpha_up, gla_norm_w, w_gla_out,
           conv_w, conv_b, w_conv_out, w_mix_out, norm_ffn_w, w_router_group, b_router_group,
           w_router_expert, b_router_expert, w_e_gate, w_e_up, w_e_down, ple_norm_w,
           w_ple_gate, w_ple_proj, final_norm_w):
    t = x.shape[0]
    n_tiles = (2 * t) // EXPERT_TILE + N_EXPERTS
    n_rows = n_tiles * EXPERT_TILE

    w_up = jnp.pad(w_alpha_up, ((0, LANES - GLA_GATE_RANK), (0, 0))).astype(BF16)
    w_up = w_up.reshape(LANES, GLA_HEADS, GLA_DK).transpose(1, 0, 2)
    b_up = b_alpha_up.reshape(GLA_HEADS, 1, GLA_DK)
    gnw = gla_norm_w.reshape(GLA_HEADS, 1, GLA_DV)
    wr_t = jnp.zeros((ROUTER_ROWS, D_MODEL), F32)
    wr_t = wr_t.at[0:N_GROUPS].set(w_router_group.T)
    wr_t = wr_t.at[EXPERT_ROW0:EXPERT_ROW0 + N_EXPERTS].set(w_router_expert.T)
    br = jnp.zeros((ROUTER_ROWS, 1), F32)
    br = br.at[0:N_GROUPS, 0].set(b_router_group)
    br = br.at[EXPERT_ROW0:EXPERT_ROW0 + N_EXPERTS, 0].set(b_router_expert)

    w_in_t = w_in.T
    h, a_low = _norm_in(x, norm_mix_w.reshape(1, D_MODEL), w_in_t)
    qkvg = _proj_qkvg(h, w_in_t)
    ob = _proj_conv(h, w_in_t, conv_w, conv_b.reshape(1, D_MODEL))
    gates = _proj_gates(h, w_in_t, b_merge.reshape(1, 2 * D_MODEL))
    oa = _gla(qkvg, a_low, w_up, b_up, gnw)
    mixed = _merge(oa, ob, w_gla_out, w_conv_out, gates)
    x1 = _mix(mixed, w_mix_out, x)

    hp, ids, wts = _route(x1, norm_ffn_w.reshape(1, D_MODEL), wr_t, br)
    dest, counts = _plan(ids)
    dest_flat = dest[1, 0:2].reshape(2 * t)
    tile_e, next_e, slot, valid, n_act = _tile_table(counts[:, 0], n_tiles)
    row_slot = _invert(dest_flat, n_rows)
    yt = _experts(tile_e, next_e, slot, valid, row_slot, n_act, hp, w_e_gate, w_e_up, w_e_down, n_tiles)
    return _combine_ple(yt, x1, wts.T, p,
                        ple_norm_w.reshape(1, D_MODEL), w_ple_gate.astype(BF16),
                        w_ple_proj.astype(BF16), final_norm_w.reshape(1, D_MODEL))


def kernel(x, p, norm_mix_w, w_in, b_merge, w_alpha_up, b_alpha_up, gla_norm_w, w_gla_out, conv_w, conv_b, w_conv_out, w_mix_out, norm_ffn_w, w_router_group, b_router_group, w_router_expert, b_router_expert, w_e_gate, w_e_up, w_e_down, ple_norm_w, w_ple_gate, w_ple_proj, final_norm_w):
    depth, batch = p.shape[0], x.shape[0]
    assert depth == 1 and batch == 1, "kernel is specialised to one layer and one sequence"
    out = _block(x[0], p[0, 0], norm_mix_w[0], w_in[0], b_merge[0], w_alpha_up[0], b_alpha_up[0],
                 gla_norm_w[0], w_gla_out[0], conv_w[0], conv_b[0], w_conv_out[0], w_mix_out[0],
                 norm_ffn_w[0], w_router_group[0], b_router_group[0], w_router_expert[0],
                 b_router_expert[0], w_e_gate[0], w_e_up[0], w_e_down[0], ple_norm_w[0],
                 w_ple_gate[0], w_ple_proj[0], final_norm_w)
    return out[None]
```

```python
import functools

import jax
import jax.numpy as jnp
from jax import lax
from jax.experimental import pallas as pl
from jax.experimental.pallas import tpu as pltpu

F32 = jnp.float32
BF16 = jnp.bfloat16
I32 = jnp.int32

D_MODEL = 2048
PLE_DIM = 256
EPS = 1e-6
LOG2_E = 1.4426950408889634
GLA_HEADS = 4
GLA_DK = 256
GLA_DV = 512
GLA_KEY = GLA_HEADS * GLA_DK
GLA_VAL = GLA_HEADS * GLA_DV
GLA_GATE_RANK = 16
GLA_GATE_NORM = 16.0
GLA_CHUNK = 64
CONV_K = 3
N_GROUPS = 4
EXPERTS_PER_GROUP = 8
N_EXPERTS = N_GROUPS * EXPERTS_PER_GROUP
D_FF = 512

QKVG_COLS = 2 * GLA_KEY + 2 * GLA_VAL
CONV_COL0 = QKVG_COLS + GLA_GATE_RANK

LANES = 128
SUBLANES = 8
MXU_COLS = 256
ROW_CHUNK = 256
ROW_TILES = D_MODEL // LANES

EXPERT_TILE = 256
SORT_BLOCK = 512
DMA_UNROLL = 8
ROUTER_ROWS = 64
EXPERT_ROW0 = 8

MIB = 1024 * 1024


def _cp(sem, vmem_mib, **kw):
    return pltpu.CompilerParams(dimension_semantics=sem, vmem_limit_bytes=int(vmem_mib * MIB), **kw)


def _rms(x, w):
    return x * lax.rsqrt(jnp.mean(x * x, axis=-1, keepdims=True) + EPS) * w


def _dot(a, b):
    return jnp.dot(a, b, preferred_element_type=F32)


def _dot_nt(a, b):
    return lax.dot_general(a, b, (((1,), (1,)), ((), ())), preferred_element_type=F32)


def _dot_tn(a, b):
    return lax.dot_general(a, b, (((0,), (0,)), ((), ())), preferred_element_type=F32)


def _split_bf16(x):
    hi = x.astype(BF16)
    lo = (x - hi.astype(F32)).astype(BF16)
    return hi, lo


def _excl_prefix_rows(col):
    n = col.shape[0]
    r = lax.broadcasted_iota(I32, (n, n), 0)
    c = lax.broadcasted_iota(I32, (n, n), 1)
    lower = jnp.where(c < r, 1.0, 0.0).astype(BF16)
    hi, lo = _split_bf16(jnp.broadcast_to(col, (n, LANES)))
    return (_dot(lower, hi) + _dot(lower, lo))[:, 0:1]


def _to_token_major(val):
    return val.reshape(val.shape[0], ROW_TILES, LANES)


def _from_token_major(val):
    return val.reshape(val.shape[0], D_MODEL)


def _norm_in_kernel(x_ref, w_ref, wal_ref, h_ref, al_ref, walbf_ref):
    @pl.when(pl.program_id(0) == 0)
    def _():
        walbf_ref[...] = wal_ref[...].astype(BF16)

    h = _rms(x_ref[...], w_ref[...]).astype(BF16)
    h_ref[...] = h
    al_ref[...] = _dot_nt(h, walbf_ref[...])


def _norm_in(x, w, w_in_t, tm=512):
    t = x.shape[0]
    return pl.pallas_call(
        _norm_in_kernel,
        grid=(t // tm,),
        in_specs=[pl.BlockSpec((tm, D_MODEL), lambda i: (i, 0)),
                  pl.BlockSpec((1, D_MODEL), lambda i: (0, 0)),
                  pl.BlockSpec((LANES, D_MODEL), lambda i: (QKVG_COLS // LANES, 0))],
        out_specs=[pl.BlockSpec((tm, D_MODEL), lambda i: (i, 0)),
                   pl.BlockSpec((tm, LANES), lambda i: (i, 0))],
        out_shape=[jax.ShapeDtypeStruct((t, D_MODEL), BF16),
                   jax.ShapeDtypeStruct((t, LANES), F32)],
        scratch_shapes=[pltpu.VMEM((LANES, D_MODEL), BF16)],
        compiler_params=_cp(("arbitrary",), 32),
        name="norm_in",
    )(x, w, w_in_t)


def _w_rows_spec(row0, tn):
    assert row0 % SUBLANES == 0 and tn % SUBLANES == 0
    return pl.BlockSpec((pl.Element(tn), pl.Element(D_MODEL)),
                        lambda n, m: (pl.multiple_of(row0 + n * tn, SUBLANES), 0))


def _proj_qkvg_kernel(h_ref, w_ref, o_ref, wbf_ref):
    @pl.when(pl.program_id(1) == 0)
    def _():
        wbf_ref[...] = w_ref[...].astype(BF16)

    o_ref[...] = _dot_nt(h_ref[...], wbf_ref[...]).astype(o_ref.dtype)


def _proj_qkvg(h, w_in_t, tm=2048, tn=1024):
    t = h.shape[0]
    return pl.pallas_call(
        _proj_qkvg_kernel,
        grid=(QKVG_COLS // tn, t // tm),
        in_specs=[pl.BlockSpec((tm, D_MODEL), lambda n, m: (m, 0)),
                  pl.BlockSpec((tn, D_MODEL), lambda n, m: (n, 0))],
        out_specs=pl.BlockSpec((tm, tn), lambda n, m: (m, n)),
        out_shape=jax.ShapeDtypeStruct((t, QKVG_COLS), BF16),
        scratch_shapes=[pltpu.VMEM((tn, D_MODEL), BF16)],
        compiler_params=_cp(("arbitrary", "arbitrary"), 56),
        name="proj_qkvg",
    )(h, w_in_t)


def _proj_conv_kernel(h_ref, wb_ref, wc_ref, wx_ref, cw_ref, cb_ref, o_ref,
                      wbbf_ref, wcbf_ref, wxbf_ref, prev_ref):
    m = pl.program_id(1)

    @pl.when(m == 0)
    def _():
        prev_ref[...] = jnp.zeros_like(prev_ref)
        wbbf_ref[...] = wb_ref[...].astype(BF16)
        wcbf_ref[...] = wc_ref[...].astype(BF16)
        wxbf_ref[...] = wx_ref[...].astype(BF16)

    h = h_ref[...]
    tm = h.shape[0]
    row = lax.broadcasted_iota(I32, (tm, MXU_COLS), 0)
    for c in range(0, o_ref.shape[1], MXU_COLS):
        cols = slice(c, c + MXU_COLS)
        b = _dot_nt(h, wbbf_ref[cols, :])
        s = _dot_nt(h, wcbf_ref[cols, :]) * _dot_nt(h, wxbf_ref[cols, :])
        prev = prev_ref[:, cols]
        p1 = prev[SUBLANES - 1:SUBLANES, :]
        p2 = prev[SUBLANES - 2:SUBLANES - 1, :]
        s1 = jnp.where(row == 0, p1, pltpu.roll(s, 1, 0))
        s2 = jnp.where(row == 0, p2, jnp.where(row == 1, p1, pltpu.roll(s, 2, 0)))
        cw = cw_ref[:, cols]
        u = cw[2:3, :] * s + cw[1:2, :] * s1 + cw[0:1, :] * s2 + cb_ref[:, cols]
        o_ref[:, cols] = (b * u).astype(o_ref.dtype)
        prev_ref[:, cols] = s[tm - SUBLANES:, :]


def _proj_conv(h, w_in_t, conv_w, conv_b, tm=1024, tn=512):
    t = h.shape[0]
    return pl.pallas_call(
        _proj_conv_kernel,
        grid=(D_MODEL // tn, t // tm),
        in_specs=[pl.BlockSpec((tm, D_MODEL), lambda n, m: (m, 0))] +
                 [_w_rows_spec(CONV_COL0 + seg * D_MODEL, tn) for seg in range(3)] +
                 [pl.BlockSpec((CONV_K, tn), lambda n, m: (0, n)),
                  pl.BlockSpec((1, tn), lambda n, m: (0, n))],
        out_specs=pl.BlockSpec((tm, tn), lambda n, m: (m, n)),
        out_shape=jax.ShapeDtypeStruct((t, D_MODEL), BF16),
        scratch_shapes=[pltpu.VMEM((tn, D_MODEL), BF16), pltpu.VMEM((tn, D_MODEL), BF16),
                        pltpu.VMEM((tn, D_MODEL), BF16), pltpu.VMEM((SUBLANES, tn), F32)],
        compiler_params=_cp(("arbitrary", "arbitrary"), 48),
        name="proj_conv",
    )(h, w_in_t, w_in_t, w_in_t, conv_w, conv_b)


def _proj_gates_kernel(h_ref, w_ref, b_ref, o_ref, wbf_ref):
    @pl.when(pl.program_id(1) == 0)
    def _():
        wbf_ref[...] = w_ref[...].astype(BF16)

    for c in range(0, o_ref.shape[1], MXU_COLS):
        cols = slice(c, c + MXU_COLS)
        for r in range(0, o_ref.shape[0], ROW_CHUNK):
            rows = slice(r, r + ROW_CHUNK)
            z = _dot_nt(h_ref[rows, :], wbf_ref[cols, :]) + b_ref[:, cols]
            o_ref[rows, cols] = jax.nn.sigmoid(z).astype(o_ref.dtype)


def _proj_gates(h, w_in_t, b_merge, tm=2048, tn=1024):
    t = h.shape[0]
    return pl.pallas_call(
        _proj_gates_kernel,
        grid=(2 * D_MODEL // tn, t // tm),
        in_specs=[pl.BlockSpec((tm, D_MODEL), lambda n, m: (m, 0)),
                  _w_rows_spec(CONV_COL0 + 3 * D_MODEL, tn),
                  pl.BlockSpec((1, tn), lambda n, m: (0, n))],
        out_specs=pl.BlockSpec((tm, tn), lambda n, m: (m, n)),
        out_shape=jax.ShapeDtypeStruct((t, 2 * D_MODEL), BF16),
        scratch_shapes=[pltpu.VMEM((tn, D_MODEL), BF16)],
        compiler_params=_cp(("arbitrary", "arbitrary"), 56),
        name="proj_gates",
    )(h, w_in_t, b_merge)


def _gla_kernel(q_ref, k_ref, v_ref, g_ref, al_ref, wup_ref, bup_ref, nw_ref, o_ref,
                st_ref, b_ref, bl_ref, qd_ref, ki_ref, kd_ref, oi_ref, u_ref, *, n_chunks):
    c_len = GLA_CHUNK

    @pl.when(pl.program_id(1) == 0)
    def _():
        st_ref[...] = jnp.zeros_like(st_ref)

    row = lax.broadcasted_iota(I32, (c_len, c_len), 0)
    col = lax.broadcasted_iota(I32, (c_len, c_len), 1)
    causal = col <= row
    tril = jnp.where(causal, 1.0, 0.0).astype(BF16)

    def decays(h):
        kc = slice(h * GLA_DK, (h + 1) * GLA_DK)
        z = _dot(al_ref[...].astype(BF16), wup_ref[h]) + bup_ref[h]
        la = (jnp.minimum(z, 0.0) - jnp.log1p(jnp.exp(-jnp.abs(z)))) * (LOG2_E / GLA_GATE_NORM)
        la_hi, la_lo = _split_bf16(la)
        for c in range(n_chunks):
            r0 = c * c_len
            b = _dot(tril, la_hi[r0:r0 + c_len]) + _dot(tril, la_lo[r0:r0 + c_len])
            b_ref[pl.ds(r0, c_len), kc] = b
            bl_ref[pl.ds(r0, c_len), kc] = jnp.broadcast_to(b[c_len - 1:c_len, :], b.shape)
        b = b_ref[:, kc]
        q = q_ref[:, kc].astype(F32)
        k = k_ref[:, kc].astype(F32)
        qd_ref[:, kc] = (q * jnp.exp2(b)).astype(BF16)
        ki_ref[:, kc] = (k * jnp.exp2(-b)).astype(BF16)
        kd_ref[:, kc] = (k * jnp.exp2(bl_ref[:, kc] - b)).astype(BF16)

    def local_products(h):
        kc = slice(h * GLA_DK, (h + 1) * GLA_DK)
        vc = slice(h * GLA_DV, (h + 1) * GLA_DV)
        for c in range(n_chunks):
            sl = pl.ds(c * c_len, c_len)
            v = v_ref[sl, vc]
            att = jnp.where(causal, _dot_nt(qd_ref[sl, kc], ki_ref[sl, kc]), 0.0).astype(BF16)
            oi_ref[sl, vc] = _dot(att, v)
            u_ref[h, c] = _dot_tn(v, kd_ref[sl, kc])

    def recurrence(h):
        kc = slice(h * GLA_DK, (h + 1) * GLA_DK)
        vc = slice(h * GLA_DV, (h + 1) * GLA_DV)
        for c in range(n_chunks):
            sl = pl.ds(c * c_len, c_len)
            st = st_ref[h]
            oi_ref[sl, vc] = oi_ref[sl, vc] + _dot_nt(qd_ref[sl, kc], st.astype(BF16))
            st_ref[h] = st * jnp.exp2(bl_ref[pl.ds(c * c_len, 1), kc]) + u_ref[h, c]
        o = oi_ref[:, vc]
        o = o * lax.rsqrt(jnp.mean(o * o, axis=-1, keepdims=True) + EPS * GLA_DK) * nw_ref[h]
        g = g_ref[:, vc].astype(F32)
        o_ref[:, vc] = (o * (g * jax.nn.sigmoid(g))).astype(o_ref.dtype)

    decays(0)
    local_products(0)
    decays(1)
    recurrence(0)
    local_products(1)
    recurrence(1)


def _gla(qkvg, a_low, w_up, b_up, norm_w, tb=1024):
    t = qkvg.shape[0]
    hp = 2
    dk, dv = hp * GLA_DK, hp * GLA_DV
    kq = GLA_KEY // dk
    kv = 2 * GLA_KEY // dv
    kg = kv + GLA_VAL // dv
    n_chunks = tb // GLA_CHUNK
    kern = functools.partial(_gla_kernel, n_chunks=n_chunks)
    return pl.pallas_call(
        kern,
        grid=(GLA_HEADS // hp, t // tb),
        in_specs=[pl.BlockSpec((tb, dk), lambda h, i: (i, h)),
                  pl.BlockSpec((tb, dk), lambda h, i: (i, kq + h)),
                  pl.BlockSpec((tb, dv), lambda h, i: (i, kv + h)),
                  pl.BlockSpec((tb, dv), lambda h, i: (i, kg + h)),
                  pl.BlockSpec((tb, LANES), lambda h, i: (i, 0)),
                  pl.BlockSpec((hp, LANES, GLA_DK), lambda h, i: (h, 0, 0)),
                  pl.BlockSpec((hp, 1, GLA_DK), lambda h, i: (h, 0, 0)),
                  pl.BlockSpec((hp, 1, GLA_DV), lambda h, i: (h, 0, 0))],
        out_specs=pl.BlockSpec((tb, dv), lambda h, i: (i, h)),
        out_shape=jax.ShapeDtypeStruct((t, GLA_VAL), BF16),
        scratch_shapes=[pltpu.VMEM((hp, GLA_DV, GLA_DK), F32),
                        pltpu.VMEM((tb, dk), F32), pltpu.VMEM((tb, dk), F32),
                        pltpu.VMEM((tb, dk), BF16), pltpu.VMEM((tb, dk), BF16),
                        pltpu.VMEM((tb, dk), BF16),
                        pltpu.VMEM((tb, dv), F32),
                        pltpu.VMEM((hp, n_chunks, GLA_DV, GLA_DK), F32)],
        compiler_params=_cp(("arbitrary", "arbitrary"), 56),
        name="gla",
    )(qkvg, qkvg, qkvg, qkvg, a_low, w_up, b_up, norm_w)


def _merge_kernel(oa_ref, ob_ref, wa_ref, wb_ref, g0_ref, g1_ref, o_ref, wabf_ref, wbbf_ref):
    @pl.when(pl.program_id(1) == 0)
    def _():
        wabf_ref[...] = wa_ref[...].astype(BF16)
        wbbf_ref[...] = wb_ref[...].astype(BF16)

    oa = oa_ref[...]
    ob = ob_ref[...]
    for c in range(0, o_ref.shape[1], MXU_COLS):
        cols = slice(c, c + MXU_COLS)
        a = _dot(oa, wabf_ref[:, cols])
        b = _dot(ob, wbbf_ref[:, cols])
        o_ref[:, cols] = (g0_ref[:, cols].astype(F32) * a + g1_ref[:, cols].astype(F32) * b).astype(o_ref.dtype)


def _merge(oa, ob, w_a, w_b, gates, tm=1024, tn=512):
    t = oa.shape[0]
    nb = D_MODEL // tn
    return pl.pallas_call(
        _merge_kernel,
        grid=(nb, t // tm),
        in_specs=[pl.BlockSpec((tm, D_MODEL), lambda n, m: (m, 0)),
                  pl.BlockSpec((tm, D_MODEL), lambda n, m: (m, 0)),
                  pl.BlockSpec((D_MODEL, tn), lambda n, m: (0, n)),
                  pl.BlockSpec((D_MODEL, tn), lambda n, m: (0, n)),
                  pl.BlockSpec((tm, tn), lambda n, m: (m, n)),
                  pl.BlockSpec((tm, tn), lambda n, m: (m, nb + n))],
        out_specs=pl.BlockSpec((tm, tn), lambda n, m: (m, n)),
        out_shape=jax.ShapeDtypeStruct((t, D_MODEL), BF16),
        scratch_shapes=[pltpu.VMEM((D_MODEL, tn), BF16), pltpu.VMEM((D_MODEL, tn), BF16)],
        compiler_params=_cp(("arbitrary", "arbitrary"), 56),
        name="merge",
    )(oa, ob, w_a, w_b, gates, gates)


def _mix_kernel(a_ref, w_ref, x_ref, o_ref, wbf_ref):
    @pl.when(pl.program_id(1) == 0)
    def _():
        wbf_ref[...] = w_ref[...].astype(BF16)

    a = a_ref[...]
    for c in range(0, o_ref.shape[1], MXU_COLS):
        cols = slice(c, c + MXU_COLS)
        o_ref[:, cols] = x_ref[:, cols] + _dot(a, wbf_ref[:, cols])


def _mix(mixed, w, x, tm=1024, tn=1024):
    t = x.shape[0]
    return pl.pallas_call(
        _mix_kernel,
        grid=(D_MODEL // tn, t // tm),
        in_specs=[pl.BlockSpec((tm, D_MODEL), lambda n, m: (m, 0)),
                  pl.BlockSpec((D_MODEL, tn), lambda n, m: (0, n)),
                  pl.BlockSpec((tm, tn), lambda n, m: (m, n))],
        out_specs=pl.BlockSpec((tm, tn), lambda n, m: (m, n)),
        out_shape=jax.ShapeDtypeStruct((t, D_MODEL), F32),
        scratch_shapes=[pltpu.VMEM((D_MODEL, tn), BF16)],
        compiler_params=_cp(("arbitrary", "arbitrary"), 52),
        name="mix",
    )(mixed, w, x)


def _route_kernel(x_ref, nw_ref, wr_ref, br_ref, hp_ref, ids_ref, wts_ref):
    h = _rms(x_ref[...], nw_ref[...])
    h_hi, h_lo = _split_bf16(h)
    w_hi, w_lo = _split_bf16(wr_ref[...])
    logits = _dot_nt(w_hi, h_hi) + _dot_nt(w_hi, h_lo) + _dot_nt(w_lo, h_hi) + br_ref[...]
    tm = logits.shape[1]

    best = logits[0:1, :]
    gidx = jnp.zeros((1, tm), I32)
    for i in range(1, N_GROUPS):
        li = logits[i:i + 1, :]
        take = li > best
        best = jnp.where(take, li, best)
        gidx = jnp.where(take, i, gidx)
    gsum = jnp.zeros((1, tm), F32)
    for i in range(N_GROUPS):
        gsum = gsum + jnp.exp(logits[i:i + 1, :] - best)
    g_p = 1.0 / gsum

    sel = logits[EXPERT_ROW0:EXPERT_ROW0 + EXPERTS_PER_GROUP, :]
    for g in range(1, N_GROUPS):
        r0 = EXPERT_ROW0 + g * EXPERTS_PER_GROUP
        sel = jnp.where(gidx == g, logits[r0:r0 + EXPERTS_PER_GROUP, :], sel)
    eio = lax.broadcasted_iota(I32, sel.shape, 0)
    m1 = jnp.max(sel, axis=0, keepdims=True)
    i1 = jnp.min(jnp.where(sel == m1, eio, EXPERTS_PER_GROUP), axis=0, keepdims=True)
    rest = jnp.where(eio == i1, -jnp.inf, sel)
    m2 = jnp.max(rest, axis=0, keepdims=True)
    i2 = jnp.min(jnp.where(rest == m2, eio, EXPERTS_PER_GROUP), axis=0, keepdims=True)
    p2 = jnp.exp(m2 - m1)
    w1 = g_p / (1.0 + p2)
    w2 = g_p * p2 / (1.0 + p2)
    e1 = gidx * EXPERTS_PER_GROUP + i1
    e2 = gidx * EXPERTS_PER_GROUP + i2
    rio = lax.broadcasted_iota(I32, (SUBLANES, tm), 0)
    ids_ref[...] = jnp.where(rio == 0, e1, jnp.where(rio == 1, e2, 0))
    wts_ref[...] = jnp.where(rio == 0, w1, jnp.where(rio == 1, w2, 0.0))

    eio32 = lax.broadcasted_iota(I32, (N_EXPERTS, tm), 0)
    oh = [jnp.where(eio32 == e, 1.0, 0.0) for e in (e1, e2)]
    r = lax.broadcasted_iota(I32, (tm, tm), 0)
    c = lax.broadcasted_iota(I32, (tm, tm), 1)
    upper = jnp.where(r <= c, 1.0, 0.0).astype(BF16)
    cum = [_dot(o.astype(BF16), upper) for o in oh]
    cnt0 = cum[0][:, tm - 1:tm]
    first = _excl_prefix_rows(cnt0 + cum[1][:, tm - 1:tm])
    lpos0 = jnp.sum(oh[0] * (cum[0] - 1.0 + first), axis=0, keepdims=True).astype(I32)
    lpos1 = jnp.sum(oh[1] * (cum[1] - 1.0 + first + cnt0), axis=0, keepdims=True).astype(I32)
    jio = lax.broadcasted_iota(I32, (2 * tm, tm), 0)
    perm = jnp.where((jio == lpos0) | (jio == lpos1), 1.0, 0.0).astype(BF16)
    hp_ref[...] = _to_token_major(_dot(perm, h_hi).astype(BF16))


def _route(x1, norm_w, wr_t, br, tm=SORT_BLOCK):
    t = x1.shape[0]
    return pl.pallas_call(
        _route_kernel,
        grid=(t // tm,),
        in_specs=[pl.BlockSpec((tm, D_MODEL), lambda i: (i, 0)),
                  pl.BlockSpec((1, D_MODEL), lambda i: (0, 0)),
                  pl.BlockSpec((ROUTER_ROWS, D_MODEL), lambda i: (0, 0)),
                  pl.BlockSpec((ROUTER_ROWS, 1), lambda i: (0, 0))],
        out_specs=[pl.BlockSpec((2 * tm, ROW_TILES, LANES), lambda i: (i, 0, 0)),
                   pl.BlockSpec((SUBLANES, tm), lambda i: (0, i)),
                   pl.BlockSpec((SUBLANES, tm), lambda i: (0, i))],
        out_shape=[jax.ShapeDtypeStruct((2 * t, ROW_TILES, LANES), BF16),
                   jax.ShapeDtypeStruct((SUBLANES, t), I32),
                   jax.ShapeDtypeStruct((SUBLANES, t), F32)],
        compiler_params=_cp(("arbitrary",), 48),
        name="route",
    )(x1, norm_w, wr_t, br)


def _plan_kernel(ids_ref, dest_ref, cnt_ref, runs_ref, base_ref):
    phase = pl.program_id(0)
    step = pl.program_id(1)
    tm = ids_ref.shape[1]
    eio = lax.broadcasted_iota(I32, (N_EXPERTS, tm), 0)
    ids = ids_ref[...]
    oh = [jnp.where(eio == ids[k:k + 1, :], 1.0, 0.0) for k in range(2)]

    @pl.when((phase == 0) & (step == 0))
    def _():
        base_ref[...] = jnp.zeros_like(base_ref)

    @pl.when(phase == 0)
    def _():
        cnt = jnp.sum(oh[0] + oh[1], axis=1, keepdims=True)
        base_ref[...] = base_ref[...] + cnt
        dest_ref[0] = jnp.zeros(dest_ref.shape[1:], I32)
        runs_ref[0, 0] = jnp.zeros(runs_ref.shape[2:], F32)
        cnt_ref[...] = base_ref[...]

    @pl.when((phase == 1) & (step == 0))
    def _():
        tiles = jnp.floor((base_ref[...][:, 0:1] + (EXPERT_TILE - 1)) * (1.0 / EXPERT_TILE))
        base_ref[...] = jnp.broadcast_to(_excl_prefix_rows(tiles) * float(EXPERT_TILE), base_ref.shape)

    @pl.when(phase == 1)
    def _():
        r = lax.broadcasted_iota(I32, (tm, tm), 0)
        c = lax.broadcasted_iota(I32, (tm, tm), 1)
        upper = jnp.where(r <= c, 1.0, 0.0).astype(BF16)
        base = base_ref[...][:, 0:1]
        start = base
        rows = []
        for k in range(2):
            cum = _dot(oh[k].astype(BF16), upper)
            rows.append(jnp.sum(oh[k] * (cum - 1.0 + base), axis=0, keepdims=True))
            base = base + cum[:, tm - 1:tm]
        base_ref[...] = jnp.broadcast_to(base, base_ref.shape)
        length = base - start
        runs_ref[0, 0] = jnp.concatenate(
            [jnp.broadcast_to(v, (N_EXPERTS, LANES)) for v in (start, length, _excl_prefix_rows(length))], axis=0)
        rio = lax.broadcasted_iota(I32, (SUBLANES, tm), 0)
        d0 = rows[0].astype(I32)
        d1 = rows[1].astype(I32)
        dest_ref[0] = jnp.where(rio == 0, d0, jnp.where(rio == 1, d1, 0))


def _plan(ids, tm=SORT_BLOCK):
    t = ids.shape[1]
    return pl.pallas_call(
        _plan_kernel,
        grid=(2, t // tm),
        in_specs=[pl.BlockSpec((SUBLANES, tm), lambda p, i: (0, i))],
        out_specs=[pl.BlockSpec((1, SUBLANES, tm), lambda p, i: (p, 0, i)),
                   pl.BlockSpec((N_EXPERTS, LANES), lambda p, i: (0, 0)),
                   pl.BlockSpec((1, 1, 3 * N_EXPERTS, LANES), lambda p, i: (p, i, 0, 0))],
        out_shape=[jax.ShapeDtypeStruct((2, SUBLANES, t), I32),
                   jax.ShapeDtypeStruct((N_EXPERTS, LANES), F32),
                   jax.ShapeDtypeStruct((2, t // tm, 3 * N_EXPERTS, LANES), F32)],
        scratch_shapes=[pltpu.VMEM((N_EXPERTS, LANES), F32)],
        compiler_params=_cp(("arbitrary", "arbitrary"), 32),
        name="plan",
    )(ids)


def _invert_kernel(dest_ref, slot_ref, fill_ref, sem, *, n_slots):
    fill_ref[...] = jnp.full(fill_ref.shape, n_slots, I32)
    fill = pltpu.make_async_copy(fill_ref, slot_ref, sem)
    fill.start()
    fill.wait()

    def scatter(j, c):
        for u in range(DMA_UNROLL):
            q = j * DMA_UNROLL + u
            slot_ref[dest_ref[q]] = q
        return c

    lax.fori_loop(0, n_slots // DMA_UNROLL, scatter, 0)


def _invert(dest_flat, n_rows):
    n_slots = dest_flat.shape[0]
    assert n_slots % DMA_UNROLL == 0
    return pl.pallas_call(
        functools.partial(_invert_kernel, n_slots=n_slots),
        grid_spec=pltpu.PrefetchScalarGridSpec(
            num_scalar_prefetch=1,
            grid=(1,),
            in_specs=[],
            out_specs=pl.BlockSpec(memory_space=pltpu.SMEM),
            scratch_shapes=[pltpu.VMEM((n_rows,), I32), pltpu.SemaphoreType.DMA(())]),
        out_shape=jax.ShapeDtypeStruct((n_rows,), I32),
        compiler_params=_cp(("arbitrary",), 16),
        name="invert",
    )(dest_flat)


def _experts_kernel(te_ref, nx_ref, sl_ref, vr_ref, rs_ref, rd_ref, rc_ref, rl_ref, na_ref,
                    hp_hbm, wg_hbm, wu_hbm, wd_hbm, yt_hbm,
                    wgf_ref, wuf_ref, wdf_ref, wgbf_ref, wubf_ref, wdbf_ref, xbuf_ref, ybuf_ref,
                    sems, xsems, ysems, *, n_blocks):
    i = pl.program_id(0)
    n_act = na_ref[0]
    active = i < n_act
    changed = (i == 0) | (te_ref[i] != te_ref[jnp.maximum(i - 1, 0)])

    def weight_copies(e, slot):
        return (pltpu.make_async_copy(wg_hbm.at[e], wgf_ref.at[slot], sems.at[slot, 0]),
                pltpu.make_async_copy(wu_hbm.at[e], wuf_ref.at[slot], sems.at[slot, 1]),
                pltpu.make_async_copy(wd_hbm.at[e], wdf_ref.at[slot], sems.at[slot, 2]))

    def in_piece(tile, src, dst, rows):
        return pltpu.make_async_copy(hp_hbm.at[pl.ds(src, rows)], xbuf_ref.at[tile % 2, pl.ds(dst, rows)],
                                     xsems.at[tile % 2])

    def for_pieces(n, fn):
        for bit in reversed(range(EXPERT_TILE.bit_length())):
            rows = 1 << bit
            offset = lax.shift_left(lax.shift_right_logical(n, bit + 1), bit + 1)

            @pl.when((n > 0) & ((n & rows) != 0))
            def _():
                fn(offset, rows)

    def start_block(tile, b):
        run = b * N_EXPERTS + te_ref[tile]
        r0 = tile * EXPERT_TILE
        first = rd_ref[run]
        lo = jnp.maximum(first, r0)
        n = jnp.minimum(first + rc_ref[run], r0 + vr_ref[tile]) - lo
        src = b * (2 * SORT_BLOCK) + rl_ref[run] + (lo - first)
        for_pieces(n, lambda off, rows: in_piece(tile, src + off, lo - r0 + off, rows).start())

    def wait_inputs(tile):
        for_pieces(vr_ref[tile], lambda off, rows: in_piece(tile, off, off, rows).wait())

    def out_copy(tile, r):
        dst = yt_hbm.at[rs_ref[tile * EXPERT_TILE + r]]
        return pltpu.make_async_copy(ybuf_ref.at[tile % 2, r], dst, ysems.at[tile % 2])

    def for_rows(tile, make_copy, fn):
        n = vr_ref[tile]
        groups = lax.shift_right_logical(n, DMA_UNROLL.bit_length() - 1)

        def body(j, c):
            for u in range(DMA_UNROLL):
                fn(make_copy(tile, j * DMA_UNROLL + u))
            return c

        def tail(r, c):
            fn(make_copy(tile, r))
            return c

        lax.fori_loop(0, groups, body, 0)
        lax.fori_loop(groups * DMA_UNROLL, n, tail, 0)

    def wait_rows(tile, make_copy, whole_tile_copy):
        full = vr_ref[tile] == EXPERT_TILE

        @pl.when(full)
        def _():
            whole_tile_copy.wait()

        @pl.when(jnp.logical_not(full))
        def _():
            for_rows(tile, make_copy, lambda cp: cp.wait())

    def out_tile(tile):
        return pltpu.make_async_copy(ybuf_ref.at[tile % 2], yt_hbm.at[pl.ds(0, EXPERT_TILE)], ysems.at[tile % 2])

    @pl.when(i == 0)
    def _():
        xbuf_ref[...] = jnp.zeros_like(xbuf_ref)
        for cp in weight_copies(te_ref[0], 0):
            cp.start(priority=1)
        lax.fori_loop(0, n_blocks, lambda b, c: (start_block(i, b), c)[1], 0)

    last = pl.num_programs(0) - 1
    nxt_active = i + 1 < n_act
    prev_active = (i >= 1) & (i - 1 < n_act)
    prev_full = prev_active & (vr_ref[jnp.maximum(i - 1, 0)] == EXPERT_TILE)

    @pl.when((i >= 2) & (i - 2 < n_act))
    def _():
        wait_rows(i - 2, out_copy, out_tile(i - 2))

    @pl.when(prev_active & jnp.logical_not(active & prev_full))
    def _():
        for_rows(i - 1, out_copy, lambda cp: cp.start())

    @pl.when(active & changed)
    def _():
        slot = sl_ref[i]
        nxt = nx_ref[i]

        @pl.when(nxt >= 0)
        def _():
            for cp in weight_copies(nxt, 1 - slot):
                cp.start(priority=1)

        for cp in weight_copies(te_ref[i], slot):
            cp.wait()
        wgbf_ref[...] = wgf_ref[slot].astype(BF16)
        wubf_ref[...] = wuf_ref[slot].astype(BF16)
        wdbf_ref[...] = wdf_ref[slot].astype(BF16)

    def compute(fetch_next, send_prev):
        def copies(g):
            if fetch_next:
                for b in range(g * n_blocks // 4, (g + 1) * n_blocks // 4):
                    start_block(i + 1, b)
            if send_prev:
                for r in range(g * (EXPERT_TILE // 4), (g + 1) * (EXPERT_TILE // 4)):
                    out_copy(i - 1, r).start()

        wait_inputs(i)
        x = _from_token_major(xbuf_ref[i % 2])
        copies(0)
        hg = _dot(x, wgbf_ref[...])
        copies(1)
        hu = _dot(x, wubf_ref[...])
        copies(2)
        act = (hg * jax.nn.sigmoid(hg) * hu).astype(BF16)
        y = _dot(act, wdbf_ref[...])
        copies(3)
        ybuf_ref[i % 2] = _to_token_major(y.astype(BF16))

    for fetch_next in (False, True):
        for send_prev in (False, True):
            @pl.when(active & (nxt_active == fetch_next) & (prev_full == send_prev))
            def _(fetch_next=fetch_next, send_prev=send_prev):
                compute(fetch_next, send_prev)

    @pl.when(i == last)
    def _():
        @pl.when(active)
        def _():
            for_rows(i, out_copy, lambda cp: cp.start())
            wait_rows(i, out_copy, out_tile(i))

        @pl.when(prev_active)
        def _():
            wait_rows(i - 1, out_copy, out_tile(i - 1))


def _experts(tile_e, next_e, slot, valid, row_slot, run_first, run_len, run_local, n_act, hp,
             w_gate, w_up, w_down, n_tiles):
    n_slots = hp.shape[0]
    any_spec = pl.BlockSpec(memory_space=pl.ANY)
    tile_buf = pltpu.VMEM((2, EXPERT_TILE, ROW_TILES, LANES), BF16)
    return pl.pallas_call(
        functools.partial(_experts_kernel, n_blocks=n_slots // (2 * SORT_BLOCK)),
        grid_spec=pltpu.PrefetchScalarGridSpec(
            num_scalar_prefetch=9,
            grid=(n_tiles,),
            in_specs=[any_spec, any_spec, any_spec, any_spec],
            out_specs=any_spec,
            scratch_shapes=[pltpu.VMEM((2, D_MODEL, D_FF), F32),
                            pltpu.VMEM((2, D_MODEL, D_FF), F32),
                            pltpu.VMEM((2, D_FF, D_MODEL), F32),
                            pltpu.VMEM((D_MODEL, D_FF), BF16),
                            pltpu.VMEM((D_MODEL, D_FF), BF16),
                            pltpu.VMEM((D_FF, D_MODEL), BF16),
                            tile_buf, tile_buf,
                            pltpu.SemaphoreType.DMA((2, 3)),
                            pltpu.SemaphoreType.DMA((2,)),
                            pltpu.SemaphoreType.DMA((2,))]),
        out_shape=jax.ShapeDtypeStruct((n_slots, ROW_TILES, LANES), BF16),
        compiler_params=_cp(("arbitrary",), 48, has_side_effects=True),
        name="experts",
    )(tile_e, next_e, slot, valid, row_slot, run_first, run_len, run_local, n_act, hp, w_gate, w_up, w_down)


def _combine_kernel(y0_ref, y1_ref, x_ref, wt_ref, p_ref, nw_ref, wg_ref, wp_ref, fw_ref, o_ref):
    wt = wt_ref[...]
    y0 = _from_token_major(y0_ref[...]).astype(F32)
    y1 = _from_token_major(y1_ref[...]).astype(F32)
    x2 = x_ref[...] + wt[:, 0:1] * y0 + wt[:, 1:2] * y1
    hn = _rms(x2, nw_ref[...]).astype(BF16)
    pg = jax.nn.sigmoid(_dot(hn, wg_ref[...]))
    x3 = x2 + pg * _dot(p_ref[...].astype(BF16), wp_ref[...])
    o_ref[...] = _rms(x3, fw_ref[...])


def _combine_ple(yt, x1, wts_t, p, ple_norm_w, w_gate, w_proj, final_w, tm=512):
    t = x1.shape[0]
    nb = t // tm
    row = lambda i: (i, 0)
    fix = lambda i: (0, 0)
    return pl.pallas_call(
        _combine_kernel,
        grid=(nb,),
        in_specs=[pl.BlockSpec((tm, ROW_TILES, LANES), lambda i: (i, 0, 0)),
                  pl.BlockSpec((tm, ROW_TILES, LANES), lambda i: (nb + i, 0, 0)),
                  pl.BlockSpec((tm, D_MODEL), row),
                  pl.BlockSpec((tm, SUBLANES), row),
                  pl.BlockSpec((tm, PLE_DIM), row),
                  pl.BlockSpec((1, D_MODEL), fix),
                  pl.BlockSpec((D_MODEL, D_MODEL), fix),
                  pl.BlockSpec((PLE_DIM, D_MODEL), fix),
                  pl.BlockSpec((1, D_MODEL), fix)],
        out_specs=pl.BlockSpec((tm, D_MODEL), row),
        out_shape=jax.ShapeDtypeStruct((t, D_MODEL), F32),
        compiler_params=_cp(("arbitrary",), 56),
        name="combine_ple",
    )(yt, yt, x1, wts_t, p, ple_norm_w, w_gate, w_proj, final_w)


def _tile_table(counts, n_tiles):
    tiles = (counts.astype(I32) + (EXPERT_TILE - 1)) // EXPERT_TILE
    ends = jnp.cumsum(tiles)
    n_act = ends[-1]
    idx = jnp.minimum(jnp.arange(n_tiles, dtype=I32), n_act - 1)
    tile_e = jnp.sum((idx[:, None] >= ends[None, :]).astype(I32), axis=1).astype(I32)
    run_end = ends[tile_e]
    next_e = jnp.where(run_end < n_act, tile_e[jnp.minimum(run_end, n_tiles - 1)], -1).astype(I32)
    new_run = jnp.concatenate([jnp.ones((1,), I32), (tile_e[1:] != tile_e[:-1]).astype(I32)])
    slot = ((jnp.cumsum(new_run) - 1) % 2).astype(I32)
    arange = jnp.arange(n_tiles, dtype=I32)
    last = arange == run_end - 1
    valid = jnp.where(last, counts.astype(I32)[tile_e] - (tiles[tile_e] - 1) * EXPERT_TILE, EXPERT_TILE)
    valid = jnp.where(arange < n_act, valid, 0).astype(I32)
    return tile_e, next_e, slot, valid, n_act.reshape(1).astype(I32)


def _block(x, p, norm_mix_w, w_in, b_merge, w_alpha_up, b_alpha_up, gla_norm_w, w_gla_out,
           conv_w, conv_b, w_conv_out, w_mix_out, norm_ffn_w, w_router_group, b_router_group,
           w_router_expert, b_router_expert, w_e_gate, w_e_up, w_e_down, ple_norm_w,
           w_ple_gate, w_ple_proj, final_norm_w):
    t = x.shape[0]
    n_tiles = (2 * t) // EXPERT_TILE + N_EXPERTS
    n_rows = n_tiles * EXPERT_TILE

    w_up = jnp.pad(w_alpha_up, ((0, LANES - GLA_GATE_RANK), (0, 0))).astype(BF16)
    w_up = w_up.reshape(LANES, GLA_HEADS, GLA_DK).transpose(1, 0, 2)
    b_up = b_alpha_up.reshape(GLA_HEADS, 1, GLA_DK)
    gnw = gla_norm_w.reshape(GLA_HEADS, 1, GLA_DV)
    wr_t = jnp.zeros((ROUTER_ROWS, D_MODEL), F32)
    wr_t = wr_t.at[0:N_GROUPS].set(w_router_group.T)
    wr_t = wr_t.at[EXPERT_ROW0:EXPERT_ROW0 + N_EXPERTS].set(w_router_expert.T)
    br = jnp.zeros((ROUTER_ROWS, 1), F32)
    br = br.at[0:N_GROUPS, 0].set(b_router_group)
    br = br.at[EXPERT_ROW0:EXPERT_ROW0 + N_EXPERTS, 0].set(b_router_expert)

    w_in_t = w_in.T
    h, a_low = _norm_in(x, norm_mix_w.reshape(1, D_MODEL), w_in_t)
    qkvg = _proj_qkvg(h, w_in_t)
    ob = _proj_conv(h, w_in_t, conv_w, conv_b.reshape(1, D_MODEL))
    gates = _proj_gates(h, w_in_t, b_merge.reshape(1, 2 * D_MODEL))
    oa = _gla(qkvg, a_low, w_up, b_up, gnw)
    mixed = _merge(oa, ob, w_gla_out, w_conv_out, gates)
    x1 = _mix(mixed, w_mix_out, x)

    hp, ids, wts = _route(x1, norm_ffn_w.reshape(1, D_MODEL), wr_t, br)
    dest, counts, runs = _plan(ids)
    runs = runs[1, :, :, 0].astype(I32)
    run_first, run_len, run_local = (runs[:, k * N_EXPERTS:(k + 1) * N_EXPERTS].reshape(-1) for k in range(3))
    dest_flat = dest[1, 0:2].reshape(2 * t)
    tile_e, next_e, slot, valid, n_act = _tile_table(counts[:, 0], n_tiles)
    row_slot = _invert(dest_flat, n_rows)
    yt = _experts(tile_e, next_e, slot, valid, row_slot, run_first, run_len, run_local, n_act, hp,
                  w_e_gate, w_e_up, w_e_down, n_tiles)
    return _combine_ple(yt, x1, wts.T, p,
                        ple_norm_w.reshape(1, D_MODEL), w_ple_gate.astype(BF16),
                        w_ple_proj.astype(BF16), final_norm_w.reshape(1, D_MODEL))


def kernel(x, p, norm_mix_w, w_in, b_merge, w_alpha_up, b_alpha_up, gla_norm_w, w_gla_out, conv_w, conv_b, w_conv_out, w_mix_out, norm_ffn_w, w_router_group, b_router_group, w_router_expert, b_router_expert, w_e_gate, w_e_up, w_e_down, ple_norm_w, w_ple_gate, w_ple_proj, final_norm_w):
    depth, batch = p.shape[0], x.shape[0]
    assert depth == 1 and batch == 1, "kernel is specialised to one layer and one sequence"
    out = _block(x[0], p[0, 0], norm_mix_w[0], w_in[0], b_merge[0], w_alpha_up[0], b_alpha_up[0],
                 gla_norm_w[0], w_gla_out[0], conv_w[0], conv_b[0], w_conv_out[0], w_mix_out[0],
                 norm_ffn_w[0], w_router_group[0], b_router_group[0], w_router_expert[0],
                 b_router_expert[0], w_e_gate[0], w_e_up[0], w_e_down[0], ple_norm_w[0],
                 w_ple_gate[0], w_ple_proj[0], final_norm_w)
    return out[None]
```

```python
import functools

import jax
import jax.numpy as jnp
from jax import lax
from jax.experimental import pallas as pl
from jax.experimental.pallas import tpu as pltpu

F32 = jnp.float32
BF16 = jnp.bfloat16
I32 = jnp.int32

D_MODEL = 2048
PLE_DIM = 256
EPS = 1e-6
LOG2_E = 1.4426950408889634
GLA_HEADS = 4
GLA_DK = 256
GLA_DV = 512
GLA_KEY = GLA_HEADS * GLA_DK
GLA_VAL = GLA_HEADS * GLA_DV
GLA_GATE_RANK = 16
GLA_GATE_NORM = 16.0
GLA_CHUNK = 64
CONV_K = 3
N_GROUPS = 4
EXPERTS_PER_GROUP = 8
N_EXPERTS = N_GROUPS * EXPERTS_PER_GROUP
D_FF = 512

QKVG_COLS = 2 * GLA_KEY + 2 * GLA_VAL
CONV_COL0 = QKVG_COLS + GLA_GATE_RANK

LANES = 128
SUBLANES = 8
MXU_COLS = 256
ROW_CHUNK = 256
ROW_TILES = D_MODEL // LANES

EXPERT_TILE = 256
DMA_UNROLL = 8
X_SLOTS = 3
ROUTER_ROWS = 64
EXPERT_ROW0 = 8

MIB = 1024 * 1024


def _cp(sem, vmem_mib, **kw):
    return pltpu.CompilerParams(dimension_semantics=sem, vmem_limit_bytes=int(vmem_mib * MIB), **kw)


def _rms(x, w):
    return x * lax.rsqrt(jnp.mean(x * x, axis=-1, keepdims=True) + EPS) * w


def _dot(a, b):
    return jnp.dot(a, b, preferred_element_type=F32)


def _dot_nt(a, b):
    return lax.dot_general(a, b, (((1,), (1,)), ((), ())), preferred_element_type=F32)


def _dot_tn(a, b):
    return lax.dot_general(a, b, (((0,), (0,)), ((), ())), preferred_element_type=F32)


def _split_bf16(x):
    hi = x.astype(BF16)
    lo = (x - hi.astype(F32)).astype(BF16)
    return hi, lo


def _to_token_major(val):
    return val.reshape(val.shape[0], ROW_TILES, LANES)


def _from_token_major(val):
    return val.reshape(val.shape[0], D_MODEL)


def _norm_in_kernel(x_ref, w_ref, wal_ref, h_ref, al_ref, walbf_ref):
    @pl.when(pl.program_id(0) == 0)
    def _():
        walbf_ref[...] = wal_ref[...].astype(BF16)

    h = _rms(x_ref[...], w_ref[...]).astype(BF16)
    h_ref[...] = h
    al_ref[...] = _dot_nt(h, walbf_ref[...])


def _norm_in(x, w, w_in_t, tm=512):
    t = x.shape[0]
    return pl.pallas_call(
        _norm_in_kernel,
        grid=(t // tm,),
        in_specs=[pl.BlockSpec((tm, D_MODEL), lambda i: (i, 0)),
                  pl.BlockSpec((1, D_MODEL), lambda i: (0, 0)),
                  pl.BlockSpec((LANES, D_MODEL), lambda i: (QKVG_COLS // LANES, 0))],
        out_specs=[pl.BlockSpec((tm, D_MODEL), lambda i: (i, 0)),
                   pl.BlockSpec((tm, LANES), lambda i: (i, 0))],
        out_shape=[jax.ShapeDtypeStruct((t, D_MODEL), BF16),
                   jax.ShapeDtypeStruct((t, LANES), F32)],
        scratch_shapes=[pltpu.VMEM((LANES, D_MODEL), BF16)],
        compiler_params=_cp(("arbitrary",), 32),
        name="norm_in",
    )(x, w, w_in_t)


def _w_rows_spec(row0, tn):
    assert row0 % SUBLANES == 0 and tn % SUBLANES == 0
    return pl.BlockSpec((pl.Element(tn), pl.Element(D_MODEL)),
                        lambda n, m: (pl.multiple_of(row0 + n * tn, SUBLANES), 0))


def _proj_qkvg_kernel(h_ref, w_ref, o_ref, wbf_ref):
    @pl.when(pl.program_id(1) == 0)
    def _():
        wbf_ref[...] = w_ref[...].astype(BF16)

    o_ref[...] = _dot_nt(h_ref[...], wbf_ref[...]).astype(o_ref.dtype)


def _proj_qkvg(h, w_in_t, tm=2048, tn=1024):
    t = h.shape[0]
    return pl.pallas_call(
        _proj_qkvg_kernel,
        grid=(QKVG_COLS // tn, t // tm),
        in_specs=[pl.BlockSpec((tm, D_MODEL), lambda n, m: (m, 0)),
                  pl.BlockSpec((tn, D_MODEL), lambda n, m: (n, 0))],
        out_specs=pl.BlockSpec((tm, tn), lambda n, m: (m, n)),
        out_shape=jax.ShapeDtypeStruct((t, QKVG_COLS), BF16),
        scratch_shapes=[pltpu.VMEM((tn, D_MODEL), BF16)],
        compiler_params=_cp(("arbitrary", "arbitrary"), 56),
        name="proj_qkvg",
    )(h, w_in_t)


def _proj_conv_kernel(h_ref, wb_ref, wc_ref, wx_ref, cw_ref, cb_ref, o_ref,
                      wbbf_ref, wcbf_ref, wxbf_ref, prev_ref):
    m = pl.program_id(1)

    @pl.when(m == 0)
    def _():
        prev_ref[...] = jnp.zeros_like(prev_ref)
        wbbf_ref[...] = wb_ref[...].astype(BF16)
        wcbf_ref[...] = wc_ref[...].astype(BF16)
        wxbf_ref[...] = wx_ref[...].astype(BF16)

    h = h_ref[...]
    tm = h.shape[0]
    row = lax.broadcasted_iota(I32, (tm, MXU_COLS), 0)
    for c in range(0, o_ref.shape[1], MXU_COLS):
        cols = slice(c, c + MXU_COLS)
        b = _dot_nt(h, wbbf_ref[cols, :])
        s = _dot_nt(h, wcbf_ref[cols, :]) * _dot_nt(h, wxbf_ref[cols, :])
        prev = prev_ref[:, cols]
        p1 = prev[SUBLANES - 1:SUBLANES, :]
        p2 = prev[SUBLANES - 2:SUBLANES - 1, :]
        s1 = jnp.where(row == 0, p1, pltpu.roll(s, 1, 0))
        s2 = jnp.where(row == 0, p2, jnp.where(row == 1, p1, pltpu.roll(s, 2, 0)))
        cw = cw_ref[:, cols]
        u = cw[2:3, :] * s + cw[1:2, :] * s1 + cw[0:1, :] * s2 + cb_ref[:, cols]
        o_ref[:, cols] = (b * u).astype(o_ref.dtype)
        prev_ref[:, cols] = s[tm - SUBLANES:, :]


def _proj_conv(h, w_in_t, conv_w, conv_b, tm=1024, tn=512):
    t = h.shape[0]
    return pl.pallas_call(
        _proj_conv_kernel,
        grid=(D_MODEL // tn, t // tm),
        in_specs=[pl.BlockSpec((tm, D_MODEL), lambda n, m: (m, 0))] +
                 [_w_rows_spec(CONV_COL0 + seg * D_MODEL, tn) for seg in range(3)] +
                 [pl.BlockSpec((CONV_K, tn), lambda n, m: (0, n)),
                  pl.BlockSpec((1, tn), lambda n, m: (0, n))],
        out_specs=pl.BlockSpec((tm, tn), lambda n, m: (m, n)),
        out_shape=jax.ShapeDtypeStruct((t, D_MODEL), BF16),
        scratch_shapes=[pltpu.VMEM((tn, D_MODEL), BF16), pltpu.VMEM((tn, D_MODEL), BF16),
                        pltpu.VMEM((tn, D_MODEL), BF16), pltpu.VMEM((SUBLANES, tn), F32)],
        compiler_params=_cp(("arbitrary", "arbitrary"), 48),
        name="proj_conv",
    )(h, w_in_t, w_in_t, w_in_t, conv_w, conv_b)


def _proj_gates_kernel(h_ref, w_ref, b_ref, o_ref, wbf_ref):
    @pl.when(pl.program_id(1) == 0)
    def _():
        wbf_ref[...] = w_ref[...].astype(BF16)

    for c in range(0, o_ref.shape[1], MXU_COLS):
        cols = slice(c, c + MXU_COLS)
        for r in range(0, o_ref.shape[0], ROW_CHUNK):
            rows = slice(r, r + ROW_CHUNK)
            z = _dot_nt(h_ref[rows, :], wbf_ref[cols, :]) + b_ref[:, cols]
            o_ref[rows, cols] = jax.nn.sigmoid(z).astype(o_ref.dtype)


def _proj_gates(h, w_in_t, b_merge, tm=2048, tn=1024):
    t = h.shape[0]
    return pl.pallas_call(
        _proj_gates_kernel,
        grid=(2 * D_MODEL // tn, t // tm),
        in_specs=[pl.BlockSpec((tm, D_MODEL), lambda n, m: (m, 0)),
                  _w_rows_spec(CONV_COL0 + 3 * D_MODEL, tn),
                  pl.BlockSpec((1, tn), lambda n, m: (0, n))],
        out_specs=pl.BlockSpec((tm, tn), lambda n, m: (m, n)),
        out_shape=jax.ShapeDtypeStruct((t, 2 * D_MODEL), BF16),
        scratch_shapes=[pltpu.VMEM((tn, D_MODEL), BF16)],
        compiler_params=_cp(("arbitrary", "arbitrary"), 56),
        name="proj_gates",
    )(h, w_in_t, b_merge)


def _gla_kernel(q_ref, k_ref, v_ref, g_ref, al_ref, wup_ref, bup_ref, nw_ref, o_ref,
                st_ref, b_ref, bl_ref, qd_ref, ki_ref, kd_ref, oi_ref, u_ref, *, n_chunks):
    c_len = GLA_CHUNK

    @pl.when(pl.program_id(1) == 0)
    def _():
        st_ref[...] = jnp.zeros_like(st_ref)

    row = lax.broadcasted_iota(I32, (c_len, c_len), 0)
    col = lax.broadcasted_iota(I32, (c_len, c_len), 1)
    causal = col <= row
    tril = jnp.where(causal, 1.0, 0.0).astype(BF16)

    def decays(h):
        kc = slice(h * GLA_DK, (h + 1) * GLA_DK)
        z = _dot(al_ref[...].astype(BF16), wup_ref[h]) + bup_ref[h]
        la = (jnp.minimum(z, 0.0) - jnp.log1p(jnp.exp(-jnp.abs(z)))) * (LOG2_E / GLA_GATE_NORM)
        la_hi, la_lo = _split_bf16(la)
        for c in range(n_chunks):
            r0 = c * c_len
            b = _dot(tril, la_hi[r0:r0 + c_len]) + _dot(tril, la_lo[r0:r0 + c_len])
            b_ref[pl.ds(r0, c_len), kc] = b
            bl_ref[pl.ds(r0, c_len), kc] = jnp.broadcast_to(b[c_len - 1:c_len, :], b.shape)
        b = b_ref[:, kc]
        q = q_ref[:, kc].astype(F32)
        k = k_ref[:, kc].astype(F32)
        qd_ref[:, kc] = (q * jnp.exp2(b)).astype(BF16)
        ki_ref[:, kc] = (k * jnp.exp2(-b)).astype(BF16)
        kd_ref[:, kc] = (k * jnp.exp2(bl_ref[:, kc] - b)).astype(BF16)

    def local_products(h):
        kc = slice(h * GLA_DK, (h + 1) * GLA_DK)
        vc = slice(h * GLA_DV, (h + 1) * GLA_DV)
        for c in range(n_chunks):
            sl = pl.ds(c * c_len, c_len)
            v = v_ref[sl, vc]
            att = jnp.where(causal, _dot_nt(qd_ref[sl, kc], ki_ref[sl, kc]), 0.0).astype(BF16)
            oi_ref[sl, vc] = _dot(att, v)
            u_ref[h, c] = _dot_tn(v, kd_ref[sl, kc])

    def recurrence(h):
        kc = slice(h * GLA_DK, (h + 1) * GLA_DK)
        vc = slice(h * GLA_DV, (h + 1) * GLA_DV)
        for c in range(n_chunks):
            sl = pl.ds(c * c_len, c_len)
            st = st_ref[h]
            oi_ref[sl, vc] = oi_ref[sl, vc] + _dot_nt(qd_ref[sl, kc], st.astype(BF16))
            st_ref[h] = st * jnp.exp2(bl_ref[pl.ds(c * c_len, 1), kc]) + u_ref[h, c]
        o = oi_ref[:, vc]
        o = o * lax.rsqrt(jnp.mean(o * o, axis=-1, keepdims=True) + EPS * GLA_DK) * nw_ref[h]
        g = g_ref[:, vc].astype(F32)
        o_ref[:, vc] = (o * (g * jax.nn.sigmoid(g))).astype(o_ref.dtype)

    decays(0)
    local_products(0)
    decays(1)
    recurrence(0)
    local_products(1)
    recurrence(1)


def _gla(qkvg, a_low, w_up, b_up, norm_w, tb=1024):
    t = qkvg.shape[0]
    hp = 2
    dk, dv = hp * GLA_DK, hp * GLA_DV
    kq = GLA_KEY // dk
    kv = 2 * GLA_KEY // dv
    kg = kv + GLA_VAL // dv
    n_chunks = tb // GLA_CHUNK
    kern = functools.partial(_gla_kernel, n_chunks=n_chunks)
    return pl.pallas_call(
        kern,
        grid=(GLA_HEADS // hp, t // tb),
        in_specs=[pl.BlockSpec((tb, dk), lambda h, i: (i, h)),
                  pl.BlockSpec((tb, dk), lambda h, i: (i, kq + h)),
                  pl.BlockSpec((tb, dv), lambda h, i: (i, kv + h)),
                  pl.BlockSpec((tb, dv), lambda h, i: (i, kg + h)),
                  pl.BlockSpec((tb, LANES), lambda h, i: (i, 0)),
                  pl.BlockSpec((hp, LANES, GLA_DK), lambda h, i: (h, 0, 0)),
                  pl.BlockSpec((hp, 1, GLA_DK), lambda h, i: (h, 0, 0)),
                  pl.BlockSpec((hp, 1, GLA_DV), lambda h, i: (h, 0, 0))],
        out_specs=pl.BlockSpec((tb, dv), lambda h, i: (i, h)),
        out_shape=jax.ShapeDtypeStruct((t, GLA_VAL), BF16),
        scratch_shapes=[pltpu.VMEM((hp, GLA_DV, GLA_DK), F32),
                        pltpu.VMEM((tb, dk), F32), pltpu.VMEM((tb, dk), F32),
                        pltpu.VMEM((tb, dk), BF16), pltpu.VMEM((tb, dk), BF16),
                        pltpu.VMEM((tb, dk), BF16),
                        pltpu.VMEM((tb, dv), F32),
                        pltpu.VMEM((hp, n_chunks, GLA_DV, GLA_DK), F32)],
        compiler_params=_cp(("arbitrary", "arbitrary"), 56),
        name="gla",
    )(qkvg, qkvg, qkvg, qkvg, a_low, w_up, b_up, norm_w)


def _merge_kernel(oa_ref, ob_ref, wa_ref, wb_ref, g0_ref, g1_ref, o_ref, wabf_ref, wbbf_ref):
    @pl.when(pl.program_id(1) == 0)
    def _():
        wabf_ref[...] = wa_ref[...].astype(BF16)
        wbbf_ref[...] = wb_ref[...].astype(BF16)

    oa = oa_ref[...]
    ob = ob_ref[...]
    for c in range(0, o_ref.shape[1], MXU_COLS):
        cols = slice(c, c + MXU_COLS)
        a = _dot(oa, wabf_ref[:, cols])
        b = _dot(ob, wbbf_ref[:, cols])
        o_ref[:, cols] = (g0_ref[:, cols].astype(F32) * a + g1_ref[:, cols].astype(F32) * b).astype(o_ref.dtype)


def _merge(oa, ob, w_a, w_b, gates, tm=1024, tn=512):
    t = oa.shape[0]
    nb = D_MODEL // tn
    return pl.pallas_call(
        _merge_kernel,
        grid=(nb, t // tm),
        in_specs=[pl.BlockSpec((tm, D_MODEL), lambda n, m: (m, 0)),
                  pl.BlockSpec((tm, D_MODEL), lambda n, m: (m, 0)),
                  pl.BlockSpec((D_MODEL, tn), lambda n, m: (0, n)),
                  pl.BlockSpec((D_MODEL, tn), lambda n, m: (0, n)),
                  pl.BlockSpec((tm, tn), lambda n, m: (m, n)),
                  pl.BlockSpec((tm, tn), lambda n, m: (m, nb + n))],
        out_specs=pl.BlockSpec((tm, tn), lambda n, m: (m, n)),
        out_shape=jax.ShapeDtypeStruct((t, D_MODEL), BF16),
        scratch_shapes=[pltpu.VMEM((D_MODEL, tn), BF16), pltpu.VMEM((D_MODEL, tn), BF16)],
        compiler_params=_cp(("arbitrary", "arbitrary"), 56),
        name="merge",
    )(oa, ob, w_a, w_b, gates, gates)


def _mix_kernel(a_ref, w_ref, x_ref, o_ref, wbf_ref):
    @pl.when(pl.program_id(1) == 0)
    def _():
        wbf_ref[...] = w_ref[...].astype(BF16)

    a = a_ref[...]
    for c in range(0, o_ref.shape[1], MXU_COLS):
        cols = slice(c, c + MXU_COLS)
        o_ref[:, cols] = x_ref[:, cols] + _dot(a, wbf_ref[:, cols])


def _mix(mixed, w, x, tm=1024, tn=1024):
    t = x.shape[0]
    return pl.pallas_call(
        _mix_kernel,
        grid=(D_MODEL // tn, t // tm),
        in_specs=[pl.BlockSpec((tm, D_MODEL), lambda n, m: (m, 0)),
                  pl.BlockSpec((D_MODEL, tn), lambda n, m: (0, n)),
                  pl.BlockSpec((tm, tn), lambda n, m: (m, n))],
        out_specs=pl.BlockSpec((tm, tn), lambda n, m: (m, n)),
        out_shape=jax.ShapeDtypeStruct((t, D_MODEL), F32),
        scratch_shapes=[pltpu.VMEM((D_MODEL, tn), BF16)],
        compiler_params=_cp(("arbitrary", "arbitrary"), 52),
        name="mix",
    )(mixed, w, x)


def _route_kernel(x_ref, nw_ref, wr_ref, br_ref, hp_ref, ids_ref, wts_ref):
    h = _rms(x_ref[...], nw_ref[...])
    hp_ref[...] = _to_token_major(h.astype(BF16))
    h_hi, h_lo = _split_bf16(h)
    w_hi, w_lo = _split_bf16(wr_ref[...])
    logits = _dot_nt(w_hi, h_hi) + _dot_nt(w_hi, h_lo) + _dot_nt(w_lo, h_hi) + br_ref[...]
    tm = logits.shape[1]

    best = logits[0:1, :]
    gidx = jnp.zeros((1, tm), I32)
    for i in range(1, N_GROUPS):
        li = logits[i:i + 1, :]
        take = li > best
        best = jnp.where(take, li, best)
        gidx = jnp.where(take, i, gidx)
    gsum = jnp.zeros((1, tm), F32)
    for i in range(N_GROUPS):
        gsum = gsum + jnp.exp(logits[i:i + 1, :] - best)
    g_p = 1.0 / gsum

    sel = logits[EXPERT_ROW0:EXPERT_ROW0 + EXPERTS_PER_GROUP, :]
    for g in range(1, N_GROUPS):
        r0 = EXPERT_ROW0 + g * EXPERTS_PER_GROUP
        sel = jnp.where(gidx == g, logits[r0:r0 + EXPERTS_PER_GROUP, :], sel)
    eio = lax.broadcasted_iota(I32, sel.shape, 0)
    m1 = jnp.max(sel, axis=0, keepdims=True)
    i1 = jnp.min(jnp.where(sel == m1, eio, EXPERTS_PER_GROUP), axis=0, keepdims=True)
    rest = jnp.where(eio == i1, -jnp.inf, sel)
    m2 = jnp.max(rest, axis=0, keepdims=True)
    i2 = jnp.min(jnp.where(rest == m2, eio, EXPERTS_PER_GROUP), axis=0, keepdims=True)
    p2 = jnp.exp(m2 - m1)
    w1 = g_p / (1.0 + p2)
    w2 = g_p * p2 / (1.0 + p2)
    e1 = gidx * EXPERTS_PER_GROUP + i1
    e2 = gidx * EXPERTS_PER_GROUP + i2
    rio = lax.broadcasted_iota(I32, (SUBLANES, tm), 0)
    ids_ref[...] = jnp.where(rio == 0, e1, jnp.where(rio == 1, e2, 0))
    wts_ref[...] = jnp.where(rio == 0, w1, jnp.where(rio == 1, w2, 0.0))


def _route(x1, norm_w, wr_t, br, tm=512):
    t = x1.shape[0]
    return pl.pallas_call(
        _route_kernel,
        grid=(t // tm,),
        in_specs=[pl.BlockSpec((tm, D_MODEL), lambda i: (i, 0)),
                  pl.BlockSpec((1, D_MODEL), lambda i: (0, 0)),
                  pl.BlockSpec((ROUTER_ROWS, D_MODEL), lambda i: (0, 0)),
                  pl.BlockSpec((ROUTER_ROWS, 1), lambda i: (0, 0))],
        out_specs=[pl.BlockSpec((tm, ROW_TILES, LANES), lambda i: (i, 0, 0)),
                   pl.BlockSpec((SUBLANES, tm), lambda i: (0, i)),
                   pl.BlockSpec((SUBLANES, tm), lambda i: (0, i))],
        out_shape=[jax.ShapeDtypeStruct((t, ROW_TILES, LANES), BF16),
                   jax.ShapeDtypeStruct((SUBLANES, t), I32),
                   jax.ShapeDtypeStruct((SUBLANES, t), F32)],
        compiler_params=_cp(("arbitrary",), 32),
        name="route",
    )(x1, norm_w, wr_t, br)


def _plan_kernel(ids_ref, dest_ref, cnt_ref, base_ref):
    phase = pl.program_id(0)
    step = pl.program_id(1)
    tm = ids_ref.shape[1]
    eio = lax.broadcasted_iota(I32, (N_EXPERTS, tm), 0)
    ids = ids_ref[...]
    oh = [jnp.where(eio == ids[k:k + 1, :], 1.0, 0.0) for k in range(2)]

    @pl.when((phase == 0) & (step == 0))
    def _():
        base_ref[...] = jnp.zeros_like(base_ref)

    @pl.when(phase == 0)
    def _():
        cnt = jnp.sum(oh[0] + oh[1], axis=1, keepdims=True)
        base_ref[...] = base_ref[...] + cnt
        dest_ref[0] = jnp.zeros(dest_ref.shape[1:], I32)
        cnt_ref[...] = base_ref[...]

    @pl.when((phase == 1) & (step == 0))
    def _():
        tiles = jnp.floor((base_ref[...] + (EXPERT_TILE - 1)) * (1.0 / EXPERT_TILE))
        r = lax.broadcasted_iota(I32, (N_EXPERTS, N_EXPERTS), 0)
        c = lax.broadcasted_iota(I32, (N_EXPERTS, N_EXPERTS), 1)
        lower = jnp.where(c < r, 1.0, 0.0).astype(BF16)
        base_ref[...] = _dot(lower, tiles.astype(BF16)) * float(EXPERT_TILE)

    @pl.when(phase == 1)
    def _():
        r = lax.broadcasted_iota(I32, (tm, tm), 0)
        c = lax.broadcasted_iota(I32, (tm, tm), 1)
        upper = jnp.where(r <= c, 1.0, 0.0).astype(BF16)
        base = base_ref[...][:, 0:1]
        rows = []
        for k in range(2):
            cum = _dot(oh[k].astype(BF16), upper)
            rows.append(jnp.sum(oh[k] * (cum - 1.0 + base), axis=0, keepdims=True))
            base = base + cum[:, tm - 1:tm]
        base_ref[...] = jnp.broadcast_to(base, base_ref.shape)
        rio = lax.broadcasted_iota(I32, (SUBLANES, tm), 0)
        d0 = rows[0].astype(I32)
        d1 = rows[1].astype(I32)
        dest_ref[0] = jnp.where(rio == 0, d0, jnp.where(rio == 1, d1, 0))


def _plan(ids, tm=512):
    t = ids.shape[1]
    return pl.pallas_call(
        _plan_kernel,
        grid=(2, t // tm),
        in_specs=[pl.BlockSpec((SUBLANES, tm), lambda p, i: (0, i))],
        out_specs=[pl.BlockSpec((1, SUBLANES, tm), lambda p, i: (p, 0, i)),
                   pl.BlockSpec((N_EXPERTS, LANES), lambda p, i: (0, 0))],
        out_shape=[jax.ShapeDtypeStruct((2, SUBLANES, t), I32),
                   jax.ShapeDtypeStruct((N_EXPERTS, LANES), F32)],
        scratch_shapes=[pltpu.VMEM((N_EXPERTS, LANES), F32)],
        compiler_params=_cp(("arbitrary", "arbitrary"), 32),
        name="plan",
    )(ids)


def _invert_kernel(dest_ref, slot_ref, fill_ref, sem, *, n_slots):
    fill_ref[...] = jnp.full(fill_ref.shape, n_slots, I32)
    fill = pltpu.make_async_copy(fill_ref, slot_ref, sem)
    fill.start()
    fill.wait()

    def scatter(j, c):
        for u in range(DMA_UNROLL):
            q = j * DMA_UNROLL + u
            slot_ref[dest_ref[q]] = q
        return c

    lax.fori_loop(0, n_slots // DMA_UNROLL, scatter, 0)


def _invert(dest_flat, n_rows):
    n_slots = dest_flat.shape[0]
    assert n_slots % DMA_UNROLL == 0
    return pl.pallas_call(
        functools.partial(_invert_kernel, n_slots=n_slots),
        grid_spec=pltpu.PrefetchScalarGridSpec(
            num_scalar_prefetch=1,
            grid=(1,),
            in_specs=[],
            out_specs=pl.BlockSpec(memory_space=pltpu.SMEM),
            scratch_shapes=[pltpu.VMEM((n_rows,), I32), pltpu.SemaphoreType.DMA(())]),
        out_shape=jax.ShapeDtypeStruct((n_rows,), I32),
        compiler_params=_cp(("arbitrary",), 16),
        name="invert",
    )(dest_flat)


def _experts_kernel(te_ref, nx_ref, sl_ref, vr_ref, rs_ref, na_ref, hp_hbm, wg_hbm, wu_hbm, wd_hbm, yt_hbm,
                    wgf_ref, wuf_ref, wdf_ref, wgbf_ref, wubf_ref, wdbf_ref, xbuf_ref, ybuf_ref,
                    sems, xsems, ysems, *, t_total):
    i = pl.program_id(0)
    n_act = na_ref[0]
    active = i < n_act
    changed = (i == 0) | (te_ref[i] != te_ref[jnp.maximum(i - 1, 0)])

    def weight_copies(e, slot):
        return (pltpu.make_async_copy(wg_hbm.at[e], wgf_ref.at[slot], sems.at[slot, 0]),
                pltpu.make_async_copy(wu_hbm.at[e], wuf_ref.at[slot], sems.at[slot, 1]),
                pltpu.make_async_copy(wd_hbm.at[e], wdf_ref.at[slot], sems.at[slot, 2]))

    def in_copy(tile, r):
        tok = rs_ref[tile * EXPERT_TILE + r] & (t_total - 1)
        return pltpu.make_async_copy(hp_hbm.at[tok], xbuf_ref.at[tile % X_SLOTS, r], xsems.at[tile % X_SLOTS])

    def out_copy(tile, r):
        dst = yt_hbm.at[rs_ref[tile * EXPERT_TILE + r]]
        return pltpu.make_async_copy(ybuf_ref.at[tile % 2, r], dst, ysems.at[tile % 2])

    def for_rows(tile, make_copy, fn):
        n = vr_ref[tile]
        groups = lax.shift_right_logical(n, DMA_UNROLL.bit_length() - 1)

        def body(j, c):
            for u in range(DMA_UNROLL):
                fn(make_copy(tile, j * DMA_UNROLL + u))
            return c

        def tail(r, c):
            fn(make_copy(tile, r))
            return c

        lax.fori_loop(0, groups, body, 0)
        lax.fori_loop(groups * DMA_UNROLL, n, tail, 0)

    def wait_rows(tile, make_copy, whole_tile_copy):
        full = vr_ref[tile] == EXPERT_TILE

        @pl.when(full)
        def _():
            whole_tile_copy.wait()

        @pl.when(jnp.logical_not(full))
        def _():
            for_rows(tile, make_copy, lambda cp: cp.wait())

    def in_tile(tile):
        return pltpu.make_async_copy(hp_hbm.at[pl.ds(0, EXPERT_TILE)], xbuf_ref.at[tile % X_SLOTS],
                                     xsems.at[tile % X_SLOTS])

    def out_tile(tile):
        return pltpu.make_async_copy(ybuf_ref.at[tile % 2], yt_hbm.at[pl.ds(0, EXPERT_TILE)], ysems.at[tile % 2])

    @pl.when(i == 0)
    def _():
        xbuf_ref[...] = jnp.zeros_like(xbuf_ref)
        for cp in weight_copies(te_ref[0], 0):
            cp.start(priority=1)
        for_rows(i, in_copy, lambda cp: cp.start())

        @pl.when(n_act > 1)
        def _():
            for_rows(i + 1, in_copy, lambda cp: cp.start())

    last = pl.num_programs(0) - 1
    nxt_active = i + 2 < n_act
    nxt_full = nxt_active & (vr_ref[jnp.minimum(i + 2, last)] == EXPERT_TILE)
    prev_active = (i >= 1) & (i - 1 < n_act)
    prev_full = prev_active & (vr_ref[jnp.maximum(i - 1, 0)] == EXPERT_TILE)

    @pl.when((i >= 2) & (i - 2 < n_act))
    def _():
        wait_rows(i - 2, out_copy, out_tile(i - 2))

    @pl.when(nxt_active & jnp.logical_not(nxt_full))
    def _():
        for_rows(i + 2, in_copy, lambda cp: cp.start())

    @pl.when(prev_active & jnp.logical_not(active & prev_full))
    def _():
        for_rows(i - 1, out_copy, lambda cp: cp.start())

    @pl.when(active & changed)
    def _():
        slot = sl_ref[i]
        nxt = nx_ref[i]

        @pl.when(nxt >= 0)
        def _():
            for cp in weight_copies(nxt, 1 - slot):
                cp.start(priority=1)

        for cp in weight_copies(te_ref[i], slot):
            cp.wait()
        wgbf_ref[...] = wgf_ref[slot].astype(BF16)
        wubf_ref[...] = wuf_ref[slot].astype(BF16)
        wdbf_ref[...] = wdf_ref[slot].astype(BF16)

    def compute(fetch_next, send_prev):
        def row_copies(g):
            for r in range(g * (EXPERT_TILE // 4), (g + 1) * (EXPERT_TILE // 4)):
                if fetch_next:
                    in_copy(i + 2, r).start()
                if send_prev:
                    out_copy(i - 1, r).start()

        wait_rows(i, in_copy, in_tile(i))
        x = _from_token_major(xbuf_ref[i % X_SLOTS])
        row_copies(0)
        hg = _dot(x, wgbf_ref[...])
        row_copies(1)
        hu = _dot(x, wubf_ref[...])
        row_copies(2)
        act = (hg * jax.nn.sigmoid(hg) * hu).astype(BF16)
        y = _dot(act, wdbf_ref[...])
        row_copies(3)
        ybuf_ref[i % 2] = _to_token_major(y.astype(BF16))

    for fetch_next in (False, True):
        for send_prev in (False, True):
            @pl.when(active & (nxt_full == fetch_next) & (prev_full == send_prev))
            def _(fetch_next=fetch_next, send_prev=send_prev):
                compute(fetch_next, send_prev)

    @pl.when(i == last)
    def _():
        @pl.when(active)
        def _():
            for_rows(i, out_copy, lambda cp: cp.start())
            wait_rows(i, out_copy, out_tile(i))

        @pl.when(prev_active)
        def _():
            wait_rows(i - 1, out_copy, out_tile(i - 1))


def _experts(tile_e, next_e, slot, valid, row_slot, n_act, hp, w_gate, w_up, w_down, n_tiles):
    t = hp.shape[0]
    assert t & (t - 1) == 0, "the row map packs slot * T + token with T a power of two"
    any_spec = pl.BlockSpec(memory_space=pl.ANY)
    tile_buf = lambda slots: pltpu.VMEM((slots, EXPERT_TILE, ROW_TILES, LANES), BF16)
    return pl.pallas_call(
        functools.partial(_experts_kernel, t_total=t),
        grid_spec=pltpu.PrefetchScalarGridSpec(
            num_scalar_prefetch=6,
            grid=(n_tiles,),
            in_specs=[any_spec, any_spec, any_spec, any_spec],
            out_specs=any_spec,
            scratch_shapes=[pltpu.VMEM((2, D_MODEL, D_FF), F32),
                            pltpu.VMEM((2, D_MODEL, D_FF), F32),
                            pltpu.VMEM((2, D_FF, D_MODEL), F32),
                            pltpu.VMEM((D_MODEL, D_FF), BF16),
                            pltpu.VMEM((D_MODEL, D_FF), BF16),
                            pltpu.VMEM((D_FF, D_MODEL), BF16),
                            tile_buf(X_SLOTS), tile_buf(2),
                            pltpu.SemaphoreType.DMA((2, 3)),
                            pltpu.SemaphoreType.DMA((X_SLOTS,)),
                            pltpu.SemaphoreType.DMA((2,))]),
        out_shape=jax.ShapeDtypeStruct((2 * t, ROW_TILES, LANES), BF16),
        compiler_params=_cp(("arbitrary",), 48, has_side_effects=True),
        name="experts",
    )(tile_e, next_e, slot, valid, row_slot, n_act, hp, w_gate, w_up, w_down)


def _combine_kernel(y0_ref, y1_ref, x_ref, wt_ref, p_ref, nw_ref, wg_ref, wp_ref, fw_ref, o_ref):
    wt = wt_ref[...]
    y0 = _from_token_major(y0_ref[...]).astype(F32)
    y1 = _from_token_major(y1_ref[...]).astype(F32)
    x2 = x_ref[...] + wt[:, 0:1] * y0 + wt[:, 1:2] * y1
    hn = _rms(x2, nw_ref[...]).astype(BF16)
    pg = jax.nn.sigmoid(_dot(hn, wg_ref[...]))
    x3 = x2 + pg * _dot(p_ref[...].astype(BF16), wp_ref[...])
    o_ref[...] = _rms(x3, fw_ref[...])


def _combine_ple(yt, x1, wts_t, p, ple_norm_w, w_gate, w_proj, final_w, tm=512):
    t = x1.shape[0]
    nb = t // tm
    row = lambda i: (i, 0)
    fix = lambda i: (0, 0)
    return pl.pallas_call(
        _combine_kernel,
        grid=(nb,),
        in_specs=[pl.BlockSpec((tm, ROW_TILES, LANES), lambda i: (i, 0, 0)),
                  pl.BlockSpec((tm, ROW_TILES, LANES), lambda i: (nb + i, 0, 0)),
                  pl.BlockSpec((tm, D_MODEL), row),
                  pl.BlockSpec((tm, SUBLANES), row),
                  pl.BlockSpec((tm, PLE_DIM), row),
                  pl.BlockSpec((1, D_MODEL), fix),
                  pl.BlockSpec((D_MODEL, D_MODEL), fix),
                  pl.BlockSpec((PLE_DIM, D_MODEL), fix),
                  pl.BlockSpec((1, D_MODEL), fix)],
        out_specs=pl.BlockSpec((tm, D_MODEL), row),
        out_shape=jax.ShapeDtypeStruct((t, D_MODEL), F32),
        compiler_params=_cp(("arbitrary",), 56),
        name="combine_ple",
    )(yt, yt, x1, wts_t, p, ple_norm_w, w_gate, w_proj, final_w)


def _tile_table(counts, n_tiles):
    tiles = (counts.astype(I32) + (EXPERT_TILE - 1)) // EXPERT_TILE
    ends = jnp.cumsum(tiles)
    n_act = ends[-1]
    idx = jnp.minimum(jnp.arange(n_tiles, dtype=I32), n_act - 1)
    tile_e = jnp.sum((idx[:, None] >= ends[None, :]).astype(I32), axis=1).astype(I32)
    run_end = ends[tile_e]
    next_e = jnp.where(run_end < n_act, tile_e[jnp.minimum(run_end, n_tiles - 1)], -1).astype(I32)
    new_run = jnp.concatenate([jnp.ones((1,), I32), (tile_e[1:] != tile_e[:-1]).astype(I32)])
    slot = ((jnp.cumsum(new_run) - 1) % 2).astype(I32)
    arange = jnp.arange(n_tiles, dtype=I32)
    last = arange == run_end - 1
    valid = jnp.where(last, counts.astype(I32)[tile_e] - (tiles[tile_e] - 1) * EXPERT_TILE, EXPERT_TILE)
    valid = jnp.where(arange < n_act, valid, 0).astype(I32)
    return tile_e, next_e, slot, valid, n_act.reshape(1).astype(I32)


def _block(x, p, norm_mix_w, w_in, b_merge, w_alpha_up, b_alpha_up, gla_norm_w, w_gla_out,
           conv_w, conv_b, w_conv_out, w_mix_out, norm_ffn_w, w_router_group, b_router_group,
           w_router_expert, b_router_expert, w_e_gate, w_e_up, w_e_down, ple_norm_w,
           w_ple_gate, w_ple_proj, final_norm_w):
    t = x.shape[0]
    n_tiles = (2 * t) // EXPERT_TILE + N_EXPERTS
    n_rows = n_tiles * EXPERT_TILE

    w_up = jnp.pad(w_alpha_up, ((0, LANES - GLA_GATE_RANK), (0, 0))).astype(BF16)
    w_up = w_up.reshape(LANES, GLA_HEADS, GLA_DK).transpose(1, 0, 2)
    b_up = b_alpha_up.reshape(GLA_HEADS, 1, GLA_DK)
    gnw = gla_norm_w.reshape(GLA_HEADS, 1, GLA_DV)
    wr_t = jnp.zeros((ROUTER_ROWS, D_MODEL), F32)
    wr_t = wr_t.at[0:N_GROUPS].set(w_router_group.T)
    wr_t = wr_t.at[EXPERT_ROW0:EXPERT_ROW0 + N_EXPERTS].set(w_router_expert.T)
    br = jnp.zeros((ROUTER_ROWS, 1), F32)
    br = br.at[0:N_GROUPS, 0].set(b_router_group)
    br = br.at[EXPERT_ROW0:EXPERT_ROW0 + N_EXPERTS, 0].set(b_router_expert)

    w_in_t = w_in.T
    h, a_low = _norm_in(x, norm_mix_w.reshape(1, D_MODEL), w_in_t)
    qkvg = _proj_qkvg(h, w_in_t)
    ob = _proj_conv(h, w_in_t, conv_w, conv_b.reshape(1, D_MODEL))
    gates = _proj_gates(h, w_in_t, b_merge.reshape(1, 2 * D_MODEL))
    oa = _gla(qkvg, a_low, w_up, b_up, gnw)
    mixed = _merge(oa, ob, w_gla_out, w_conv_out, gates)
    x1 = _mix(mixed, w_mix_out, x)

    hp, ids, wts = _route(x1, norm_ffn_w.reshape(1, D_MODEL), wr_t, br)
    dest, counts = _plan(ids)
    dest_flat = dest[1, 0:2].reshape(2 * t)
    tile_e, next_e, slot, valid, n_act = _tile_table(counts[:, 0], n_tiles)
    row_slot = _invert(dest_flat, n_rows)
    yt = _experts(tile_e, next_e, slot, valid, row_slot, n_act, hp, w_e_gate, w_e_up, w_e_down, n_tiles)
    return _combine_ple(yt, x1, wts.T, p,
                        ple_norm_w.reshape(1, D_MODEL), w_ple_gate.astype(BF16),
                        w_ple_proj.astype(BF16), final_norm_w.reshape(1, D_MODEL))


def kernel(x, p, norm_mix_w, w_in, b_merge, w_alpha_up, b_alpha_up, gla_norm_w, w_gla_out, conv_w, conv_b, w_conv_out, w_mix_out, norm_ffn_w, w_router_group, b_router_group, w_router_expert, b_router_expert, w_e_gate, w_e_up, w_e_down, ple_norm_w, w_ple_gate, w_ple_proj, final_norm_w):
    depth, batch = p.shape[0], x.shape[0]
    assert depth == 1 and batch == 1, "kernel is specialised to one layer and one sequence"
    out = _block(x[0], p[0, 0], norm_mix_w[0], w_in[0], b_merge[0], w_alpha_up[0], b_alpha_up[0],
                 gla_norm_w[0], w_gla_out[0], conv_w[0], conv_b[0], w_conv_out[0], w_mix_out[0],
                 norm_ffn_w[0], w_router_group[0], b_router_group[0], w_router_expert[0],
                 b_router_expert[0], w_e_gate[0], w_e_up[0], w_e_down[0], ple_norm_w[0],
                 w_ple_gate[0], w_ple_proj[0], final_norm_w)
    return out[None]
```

```python
import functools

import jax
import jax.numpy as jnp
from jax import lax
from jax.experimental import pallas as pl
from jax.experimental.pallas import tpu as pltpu

F32 = jnp.float32
BF16 = jnp.bfloat16
I32 = jnp.int32

D_MODEL = 2048
PLE_DIM = 256
EPS = 1e-6
LOG2_E = 1.4426950408889634
GLA_HEADS = 4
GLA_DK = 256
GLA_DV = 512
GLA_KEY = GLA_HEADS * GLA_DK
GLA_VAL = GLA_HEADS * GLA_DV
GLA_GATE_RANK = 16
GLA_GATE_NORM = 16.0
GLA_CHUNK = 64
CONV_K = 3
N_GROUPS = 4
EXPERTS_PER_GROUP = 8
N_EXPERTS = N_GROUPS * EXPERTS_PER_GROUP
D_FF = 512

QKVG_COLS = 2 * GLA_KEY + 2 * GLA_VAL
CONV_COL0 = QKVG_COLS + GLA_GATE_RANK

LANES = 128
SUBLANES = 8
MXU_COLS = 256
ROW_CHUNK = 256
ROW_TILES = D_MODEL // LANES

EXPERT_TILE = 256
DMA_UNROLL = 8
X_SLOTS = 3
W_SLOTS = 3
ROUTER_ROWS = 64
EXPERT_ROW0 = 8

MIB = 1024 * 1024


def _cp(sem, vmem_mib, **kw):
    return pltpu.CompilerParams(dimension_semantics=sem, vmem_limit_bytes=int(vmem_mib * MIB), **kw)


def _rms(x, w):
    return x * lax.rsqrt(jnp.mean(x * x, axis=-1, keepdims=True) + EPS) * w


def _dot(a, b):
    return jnp.dot(a, b, preferred_element_type=F32)


def _dot_nt(a, b):
    return lax.dot_general(a, b, (((1,), (1,)), ((), ())), preferred_element_type=F32)


def _dot_tn(a, b):
    return lax.dot_general(a, b, (((0,), (0,)), ((), ())), preferred_element_type=F32)


def _split_bf16(x):
    hi = x.astype(BF16)
    lo = (x - hi.astype(F32)).astype(BF16)
    return hi, lo


def _to_token_major(val):
    return val.reshape(val.shape[0], ROW_TILES, LANES)


def _from_token_major(val):
    return val.reshape(val.shape[0], D_MODEL)


def _norm_in_kernel(x_ref, w_ref, wal_ref, h_ref, al_ref, walbf_ref):
    @pl.when(pl.program_id(0) == 0)
    def _():
        walbf_ref[...] = wal_ref[...].astype(BF16)

    h = _rms(x_ref[...], w_ref[...]).astype(BF16)
    h_ref[...] = h
    al_ref[...] = _dot_nt(h, walbf_ref[...])


def _norm_in(x, w, w_in_t, tm=512):
    t = x.shape[0]
    return pl.pallas_call(
        _norm_in_kernel,
        grid=(t // tm,),
        in_specs=[pl.BlockSpec((tm, D_MODEL), lambda i: (i, 0)),
                  pl.BlockSpec((1, D_MODEL), lambda i: (0, 0)),
                  pl.BlockSpec((LANES, D_MODEL), lambda i: (QKVG_COLS // LANES, 0))],
        out_specs=[pl.BlockSpec((tm, D_MODEL), lambda i: (i, 0)),
                   pl.BlockSpec((tm, LANES), lambda i: (i, 0))],
        out_shape=[jax.ShapeDtypeStruct((t, D_MODEL), BF16),
                   jax.ShapeDtypeStruct((t, LANES), F32)],
        scratch_shapes=[pltpu.VMEM((LANES, D_MODEL), BF16)],
        compiler_params=_cp(("arbitrary",), 32),
        name="norm_in",
    )(x, w, w_in_t)


def _w_rows_spec(row0, tn):
    assert row0 % SUBLANES == 0 and tn % SUBLANES == 0
    return pl.BlockSpec((pl.Element(tn), pl.Element(D_MODEL)),
                        lambda n, m: (pl.multiple_of(row0 + n * tn, SUBLANES), 0))


def _proj_qkvg_kernel(h_ref, w_ref, o_ref, wbf_ref):
    @pl.when(pl.program_id(1) == 0)
    def _():
        wbf_ref[...] = w_ref[...].astype(BF16)

    o_ref[...] = _dot_nt(h_ref[...], wbf_ref[...]).astype(o_ref.dtype)


def _proj_qkvg(h, w_in_t, tm=2048, tn=1024):
    t = h.shape[0]
    return pl.pallas_call(
        _proj_qkvg_kernel,
        grid=(QKVG_COLS // tn, t // tm),
        in_specs=[pl.BlockSpec((tm, D_MODEL), lambda n, m: (m, 0)),
                  pl.BlockSpec((tn, D_MODEL), lambda n, m: (n, 0))],
        out_specs=pl.BlockSpec((tm, tn), lambda n, m: (m, n)),
        out_shape=jax.ShapeDtypeStruct((t, QKVG_COLS), BF16),
        scratch_shapes=[pltpu.VMEM((tn, D_MODEL), BF16)],
        compiler_params=_cp(("arbitrary", "arbitrary"), 56),
        name="proj_qkvg",
    )(h, w_in_t)


def _proj_conv_kernel(h_ref, wb_ref, wc_ref, wx_ref, cw_ref, cb_ref, o_ref,
                      wbbf_ref, wcbf_ref, wxbf_ref, prev_ref):
    m = pl.program_id(1)

    @pl.when(m == 0)
    def _():
        prev_ref[...] = jnp.zeros_like(prev_ref)
        wbbf_ref[...] = wb_ref[...].astype(BF16)
        wcbf_ref[...] = wc_ref[...].astype(BF16)
        wxbf_ref[...] = wx_ref[...].astype(BF16)

    h = h_ref[...]
    tm = h.shape[0]
    row = lax.broadcasted_iota(I32, (tm, MXU_COLS), 0)
    for c in range(0, o_ref.shape[1], MXU_COLS):
        cols = slice(c, c + MXU_COLS)
        b = _dot_nt(h, wbbf_ref[cols, :])
        s = _dot_nt(h, wcbf_ref[cols, :]) * _dot_nt(h, wxbf_ref[cols, :])
        prev = prev_ref[:, cols]
        p1 = prev[SUBLANES - 1:SUBLANES, :]
        p2 = prev[SUBLANES - 2:SUBLANES - 1, :]
        s1 = jnp.where(row == 0, p1, pltpu.roll(s, 1, 0))
        s2 = jnp.where(row == 0, p2, jnp.where(row == 1, p1, pltpu.roll(s, 2, 0)))
        cw = cw_ref[:, cols]
        u = cw[2:3, :] * s + cw[1:2, :] * s1 + cw[0:1, :] * s2 + cb_ref[:, cols]
        o_ref[:, cols] = (b * u).astype(o_ref.dtype)
        prev_ref[:, cols] = s[tm - SUBLANES:, :]


def _proj_conv(h, w_in_t, conv_w, conv_b, tm=1024, tn=512):
    t = h.shape[0]
    return pl.pallas_call(
        _proj_conv_kernel,
        grid=(D_MODEL // tn, t // tm),
        in_specs=[pl.BlockSpec((tm, D_MODEL), lambda n, m: (m, 0))] +
                 [_w_rows_spec(CONV_COL0 + seg * D_MODEL, tn) for seg in range(3)] +
                 [pl.BlockSpec((CONV_K, tn), lambda n, m: (0, n)),
                  pl.BlockSpec((1, tn), lambda n, m: (0, n))],
        out_specs=pl.BlockSpec((tm, tn), lambda n, m: (m, n)),
        out_shape=jax.ShapeDtypeStruct((t, D_MODEL), BF16),
        scratch_shapes=[pltpu.VMEM((tn, D_MODEL), BF16), pltpu.VMEM((tn, D_MODEL), BF16),
                        pltpu.VMEM((tn, D_MODEL), BF16), pltpu.VMEM((SUBLANES, tn), F32)],
        compiler_params=_cp(("arbitrary", "arbitrary"), 48),
        name="proj_conv",
    )(h, w_in_t, w_in_t, w_in_t, conv_w, conv_b)


def _proj_gates_kernel(h_ref, w_ref, b_ref, o_ref, wbf_ref):
    @pl.when(pl.program_id(1) == 0)
    def _():
        wbf_ref[...] = w_ref[...].astype(BF16)

    for c in range(0, o_ref.shape[1], MXU_COLS):
        cols = slice(c, c + MXU_COLS)
        for r in range(0, o_ref.shape[0], ROW_CHUNK):
            rows = slice(r, r + ROW_CHUNK)
            z = _dot_nt(h_ref[rows, :], wbf_ref[cols, :]) + b_ref[:, cols]
            o_ref[rows, cols] = jax.nn.sigmoid(z).astype(o_ref.dtype)


def _proj_gates(h, w_in_t, b_merge, tm=2048, tn=1024):
    t = h.shape[0]
    return pl.pallas_call(
        _proj_gates_kernel,
        grid=(2 * D_MODEL // tn, t // tm),
        in_specs=[pl.BlockSpec((tm, D_MODEL), lambda n, m: (m, 0)),
                  _w_rows_spec(CONV_COL0 + 3 * D_MODEL, tn),
                  pl.BlockSpec((1, tn), lambda n, m: (0, n))],
        out_specs=pl.BlockSpec((tm, tn), lambda n, m: (m, n)),
        out_shape=jax.ShapeDtypeStruct((t, 2 * D_MODEL), BF16),
        scratch_shapes=[pltpu.VMEM((tn, D_MODEL), BF16)],
        compiler_params=_cp(("arbitrary", "arbitrary"), 56),
        name="proj_gates",
    )(h, w_in_t, b_merge)


def _gla_kernel(q_ref, k_ref, v_ref, g_ref, al_ref, wup_ref, bup_ref, nw_ref, o_ref,
                st_ref, b_ref, bl_ref, qd_ref, ki_ref, kd_ref, oi_ref, u_ref, *, n_chunks):
    c_len = GLA_CHUNK

    @pl.when(pl.program_id(1) == 0)
    def _():
        st_ref[...] = jnp.zeros_like(st_ref)

    row = lax.broadcasted_iota(I32, (c_len, c_len), 0)
    col = lax.broadcasted_iota(I32, (c_len, c_len), 1)
    causal = col <= row
    tril = jnp.where(causal, 1.0, 0.0).astype(BF16)

    def decays(h):
        kc = slice(h * GLA_DK, (h + 1) * GLA_DK)
        z = _dot(al_ref[...].astype(BF16), wup_ref[h]) + bup_ref[h]
        la = (jnp.minimum(z, 0.0) - jnp.log1p(jnp.exp(-jnp.abs(z)))) * (LOG2_E / GLA_GATE_NORM)
        la_hi, la_lo = _split_bf16(la)
        for c in range(n_chunks):
            r0 = c * c_len
            b = _dot(tril, la_hi[r0:r0 + c_len]) + _dot(tril, la_lo[r0:r0 + c_len])
            b_ref[pl.ds(r0, c_len), kc] = b
            bl_ref[pl.ds(r0, c_len), kc] = jnp.broadcast_to(b[c_len - 1:c_len, :], b.shape)
        b = b_ref[:, kc]
        q = q_ref[:, kc].astype(F32)
        k = k_ref[:, kc].astype(F32)
        qd_ref[:, kc] = (q * jnp.exp2(b)).astype(BF16)
        ki_ref[:, kc] = (k * jnp.exp2(-b)).astype(BF16)
        kd_ref[:, kc] = (k * jnp.exp2(bl_ref[:, kc] - b)).astype(BF16)

    def local_products(h):
        kc = slice(h * GLA_DK, (h + 1) * GLA_DK)
        vc = slice(h * GLA_DV, (h + 1) * GLA_DV)
        for c in range(n_chunks):
            sl = pl.ds(c * c_len, c_len)
            v = v_ref[sl, vc]
            att = jnp.where(causal, _dot_nt(qd_ref[sl, kc], ki_ref[sl, kc]), 0.0).astype(BF16)
            oi_ref[sl, vc] = _dot(att, v)
            u_ref[h, c] = _dot_tn(v, kd_ref[sl, kc])

    def recurrence(h):
        kc = slice(h * GLA_DK, (h + 1) * GLA_DK)
        vc = slice(h * GLA_DV, (h + 1) * GLA_DV)
        for c in range(n_chunks):
            sl = pl.ds(c * c_len, c_len)
            st = st_ref[h]
            oi_ref[sl, vc] = oi_ref[sl, vc] + _dot_nt(qd_ref[sl, kc], st.astype(BF16))
            st_ref[h] = st * jnp.exp2(bl_ref[pl.ds(c * c_len, 1), kc]) + u_ref[h, c]
        o = oi_ref[:, vc]
        o = o * lax.rsqrt(jnp.mean(o * o, axis=-1, keepdims=True) + EPS * GLA_DK) * nw_ref[h]
        g = g_ref[:, vc].astype(F32)
        o_ref[:, vc] = (o * (g * jax.nn.sigmoid(g))).astype(o_ref.dtype)

    decays(0)
    local_products(0)
    decays(1)
    recurrence(0)
    local_products(1)
    recurrence(1)


def _gla(qkvg, a_low, w_up, b_up, norm_w, tb=1024):
    t = qkvg.shape[0]
    hp = 2
    dk, dv = hp * GLA_DK, hp * GLA_DV
    kq = GLA_KEY // dk
    kv = 2 * GLA_KEY // dv
    kg = kv + GLA_VAL // dv
    n_chunks = tb // GLA_CHUNK
    kern = functools.partial(_gla_kernel, n_chunks=n_chunks)
    return pl.pallas_call(
        kern,
        grid=(GLA_HEADS // hp, t // tb),
        in_specs=[pl.BlockSpec((tb, dk), lambda h, i: (i, h)),
                  pl.BlockSpec((tb, dk), lambda h, i: (i, kq + h)),
                  pl.BlockSpec((tb, dv), lambda h, i: (i, kv + h)),
                  pl.BlockSpec((tb, dv), lambda h, i: (i, kg + h)),
                  pl.BlockSpec((tb, LANES), lambda h, i: (i, 0)),
                  pl.BlockSpec((hp, LANES, GLA_DK), lambda h, i: (h, 0, 0)),
                  pl.BlockSpec((hp, 1, GLA_DK), lambda h, i: (h, 0, 0)),
                  pl.BlockSpec((hp, 1, GLA_DV), lambda h, i: (h, 0, 0))],
        out_specs=pl.BlockSpec((tb, dv), lambda h, i: (i, h)),
        out_shape=jax.ShapeDtypeStruct((t, GLA_VAL), BF16),
        scratch_shapes=[pltpu.VMEM((hp, GLA_DV, GLA_DK), F32),
                        pltpu.VMEM((tb, dk), F32), pltpu.VMEM((tb, dk), F32),
                        pltpu.VMEM((tb, dk), BF16), pltpu.VMEM((tb, dk), BF16),
                        pltpu.VMEM((tb, dk), BF16),
                        pltpu.VMEM((tb, dv), F32),
                        pltpu.VMEM((hp, n_chunks, GLA_DV, GLA_DK), F32)],
        compiler_params=_cp(("arbitrary", "arbitrary"), 56),
        name="gla",
    )(qkvg, qkvg, qkvg, qkvg, a_low, w_up, b_up, norm_w)


def _merge_kernel(oa_ref, ob_ref, wa_ref, wb_ref, g0_ref, g1_ref, o_ref, wabf_ref, wbbf_ref):
    @pl.when(pl.program_id(1) == 0)
    def _():
        wabf_ref[...] = wa_ref[...].astype(BF16)
        wbbf_ref[...] = wb_ref[...].astype(BF16)

    oa = oa_ref[...]
    ob = ob_ref[...]
    for c in range(0, o_ref.shape[1], MXU_COLS):
        cols = slice(c, c + MXU_COLS)
        a = _dot(oa, wabf_ref[:, cols])
        b = _dot(ob, wbbf_ref[:, cols])
        o_ref[:, cols] = (g0_ref[:, cols].astype(F32) * a + g1_ref[:, cols].astype(F32) * b).astype(o_ref.dtype)


def _merge(oa, ob, w_a, w_b, gates, tm=1024, tn=512):
    t = oa.shape[0]
    nb = D_MODEL // tn
    return pl.pallas_call(
        _merge_kernel,
        grid=(nb, t // tm),
        in_specs=[pl.BlockSpec((tm, D_MODEL), lambda n, m: (m, 0)),
                  pl.BlockSpec((tm, D_MODEL), lambda n, m: (m, 0)),
                  pl.BlockSpec((D_MODEL, tn), lambda n, m: (0, n)),
                  pl.BlockSpec((D_MODEL, tn), lambda n, m: (0, n)),
                  pl.BlockSpec((tm, tn), lambda n, m: (m, n)),
                  pl.BlockSpec((tm, tn), lambda n, m: (m, nb + n))],
        out_specs=pl.BlockSpec((tm, tn), lambda n, m: (m, n)),
        out_shape=jax.ShapeDtypeStruct((t, D_MODEL), BF16),
        scratch_shapes=[pltpu.VMEM((D_MODEL, tn), BF16), pltpu.VMEM((D_MODEL, tn), BF16)],
        compiler_params=_cp(("arbitrary", "arbitrary"), 56),
        name="merge",
    )(oa, ob, w_a, w_b, gates, gates)


def _mix_kernel(a_ref, w_ref, x_ref, o_ref, wbf_ref):
    @pl.when(pl.program_id(1) == 0)
    def _():
        wbf_ref[...] = w_ref[...].astype(BF16)

    a = a_ref[...]
    for c in range(0, o_ref.shape[1], MXU_COLS):
        cols = slice(c, c + MXU_COLS)
        o_ref[:, cols] = x_ref[:, cols] + _dot(a, wbf_ref[:, cols])


def _mix(mixed, w, x, tm=1024, tn=1024):
    t = x.shape[0]
    return pl.pallas_call(
        _mix_kernel,
        grid=(D_MODEL // tn, t // tm),
        in_specs=[pl.BlockSpec((tm, D_MODEL), lambda n, m: (m, 0)),
                  pl.BlockSpec((D_MODEL, tn), lambda n, m: (0, n)),
                  pl.BlockSpec((tm, tn), lambda n, m: (m, n))],
        out_specs=pl.BlockSpec((tm, tn), lambda n, m: (m, n)),
        out_shape=jax.ShapeDtypeStruct((t, D_MODEL), F32),
        scratch_shapes=[pltpu.VMEM((D_MODEL, tn), BF16)],
        compiler_params=_cp(("arbitrary", "arbitrary"), 52),
        name="mix",
    )(mixed, w, x)


def _route_kernel(x_ref, nw_ref, wr_ref, br_ref, hp_ref, ids_ref, wts_ref):
    h = _rms(x_ref[...], nw_ref[...])
    hp_ref[...] = _to_token_major(h.astype(BF16))
    h_hi, h_lo = _split_bf16(h)
    w_hi, w_lo = _split_bf16(wr_ref[...])
    logits = _dot_nt(w_hi, h_hi) + _dot_nt(w_hi, h_lo) + _dot_nt(w_lo, h_hi) + br_ref[...]
    tm = logits.shape[1]

    best = logits[0:1, :]
    gidx = jnp.zeros((1, tm), I32)
    for i in range(1, N_GROUPS):
        li = logits[i:i + 1, :]
        take = li > best
        best = jnp.where(take, li, best)
        gidx = jnp.where(take, i, gidx)
    gsum = jnp.zeros((1, tm), F32)
    for i in range(N_GROUPS):
        gsum = gsum + jnp.exp(logits[i:i + 1, :] - best)
    g_p = 1.0 / gsum

    sel = logits[EXPERT_ROW0:EXPERT_ROW0 + EXPERTS_PER_GROUP, :]
    for g in range(1, N_GROUPS):
        r0 = EXPERT_ROW0 + g * EXPERTS_PER_GROUP
        sel = jnp.where(gidx == g, logits[r0:r0 + EXPERTS_PER_GROUP, :], sel)
    eio = lax.broadcasted_iota(I32, sel.shape, 0)
    m1 = jnp.max(sel, axis=0, keepdims=True)
    i1 = jnp.min(jnp.where(sel == m1, eio, EXPERTS_PER_GROUP), axis=0, keepdims=True)
    rest = jnp.where(eio == i1, -jnp.inf, sel)
    m2 = jnp.max(rest, axis=0, keepdims=True)
    i2 = jnp.min(jnp.where(rest == m2, eio, EXPERTS_PER_GROUP), axis=0, keepdims=True)
    p2 = jnp.exp(m2 - m1)
    w1 = g_p / (1.0 + p2)
    w2 = g_p * p2 / (1.0 + p2)
    e1 = gidx * EXPERTS_PER_GROUP + i1
    e2 = gidx * EXPERTS_PER_GROUP + i2
    rio = lax.broadcasted_iota(I32, (SUBLANES, tm), 0)
    ids_ref[...] = jnp.where(rio == 0, e1, jnp.where(rio == 1, e2, 0))
    wts_ref[...] = jnp.where(rio == 0, w1, jnp.where(rio == 1, w2, 0.0))


def _route(x1, norm_w, wr_t, br, tm=512):
    t = x1.shape[0]
    return pl.pallas_call(
        _route_kernel,
        grid=(t // tm,),
        in_specs=[pl.BlockSpec((tm, D_MODEL), lambda i: (i, 0)),
                  pl.BlockSpec((1, D_MODEL), lambda i: (0, 0)),
                  pl.BlockSpec((ROUTER_ROWS, D_MODEL), lambda i: (0, 0)),
                  pl.BlockSpec((ROUTER_ROWS, 1), lambda i: (0, 0))],
        out_specs=[pl.BlockSpec((tm, ROW_TILES, LANES), lambda i: (i, 0, 0)),
                   pl.BlockSpec((SUBLANES, tm), lambda i: (0, i)),
                   pl.BlockSpec((SUBLANES, tm), lambda i: (0, i))],
        out_shape=[jax.ShapeDtypeStruct((t, ROW_TILES, LANES), BF16),
                   jax.ShapeDtypeStruct((SUBLANES, t), I32),
                   jax.ShapeDtypeStruct((SUBLANES, t), F32)],
        compiler_params=_cp(("arbitrary",), 32),
        name="route",
    )(x1, norm_w, wr_t, br)


def _plan_kernel(ids_ref, dest_ref, cnt_ref, base_ref):
    phase = pl.program_id(0)
    step = pl.program_id(1)
    tm = ids_ref.shape[1]
    eio = lax.broadcasted_iota(I32, (N_EXPERTS, tm), 0)
    ids = ids_ref[...]
    oh = [jnp.where(eio == ids[k:k + 1, :], 1.0, 0.0) for k in range(2)]

    @pl.when((phase == 0) & (step == 0))
    def _():
        base_ref[...] = jnp.zeros_like(base_ref)

    @pl.when(phase == 0)
    def _():
        cnt = jnp.sum(oh[0] + oh[1], axis=1, keepdims=True)
        base_ref[...] = base_ref[...] + cnt
        dest_ref[0] = jnp.zeros(dest_ref.shape[1:], I32)
        cnt_ref[...] = base_ref[...]

    @pl.when((phase == 1) & (step == 0))
    def _():
        tiles = jnp.floor((base_ref[...] + (EXPERT_TILE - 1)) * (1.0 / EXPERT_TILE))
        r = lax.broadcasted_iota(I32, (N_EXPERTS, N_EXPERTS), 0)
        c = lax.broadcasted_iota(I32, (N_EXPERTS, N_EXPERTS), 1)
        lower = jnp.where(c < r, 1.0, 0.0).astype(BF16)
        base_ref[...] = _dot(lower, tiles.astype(BF16)) * float(EXPERT_TILE)

    @pl.when(phase == 1)
    def _():
        r = lax.broadcasted_iota(I32, (tm, tm), 0)
        c = lax.broadcasted_iota(I32, (tm, tm), 1)
        upper = jnp.where(r <= c, 1.0, 0.0).astype(BF16)
        base = base_ref[...][:, 0:1]
        rows = []
        for k in range(2):
            cum = _dot(oh[k].astype(BF16), upper)
            rows.append(jnp.sum(oh[k] * (cum - 1.0 + base), axis=0, keepdims=True))
            base = base + cum[:, tm - 1:tm]
        base_ref[...] = jnp.broadcast_to(base, base_ref.shape)
        rio = lax.broadcasted_iota(I32, (SUBLANES, tm), 0)
        d0 = rows[0].astype(I32)
        d1 = rows[1].astype(I32)
        dest_ref[0] = jnp.where(rio == 0, d0, jnp.where(rio == 1, d1, 0))


def _plan(ids, tm=512):
    t = ids.shape[1]
    return pl.pallas_call(
        _plan_kernel,
        grid=(2, t // tm),
        in_specs=[pl.BlockSpec((SUBLANES, tm), lambda p, i: (0, i))],
        out_specs=[pl.BlockSpec((1, SUBLANES, tm), lambda p, i: (p, 0, i)),
                   pl.BlockSpec((N_EXPERTS, LANES), lambda p, i: (0, 0))],
        out_shape=[jax.ShapeDtypeStruct((2, SUBLANES, t), I32),
                   jax.ShapeDtypeStruct((N_EXPERTS, LANES), F32)],
        scratch_shapes=[pltpu.VMEM((N_EXPERTS, LANES), F32)],
        compiler_params=_cp(("arbitrary", "arbitrary"), 32),
        name="plan",
    )(ids)


def _invert_kernel(dest_ref, slot_ref, fill_ref, sem, *, n_slots):
    fill_ref[...] = jnp.full(fill_ref.shape, n_slots, I32)
    fill = pltpu.make_async_copy(fill_ref, slot_ref, sem)
    fill.start()
    fill.wait()

    def scatter(j, c):
        for u in range(DMA_UNROLL):
            q = j * DMA_UNROLL + u
            slot_ref[dest_ref[q]] = q
        return c

    lax.fori_loop(0, n_slots // DMA_UNROLL, scatter, 0)


def _invert(dest_flat, n_rows):
    n_slots = dest_flat.shape[0]
    assert n_slots % DMA_UNROLL == 0
    return pl.pallas_call(
        functools.partial(_invert_kernel, n_slots=n_slots),
        grid_spec=pltpu.PrefetchScalarGridSpec(
            num_scalar_prefetch=1,
            grid=(1,),
            in_specs=[],
            out_specs=pl.BlockSpec(memory_space=pltpu.SMEM),
            scratch_shapes=[pltpu.VMEM((n_rows,), I32), pltpu.SemaphoreType.DMA(())]),
        out_shape=jax.ShapeDtypeStruct((n_rows,), I32),
        compiler_params=_cp(("arbitrary",), 16),
        name="invert",
    )(dest_flat)


def _experts_kernel(te_ref, nx_ref, n2_ref, sl_ref, vr_ref, rs_ref, na_ref, hp_hbm, wg_hbm, wu_hbm, wd_hbm, yt_hbm,
                    wgf_ref, wuf_ref, wdf_ref, wgbf_ref, wubf_ref, wdbf_ref, xbuf_ref, ybuf_ref,
                    sems, xsems, ysems, *, t_total):
    i = pl.program_id(0)
    n_act = na_ref[0]
    active = i < n_act
    changed = (i == 0) | (te_ref[i] != te_ref[jnp.maximum(i - 1, 0)])

    def weight_copies(e, slot):
        return (pltpu.make_async_copy(wg_hbm.at[e], wgf_ref.at[slot], sems.at[slot, 0]),
                pltpu.make_async_copy(wu_hbm.at[e], wuf_ref.at[slot], sems.at[slot, 1]),
                pltpu.make_async_copy(wd_hbm.at[e], wdf_ref.at[slot], sems.at[slot, 2]))

    def in_copy(tile, r):
        tok = rs_ref[tile * EXPERT_TILE + r] & (t_total - 1)
        return pltpu.make_async_copy(hp_hbm.at[tok], xbuf_ref.at[tile % X_SLOTS, r], xsems.at[tile % X_SLOTS])

    def out_copy(tile, r):
        dst = yt_hbm.at[rs_ref[tile * EXPERT_TILE + r]]
        return pltpu.make_async_copy(ybuf_ref.at[tile % 2, r], dst, ysems.at[tile % 2])

    def for_rows(tile, make_copy, fn):
        n = vr_ref[tile]
        groups = lax.shift_right_logical(n, DMA_UNROLL.bit_length() - 1)

        def body(j, c):
            for u in range(DMA_UNROLL):
                fn(make_copy(tile, j * DMA_UNROLL + u))
            return c

        def tail(r, c):
            fn(make_copy(tile, r))
            return c

        lax.fori_loop(0, groups, body, 0)
        lax.fori_loop(groups * DMA_UNROLL, n, tail, 0)

    def wait_rows(tile, make_copy, whole_tile_copy):
        full = vr_ref[tile] == EXPERT_TILE

        @pl.when(full)
        def _():
            whole_tile_copy.wait()

        @pl.when(jnp.logical_not(full))
        def _():
            for_rows(tile, make_copy, lambda cp: cp.wait())

    def in_tile(tile):
        return pltpu.make_async_copy(hp_hbm.at[pl.ds(0, EXPERT_TILE)], xbuf_ref.at[tile % X_SLOTS],
                                     xsems.at[tile % X_SLOTS])

    def out_tile(tile):
        return pltpu.make_async_copy(ybuf_ref.at[tile % 2], yt_hbm.at[pl.ds(0, EXPERT_TILE)], ysems.at[tile % 2])

    @pl.when(i == 0)
    def _():
        xbuf_ref[...] = jnp.zeros_like(xbuf_ref)
        for cp in weight_copies(te_ref[0], 0):
            cp.start(priority=1)

        @pl.when(nx_ref[0] >= 0)
        def _():
            for cp in weight_copies(nx_ref[0], 1):
                cp.start(priority=1)
        for_rows(i, in_copy, lambda cp: cp.start())

        @pl.when(n_act > 1)
        def _():
            for_rows(i + 1, in_copy, lambda cp: cp.start())

    last = pl.num_programs(0) - 1
    nxt_active = i + 2 < n_act
    prev_active = (i >= 1) & (i - 1 < n_act)

    @pl.when((i >= 2) & (i - 2 < n_act))
    def _():
        wait_rows(i - 2, out_copy, out_tile(i - 2))

    @pl.when(prev_active & jnp.logical_not(active))
    def _():
        for_rows(i - 1, out_copy, lambda cp: cp.start())

    @pl.when(active & changed)
    def _():
        slot = sl_ref[i]
        nxt = n2_ref[i]

        @pl.when(nxt >= 0)
        def _():
            for cp in weight_copies(nxt, jnp.where(slot == 0, W_SLOTS - 1, slot - 1)):
                cp.start(priority=1)

        for cp in weight_copies(te_ref[i], slot):
            cp.wait()
        wgbf_ref[...] = wgf_ref[slot].astype(BF16)
        wubf_ref[...] = wuf_ref[slot].astype(BF16)
        wdbf_ref[...] = wdf_ref[slot].astype(BF16)

    def compute(fetch_next, send_prev):
        n_next = vr_ref[jnp.minimum(i + 2, last)]
        n_prev = vr_ref[jnp.maximum(i - 1, 0)]

        def row_copies(g):
            for r in range(g * (EXPERT_TILE // 4), (g + 1) * (EXPERT_TILE // 4)):
                if fetch_next:
                    @pl.when(r < n_next)
                    def _():
                        in_copy(i + 2, r).start()
                if send_prev:
                    @pl.when(r < n_prev)
                    def _():
                        out_copy(i - 1, r).start()

        wait_rows(i, in_copy, in_tile(i))
        x = _from_token_major(xbuf_ref[i % X_SLOTS])
        row_copies(0)
        hg = _dot(x, wgbf_ref[...])
        row_copies(1)
        hu = _dot(x, wubf_ref[...])
        row_copies(2)
        act = (hg * jax.nn.sigmoid(hg) * hu).astype(BF16)
        y = _dot(act, wdbf_ref[...])
        row_copies(3)
        ybuf_ref[i % 2] = _to_token_major(y.astype(BF16))

    for fetch_next in (False, True):
        for send_prev in (False, True):
            @pl.when(active & (nxt_active == fetch_next) & (prev_active == send_prev))
            def _(fetch_next=fetch_next, send_prev=send_prev):
                compute(fetch_next, send_prev)

    @pl.when(i == last)
    def _():
        @pl.when(active)
        def _():
            for_rows(i, out_copy, lambda cp: cp.start())
            wait_rows(i, out_copy, out_tile(i))

        @pl.when(prev_active)
        def _():
            wait_rows(i - 1, out_copy, out_tile(i - 1))


def _experts(tile_e, next_e, next2_e, slot, valid, row_slot, n_act, hp, w_gate, w_up, w_down, n_tiles):
    t = hp.shape[0]
    assert t & (t - 1) == 0, "the row map packs slot * T + token with T a power of two"
    any_spec = pl.BlockSpec(memory_space=pl.ANY)
    tile_buf = lambda slots: pltpu.VMEM((slots, EXPERT_TILE, ROW_TILES, LANES), BF16)
    return pl.pallas_call(
        functools.partial(_experts_kernel, t_total=t),
        grid_spec=pltpu.PrefetchScalarGridSpec(
            num_scalar_prefetch=7,
            grid=(n_tiles,),
            in_specs=[any_spec, any_spec, any_spec, any_spec],
            out_specs=any_spec,
            scratch_shapes=[pltpu.VMEM((W_SLOTS, D_MODEL, D_FF), F32),
                            pltpu.VMEM((W_SLOTS, D_MODEL, D_FF), F32),
                            pltpu.VMEM((W_SLOTS, D_FF, D_MODEL), F32),
                            pltpu.VMEM((D_MODEL, D_FF), BF16),
                            pltpu.VMEM((D_MODEL, D_FF), BF16),
                            pltpu.VMEM((D_FF, D_MODEL), BF16),
                            tile_buf(X_SLOTS), tile_buf(2),
                            pltpu.SemaphoreType.DMA((W_SLOTS, 3)),
                            pltpu.SemaphoreType.DMA((X_SLOTS,)),
                            pltpu.SemaphoreType.DMA((2,))]),
        out_shape=jax.ShapeDtypeStruct((2 * t, ROW_TILES, LANES), BF16),
        compiler_params=_cp(("arbitrary",), 58, has_side_effects=True),
        name="experts",
    )(tile_e, next_e, next2_e, slot, valid, row_slot, n_act, hp, w_gate, w_up, w_down)


def _combine_kernel(y0_ref, y1_ref, x_ref, wt_ref, p_ref, nw_ref, wg_ref, wp_ref, fw_ref, o_ref):
    wt = wt_ref[...]
    y0 = _from_token_major(y0_ref[...]).astype(F32)
    y1 = _from_token_major(y1_ref[...]).astype(F32)
    x2 = x_ref[...] + wt[:, 0:1] * y0 + wt[:, 1:2] * y1
    hn = _rms(x2, nw_ref[...]).astype(BF16)
    pg = jax.nn.sigmoid(_dot(hn, wg_ref[...]))
    x3 = x2 + pg * _dot(p_ref[...].astype(BF16), wp_ref[...])
    o_ref[...] = _rms(x3, fw_ref[...])


def _combine_ple(yt, x1, wts_t, p, ple_norm_w, w_gate, w_proj, final_w, tm=512):
    t = x1.shape[0]
    nb = t // tm
    row = lambda i: (i, 0)
    fix = lambda i: (0, 0)
    return pl.pallas_call(
        _combine_kernel,
        grid=(nb,),
        in_specs=[pl.BlockSpec((tm, ROW_TILES, LANES), lambda i: (i, 0, 0)),
                  pl.BlockSpec((tm, ROW_TILES, LANES), lambda i: (nb + i, 0, 0)),
                  pl.BlockSpec((tm, D_MODEL), row),
                  pl.BlockSpec((tm, SUBLANES), row),
                  pl.BlockSpec((tm, PLE_DIM), row),
                  pl.BlockSpec((1, D_MODEL), fix),
                  pl.BlockSpec((D_MODEL, D_MODEL), fix),
                  pl.BlockSpec((PLE_DIM, D_MODEL), fix),
                  pl.BlockSpec((1, D_MODEL), fix)],
        out_specs=pl.BlockSpec((tm, D_MODEL), row),
        out_shape=jax.ShapeDtypeStruct((t, D_MODEL), F32),
        compiler_params=_cp(("arbitrary",), 56),
        name="combine_ple",
    )(yt, yt, x1, wts_t, p, ple_norm_w, w_gate, w_proj, final_w)


def _tile_table(counts, n_tiles):
    tiles = (counts.astype(I32) + (EXPERT_TILE - 1)) // EXPERT_TILE
    ends = jnp.cumsum(tiles)
    n_act = ends[-1]
    idx = jnp.minimum(jnp.arange(n_tiles, dtype=I32), n_act - 1)
    tile_e = jnp.sum((idx[:, None] >= ends[None, :]).astype(I32), axis=1).astype(I32)
    run_end = ends[tile_e]
    next_e = jnp.where(run_end < n_act, tile_e[jnp.minimum(run_end, n_tiles - 1)], -1).astype(I32)
    new_run = jnp.concatenate([jnp.ones((1,), I32), (tile_e[1:] != tile_e[:-1]).astype(I32)])
    slot = ((jnp.cumsum(new_run) - 1) % W_SLOTS).astype(I32)
    next2_e = jnp.where(next_e >= 0, next_e[jnp.minimum(run_end, n_tiles - 1)], -1).astype(I32)
    arange = jnp.arange(n_tiles, dtype=I32)
    last = arange == run_end - 1
    valid = jnp.where(last, counts.astype(I32)[tile_e] - (tiles[tile_e] - 1) * EXPERT_TILE, EXPERT_TILE)
    valid = jnp.where(arange < n_act, valid, 0).astype(I32)
    return tile_e, next_e, next2_e, slot, valid, n_act.reshape(1).astype(I32)


def _block(x, p, norm_mix_w, w_in, b_merge, w_alpha_up, b_alpha_up, gla_norm_w, w_gla_out,
           conv_w, conv_b, w_conv_out, w_mix_out, norm_ffn_w, w_router_group, b_router_group,
           w_router_expert, b_router_expert, w_e_gate, w_e_up, w_e_down, ple_norm_w,
           w_ple_gate, w_ple_proj, final_norm_w):
    t = x.shape[0]
    n_tiles = (2 * t) // EXPERT_TILE + N_EXPERTS
    n_rows = n_tiles * EXPERT_TILE

    w_up = jnp.pad(w_alpha_up, ((0, LANES - GLA_GATE_RANK), (0, 0))).astype(BF16)
    w_up = w_up.reshape(LANES, GLA_HEADS, GLA_DK).transpose(1, 0, 2)
    b_up = b_alpha_up.reshape(GLA_HEADS, 1, GLA_DK)
    gnw = gla_norm_w.reshape(GLA_HEADS, 1, GLA_DV)
    wr_t = jnp.zeros((ROUTER_ROWS, D_MODEL), F32)
    wr_t = wr_t.at[0:N_GROUPS].set(w_router_group.T)
    wr_t = wr_t.at[EXPERT_ROW0:EXPERT_ROW0 + N_EXPERTS].set(w_router_expert.T)
    br = jnp.zeros((ROUTER_ROWS, 1), F32)
    br = br.at[0:N_GROUPS, 0].set(b_router_group)
    br = br.at[EXPERT_ROW0:EXPERT_ROW0 + N_EXPERTS, 0].set(b_router_expert)

    w_in_t = w_in.T
    h, a_low = _norm_in(x, norm_mix_w.reshape(1, D_MODEL), w_in_t)
    qkvg = _proj_qkvg(h, w_in_t)
    ob = _proj_conv(h, w_in_t, conv_w, conv_b.reshape(1, D_MODEL))
    gates = _proj_gates(h, w_in_t, b_merge.reshape(1, 2 * D_MODEL))
    oa = _gla(qkvg, a_low, w_up, b_up, gnw)
    mixed = _merge(oa, ob, w_gla_out, w_conv_out, gates)
    x1 = _mix(mixed, w_mix_out, x)

    hp, ids, wts = _route(x1, norm_ffn_w.reshape(1, D_MODEL), wr_t, br)
    dest, counts = _plan(ids)
    dest_flat = dest[1, 0:2].reshape(2 * t)
    tile_e, next_e, next2_e, slot, valid, n_act = _tile_table(counts[:, 0], n_tiles)
    row_slot = _invert(dest_flat, n_rows)
    yt = _experts(tile_e, next_e, next2_e, slot, valid, row_slot, n_act, hp, w_e_gate, w_e_up, w_e_down, n_tiles)
    return _combine_ple(yt, x1, wts.T, p,
                        ple_norm_w.reshape(1, D_MODEL), w_ple_gate.astype(BF16),
                        w_ple_proj.astype(BF16), final_norm_w.reshape(1, D_MODEL))


def kernel(x, p, norm_mix_w, w_in, b_merge, w_alpha_up, b_alpha_up, gla_norm_w, w_gla_out, conv_w, conv_b, w_conv_out, w_mix_out, norm_ffn_w, w_router_group, b_router_group, w_router_expert, b_router_expert, w_e_gate, w_e_up, w_e_down, ple_norm_w, w_ple_gate, w_ple_proj, final_norm_w):
    depth, batch = p.shape[0], x.shape[0]
    assert depth == 1 and batch == 1, "kernel is specialised to one layer and one sequence"
    out = _block(x[0], p[0, 0], norm_mix_w[0], w_in[0], b_merge[0], w_alpha_up[0], b_alpha_up[0],
                 gla_norm_w[0], w_gla_out[0], conv_w[0], conv_b[0], w_conv_out[0], w_mix_out[0],
                 norm_ffn_w[0], w_router_group[0], b_router_group[0], w_router_expert[0],
                 b_router_expert[0], w_e_gate[0], w_e_up[0], w_e_down[0], ple_norm_w[0],
                 w_ple_gate[0], w_ple_proj[0], final_norm_w)
    return out[None]
```

```python
import functools

import jax
import jax.numpy as jnp
from jax import lax
from jax.experimental import pallas as pl
from jax.experimental.pallas import tpu as pltpu

F32 = jnp.float32
BF16 = jnp.bfloat16
I32 = jnp.int32

D_MODEL = 2048
PLE_DIM = 256
EPS = 1e-6
LOG2_E = 1.4426950408889634
GLA_HEADS = 4
GLA_DK = 256
GLA_DV = 512
GLA_KEY = GLA_HEADS * GLA_DK
GLA_VAL = GLA_HEADS * GLA_DV
GLA_GATE_RANK = 16
GLA_GATE_NORM = 16.0
GLA_CHUNK = 64
CONV_K = 3
N_GROUPS = 4
EXPERTS_PER_GROUP = 8
N_EXPERTS = N_GROUPS * EXPERTS_PER_GROUP
D_FF = 512

QKVG_COLS = 2 * GLA_KEY + 2 * GLA_VAL
CONV_COL0 = QKVG_COLS + GLA_GATE_RANK

LANES = 128
SUBLANES = 8
MXU_COLS = 256
ROW_CHUNK = 256
ROW_TILES = D_MODEL // LANES

EXPERT_TILE = 256
DMA_UNROLL = 8
X_SLOTS = 3
ROUTER_ROWS = 64
EXPERT_ROW0 = 8

MIB = 1024 * 1024


def _cp(sem, vmem_mib, **kw):
    return pltpu.CompilerParams(dimension_semantics=sem, vmem_limit_bytes=int(vmem_mib * MIB), **kw)


def _rms(x, w):
    return x * lax.rsqrt(jnp.mean(x * x, axis=-1, keepdims=True) + EPS) * w


def _dot(a, b):
    return jnp.dot(a, b, preferred_element_type=F32)


def _dot_nt(a, b):
    return lax.dot_general(a, b, (((1,), (1,)), ((), ())), preferred_element_type=F32)


def _dot_tn(a, b):
    return lax.dot_general(a, b, (((0,), (0,)), ((), ())), preferred_element_type=F32)


def _split_bf16(x):
    hi = x.astype(BF16)
    lo = (x - hi.astype(F32)).astype(BF16)
    return hi, lo


def _to_token_major(val):
    return val.reshape(val.shape[0], ROW_TILES, LANES)


def _from_token_major(val):
    return val.reshape(val.shape[0], D_MODEL)


def _norm_in_kernel(x_ref, w_ref, wal_ref, h_ref, al_ref, walbf_ref):
    @pl.when(pl.program_id(0) == 0)
    def _():
        walbf_ref[...] = wal_ref[...].astype(BF16)

    h = _rms(x_ref[...], w_ref[...]).astype(BF16)
    h_ref[...] = h
    al_ref[...] = _dot_nt(h, walbf_ref[...])


def _norm_in(x, w, w_in_t, tm=512):
    t = x.shape[0]
    return pl.pallas_call(
        _norm_in_kernel,
        grid=(t // tm,),
        in_specs=[pl.BlockSpec((tm, D_MODEL), lambda i: (i, 0)),
                  pl.BlockSpec((1, D_MODEL), lambda i: (0, 0)),
                  pl.BlockSpec((LANES, D_MODEL), lambda i: (QKVG_COLS // LANES, 0))],
        out_specs=[pl.BlockSpec((tm, D_MODEL), lambda i: (i, 0)),
                   pl.BlockSpec((tm, LANES), lambda i: (i, 0))],
        out_shape=[jax.ShapeDtypeStruct((t, D_MODEL), BF16),
                   jax.ShapeDtypeStruct((t, LANES), F32)],
        scratch_shapes=[pltpu.VMEM((LANES, D_MODEL), BF16)],
        compiler_params=_cp(("arbitrary",), 32),
        name="norm_in",
    )(x, w, w_in_t)


def _w_rows_spec(row0, tn):
    assert row0 % SUBLANES == 0 and tn % SUBLANES == 0
    return pl.BlockSpec((pl.Element(tn), pl.Element(D_MODEL)),
                        lambda n, m: (pl.multiple_of(row0 + n * tn, SUBLANES), 0))


def _proj_qkvg_kernel(h_ref, w_ref, o_ref, wbf_ref):
    @pl.when(pl.program_id(1) == 0)
    def _():
        wbf_ref[...] = w_ref[...].astype(BF16)

    o_ref[...] = _dot_nt(h_ref[...], wbf_ref[...]).astype(o_ref.dtype)


def _proj_qkvg(h, w_in_t, tm=2048, tn=1024):
    t = h.shape[0]
    return pl.pallas_call(
        _proj_qkvg_kernel,
        grid=(QKVG_COLS // tn, t // tm),
        in_specs=[pl.BlockSpec((tm, D_MODEL), lambda n, m: (m, 0)),
                  pl.BlockSpec((tn, D_MODEL), lambda n, m: (n, 0))],
        out_specs=pl.BlockSpec((tm, tn), lambda n, m: (m, n)),
        out_shape=jax.ShapeDtypeStruct((t, QKVG_COLS), BF16),
        scratch_shapes=[pltpu.VMEM((tn, D_MODEL), BF16)],
        compiler_params=_cp(("arbitrary", "arbitrary"), 56),
        name="proj_qkvg",
    )(h, w_in_t)


def _proj_conv_kernel(h_ref, wb_ref, wc_ref, wx_ref, cw_ref, cb_ref, o_ref,
                      wbbf_ref, wcbf_ref, wxbf_ref, prev_ref):
    m = pl.program_id(1)

    @pl.when(m == 0)
    def _():
        prev_ref[...] = jnp.zeros_like(prev_ref)
        wbbf_ref[...] = wb_ref[...].astype(BF16)
        wcbf_ref[...] = wc_ref[...].astype(BF16)
        wxbf_ref[...] = wx_ref[...].astype(BF16)

    h = h_ref[...]
    tm = h.shape[0]
    row = lax.broadcasted_iota(I32, (tm, MXU_COLS), 0)
    for c in range(0, o_ref.shape[1], MXU_COLS):
        cols = slice(c, c + MXU_COLS)
        b = _dot_nt(h, wbbf_ref[cols, :])
        s = _dot_nt(h, wcbf_ref[cols, :]) * _dot_nt(h, wxbf_ref[cols, :])
        prev = prev_ref[:, cols]
        p1 = prev[SUBLANES - 1:SUBLANES, :]
        p2 = prev[SUBLANES - 2:SUBLANES - 1, :]
        s1 = jnp.where(row == 0, p1, pltpu.roll(s, 1, 0))
        s2 = jnp.where(row == 0, p2, jnp.where(row == 1, p1, pltpu.roll(s, 2, 0)))
        cw = cw_ref[:, cols]
        u = cw[2:3, :] * s + cw[1:2, :] * s1 + cw[0:1, :] * s2 + cb_ref[:, cols]
        o_ref[:, cols] = (b * u).astype(o_ref.dtype)
        prev_ref[:, cols] = s[tm - SUBLANES:, :]


def _proj_conv(h, w_in_t, conv_w, conv_b, tm=1024, tn=512):
    t = h.shape[0]
    return pl.pallas_call(
        _proj_conv_kernel,
        grid=(D_MODEL // tn, t // tm),
        in_specs=[pl.BlockSpec((tm, D_MODEL), lambda n, m: (m, 0))] +
                 [_w_rows_spec(CONV_COL0 + seg * D_MODEL, tn) for seg in range(3)] +
                 [pl.BlockSpec((CONV_K, tn), lambda n, m: (0, n)),
                  pl.BlockSpec((1, tn), lambda n, m: (0, n))],
        out_specs=pl.BlockSpec((tm, tn), lambda n, m: (m, n)),
        out_shape=jax.ShapeDtypeStruct((t, D_MODEL), BF16),
        scratch_shapes=[pltpu.VMEM((tn, D_MODEL), BF16), pltpu.VMEM((tn, D_MODEL), BF16),
                        pltpu.VMEM((tn, D_MODEL), BF16), pltpu.VMEM((SUBLANES, tn), F32)],
        compiler_params=_cp(("arbitrary", "arbitrary"), 48),
        name="proj_conv",
    )(h, w_in_t, w_in_t, w_in_t, conv_w, conv_b)


def _proj_gates_kernel(h_ref, w_ref, b_ref, o_ref, wbf_ref):
    @pl.when(pl.program_id(1) == 0)
    def _():
        wbf_ref[...] = w_ref[...].astype(BF16)

    for c in range(0, o_ref.shape[1], MXU_COLS):
        cols = slice(c, c + MXU_COLS)
        for r in range(0, o_ref.shape[0], ROW_CHUNK):
            rows = slice(r, r + ROW_CHUNK)
            z = _dot_nt(h_ref[rows, :], wbf_ref[cols, :]) + b_ref[:, cols]
            o_ref[rows, cols] = jax.nn.sigmoid(z).astype(o_ref.dtype)


def _proj_gates(h, w_in_t, b_merge, tm=2048, tn=1024):
    t = h.shape[0]
    return pl.pallas_call(
        _proj_gates_kernel,
        grid=(2 * D_MODEL // tn, t // tm),
        in_specs=[pl.BlockSpec((tm, D_MODEL), lambda n, m: (m, 0)),
                  _w_rows_spec(CONV_COL0 + 3 * D_MODEL, tn),
                  pl.BlockSpec((1, tn), lambda n, m: (0, n))],
        out_specs=pl.BlockSpec((tm, tn), lambda n, m: (m, n)),
        out_shape=jax.ShapeDtypeStruct((t, 2 * D_MODEL), BF16),
        scratch_shapes=[pltpu.VMEM((tn, D_MODEL), BF16)],
        compiler_params=_cp(("arbitrary", "arbitrary"), 56),
        name="proj_gates",
    )(h, w_in_t, b_merge)


def _gla_kernel(q_ref, k_ref, v_ref, g_ref, al_ref, wup_ref, bup_ref, nw_ref, o_ref,
                st_ref, b_ref, bl_ref, qd_ref, ki_ref, kd_ref, oi_ref, u_ref, *, n_chunks):
    c_len = GLA_CHUNK

    @pl.when(pl.program_id(1) == 0)
    def _():
        st_ref[...] = jnp.zeros_like(st_ref)

    row = lax.broadcasted_iota(I32, (c_len, c_len), 0)
    col = lax.broadcasted_iota(I32, (c_len, c_len), 1)
    causal = col <= row
    tril = jnp.where(causal, 1.0, 0.0).astype(BF16)

    def decays(h):
        kc = slice(h * GLA_DK, (h + 1) * GLA_DK)
        z = _dot(al_ref[...].astype(BF16), wup_ref[h]) + bup_ref[h]
        la = (jnp.minimum(z, 0.0) - jnp.log1p(jnp.exp(-jnp.abs(z)))) * (LOG2_E / GLA_GATE_NORM)
        la_hi, la_lo = _split_bf16(la)
        for c in range(n_chunks):
            r0 = c * c_len
            b = _dot(tril, la_hi[r0:r0 + c_len]) + _dot(tril, la_lo[r0:r0 + c_len])
            b_ref[pl.ds(r0, c_len), kc] = b
            bl_ref[pl.ds(r0, c_len), kc] = jnp.broadcast_to(b[c_len - 1:c_len, :], b.shape)
        b = b_ref[:, kc]
        q = q_ref[:, kc].astype(F32)
        k = k_ref[:, kc].astype(F32)
        qd_ref[:, kc] = (q * jnp.exp2(b)).astype(BF16)
        ki_ref[:, kc] = (k * jnp.exp2(-b)).astype(BF16)
        kd_ref[:, kc] = (k * jnp.exp2(bl_ref[:, kc] - b)).astype(BF16)

    def local_products(h):
        kc = slice(h * GLA_DK, (h + 1) * GLA_DK)
        vc = slice(h * GLA_DV, (h + 1) * GLA_DV)
        for c in range(n_chunks):
            sl = pl.ds(c * c_len, c_len)
            v = v_ref[sl, vc]
            att = jnp.where(causal, _dot_nt(qd_ref[sl, kc], ki_ref[sl, kc]), 0.0).astype(BF16)
            oi_ref[sl, vc] = _dot(att, v)
            u_ref[h, c] = _dot_tn(v, kd_ref[sl, kc])

    def recurrence(h):
        kc = slice(h * GLA_DK, (h + 1) * GLA_DK)
        vc = slice(h * GLA_DV, (h + 1) * GLA_DV)
        for c in range(n_chunks):
            sl = pl.ds(c * c_len, c_len)
            st = st_ref[h]
            oi_ref[sl, vc] = oi_ref[sl, vc] + _dot_nt(qd_ref[sl, kc], st.astype(BF16))
            st_ref[h] = st * jnp.exp2(bl_ref[pl.ds(c * c_len, 1), kc]) + u_ref[h, c]
        o = oi_ref[:, vc]
        o = o * lax.rsqrt(jnp.mean(o * o, axis=-1, keepdims=True) + EPS * GLA_DK) * nw_ref[h]
        g = g_ref[:, vc].astype(F32)
        o_ref[:, vc] = (o * (g * jax.nn.sigmoid(g))).astype(o_ref.dtype)

    decays(0)
    local_products(0)
    decays(1)
    recurrence(0)
    local_products(1)
    recurrence(1)


def _gla(qkvg, a_low, w_up, b_up, norm_w, tb=1024):
    t = qkvg.shape[0]
    hp = 2
    dk, dv = hp * GLA_DK, hp * GLA_DV
    kq = GLA_KEY // dk
    kv = 2 * GLA_KEY // dv
    kg = kv + GLA_VAL // dv
    n_chunks = tb // GLA_CHUNK
    kern = functools.partial(_gla_kernel, n_chunks=n_chunks)
    return pl.pallas_call(
        kern,
        grid=(GLA_HEADS // hp, t // tb),
        in_specs=[pl.BlockSpec((tb, dk), lambda h, i: (i, h)),
                  pl.BlockSpec((tb, dk), lambda h, i: (i, kq + h)),
                  pl.BlockSpec((tb, dv), lambda h, i: (i, kv + h)),
                  pl.BlockSpec((tb, dv), lambda h, i: (i, kg + h)),
                  pl.BlockSpec((tb, LANES), lambda h, i: (i, 0)),
                  pl.BlockSpec((hp, LANES, GLA_DK), lambda h, i: (h, 0, 0)),
                  pl.BlockSpec((hp, 1, GLA_DK), lambda h, i: (h, 0, 0)),
                  pl.BlockSpec((hp, 1, GLA_DV), lambda h, i: (h, 0, 0))],
        out_specs=pl.BlockSpec((tb, dv), lambda h, i: (i, h)),
        out_shape=jax.ShapeDtypeStruct((t, GLA_VAL), BF16),
        scratch_shapes=[pltpu.VMEM((hp, GLA_DV, GLA_DK), F32),
                        pltpu.VMEM((tb, dk), F32), pltpu.VMEM((tb, dk), F32),
                        pltpu.VMEM((tb, dk), BF16), pltpu.VMEM((tb, dk), BF16),
                        pltpu.VMEM((tb, dk), BF16),
                        pltpu.VMEM((tb, dv), F32),
                        pltpu.VMEM((hp, n_chunks, GLA_DV, GLA_DK), F32)],
        compiler_params=_cp(("arbitrary", "arbitrary"), 56),
        name="gla",
    )(qkvg, qkvg, qkvg, qkvg, a_low, w_up, b_up, norm_w)


def _merge_kernel(oa_ref, ob_ref, wa_ref, wb_ref, g0_ref, g1_ref, o_ref, wabf_ref, wbbf_ref):
    @pl.when(pl.program_id(1) == 0)
    def _():
        wabf_ref[...] = wa_ref[...].astype(BF16)
        wbbf_ref[...] = wb_ref[...].astype(BF16)

    oa = oa_ref[...]
    ob = ob_ref[...]
    for c in range(0, o_ref.shape[1], MXU_COLS):
        cols = slice(c, c + MXU_COLS)
        a = _dot(oa, wabf_ref[:, cols])
        b = _dot(ob, wbbf_ref[:, cols])
        o_ref[:, cols] = (g0_ref[:, cols].astype(F32) * a + g1_ref[:, cols].astype(F32) * b).astype(o_ref.dtype)


def _merge(oa, ob, w_a, w_b, gates, tm=1024, tn=512):
    t = oa.shape[0]
    nb = D_MODEL // tn
    return pl.pallas_call(
        _merge_kernel,
        grid=(nb, t // tm),
        in_specs=[pl.BlockSpec((tm, D_MODEL), lambda n, m: (m, 0)),
                  pl.BlockSpec((tm, D_MODEL), lambda n, m: (m, 0)),
                  pl.BlockSpec((D_MODEL, tn), lambda n, m: (0, n)),
                  pl.BlockSpec((D_MODEL, tn), lambda n, m: (0, n)),
                  pl.BlockSpec((tm, tn), lambda n, m: (m, n)),
                  pl.BlockSpec((tm, tn), lambda n, m: (m, nb + n))],
        out_specs=pl.BlockSpec((tm, tn), lambda n, m: (m, n)),
        out_shape=jax.ShapeDtypeStruct((t, D_MODEL), BF16),
        scratch_shapes=[pltpu.VMEM((D_MODEL, tn), BF16), pltpu.VMEM((D_MODEL, tn), BF16)],
        compiler_params=_cp(("arbitrary", "arbitrary"), 56),
        name="merge",
    )(oa, ob, w_a, w_b, gates, gates)


def _mix_kernel(a_ref, w_ref, x_ref, o_ref, wbf_ref):
    @pl.when(pl.program_id(1) == 0)
    def _():
        wbf_ref[...] = w_ref[...].astype(BF16)

    a = a_ref[...]
    for c in range(0, o_ref.shape[1], MXU_COLS):
        cols = slice(c, c + MXU_COLS)
        o_ref[:, cols] = x_ref[:, cols] + _dot(a, wbf_ref[:, cols])


def _mix(mixed, w, x, tm=1024, tn=1024):
    t = x.shape[0]
    return pl.pallas_call(
        _mix_kernel,
        grid=(D_MODEL // tn, t // tm),
        in_specs=[pl.BlockSpec((tm, D_MODEL), lambda n, m: (m, 0)),
                  pl.BlockSpec((D_MODEL, tn), lambda n, m: (0, n)),
                  pl.BlockSpec((tm, tn), lambda n, m: (m, n))],
        out_specs=pl.BlockSpec((tm, tn), lambda n, m: (m, n)),
        out_shape=jax.ShapeDtypeStruct((t, D_MODEL), F32),
        scratch_shapes=[pltpu.VMEM((D_MODEL, tn), BF16)],
        compiler_params=_cp(("arbitrary", "arbitrary"), 52),
        name="mix",
    )(mixed, w, x)


def _route_kernel(x_ref, nw_ref, wr_ref, br_ref, hp_ref, ids_ref, wts_ref):
    h = _rms(x_ref[...], nw_ref[...])
    hp_ref[...] = _to_token_major(h.astype(BF16))
    h_hi, h_lo = _split_bf16(h)
    w_hi, w_lo = _split_bf16(wr_ref[...])
    logits = _dot_nt(w_hi, h_hi) + _dot_nt(w_hi, h_lo) + _dot_nt(w_lo, h_hi) + br_ref[...]
    tm = logits.shape[1]

    best = logits[0:1, :]
    gidx = jnp.zeros((1, tm), I32)
    for i in range(1, N_GROUPS):
        li = logits[i:i + 1, :]
        take = li > best
        best = jnp.where(take, li, best)
        gidx = jnp.where(take, i, gidx)
    gsum = jnp.zeros((1, tm), F32)
    for i in range(N_GROUPS):
        gsum = gsum + jnp.exp(logits[i:i + 1, :] - best)
    g_p = 1.0 / gsum

    sel = logits[EXPERT_ROW0:EXPERT_ROW0 + EXPERTS_PER_GROUP, :]
    for g in range(1, N_GROUPS):
        r0 = EXPERT_ROW0 + g * EXPERTS_PER_GROUP
        sel = jnp.where(gidx == g, logits[r0:r0 + EXPERTS_PER_GROUP, :], sel)
    eio = lax.broadcasted_iota(I32, sel.shape, 0)
    m1 = jnp.max(sel, axis=0, keepdims=True)
    i1 = jnp.min(jnp.where(sel == m1, eio, EXPERTS_PER_GROUP), axis=0, keepdims=True)
    rest = jnp.where(eio == i1, -jnp.inf, sel)
    m2 = jnp.max(rest, axis=0, keepdims=True)
    i2 = jnp.min(jnp.where(rest == m2, eio, EXPERTS_PER_GROUP), axis=0, keepdims=True)
    p2 = jnp.exp(m2 - m1)
    w1 = g_p / (1.0 + p2)
    w2 = g_p * p2 / (1.0 + p2)
    e1 = gidx * EXPERTS_PER_GROUP + i1
    e2 = gidx * EXPERTS_PER_GROUP + i2
    rio = lax.broadcasted_iota(I32, (SUBLANES, tm), 0)
    ids_ref[...] = jnp.where(rio == 0, e1, jnp.where(rio == 1, e2, 0))
    wts_ref[...] = jnp.where(rio == 0, w1, jnp.where(rio == 1, w2, 0.0))


def _route(x1, norm_w, wr_t, br, tm=512):
    t = x1.shape[0]
    return pl.pallas_call(
        _route_kernel,
        grid=(t // tm,),
        in_specs=[pl.BlockSpec((tm, D_MODEL), lambda i: (i, 0)),
                  pl.BlockSpec((1, D_MODEL), lambda i: (0, 0)),
                  pl.BlockSpec((ROUTER_ROWS, D_MODEL), lambda i: (0, 0)),
                  pl.BlockSpec((ROUTER_ROWS, 1), lambda i: (0, 0))],
        out_specs=[pl.BlockSpec((tm, ROW_TILES, LANES), lambda i: (i, 0, 0)),
                   pl.BlockSpec((SUBLANES, tm), lambda i: (0, i)),
                   pl.BlockSpec((SUBLANES, tm), lambda i: (0, i))],
        out_shape=[jax.ShapeDtypeStruct((t, ROW_TILES, LANES), BF16),
                   jax.ShapeDtypeStruct((SUBLANES, t), I32),
                   jax.ShapeDtypeStruct((SUBLANES, t), F32)],
        compiler_params=_cp(("arbitrary",), 32),
        name="route",
    )(x1, norm_w, wr_t, br)


def _plan_kernel(ids_ref, dest_ref, cnt_ref, base_ref):
    phase = pl.program_id(0)
    step = pl.program_id(1)
    tm = ids_ref.shape[1]
    eio = lax.broadcasted_iota(I32, (N_EXPERTS, tm), 0)
    ids = ids_ref[...]
    oh = [jnp.where(eio == ids[k:k + 1, :], 1.0, 0.0) for k in range(2)]

    @pl.when((phase == 0) & (step == 0))
    def _():
        base_ref[...] = jnp.zeros_like(base_ref)

    @pl.when(phase == 0)
    def _():
        cnt = jnp.sum(oh[0] + oh[1], axis=1, keepdims=True)
        base_ref[...] = base_ref[...] + cnt
        dest_ref[0] = jnp.zeros(dest_ref.shape[1:], I32)
        cnt_ref[...] = base_ref[...]

    @pl.when((phase == 1) & (step == 0))
    def _():
        tiles = jnp.floor((base_ref[...] + (EXPERT_TILE - 1)) * (1.0 / EXPERT_TILE))
        r = lax.broadcasted_iota(I32, (N_EXPERTS, N_EXPERTS), 0)
        c = lax.broadcasted_iota(I32, (N_EXPERTS, N_EXPERTS), 1)
        lower = jnp.where(c < r, 1.0, 0.0).astype(BF16)
        base_ref[...] = _dot(lower, tiles.astype(BF16)) * float(EXPERT_TILE)

    @pl.when(phase == 1)
    def _():
        r = lax.broadcasted_iota(I32, (tm, tm), 0)
        c = lax.broadcasted_iota(I32, (tm, tm), 1)
        upper = jnp.where(r <= c, 1.0, 0.0).astype(BF16)
        base = base_ref[...][:, 0:1]
        rows = []
        for k in range(2):
            cum = _dot(oh[k].astype(BF16), upper)
            rows.append(jnp.sum(oh[k] * (cum - 1.0 + base), axis=0, keepdims=True))
            base = base + cum[:, tm - 1:tm]
        base_ref[...] = jnp.broadcast_to(base, base_ref.shape)
        rio = lax.broadcasted_iota(I32, (SUBLANES, tm), 0)
        d0 = rows[0].astype(I32)
        d1 = rows[1].astype(I32)
        dest_ref[0] = jnp.where(rio == 0, d0, jnp.where(rio == 1, d1, 0))


def _plan(ids, tm=512):
    t = ids.shape[1]
    return pl.pallas_call(
        _plan_kernel,
        grid=(2, t // tm),
        in_specs=[pl.BlockSpec((SUBLANES, tm), lambda p, i: (0, i))],
        out_specs=[pl.BlockSpec((1, SUBLANES, tm), lambda p, i: (p, 0, i)),
                   pl.BlockSpec((N_EXPERTS, LANES), lambda p, i: (0, 0))],
        out_shape=[jax.ShapeDtypeStruct((2, SUBLANES, t), I32),
                   jax.ShapeDtypeStruct((N_EXPERTS, LANES), F32)],
        scratch_shapes=[pltpu.VMEM((N_EXPERTS, LANES), F32)],
        compiler_params=_cp(("arbitrary", "arbitrary"), 32),
        name="plan",
    )(ids)


def _invert_kernel(dest_ref, slot_ref, fill_ref, sem, *, n_slots):
    fill_ref[...] = jnp.full(fill_ref.shape, n_slots, I32)
    fill = pltpu.make_async_copy(fill_ref, slot_ref, sem)
    fill.start()
    fill.wait()

    def scatter(j, c):
        rows = [dest_ref[j * DMA_UNROLL + u] for u in range(DMA_UNROLL)]
        for u in range(DMA_UNROLL):
            slot_ref[rows[u]] = j * DMA_UNROLL + u
        return c

    lax.fori_loop(0, n_slots // DMA_UNROLL, scatter, 0)


def _invert(dest_flat, n_rows):
    n_slots = dest_flat.shape[0]
    assert n_slots % DMA_UNROLL == 0
    return pl.pallas_call(
        functools.partial(_invert_kernel, n_slots=n_slots),
        grid_spec=pltpu.PrefetchScalarGridSpec(
            num_scalar_prefetch=1,
            grid=(1,),
            in_specs=[],
            out_specs=pl.BlockSpec(memory_space=pltpu.SMEM),
            scratch_shapes=[pltpu.VMEM((n_rows,), I32), pltpu.SemaphoreType.DMA(())]),
        out_shape=jax.ShapeDtypeStruct((n_rows,), I32),
        compiler_params=_cp(("arbitrary",), 16),
        name="invert",
    )(dest_flat)


def _experts_kernel(te_ref, nx_ref, sl_ref, vr_ref, rs_ref, na_ref, hp_hbm, wg_hbm, wu_hbm, wd_hbm, yt_hbm,
                    wgf_ref, wuf_ref, wdf_ref, wgbf_ref, wubf_ref, wdbf_ref, xbuf_ref, ybuf_ref,
                    sems, xsems, ysems, *, t_total):
    i = pl.program_id(0)
    n_act = na_ref[0]
    active = i < n_act
    changed = (i == 0) | (te_ref[i] != te_ref[jnp.maximum(i - 1, 0)])

    def weight_copies(e, slot):
        return (pltpu.make_async_copy(wg_hbm.at[e], wgf_ref.at[slot], sems.at[slot, 0]),
                pltpu.make_async_copy(wu_hbm.at[e], wuf_ref.at[slot], sems.at[slot, 1]),
                pltpu.make_async_copy(wd_hbm.at[e], wdf_ref.at[slot], sems.at[slot, 2]))

    def in_copy(tile, r):
        tok = rs_ref[tile * EXPERT_TILE + r] & (t_total - 1)
        return pltpu.make_async_copy(hp_hbm.at[tok], xbuf_ref.at[tile % X_SLOTS, r], xsems.at[tile % X_SLOTS])

    def out_copy(tile, r):
        dst = yt_hbm.at[rs_ref[tile * EXPERT_TILE + r]]
        return pltpu.make_async_copy(ybuf_ref.at[tile % 2, r], dst, ysems.at[tile % 2])

    def for_rows(tile, make_copy, fn):
        n = vr_ref[tile]
        groups = lax.shift_right_logical(n, DMA_UNROLL.bit_length() - 1)

        def body(j, c):
            for u in range(DMA_UNROLL):
                fn(make_copy(tile, j * DMA_UNROLL + u))
            return c

        def tail(r, c):
            fn(make_copy(tile, r))
            return c

        lax.fori_loop(0, groups, body, 0)
        lax.fori_loop(groups * DMA_UNROLL, n, tail, 0)

    def wait_rows(tile, make_copy, whole_tile_copy):
        full = vr_ref[tile] == EXPERT_TILE

        @pl.when(full)
        def _():
            whole_tile_copy.wait()

        @pl.when(jnp.logical_not(full))
        def _():
            for_rows(tile, make_copy, lambda cp: cp.wait())

    def in_tile(tile):
        return pltpu.make_async_copy(hp_hbm.at[pl.ds(0, EXPERT_TILE)], xbuf_ref.at[tile % X_SLOTS],
                                     xsems.at[tile % X_SLOTS])

    def out_tile(tile):
        return pltpu.make_async_copy(ybuf_ref.at[tile % 2], yt_hbm.at[pl.ds(0, EXPERT_TILE)], ysems.at[tile % 2])

    @pl.when(i == 0)
    def _():
        xbuf_ref[...] = jnp.zeros_like(xbuf_ref)
        for cp in weight_copies(te_ref[0], 0):
            cp.start(priority=1)
        for_rows(i, in_copy, lambda cp: cp.start())

        @pl.when(n_act > 1)
        def _():
            for_rows(i + 1, in_copy, lambda cp: cp.start())

    last = pl.num_programs(0) - 1
    nxt_active = i + 2 < n_act
    prev_active = (i >= 1) & (i - 1 < n_act)

    @pl.when((i >= 2) & (i - 2 < n_act))
    def _():
        wait_rows(i - 2, out_copy, out_tile(i - 2))

    @pl.when(prev_active & jnp.logical_not(active))
    def _():
        for_rows(i - 1, out_copy, lambda cp: cp.start())

    @pl.when(active & changed)
    def _():
        slot = sl_ref[i]
        nxt = nx_ref[i]

        @pl.when(nxt >= 0)
        def _():
            for cp in weight_copies(nxt, 1 - slot):
                cp.start(priority=1)

        for cp in weight_copies(te_ref[i], slot):
            cp.wait()
        wgbf_ref[...] = wgf_ref[slot].astype(BF16)
        wubf_ref[...] = wuf_ref[slot].astype(BF16)
        wdbf_ref[...] = wdf_ref[slot].astype(BF16)

    def compute(fetch_next, send_prev):
        n_next = vr_ref[jnp.minimum(i + 2, last)]
        n_prev = vr_ref[jnp.maximum(i - 1, 0)]

        def row_copies(g):
            for r in range(g * (EXPERT_TILE // 4), (g + 1) * (EXPERT_TILE // 4)):
                if fetch_next:
                    @pl.when(r < n_next)
                    def _():
                        in_copy(i + 2, r).start()
                if send_prev:
                    @pl.when(r < n_prev)
                    def _():
                        out_copy(i - 1, r).start()

        wait_rows(i, in_copy, in_tile(i))
        x = _from_token_major(xbuf_ref[i % X_SLOTS])
        row_copies(0)
        hg = _dot(x, wgbf_ref[...])
        row_copies(1)
        hu = _dot(x, wubf_ref[...])
        row_copies(2)
        act = (hg * jax.nn.sigmoid(hg) * hu).astype(BF16)
        y = _dot(act, wdbf_ref[...])
        row_copies(3)
        ybuf_ref[i % 2] = _to_token_major(y.astype(BF16))

    for fetch_next in (False, True):
        for send_prev in (False, True):
            @pl.when(active & (nxt_active == fetch_next) & (prev_active == send_prev))
            def _(fetch_next=fetch_next, send_prev=send_prev):
                compute(fetch_next, send_prev)

    @pl.when(i == last)
    def _():
        @pl.when(active)
        def _():
            for_rows(i, out_copy, lambda cp: cp.start())
            wait_rows(i, out_copy, out_tile(i))

        @pl.when(prev_active)
        def _():
            wait_rows(i - 1, out_copy, out_tile(i - 1))


def _experts(tile_e, next_e, slot, valid, row_slot, n_act, hp, w_gate, w_up, w_down, n_tiles):
    t = hp.shape[0]
    assert t & (t - 1) == 0, "the row map packs slot * T + token with T a power of two"
    any_spec = pl.BlockSpec(memory_space=pl.ANY)
    tile_buf = lambda slots: pltpu.VMEM((slots, EXPERT_TILE, ROW_TILES, LANES), BF16)
    return pl.pallas_call(
        functools.partial(_experts_kernel, t_total=t),
        grid_spec=pltpu.PrefetchScalarGridSpec(
            num_scalar_prefetch=6,
            grid=(n_tiles,),
            in_specs=[any_spec, any_spec, any_spec, any_spec],
            out_specs=any_spec,
            scratch_shapes=[pltpu.VMEM((2, D_MODEL, D_FF), F32),
                            pltpu.VMEM((2, D_MODEL, D_FF), F32),
                            pltpu.VMEM((2, D_FF, D_MODEL), F32),
                            pltpu.VMEM((D_MODEL, D_FF), BF16),
                            pltpu.VMEM((D_MODEL, D_FF), BF16),
                            pltpu.VMEM((D_FF, D_MODEL), BF16),
                            tile_buf(X_SLOTS), tile_buf(2),
                            pltpu.SemaphoreType.DMA((2, 3)),
                            pltpu.SemaphoreType.DMA((X_SLOTS,)),
                            pltpu.SemaphoreType.DMA((2,))]),
        out_shape=jax.ShapeDtypeStruct((2 * t, ROW_TILES, LANES), BF16),
        compiler_params=_cp(("arbitrary",), 48, has_side_effects=True),
        name="experts",
    )(tile_e, next_e, slot, valid, row_slot, n_act, hp, w_gate, w_up, w_down)


def _combine_kernel(y0_ref, y1_ref, x_ref, wt_ref, p_ref, nw_ref, wg_ref, wp_ref, fw_ref, o_ref):
    wt = wt_ref[...]
    y0 = _from_token_major(y0_ref[...]).astype(F32)
    y1 = _from_token_major(y1_ref[...]).astype(F32)
    x2 = x_ref[...] + wt[:, 0:1] * y0 + wt[:, 1:2] * y1
    hn = _rms(x2, nw_ref[...]).astype(BF16)
    pg = jax.nn.sigmoid(_dot(hn, wg_ref[...]))
    x3 = x2 + pg * _dot(p_ref[...].astype(BF16), wp_ref[...])
    o_ref[...] = _rms(x3, fw_ref[...])


def _combine_ple(yt, x1, wts_t, p, ple_norm_w, w_gate, w_proj, final_w, tm=512):
    t = x1.shape[0]
    nb = t // tm
    row = lambda i: (i, 0)
    fix = lambda i: (0, 0)
    return pl.pallas_call(
        _combine_kernel,
        grid=(nb,),
        in_specs=[pl.BlockSpec((tm, ROW_TILES, LANES), lambda i: (i, 0, 0)),
                  pl.BlockSpec((tm, ROW_TILES, LANES), lambda i: (nb + i, 0, 0)),
                  pl.BlockSpec((tm, D_MODEL), row),
                  pl.BlockSpec((tm, SUBLANES), row),
                  pl.BlockSpec((tm, PLE_DIM), row),
                  pl.BlockSpec((1, D_MODEL), fix),
                  pl.BlockSpec((D_MODEL, D_MODEL), fix),
                  pl.BlockSpec((PLE_DIM, D_MODEL), fix),
                  pl.BlockSpec((1, D_MODEL), fix)],
        out_specs=pl.BlockSpec((tm, D_MODEL), row),
        out_shape=jax.ShapeDtypeStruct((t, D_MODEL), F32),
        compiler_params=_cp(("arbitrary",), 56),
        name="combine_ple",
    )(yt, yt, x1, wts_t, p, ple_norm_w, w_gate, w_proj, final_w)


def _tile_table(counts, n_tiles):
    tiles = (counts.astype(I32) + (EXPERT_TILE - 1)) // EXPERT_TILE
    ends = jnp.cumsum(tiles)
    n_act = ends[-1]
    idx = jnp.minimum(jnp.arange(n_tiles, dtype=I32), n_act - 1)
    tile_e = jnp.sum((idx[:, None] >= ends[None, :]).astype(I32), axis=1).astype(I32)
    run_end = ends[tile_e]
    next_e = jnp.where(run_end < n_act, tile_e[jnp.minimum(run_end, n_tiles - 1)], -1).astype(I32)
    new_run = jnp.concatenate([jnp.ones((1,), I32), (tile_e[1:] != tile_e[:-1]).astype(I32)])
    slot = ((jnp.cumsum(new_run) - 1) % 2).astype(I32)
    arange = jnp.arange(n_tiles, dtype=I32)
    last = arange == run_end - 1
    valid = jnp.where(last, counts.astype(I32)[tile_e] - (tiles[tile_e] - 1) * EXPERT_TILE, EXPERT_TILE)
    valid = jnp.where(arange < n_act, valid, 0).astype(I32)
    return tile_e, next_e, slot, valid, n_act.reshape(1).astype(I32)


def _block(x, p, norm_mix_w, w_in, b_merge, w_alpha_up, b_alpha_up, gla_norm_w, w_gla_out,
           conv_w, conv_b, w_conv_out, w_mix_out, norm_ffn_w, w_router_group, b_router_group,
           w_router_expert, b_router_expert, w_e_gate, w_e_up, w_e_down, ple_norm_w,
           w_ple_gate, w_ple_proj, final_norm_w):
    t = x.shape[0]
    n_tiles = (2 * t) // EXPERT_TILE + N_EXPERTS
    n_rows = n_tiles * EXPERT_TILE

    w_up = jnp.pad(w_alpha_up, ((0, LANES - GLA_GATE_RANK), (0, 0))).astype(BF16)
    w_up = w_up.reshape(LANES, GLA_HEADS, GLA_DK).transpose(1, 0, 2)
    b_up = b_alpha_up.reshape(GLA_HEADS, 1, GLA_DK)
    gnw = gla_norm_w.reshape(GLA_HEADS, 1, GLA_DV)
    wr_t = jnp.zeros((ROUTER_ROWS, D_MODEL), F32)
    wr_t = wr_t.at[0:N_GROUPS].set(w_router_group.T)
    wr_t = wr_t.at[EXPERT_ROW0:EXPERT_ROW0 + N_EXPERTS].set(w_router_expert.T)
    br = jnp.zeros((ROUTER_ROWS, 1), F32)
    br = br.at[0:N_GROUPS, 0].set(b_router_group)
    br = br.at[EXPERT_ROW0:EXPERT_ROW0 + N_EXPERTS, 0].set(b_router_expert)

    w_in_t = w_in.T
    h, a_low = _norm_in(x, norm_mix_w.reshape(1, D_MODEL), w_in_t)
    qkvg = _proj_qkvg(h, w_in_t)
    ob = _proj_conv(h, w_in_t, conv_w, conv_b.reshape(1, D_MODEL))
    gates = _proj_gates(h, w_in_t, b_merge.reshape(1, 2 * D_MODEL))
    oa = _gla(qkvg, a_low, w_up, b_up, gnw)
    mixed = _merge(oa, ob, w_gla_out, w_conv_out, gates)
    x1 = _mix(mixed, w_mix_out, x)

    hp, ids, wts = _route(x1, norm_ffn_w.reshape(1, D_MODEL), wr_t, br)
    dest, counts = _plan(ids)
    dest_flat = dest[1, 0:2].reshape(2 * t)
    tile_e, next_e, slot, valid, n_act = _tile_table(counts[:, 0], n_tiles)
    row_slot = _invert(dest_flat, n_rows)
    yt = _experts(tile_e, next_e, slot, valid, row_slot, n_act, hp, w_e_gate, w_e_up, w_e_down, n_tiles)
    return _combine_ple(yt, x1, wts.T, p,
                        ple_norm_w.reshape(1, D_MODEL), w_ple_gate.astype(BF16),
                        w_ple_proj.astype(BF16), final_norm_w.reshape(1, D_MODEL))


def kernel(x, p, norm_mix_w, w_in, b_merge, w_alpha_up, b_alpha_up, gla_norm_w, w_gla_out, conv_w, conv_b, w_conv_out, w_mix_out, norm_ffn_w, w_router_group, b_router_group, w_router_expert, b_router_expert, w_e_gate, w_e_up, w_e_down, ple_norm_w, w_ple_gate, w_ple_proj, final_norm_w):
    depth, batch = p.shape[0], x.shape[0]
    assert depth == 1 and batch == 1, "kernel is specialised to one layer and one sequence"
    out = _block(x[0], p[0, 0], norm_mix_w[0], w_in[0], b_merge[0], w_alpha_up[0], b_alpha_up[0],
                 gla_norm_w[0], w_gla_out[0], conv_w[0], conv_b[0], w_conv_out[0], w_mix_out[0],
                 norm_ffn_w[0], w_router_group[0], b_router_group[0], w_router_expert[0],
                 b_router_expert[0], w_e_gate[0], w_e_up[0], w_e_down[0], ple_norm_w[0],
                 w_ple_gate[0], w_ple_proj[0], final_norm_w)
    return out[None]
```
